```python
import math
import jax, jax.numpy as jnp
from jax import lax
import numpy as np

D_MODEL = 1024
BATCH = 2
SEQ = 8192
DEPTH = 2
DEC_BATCH = 128
DEC_SEQ = 4
PAST_LEN = 2048
PAGE_SIZE = 128

FOX_HEADS = 4
FOX_DIM = 64
DIFF_HEADS = 4
DIFF_QK_DIM = 32
DIFF_V_DIM = 64
DELTA_HEADS = 4
DELTA_K_DIM = 64
DELTA_V_DIM = 64
CONV_WIDTH = 4
DELTA_CHUNK = 64
N_BRANCH = 3
BRANCH_WIDTH = FOX_HEADS * FOX_DIM
D_FF = 2816
Q_BLOCK = 128
ROPE_THETA = 10000.0
NORM_EPS = 1e-6

FOX_W = FOX_HEADS * FOX_DIM
DIFF_QK_W = DIFF_HEADS * 2 * DIFF_QK_DIM
DIFF_V_W = DIFF_HEADS * DIFF_V_DIM
DELTA_QK_W = DELTA_HEADS * DELTA_K_DIM
DELTA_V_W = DELTA_HEADS * DELTA_V_DIM
CONV_CH = 2 * DELTA_QK_W + DELTA_V_W
IN_SPLITS = (FOX_W, FOX_W, FOX_W, FOX_HEADS,
             DIFF_QK_W, DIFF_QK_W, DIFF_V_W,
             CONV_CH, DELTA_HEADS, DELTA_HEADS, DELTA_V_W,
             N_BRANCH * D_MODEL)
D_IN = sum(IN_SPLITS)

kernel_name = 'hybrid_fox_diff_gdn_macaron_step'


def rmsnorm(x, gain):
    x32 = x.astype(jnp.float32)
    y = x32 * lax.rsqrt(jnp.mean(x32 * x32, axis=-1, keepdims=True) + NORM_EPS)
    return (y * gain.astype(jnp.float32)).astype(x.dtype)


def l2norm(x):
    return x * lax.rsqrt(jnp.sum(x * x, axis=-1, keepdims=True) + 1e-6)


def swiglu(h, wi, wo):
    gate, up = jnp.split(h @ wi, 2, axis=-1)
    return (jax.nn.silu(gate) * up) @ wo


def rope(x, pos):
    d = x.shape[-1]
    inv = ROPE_THETA ** (-jnp.arange(0, d, 2, dtype=jnp.float32) / d)
    ang = pos.astype(jnp.float32)[:, None] * inv
    shape = (ang.shape[0],) + (1,) * (x.ndim - 3) + (d // 2,)
    cos = jnp.cos(ang).reshape(shape)
    sin = jnp.sin(ang).reshape(shape)
    x1, x2 = jnp.split(x.astype(jnp.float32), 2, axis=-1)
    return jnp.concatenate([x1 * cos - x2 * sin, x2 * cos + x1 * sin], axis=-1).astype(x.dtype)


def causal_softmax(scores, q_pos, k_pos):
    mask = k_pos[None, :] <= q_pos[:, None]
    return jax.nn.softmax(jnp.where(mask, scores, -jnp.inf), axis=-1)


def fox_core(q, cq, k, v, ck, q_pos, k_pos):
    s = jnp.einsum('nqhd,nkhd->nhqk', q, k).astype(jnp.float32) * (FOX_DIM ** -0.5)
    s = s + jnp.swapaxes(cq, 1, 2)[..., :, None] - jnp.swapaxes(ck, 1, 2)[..., None, :]
    p = causal_softmax(s, q_pos, k_pos)
    return jnp.einsum('nhqk,nkhd->nqhd', p.astype(v.dtype), v)


def diff_core(q, k, v, lam, q_pos, k_pos):
    s = jnp.einsum('nqhcd,nkhcd->nhcqk', q, k).astype(jnp.float32) * (DIFF_QK_DIM ** -0.5)
    p = causal_softmax(s, q_pos, k_pos)
    p = p[:, :, 0] - lam * p[:, :, 1]
    return jnp.einsum('nhqk,nkhd->nqhd', p.astype(v.dtype), v)


def sweep_query_blocks(core, q_args, kv_args, seq_len):
    nb = seq_len // Q_BLOCK
    pos = jnp.arange(seq_len, dtype=jnp.int32)

    def to_blocks(a):
        return jnp.moveaxis(a.reshape((a.shape[0], nb, Q_BLOCK) + a.shape[2:]), 1, 0)

    xs = tuple(to_blocks(a) for a in q_args) + (pos.reshape(nb, Q_BLOCK),)
    out = lax.map(lambda blk: core(*blk[:-1], *kv_args, blk[-1], pos), xs)
    out = jnp.moveaxis(out, 0, 1)
    return out.reshape((out.shape[0], seq_len) + out.shape[3:])


def short_conv(xin, buf, w):
    L = xin.shape[1]
    xp = jnp.concatenate([buf.astype(xin.dtype), xin], axis=1)
    y = xp[:, 0:L] * w[0]
    for i in range(1, CONV_WIDTH):
        y = y + xp[:, i:i + L] * w[i]
    return jax.nn.silu(y), xp[:, -(CONV_WIDTH - 1):]


def gated_delta_chunked(q, k, v, beta, g, s0, chunk):
    n, L = q.shape[:2]
    nc = L // chunk

    def blocks(a):
        a = a.reshape((n, nc, chunk) + a.shape[2:])
        return jnp.moveaxis(a, (1, 3), (0, 2))

    qc, kc, vc, bc, gc = (blocks(a) for a in (q, k, v, beta, g))
    G = jnp.cumsum(gc, axis=-1)
    tri = jnp.tril(jnp.ones((chunk, chunk), bool))
    tri_strict = jnp.tril(jnp.ones((chunk, chunk), bool), -1)
    decay = jnp.exp(jnp.where(tri, G[..., :, None] - G[..., None, :], -jnp.inf))
    kb = kc * bc[..., None]
    A = jnp.where(tri_strict, jnp.einsum('...id,...jd->...ij', kb, kc) * decay, 0.0)
    eye = jnp.eye(chunk, dtype=A.dtype)
    T = lax.linalg.triangular_solve(A + eye, jnp.broadcast_to(eye, A.shape), left_side=True, lower=True)
    u_base = jnp.einsum('...ij,...jd->...id', T, vc * bc[..., None])
    w = jnp.einsum('...ij,...jd->...id', T, kb * jnp.exp(G)[..., None])
    qk = jnp.einsum('...id,...jd->...ij', qc, kc) * decay

    def step(S, xs):
        qi, ki, ui, wi, Gi, qki = xs
        u = ui - jnp.einsum('nhcd,nhdv->nhcv', wi, S)
        o = jnp.einsum('nhcd,nhdv->nhcv', qi * jnp.exp(Gi)[..., None], S) + jnp.einsum('nhij,nhjv->nhiv', qki, u)
        k_dec = ki * jnp.exp(Gi[..., -1:] - Gi)[..., None]
        S = S * jnp.exp(Gi[..., -1])[..., None, None] + jnp.einsum('nhcd,nhcv->nhdv', k_dec, u)
        return S, o

    S, o = lax.scan(step, s0, (qc, kc, u_base, w, G, qk))
    o = jnp.moveaxis(o, (0, 2), (1, 3))
    return o.reshape((n, L) + o.shape[3:]), S


def split_columns(p):
    out, start = [], 0
    for size in IN_SPLITS:
        out.append(p[..., start:start + size])
        start += size
    return out


def token_mixers(h, l, W, past, pos):
    f32 = jnp.float32
    n, L, _ = h.shape
    fq, fk, fv, ff, dq, dk, dv, cqkv, cb, ca, cz, gates = split_columns(h @ W['w_in'][l])
    fq = fq.reshape(n, L, FOX_HEADS, FOX_DIM)
    fk = fk.reshape(n, L, FOX_HEADS, FOX_DIM)
    fv = fv.reshape(n, L, FOX_HEADS, FOX_DIM)
    logf = jax.nn.log_sigmoid(ff.astype(f32) + W['fox_f_bias'][l].astype(f32))
    dq = rope(dq.reshape(n, L, DIFF_HEADS, 2, DIFF_QK_DIM), pos)
    dk = rope(dk.reshape(n, L, DIFF_HEADS, 2, DIFF_QK_DIM), pos)
    dv = dv.reshape(n, L, DIFF_HEADS, DIFF_V_DIM)
    lam_p = W['diff_lambda'][l].astype(f32)
    lam_init = 0.8 - 0.6 * math.exp(-0.3 * l)
    lam = jnp.exp(jnp.sum(lam_p[0] * lam_p[1])) - jnp.exp(jnp.sum(lam_p[2] * lam_p[3])) + lam_init
    conv_buf = jnp.zeros((n, CONV_WIDTH - 1, CONV_CH), h.dtype) if past is None else past['conv']
    cqkv, new_conv = short_conv(cqkv, conv_buf, W['delta_conv_w'][l])

    if past is None:
        c = jnp.cumsum(logf, axis=1)
        o_fox = sweep_query_blocks(fox_core, (fq, c), (fk, fv, c), L)
        o_diff = sweep_query_blocks(diff_core, (dq,), (dk, dv, lam), L)
        s0 = jnp.zeros((n, DELTA_HEADS, DELTA_K_DIM, DELTA_V_DIM), f32)
    else:
        p_len = past['fox_k'].shape[1]
        k_pos = jnp.arange(p_len + L, dtype=jnp.int32)
        c = jnp.cumsum(jnp.concatenate([past['fox_logf'].astype(f32), logf], axis=1), axis=1)
        fk_all = jnp.concatenate([past['fox_k'].astype(fk.dtype), fk], axis=1)
        fv_all = jnp.concatenate([past['fox_v'].astype(fv.dtype), fv], axis=1)
        dk_all = jnp.concatenate([past['diff_k'].astype(dk.dtype), dk], axis=1)
        dv_all = jnp.concatenate([past['diff_v'].astype(dv.dtype), dv], axis=1)
        o_fox = fox_core(fq, c[:, p_len:], fk_all, fv_all, c, pos, k_pos)
        o_diff = diff_core(dq, dk_all, dv_all, lam, pos, k_pos)
        s0 = past['delta'].astype(f32)

    o_diff = rmsnorm(o_diff, W['diff_norm'][l]) * (1.0 - lam_init)

    cq_, ck_, cv_ = jnp.split(cqkv.astype(f32), [DELTA_QK_W, 2 * DELTA_QK_W], axis=-1)
    cq_ = l2norm(cq_.reshape(n, L, DELTA_HEADS, DELTA_K_DIM)) * (DELTA_K_DIM ** -0.5)
    ck_ = l2norm(ck_.reshape(n, L, DELTA_HEADS, DELTA_K_DIM))
    cv_ = cv_.reshape(n, L, DELTA_HEADS, DELTA_V_DIM)
    beta = jax.nn.sigmoid(cb.astype(f32))
    g = -jnp.exp(W['delta_A_log'][l].astype(f32)) * jax.nn.softplus(ca.astype(f32) + W['delta_dt_bias'][l].astype(f32))
    chunk = DELTA_CHUNK if L % DELTA_CHUNK == 0 else L
    o_delta, s_new = gated_delta_chunked(cq_, ck_, cv_, beta, g, s0, chunk)
    o_delta = rmsnorm(o_delta, W['delta_norm'][l]) * jax.nn.silu(cz.astype(f32).reshape(n, L, DELTA_HEADS, DELTA_V_DIM))

    branches = jnp.stack([o_fox.reshape(n, L, BRANCH_WIDTH).astype(h.dtype),
                          o_diff.reshape(n, L, BRANCH_WIDTH).astype(h.dtype),
                          o_delta.reshape(n, L, BRANCH_WIDTH).astype(h.dtype)], axis=2)
    lifted = jnp.einsum('nlbw,bwd->nlbd', branches, W['w_branch'][l])
    gate = jax.nn.sigmoid(gates.reshape(n, L, N_BRANCH, D_MODEL))
    y = jnp.sum(gate * lifted, axis=2) @ W['w_out'][l]
    new = {'fox_k': fk, 'fox_v': fv, 'fox_logf': logf, 'diff_k': dk, 'diff_v': dv,
           'delta': s_new, 'conv': new_conv}
    return y, new


def layer(x, l, W, past, pos):
    x = x + 0.5 * swiglu(rmsnorm(x, W['norm_ffn1'][l]), W['ffn1_wi'][l], W['ffn1_wo'][l])
    y, new = token_mixers(rmsnorm(x, W['norm_mix'][l]), l, W, past, pos)
    x = x + y
    x = x + 0.5 * swiglu(rmsnorm(x, W['norm_ffn2'][l]), W['ffn2_wi'][l], W['ffn2_wo'][l])
    return x, new


def setup_inputs(seed: int = 0) -> dict:
    key = jax.random.key(seed)
    keys = jax.random.split(key, 32)
    f32 = jnp.float32

    def nrm(i, shape, scale=1.0):
        return scale * jax.random.normal(keys[i], shape, f32)

    n_pages = PAST_LEN // PAGE_SIZE
    n_pool = (5 * DEC_BATCH * n_pages) // 4
    perm = jax.random.permutation(keys[9], n_pool).astype(jnp.int32)
    page_table = perm[:DEC_BATCH * n_pages].reshape(DEC_BATCH, n_pages)

    dt = jnp.exp(jax.random.uniform(keys[20], (DEPTH, DELTA_HEADS), f32, math.log(1e-3), math.log(1e-1)))
    return {
        'x_prompt': nrm(0, (BATCH, SEQ, D_MODEL)),
        'x_sample': nrm(1, (DEC_BATCH, DEC_SEQ, D_MODEL)),
        'cache_fox_k': nrm(2, (DEPTH, n_pool, PAGE_SIZE, FOX_HEADS, FOX_DIM)),
        'cache_fox_v': nrm(3, (DEPTH, n_pool, PAGE_SIZE, FOX_HEADS, FOX_DIM)),
        'cache_fox_logf': jax.nn.log_sigmoid(2.0 + nrm(4, (DEPTH, n_pool, PAGE_SIZE, FOX_HEADS))),
        'cache_diff_k': nrm(5, (DEPTH, n_pool, PAGE_SIZE, DIFF_HEADS, 2, DIFF_QK_DIM)),
        'cache_diff_v': nrm(6, (DEPTH, n_pool, PAGE_SIZE, DIFF_HEADS, DIFF_V_DIM)),
        'state_delta': nrm(7, (DEPTH, DEC_BATCH, DELTA_HEADS, DELTA_K_DIM, DELTA_V_DIM), 0.1),
        'state_conv': nrm(8, (DEPTH, DEC_BATCH, CONV_WIDTH - 1, CONV_CH)),
        'page_table': page_table,
        'norm_ffn1': 1.0 + nrm(10, (DEPTH, D_MODEL), 0.1),
        'ffn1_wi': nrm(11, (DEPTH, D_MODEL, 2 * D_FF), D_MODEL ** -0.5),
        'ffn1_wo': nrm(12, (DEPTH, D_FF, D_MODEL), D_FF ** -0.5),
        'norm_mix': 1.0 + nrm(13, (DEPTH, D_MODEL), 0.1),
        'w_in': nrm(14, (DEPTH, D_MODEL, D_IN), D_MODEL ** -0.5),
        'fox_f_bias': 2.0 + nrm(15, (DEPTH, FOX_HEADS), 0.1),
        'diff_lambda': nrm(16, (DEPTH, 4, DIFF_QK_DIM), 0.1),
        'diff_norm': 1.0 + nrm(17, (DEPTH, DIFF_V_DIM), 0.1),
        'delta_conv_w': nrm(18, (DEPTH, CONV_WIDTH, CONV_CH), 0.5),
        'delta_A_log': jnp.log(jax.random.uniform(keys[19], (DEPTH, DELTA_HEADS), f32, 1.0, 16.0)),
        'delta_dt_bias': dt + jnp.log(-jnp.expm1(-dt)),
        'delta_norm': 1.0 + nrm(21, (DEPTH, DELTA_V_DIM), 0.1),
        'w_branch': nrm(22, (DEPTH, N_BRANCH, BRANCH_WIDTH, D_MODEL), BRANCH_WIDTH ** -0.5),
        'w_out': nrm(23, (DEPTH, D_MODEL, D_MODEL), D_MODEL ** -0.5),
        'norm_ffn2': 1.0 + nrm(24, (DEPTH, D_MODEL), 0.1),
        'ffn2_wi': nrm(25, (DEPTH, D_MODEL, 2 * D_FF), D_MODEL ** -0.5),
        'ffn2_wo': nrm(26, (DEPTH, D_FF, D_MODEL), D_FF ** -0.5),
        'norm_final': 1.0 + nrm(27, (D_MODEL,), 0.1),
    }


def reference(x_prompt, x_sample, cache_fox_k, cache_fox_v, cache_fox_logf, cache_diff_k, cache_diff_v,
              state_delta, state_conv, page_table, norm_ffn1, ffn1_wi, ffn1_wo, norm_mix, w_in, fox_f_bias,
              diff_lambda, diff_norm, delta_conv_w, delta_A_log, delta_dt_bias, delta_norm, w_branch, w_out,
              norm_ffn2, ffn2_wi, ffn2_wo, norm_final):
    W = {'norm_ffn1': norm_ffn1, 'ffn1_wi': ffn1_wi, 'ffn1_wo': ffn1_wo, 'norm_mix': norm_mix, 'w_in': w_in,
         'fox_f_bias': fox_f_bias, 'diff_lambda': diff_lambda, 'diff_norm': diff_norm,
         'delta_conv_w': delta_conv_w, 'delta_A_log': delta_A_log, 'delta_dt_bias': delta_dt_bias,
         'delta_norm': delta_norm, 'w_branch': w_branch, 'w_out': w_out, 'norm_ffn2': norm_ffn2,
         'ffn2_wi': ffn2_wi, 'ffn2_wo': ffn2_wo}
    db, n_pages = page_table.shape
    past_len = n_pages * PAGE_SIZE
    pos_prompt = jnp.arange(x_prompt.shape[1], dtype=jnp.int32)
    pos_sample = past_len + jnp.arange(x_sample.shape[1], dtype=jnp.int32)

    def gather(cache):
        rows = cache[page_table]
        return rows.reshape((db, past_len) + rows.shape[3:])

    xp, xs = x_prompt, x_sample
    new_p, new_s = [], []
    for l in range(DEPTH):
        xp, new_pl = layer(xp, l, W, None, pos_prompt)
        past = {'fox_k': gather(cache_fox_k[l]), 'fox_v': gather(cache_fox_v[l]),
                'fox_logf': gather(cache_fox_logf[l]), 'diff_k': gather(cache_diff_k[l]),
                'diff_v': gather(cache_diff_v[l]), 'delta': state_delta[l], 'conv': state_conv[l]}
        xs, new_sl = layer(xs, l, W, past, pos_sample)
        new_p.append(new_pl)
        new_s.append(new_sl)
    y_prompt = rmsnorm(xp, norm_final)
    y_sample = rmsnorm(xs, norm_final)

    def stk(lst, name):
        return jnp.stack([d[name] for d in lst])

    return (y_prompt, y_sample,
            stk(new_p, 'fox_k'), stk(new_s, 'fox_k'),
            stk(new_p, 'fox_v'), stk(new_s, 'fox_v'),
            stk(new_p, 'fox_logf'), stk(new_s, 'fox_logf'),
            stk(new_p, 'diff_k'), stk(new_s, 'diff_k'),
            stk(new_p, 'diff_v'), stk(new_s, 'diff_v'),
            stk(new_p, 'delta'), stk(new_s, 'delta'),
            stk(new_p, 'conv'), stk(new_s, 'conv'))
```

```python
import functools
import math

import numpy as np
import jax
import jax.numpy as jnp
from jax import lax
from jax.experimental import pallas as pl
from jax.experimental.pallas import tpu as pltpu

F32 = jnp.float32
BF16 = jnp.bfloat16

FOX_HEADS = 4
FOX_DIM = 64
DIFF_HEADS = 4
DIFF_QK_DIM = 32
DIFF_V_DIM = 64
DELTA_HEADS = 4
DELTA_K_DIM = 64
DELTA_V_DIM = 64
CONV_WIDTH = 4
DELTA_CHUNK = 64
N_BRANCH = 3
HEAD_W = 256
PAGE_SIZE = 128
ROPE_THETA = 10000.0
NORM_EPS = 1e-6
LOG2E = 1.4426950408889634
FOX_QSCALE = FOX_DIM ** -0.5 * LOG2E
DIFF_QSCALE = DIFF_QK_DIM ** -0.5 * LOG2E

V7X_VMEM_LIMIT_BYTES = 56 * 1024 * 1024
AUG_C = 64
AUG_K = 67
ONES_LANE = 64


def _cparams(*sem):
    return pltpu.CompilerParams(dimension_semantics=sem, vmem_limit_bytes=V7X_VMEM_LIMIT_BYTES)


def _dot(a, b):
    return jnp.dot(a, b, preferred_element_type=F32)


def _dot_nt(a, b):
    return lax.dot_general(a, b, (((1,), (1,)), ((), ())), preferred_element_type=F32)


def _dot_tn(a, b):
    return lax.dot_general(a, b, (((0,), (0,)), ((), ())), preferred_element_type=F32)


def _split3(x):
    hi = x.astype(BF16)
    r = x - hi.astype(F32)
    mid = r.astype(BF16)
    lo = (r - mid.astype(F32)).astype(BF16)
    return hi, mid, lo


def _dot3_r(x, m):
    hi, mid, lo = _split3(x)
    return _dot(hi, m) + _dot(mid, m) + _dot(lo, m)


def _dot3_l(m, x):
    hi, mid, lo = _split3(x)
    return _dot(m, hi) + _dot(m, mid) + _dot(m, lo)


def _rms(x, gain):
    ms = jnp.mean(x * x, axis=-1, keepdims=True)
    return x * lax.rsqrt(ms + NORM_EPS) * gain


def _const_spec(shape):
    nd = len(shape)
    return pl.BlockSpec(shape, lambda *_: (0,) * nd, pipeline_mode=pl.Buffered(1))


def _ffn_body(x_ref, g_ref, wg_ref, wu_ref, wo_ref, gf_ref, o_ref, h_scr, acc_scr, *, final_norm):
    f = pl.program_id(1)

    @pl.when(f == 0)
    def _():
        h_scr[...] = _rms(x_ref[...], g_ref[...]).astype(BF16)
        acc_scr[...] = jnp.zeros_like(acc_scr)

    h = h_scr[...]
    gate = _dot(h, wg_ref[...])
    up = _dot(h, wu_ref[...])
    act = (gate * jax.nn.sigmoid(gate) * up).astype(BF16)
    acc_scr[...] += _dot(act, wo_ref[...])

    @pl.when(f == pl.num_programs(1) - 1)
    def _():
        y = x_ref[...] + 0.5 * acc_scr[...]
        if final_norm:
            y = _rms(y, gf_ref[...])
        o_ref[...] = y


def _ffn(x, gain, wi, wo, layer, final_gain=None):
    r, d = x.shape
    dff = wo.shape[1]
    tm = min(r, 512)
    tf = dff // 2 if (dff // 2) % 128 == 0 else dff
    nf = dff // tf
    final_norm = final_gain is not None
    gf = final_gain if final_norm else gain
    return pl.pallas_call(
        functools.partial(_ffn_body, final_norm=final_norm),
        out_shape=jax.ShapeDtypeStruct((r, d), F32),
        grid=(r // tm, nf),
        in_specs=[
            pl.BlockSpec((tm, d), lambda i, f: (i, 0)),
            pl.BlockSpec((1, d), lambda i, f: (0, 0)),
            pl.BlockSpec((None, d, tf), lambda i, f: (layer, 0, f)),
            pl.BlockSpec((None, d, tf), lambda i, f: (layer, 0, nf + f)),
            pl.BlockSpec((None, tf, d), lambda i, f: (layer, f, 0)),
            pl.BlockSpec((1, d), lambda i, f: (0, 0)),
        ],
        out_specs=pl.BlockSpec((tm, d), lambda i, f: (i, 0)),
        scratch_shapes=[pltpu.VMEM((tm, d), BF16), pltpu.VMEM((tm, d), F32)],
        compiler_params=_cparams("arbitrary", "arbitrary"),
        name="ffn",
    )(x, gain.reshape(1, d), wi, wi, wo, gf.reshape(1, d))


def _rope_body(inv_ref, cos_ref, sin_ref, *, tm, base, period):
    i = pl.program_id(0)
    row = lax.broadcasted_iota(jnp.int32, (tm, HEAD_W), 0) + i * tm
    lane = lax.broadcasted_iota(jnp.int32, (tm, HEAD_W), 1)
    pos = base + (row & (period - 1))
    ang = pos.astype(F32) * inv_ref[...]
    cos_ref[...] = jnp.cos(ang)
    s = jnp.sin(ang)
    first_half = (lane & (DIFF_QK_DIM - 1)) < DIFF_QK_DIM // 2
    sin_ref[...] = jnp.where(first_half, -s, s)


def _rope_tables(rows, base, period):
    assert period & (period - 1) == 0
    half = DIFF_QK_DIM // 2
    inv = ROPE_THETA ** (-jnp.arange(0, DIFF_QK_DIM, 2, dtype=F32) / DIFF_QK_DIM)
    inv_lanes = jnp.tile(inv, HEAD_W // half).reshape(1, HEAD_W)
    tm = min(rows, 512)
    return pl.pallas_call(
        functools.partial(_rope_body, tm=tm, base=base, period=period),
        out_shape=(jax.ShapeDtypeStruct((rows, HEAD_W), F32),) * 2,
        grid=(rows // tm,),
        in_specs=[pl.BlockSpec((1, HEAD_W), lambda i: (0, 0))],
        out_specs=(pl.BlockSpec((tm, HEAD_W), lambda i: (i, 0)),) * 2,
        compiler_params=_cparams("arbitrary"),
        name="rope_tables",
    )(inv_lanes)


_PROMPT_COLS = (("fq_pad", 512), ("fk_pad", 512), ("fv_pad", 512), ("fk", 256), ("fv", 256),
                ("dq", 256), ("dqs", 256), ("dk", 256), ("dks", 256), ("dv", 256), ("dv_pad", 512),
                ("cqkv", 768), ("cz", 256), ("cb_rep", 256), ("ca_rep", 256), ("small", 128))
_SAMPLE_COLS = (("fq", 256), ("fk", 256), ("fv", 256),
                ("dq", 256), ("dqs", 256), ("dk", 256), ("dks", 256), ("dv", 256),
                ("cqkv", 768), ("cz", 256), ("cb_rep", 256), ("ca_rep", 256), ("small", 128))


def _layout(cols):
    lay, off = {}, 0
    for name, width in cols:
        lay[name] = (off, width)
        off += width
    return lay, off


def _relayout_w_in(w_in, cols):
    depth, d, _ = w_in.shape
    sizes = (256, 256, 256, 4, 256, 256, 256, 768, 4, 4, 256)
    offs = np.concatenate([[0], np.cumsum(sizes)])
    fq, fk, fv, ff, dq, dk, dv, cqkv, cb, ca, cz = (w_in[:, :, int(offs[i]):int(offs[i + 1])] for i in range(11))

    def pad_heads(w):
        w = w.reshape(depth, d, 4, 64)
        return jnp.pad(w, ((0, 0), (0, 0), (0, 0), (0, 64))).reshape(depth, d, 512)

    def swap_halves(w):
        w = w.reshape(depth, d, DIFF_HEADS * 2, 2, DIFF_QK_DIM // 2)
        return w[:, :, :, ::-1, :].reshape(depth, d, 256)

    def rep(w):
        return jnp.repeat(w, 64, axis=2)

    table = {
        "fq": fq, "fk": fk, "fv": fv, "fq_pad": pad_heads(fq), "fk_pad": pad_heads(fk), "fv_pad": pad_heads(fv),
        "dq": dq, "dqs": swap_halves(dq), "dk": dk, "dks": swap_halves(dk), "dv": dv, "dv_pad": pad_heads(dv),
        "cqkv": cqkv, "cz": cz, "cb_rep": rep(cb), "ca_rep": rep(ca),
        "small": jnp.pad(ff, ((0, 0), (0, 0), (0, 124))),
    }
    return jnp.concatenate([table[name] for name, _ in cols], axis=2).astype(BF16)


def _aug_place_mats():
    pq = np.zeros((384, 512), np.float32)
    pk = np.zeros((384, 512), np.float32)
    for piece in range(3):
        for h in range(FOX_HEADS):
            pq[piece * 128 + h, h * 128 + AUG_C + piece] = 1.0
            pk[piece * 128 + h, h * 128 + AUG_K + piece] = -1.0
    return jnp.asarray(pq, BF16), jnp.asarray(pk, BF16)


def _lane_ones(shape, lo, hi):
    lane = lax.broadcasted_iota(jnp.int32, shape, 1) & 127
    mid2 = lo + hi
    return jnp.where(jnp.abs(2 * lane - mid2) <= hi - lo, 1.0, 0.0).astype(F32)


def _inproj_body(*refs, lay, prompt, tiles_per_seq, tm):
    if prompt:
        (x_ref, g_ref, w_ref, cos_ref, sin_ref, fb_ref, alog_ref, dtb_ref, tri_ref, pq_ref, pk_ref,
         fq_o, fk_pad_o, fv_pad_o, fk_o, fv_o, small_o, dq_o, dk_o, dkb_o, dv_o, dv_pad_o,
         cqkv_o, cz_o, beta_o, g_o, carry_scr) = refs
    else:
        (x_ref, g_ref, w_ref, cos_ref, sin_ref, fb_ref, alog_ref, dtb_ref,
         fq_o, fk_o, fv_o, small_o, dq_o, dk_o, dv_o, cqkv_o, cz_o, beta_o, g_o) = refs

    h = _rms(x_ref[...], g_ref[...]).astype(BF16)

    def proj(name):
        off, width = lay[name]
        return _dot(h, w_ref[:, off:off + width])

    logf = jax.nn.log_sigmoid(proj("small") + fb_ref[...])
    small_o[...] = logf

    if prompt:
        i = pl.program_id(0)

        @pl.when(i % tiles_per_seq == 0)
        def _():
            carry_scr[...] = jnp.zeros_like(carry_scr)

        c = _dot3_l(tri_ref[...], logf) + carry_scr[...]
        carry_scr[...] = c[tm - 1:tm, :]
        c1, c2, c3 = _split3(c * LOG2E)
        c123 = jnp.concatenate([c1, c2, c3], axis=1)
        shape = (tm, 512)
        q = proj("fq_pad") * FOX_QSCALE + _dot(c123, pq_ref[...]) + _lane_ones(shape, AUG_K, AUG_K + 2)
        fq_o[...] = q.astype(BF16)
        k = proj("fk_pad") + _dot(c123, pk_ref[...]) + _lane_ones(shape, AUG_C, AUG_C + 2)
        fk_pad_o[...] = k.astype(BF16)
        fv_pad_o[...] = (proj("fv_pad") + _lane_ones(shape, ONES_LANE, ONES_LANE)).astype(BF16)
        dv_pad_o[...] = (proj("dv_pad") + _lane_ones(shape, ONES_LANE, ONES_LANE)).astype(BF16)
    else:
        fq_o[...] = proj("fq") * FOX_QSCALE
    fk_o[...] = proj("fk")
    fv_o[...] = proj("fv")

    cos = cos_ref[...]
    sin = sin_ref[...]
    dq_o[...] = ((proj("dq") * cos + proj("dqs") * sin) * DIFF_QSCALE).astype(dq_o.dtype)
    dk = proj("dk") * cos + proj("dks") * sin
    dk_o[...] = dk
    if prompt:
        dkb_o[...] = dk.astype(BF16)
    dv_o[...] = proj("dv")

    cqkv_o[...] = proj("cqkv")
    cz = proj("cz")
    cz_o[...] = cz * jax.nn.sigmoid(cz)
    beta_o[...] = jax.nn.sigmoid(proj("cb_rep"))
    g_o[...] = -jnp.exp(alog_ref[...]) * jax.nn.softplus(proj("ca_rep") + dtb_ref[...])


def _inproj(x, gain, w, layer, cos, sin, fbias, alog, dtb, *, prompt, seq_len):
    r, d = x.shape
    lay, nc = _layout(_PROMPT_COLS if prompt else _SAMPLE_COLS)
    assert w.shape[2] == nc
    tm = min(r, 512 if prompt else 256)
    tm = min(tm, seq_len) if prompt else tm
    fb = jnp.pad(fbias.astype(F32), (0, 124)).reshape(1, 128)
    alog_l = jnp.repeat(alog.astype(F32), 64).reshape(1, 256)
    dtb_l = jnp.repeat(dtb.astype(F32), 64).reshape(1, 256)

    def rows(width):
        return pl.BlockSpec((tm, width), lambda i: (i, 0))

    in_specs = [rows(d), _const_spec((1, d)),
                pl.BlockSpec((None, d, nc), lambda i: (layer, 0, 0), pipeline_mode=pl.Buffered(1)),
                rows(256), rows(256), _const_spec((1, 128)), _const_spec((1, 256)), _const_spec((1, 256))]
    args = [x, gain.reshape(1, d), w, cos, sin, fb, alog_l, dtb_l]
    if prompt:
        tri = jnp.asarray(np.tril(np.ones((tm, tm), np.float32)), BF16)
        pq, pk = _aug_place_mats()
        in_specs += [_const_spec((tm, tm)), _const_spec((384, 512)), _const_spec((384, 512))]
        args += [tri, pq, pk]
        outs = (("fq_pad", 512, BF16), ("fk_pad", 512, BF16), ("fv_pad", 512, BF16), ("fk", 256, F32),
                ("fv", 256, F32), ("small", 128, F32), ("dq", 256, BF16), ("dk", 256, F32), ("dkb", 256, BF16),
                ("dv", 256, F32), ("dv_pad", 512, BF16), ("cqkv", 768, F32), ("cz", 256, F32),
                ("beta", 256, F32), ("g", 256, F32))
        scratch = [pltpu.VMEM((1, 128), F32)]
    else:
        outs = (("fq", 256, F32), ("fk", 256, F32), ("fv", 256, F32), ("small", 128, F32), ("dq", 256, F32),
                ("dk", 256, F32), ("dv", 256, F32), ("cqkv", 768, F32), ("cz", 256, F32),
                ("beta", 256, F32), ("g", 256, F32))
        scratch = []
    res = pl.pallas_call(
        functools.partial(_inproj_body, lay=lay, prompt=prompt, tiles_per_seq=max(seq_len // tm, 1), tm=tm),
        out_shape=tuple(jax.ShapeDtypeStruct((r, wd), dt) for _, wd, dt in outs),
        grid=(r // tm,),
        in_specs=in_specs,
        out_specs=tuple(rows(wd) for _, wd, _ in outs),
        scratch_shapes=scratch,
        compiler_params=_cparams("arbitrary"),
        name="inproj_prompt" if prompt else "inproj_sample",
    )(*args)
    return {name: a for (name, _, _), a in zip(outs, res)}


def _tri_schedule(nq):
    qi = np.concatenate([np.full(i + 1, i, np.int32) for i in range(nq)])
    kj = np.concatenate([np.arange(i + 1, dtype=np.int32) for i in range(nq)])
    return jnp.asarray(qi), jnp.asarray(kj)


def _online_softmax_step(s, v, m_ref, acc_ref):
    m_old = m_ref[:, :1]
    m_new = jnp.maximum(m_old, jnp.max(s, axis=1, keepdims=True))
    alpha = jnp.exp2(m_old - m_new)
    p = jnp.exp2(s - m_new)
    acc_ref[...] = alpha * acc_ref[...] + _dot(p.astype(BF16), v)
    m_ref[...] = jnp.broadcast_to(m_new, m_ref.shape)


def _causal_mask(s):
    row = lax.broadcasted_iota(jnp.int32, s.shape, 0)
    col = lax.broadcasted_iota(jnp.int32, s.shape, 1)
    return jnp.where(row >= col, s, -jnp.inf)


def _fox_body(qi_ref, kj_ref, q_ref, k_ref, v_ref, o_ref, m_scr, acc_scr):
    t = pl.program_id(1)
    qi = qi_ref[t]
    kj = kj_ref[t]

    @pl.when(kj == 0)
    def _():
        m_scr[...] = jnp.full_like(m_scr, -jnp.inf)
        acc_scr[...] = jnp.zeros_like(acc_scr)

    def step(masked):
        for h in range(FOX_HEADS):
            sl = slice(h * 128, (h + 1) * 128)
            s = _dot_nt(q_ref[:, sl], k_ref[:, sl])
            if masked:
                s = _causal_mask(s)
            _online_softmax_step(s, v_ref[:, sl], m_scr.at[h], acc_scr.at[h])

    @pl.when(kj < qi)
    def _():
        step(False)

    @pl.when(kj == qi)
    def _():
        step(True)
        for h in range(FOX_HEADS):
            a = acc_scr[h]
            o = a[:, :FOX_DIM] / a[:, ONES_LANE:ONES_LANE + 1]
            o_ref[:, h * FOX_DIM:(h + 1) * FOX_DIM] = o.astype(o_ref.dtype)


def _fox_prompt(q_pad, k_pad, v_pad, n_seq, seq_len):
    tq = min(seq_len, 512)
    nq = seq_len // tq
    qi, kj = _tri_schedule(nq)
    gs = pltpu.PrefetchScalarGridSpec(
        num_scalar_prefetch=2,
        grid=(n_seq, int(qi.shape[0])),
        in_specs=[
            pl.BlockSpec((tq, 512), lambda n, t, qi, kj: (n * nq + qi[t], 0)),
            pl.BlockSpec((tq, 512), lambda n, t, qi, kj: (n * nq + kj[t], 0)),
            pl.BlockSpec((tq, 512), lambda n, t, qi, kj: (n * nq + kj[t], 0)),
        ],
        out_specs=pl.BlockSpec((tq, HEAD_W), lambda n, t, qi, kj: (n * nq + qi[t], 0)),
        scratch_shapes=[pltpu.VMEM((FOX_HEADS, tq, 128), F32), pltpu.VMEM((FOX_HEADS, tq, 128), F32)],
    )
    return pl.pallas_call(
        _fox_body,
        out_shape=jax.ShapeDtypeStruct((n_seq * seq_len, HEAD_W), BF16),
        grid_spec=gs,
        compiler_params=_cparams("arbitrary", "arbitrary"),
        name="fox_prompt",
    )(qi, kj, q_pad, k_pad, v_pad)


def _diff_lambda(lam_ref, lam_init):
    lp = lam_ref[...]
    s1 = jnp.sum(lp[0:1] * lp[1:2], axis=1, keepdims=True)
    s2 = jnp.sum(lp[2:3] * lp[3:4], axis=1, keepdims=True)
    return jnp.exp(s1) - jnp.exp(s2) + lam_init


def _diff_body(qi_ref, kj_ref, q_ref, k_ref, v_ref, lam_ref, gain_ref, o_ref, qm_scr, m_scr, acc_scr, *, lam_init):
    t = pl.program_id(1)
    qi = qi_ref[t]
    kj = kj_ref[t]
    n_maps = 2 * DIFF_HEADS

    @pl.when(kj == 0)
    def _():
        m_scr[...] = jnp.full_like(m_scr, -jnp.inf)
        acc_scr[...] = jnp.zeros_like(acc_scr)
        q = q_ref[...]
        lane_map = lax.broadcasted_iota(jnp.int32, q.shape, 1) // DIFF_QK_DIM
        for m in range(n_maps):
            qm_scr[m] = jnp.where(lane_map == m, q, jnp.zeros_like(q))

    def step(masked):
        k = k_ref[...]
        for m in range(n_maps):
            h = m // 2
            s = _dot_nt(qm_scr[m], k)
            if masked:
                s = _causal_mask(s)
            _online_softmax_step(s, v_ref[:, h * 128:(h + 1) * 128], m_scr.at[m], acc_scr.at[m])

    @pl.when(kj < qi)
    def _():
        step(False)

    @pl.when(kj == qi)
    def _():
        step(True)
        lam = _diff_lambda(lam_ref, lam_init)
        for h in range(DIFF_HEADS):
            a0 = acc_scr[2 * h]
            a1 = acc_scr[2 * h + 1]
            o = (a0[:, :DIFF_V_DIM] / a0[:, ONES_LANE:ONES_LANE + 1]
                 - lam * (a1[:, :DIFF_V_DIM] / a1[:, ONES_LANE:ONES_LANE + 1]))
            o = _rms(o, gain_ref[...]) * (1.0 - lam_init)
            o_ref[:, h * DIFF_V_DIM:(h + 1) * DIFF_V_DIM] = o.astype(o_ref.dtype)


def _diff_prompt(q, k, v_pad, lam_p, gain, lam_init, n_seq, seq_len):
    tq = min(seq_len, 512)
    nq = seq_len // tq
    qi, kj = _tri_schedule(nq)
    n_maps = 2 * DIFF_HEADS
    gs = pltpu.PrefetchScalarGridSpec(
        num_scalar_prefetch=2,
        grid=(n_seq, int(qi.shape[0])),
        in_specs=[
            pl.BlockSpec((tq, HEAD_W), lambda n, t, qi, kj: (n * nq + qi[t], 0)),
            pl.BlockSpec((tq, HEAD_W), lambda n, t, qi, kj: (n * nq + kj[t], 0)),
            pl.BlockSpec((tq, 512), lambda n, t, qi, kj: (n * nq + kj[t], 0)),
            pl.BlockSpec((4, DIFF_QK_DIM), lambda n, t, qi, kj: (0, 0)),
            pl.BlockSpec((1, DIFF_V_DIM), lambda n, t, qi, kj: (0, 0)),
        ],
        out_specs=pl.BlockSpec((tq, HEAD_W), lambda n, t, qi, kj: (n * nq + qi[t], 0)),
        scratch_shapes=[pltpu.VMEM((n_maps, tq, HEAD_W), BF16), pltpu.VMEM((n_maps, tq, 128), F32),
                        pltpu.VMEM((n_maps, tq, 128), F32)],
    )
    return pl.pallas_call(
        functools.partial(_diff_body, lam_init=lam_init),
        out_shape=jax.ShapeDtypeStruct((n_seq * seq_len, HEAD_W), BF16),
        grid_spec=gs,
        compiler_params=_cparams("arbitrary", "arbitrary"),
        name="diff_prompt",
    )(qi, kj, q, k, v_pad, lam_p.astype(F32), gain.astype(F32).reshape(1, DIFF_V_DIM))


def _merge_body(x_ref, g_ref, wg_ref, wb_ref, wo_ref, b0_ref, b1_ref, b2_ref, o_ref):
    x = x_ref[...]
    d = x.shape[1]
    h = _rms(x, g_ref[...]).astype(BF16)
    mix = None
    for b, b_ref in enumerate((b0_ref, b1_ref, b2_ref)):
        gate = jax.nn.sigmoid(_dot(h, wg_ref[:, b * d:(b + 1) * d]))
        term = gate * _dot(b_ref[...].astype(BF16), wb_ref[b])
        mix = term if mix is None else mix + term
    o_ref[...] = x + _dot(mix.astype(BF16), wo_ref[...])


def _merge(x, gain, w_gates, w_branch, w_out, layer, branches):
    r, d = x.shape
    tm = min(r, 512)

    def rows(width):
        return pl.BlockSpec((tm, width), lambda i: (i, 0))

    return pl.pallas_call(
        _merge_body,
        out_shape=jax.ShapeDtypeStruct((r, d), F32),
        grid=(r // tm,),
        in_specs=[rows(d), _const_spec((1, d)),
                  pl.BlockSpec((None, d, N_BRANCH * d), lambda i: (layer, 0, 0), pipeline_mode=pl.Buffered(1)),
                  pl.BlockSpec((None, N_BRANCH, HEAD_W, d), lambda i: (layer, 0, 0, 0), pipeline_mode=pl.Buffered(1)),
                  pl.BlockSpec((None, d, d), lambda i: (layer, 0, 0), pipeline_mode=pl.Buffered(1)),
                  rows(HEAD_W), rows(HEAD_W), rows(HEAD_W)],
        out_specs=rows(d),
        compiler_params=_cparams("arbitrary"),
        name="merge",
    )(x, gain.reshape(1, d), w_gates, w_branch, w_out, *branches)


def _delta_consts():
    c = DELTA_CHUNK
    i = np.arange(c)
    lane = np.arange(HEAD_W)
    lh, lj = lane // c, lane % c
    tril = i[:, None] >= i[None, :]
    suffix = i[None, :] > i[:, None]
    lvl = []
    for k in range(1, 7):
        b = 1 << k
        lvl.append((i[:, None] // b == lj[None, :] // b) & (i[:, None] // (b // 2) != lj[None, :] // (b // 2)))
    f = lambda a, dt: jnp.asarray(np.asarray(a, np.float32), dt)
    return dict(
        lhsg=f(np.concatenate([tril, suffix], axis=0), BF16),
        ones=f(np.ones((c, c)), BF16),
        triu4=f(i[:, None] <= lj[None, :], F32),
        tril4=f(i[:, None] >= lj[None, :], F32),
        stril4=f(i[:, None] > lj[None, :], F32),
        eye4=f(i[:, None] == lj[None, :], F32),
        lvl=f(np.stack(lvl), F32),
        maskbd=f(lh[:, None] == lh[None, :], BF16),
    )


def _bd(x4, maskbd):
    return jnp.concatenate([x4, x4, x4, x4], axis=0) * maskbd


def _mm_hi(x, y, maskbd):
    xh = x.astype(BF16)
    xl = (x - xh.astype(F32)).astype(BF16)
    yh = y.astype(BF16)
    yl = (y - yh.astype(F32)).astype(BF16)
    ybh = _bd(yh, maskbd)
    return _dot(xh, ybh) + _dot(xl, ybh) + _dot(xh, _bd(yl, maskbd))


def _unit_lower_inverse(a4, eye4, lvl_ref, maskbd):
    t = eye4 - a4 * lvl_ref[0]
    for k in range(1, 6):
        m = a4 * lvl_ref[k]
        x = _mm_hi(m, t, maskbd)
        t = t - _mm_hi(t, x, maskbd)
    return t


def _head_sums(x, maskbd):
    return _dot3_r(x, maskbd)


def _delta_body(cqkv_ref, beta_ref, g_ref, cz_ref, cw_ref, gain_ref,
                lhsg_ref, ones_ref, triu4_ref, tril4_ref, stril4_ref, eye4_ref, lvl_ref, maskbd_ref,
                o_ref, s_ref, xbuf, q_scr, k_scr, v_scr, o_scr, sbd_scr, *, tc):
    j = pl.program_id(1)
    c = DELTA_CHUNK
    maskbd = maskbd_ref[...]

    @pl.when(j == 0)
    def _():
        xbuf[0:8, :] = jnp.zeros((8, xbuf.shape[1]), F32)
        sbd_scr[...] = jnp.zeros_like(sbd_scr)

    xbuf[8:8 + tc, :] = cqkv_ref[...]
    cw = cw_ref[...]
    y = cw[3:4] * xbuf[8:8 + tc, :]
    for i in range(1, CONV_WIDTH):
        y = y + cw[3 - i:4 - i] * xbuf[8 - i:8 - i + tc, :]
    xbuf[0:8, :] = xbuf[tc:tc + 8, :]
    y = y * jax.nn.sigmoid(y)
    q = y[:, 0:HEAD_W]
    k = y[:, HEAD_W:2 * HEAD_W]
    q_scr[...] = q * lax.rsqrt(_head_sums(q * q, maskbd) + 1e-6) * (DELTA_K_DIM ** -0.5)
    k_scr[...] = k * lax.rsqrt(_head_sums(k * k, maskbd) + 1e-6)
    v_scr[...] = y[:, 2 * HEAD_W:3 * HEAD_W]

    def chunk(ci, carry):
        sl = pl.ds(pl.multiple_of(ci * c, c), c)
        qc, kc, vc = q_scr[sl, :], k_scr[sl, :], v_scr[sl, :]
        bc, gc = beta_ref[sl, :], g_ref[sl, :]
        gsum = _dot3_l(lhsg_ref[...], gc)
        gcol, gsuf = gsum[0:c], gsum[c:2 * c]
        grow = _dot3_l(ones_ref[...], gc * triu4_ref[...])
        decay = jnp.where(tril4_ref[...] > 0, jnp.exp(gcol - grow), 0.0)
        exp_g = jnp.exp(gcol)
        kb = kc * bc
        kbd = _bd(kc.astype(BF16), maskbd)
        a4 = _dot_nt(kb.astype(BF16), kbd) * decay * stril4_ref[...]
        qk4 = _dot_nt(qc.astype(BF16), kbd) * decay
        t16 = _unit_lower_inverse(a4, eye4_ref[...], lvl_ref, maskbd).astype(BF16)
        u_base = _dot(t16, _bd((vc * bc).astype(BF16), maskbd))
        w = _dot(t16, _bd((kb * exp_g).astype(BF16), maskbd))
        s_old = sbd_scr[...]
        s16 = s_old.astype(BF16)
        u = u_base - _dot(w.astype(BF16), s16)
        u16 = u.astype(BF16)
        o_scr[sl, :] = _dot((qc * exp_g).astype(BF16), s16) + _dot(qk4.astype(BF16), _bd(u16, maskbd))
        k_dec = (kc * jnp.exp(gsuf)).astype(BF16)
        sbd_scr[...] = s_old * exp_g[c - 1:c, :] + maskbd.astype(F32) * _dot_tn(k_dec, u16)
        return carry

    lax.fori_loop(0, tc // c, chunk, 0)

    o = o_scr[...]
    ms = _head_sums(o * o, maskbd) * (1.0 / DELTA_V_DIM)
    o_ref[...] = (o * lax.rsqrt(ms + NORM_EPS) * gain_ref[...] * cz_ref[...]).astype(o_ref.dtype)

    @pl.when(j == pl.num_programs(1) - 1)
    def _():
        s_ref[...] = sbd_scr[...]


def _delta_prompt(cqkv, beta, g, cz, conv_w, gain, n_seq, seq_len):
    tc = min(seq_len, 256)
    nt = seq_len // tc
    cch = cqkv.shape[1]
    consts = _delta_consts()
    names = ("lhsg", "ones", "triu4", "tril4", "stril4", "eye4", "lvl", "maskbd")

    def rows(width):
        return pl.BlockSpec((tc, width), lambda n, j: (n * nt + j, 0))

    return pl.pallas_call(
        functools.partial(_delta_body, tc=tc),
        out_shape=(jax.ShapeDtypeStruct((n_seq * seq_len, HEAD_W), BF16),
                   jax.ShapeDtypeStruct((n_seq, HEAD_W, HEAD_W), F32)),
        grid=(n_seq, nt),
        in_specs=[rows(cch), rows(HEAD_W), rows(HEAD_W), rows(HEAD_W),
                  _const_spec((CONV_WIDTH, cch)), _const_spec((1, HEAD_W))]
                 + [_const_spec(consts[nm].shape) for nm in names],
        out_specs=(rows(HEAD_W), pl.BlockSpec((None, HEAD_W, HEAD_W), lambda n, j: (n, 0, 0))),
        scratch_shapes=[pltpu.VMEM((tc + 8, cch), F32), pltpu.VMEM((tc, HEAD_W), F32),
                        pltpu.VMEM((tc, HEAD_W), F32), pltpu.VMEM((tc, HEAD_W), F32),
                        pltpu.VMEM((tc, HEAD_W), F32), pltpu.VMEM((HEAD_W, HEAD_W), F32)],
        compiler_params=_cparams("arbitrary", "arbitrary"),
        name="delta_prompt",
    )(cqkv, beta, g, cz, conv_w.astype(F32), jnp.tile(gain.astype(F32), DELTA_HEADS).reshape(1, HEAD_W),
      *[consts[nm] for nm in names])


SROWS = 8


def _sample_attn_consts(n_pages):
    r = np.arange(PAGE_SIZE)
    lane = np.arange(4 * PAGE_SIZE)
    m2 = (lane[:, None] // 4) > r[None, :]
    m2ones = np.concatenate([m2, np.ones((4 * PAGE_SIZE, PAGE_SIZE), bool)], axis=1)
    idx = np.arange(4 * n_pages)
    u2 = (idx[None, :] // 4 > idx[:, None] // 4) & (idx[None, :] % 4 == idx[:, None] % 4)
    t = np.arange(SROWS)
    tri8 = t[None, :] <= t[:, None]
    sup = t[:, None] > r[None, :]
    lh = np.arange(HEAD_W) // 64
    f = lambda a, dt: jnp.asarray(np.asarray(a, np.float32), dt)
    return dict(m2ones=f(m2ones, BF16), u2=f(u2, BF16), tri8=f(tri8, BF16), sup=f(sup, F32),
                maskbd=f(lh[:, None] == lh[None, :], BF16))


def _pad_rows(x, rows):
    return jnp.concatenate([x, jnp.zeros((rows - x.shape[0], x.shape[1]), x.dtype)], axis=0)


def _block_rows(x, groups, lanes_per_group):
    xr = jnp.concatenate([x] * groups, axis=0)
    rg = lax.broadcasted_iota(jnp.int32, xr.shape, 0) // SROWS
    lg = lax.broadcasted_iota(jnp.int32, xr.shape, 1) // lanes_per_group
    return jnp.where(rg == lg, xr, 0.0).astype(BF16)


def _paged_softmax_pv(qbd, k_pages, v_pages, k_new, v_new, bias_pages, bias_new):
    tiles = []
    for p, k_ref in enumerate(k_pages):
        s = _dot_nt(qbd, k_ref[...].astype(BF16))
        tiles.append(s if bias_pages is None else s + bias_pages[p])
    s = _dot_nt(qbd, _pad_rows(k_new, PAGE_SIZE).astype(BF16))
    if bias_new is not None:
        s = s + bias_new
    row = lax.broadcasted_iota(jnp.int32, s.shape, 0) & (SROWS - 1)
    col = lax.broadcasted_iota(jnp.int32, s.shape, 1)
    tiles.append(jnp.where(col <= row, s, -jnp.inf))
    m = tiles[0].max(axis=1, keepdims=True)
    for s in tiles[1:]:
        m = jnp.maximum(m, s.max(axis=1, keepdims=True))
    acc = None
    den = None
    values = [v_ref[...].astype(BF16) for v_ref in v_pages] + [_pad_rows(v_new, PAGE_SIZE).astype(BF16)]
    for s, v in zip(tiles, values):
        p = jnp.exp2(s - m)
        ps = p.sum(axis=1, keepdims=True)
        pv = _dot(p.astype(BF16), v)
        acc = pv if acc is None else acc + pv
        den = ps if den is None else den + ps
    return acc / den


def _own_head_rows(x, stride):
    lh = lax.broadcasted_iota(jnp.int32, (SROWS, HEAD_W), 1) // 64
    out = None
    for h in range(4):
        part = jnp.where(lh == h, x[h * stride:h * stride + SROWS], 0.0)
        out = part if out is None else out + part
    return out


def _sample_attn_body(pt_ref, *refs, n_pages, lam_init):
    del pt_ref
    groups = [refs[i * n_pages:(i + 1) * n_pages] for i in range(5)]
    fk_pages, fv_pages, lf_pages, dk_pages, dv_pages = groups
    (fq_ref, fkn_ref, fvn_ref, ln_ref, dq_ref, dkn_ref, dvn_ref,
     m2_ref, u2_ref, tri8_ref, sup_ref, lam_ref, gain_ref, maskbd_ref,
     of_ref, od_ref, xm_scr) = refs[5 * n_pages:]

    lane4 = lax.broadcasted_iota(jnp.int32, (4, 4 * PAGE_SIZE), 1) & 3
    row4 = lax.broadcasted_iota(jnp.int32, (4, 4 * PAGE_SIZE), 0)
    for p in range(n_pages):
        x = jnp.broadcast_to(lf_pages[p][...], (4, 4 * PAGE_SIZE))
        xm_scr[4 * p:4 * p + 4, :] = jnp.where(lane4 == row4, x, 0.0)
    wt = _dot3_r(xm_scr[...], m2_ref[...])
    within, totals = wt[:, :PAGE_SIZE], wt[:, PAGE_SIZE:]
    d_past = (within + _dot3_l(u2_ref[...], totals)) * LOG2E
    lnew = ln_ref[...]
    tri8 = tri8_ref[...]
    cn = _dot3_l(tri8, lnew) * LOG2E
    cn_h = [jnp.broadcast_to(cn[:, h:h + 1], (SROWS, PAGE_SIZE)) for h in range(FOX_HEADS)]
    bias_pages = [
        jnp.concatenate([d_past[4 * p + h:4 * p + h + 1, :] + cn_h[h] for h in range(FOX_HEADS)], axis=0)
        for p in range(n_pages)]
    sup = sup_ref[...]
    bias_new = jnp.concatenate(
        [_dot3_l(tri8, jnp.broadcast_to(lnew[:, h:h + 1], (SROWS, PAGE_SIZE)) * sup) * LOG2E
         for h in range(FOX_HEADS)], axis=0)
    qbd = _block_rows(fq_ref[...], FOX_HEADS, FOX_DIM)
    o = _paged_softmax_pv(qbd, fk_pages, fv_pages, fkn_ref[...], fvn_ref[...], bias_pages, bias_new)
    of_ref[...] = _own_head_rows(o, SROWS)

    qbd = _block_rows(dq_ref[...], 2 * DIFF_HEADS, DIFF_QK_DIM)
    o = _paged_softmax_pv(qbd, dk_pages, dv_pages, dkn_ref[...], dvn_ref[...], None, None)
    lam = _diff_lambda(lam_ref, lam_init)
    o = _own_head_rows(o, 2 * SROWS) - lam * _own_head_rows(o[SROWS:], 2 * SROWS)
    ms = _head_sums(o * o, maskbd_ref[...]) * (1.0 / DIFF_V_DIM)
    od_ref[...] = o * lax.rsqrt(ms + NORM_EPS) * gain_ref[...] * (1.0 - lam_init)


def _sample_attn(page_table, caches, layer, proj, lam_p, gain, lam_init):
    n_samples, n_pages = page_table.shape
    consts = _sample_attn_consts(n_pages)

    def page_spec(arr, p):
        return pl.BlockSpec((None, None) + arr.shape[2:], lambda b, pt: (layer, pt[b, p], 0, 0))

    def tile(width):
        return pl.BlockSpec((SROWS, width), lambda b, pt: (b, 0))

    def const(shape):
        nd = len(shape)
        return pl.BlockSpec(shape, lambda b, pt: (0,) * nd, pipeline_mode=pl.Buffered(1))

    in_specs, args = [], []
    for arr in caches:
        for p in range(n_pages):
            in_specs.append(page_spec(arr, p))
            args.append(arr)
    new = (proj["fq"], proj["fk"], proj["fv"], proj["small"], proj["dq"], proj["dk"], proj["dv"])
    in_specs += [tile(a.shape[1]) for a in new]
    args += list(new)
    cvals = (consts["m2ones"], consts["u2"], consts["tri8"], consts["sup"], lam_p.astype(F32),
             jnp.tile(gain.astype(F32), DIFF_HEADS).reshape(1, HEAD_W), consts["maskbd"])
    in_specs += [const(c.shape) for c in cvals]
    args += list(cvals)
    gs = pltpu.PrefetchScalarGridSpec(
        num_scalar_prefetch=1, grid=(n_samples,), in_specs=in_specs,
        out_specs=(tile(HEAD_W), tile(HEAD_W)),
        scratch_shapes=[pltpu.VMEM((4 * n_pages, 4 * PAGE_SIZE), F32)])
    return pl.pallas_call(
        functools.partial(_sample_attn_body, n_pages=n_pages, lam_init=lam_init),
        out_shape=(jax.ShapeDtypeStruct((n_samples * SROWS, HEAD_W), F32),) * 2,
        grid_spec=gs,
        compiler_params=_cparams("arbitrary"),
        name="sample_attn",
    )(page_table, *args)


def _delta_sample_body(cqkv_ref, buf_ref, beta_ref, g_ref, cz_ref, cw_ref, gain_ref, s0_ref,
                       tile_ref, tile_t_ref, maskbd_ref, o_ref, s_ref, xp_scr, pr_scr, u_scr, kd_scr, o_scr, *, n_tok):
    maskbd = maskbd_ref[...]
    cch = xp_scr.shape[1]
    xp_scr[0:8, :] = jnp.zeros((8, cch), F32)
    xp_scr[8 - (CONV_WIDTH - 1):8, :] = buf_ref[...]
    xp_scr[8:16, :] = cqkv_ref[...]
    cw = cw_ref[...]
    y = cw[3:4] * xp_scr[8:16, :]
    for i in range(1, CONV_WIDTH):
        y = y + cw[3 - i:4 - i] * xp_scr[8 - i:16 - i, :]
    y = y * jax.nn.sigmoid(y)
    q = y[:, 0:HEAD_W]
    k = y[:, HEAD_W:2 * HEAD_W]
    v = y[:, 2 * HEAD_W:3 * HEAD_W]
    q = q * lax.rsqrt(_head_sums(q * q, maskbd) + 1e-6) * (DELTA_K_DIM ** -0.5)
    k = k * lax.rsqrt(_head_sums(k * k, maskbd) + 1e-6)
    beta = beta_ref[...]
    a = jnp.exp(g_ref[...])

    maskf = maskbd.astype(F32)
    sbd0 = maskf * _dot3_r(s0_ref[...], tile_ref[...])
    s16 = sbd0.astype(BF16)
    rk = _dot(k.astype(BF16), s16)
    rq = _dot(q.astype(BF16), s16)

    pairs_k = [(t, j) for t in range(n_tok) for j in range(t)]
    pairs_q = [(t, j) for t in range(n_tok) for j in range(t + 1)]
    assert len(pairs_k) + len(pairs_q) <= pr_scr.shape[0]
    pr_scr[...] = jnp.zeros_like(pr_scr)
    for r, (t, j) in enumerate(pairs_k):
        pr_scr[r:r + 1, :] = k[t:t + 1] * k[j:j + 1]
    for r, (t, j) in enumerate(pairs_q):
        pr_scr[len(pairs_k) + r:len(pairs_k) + r + 1, :] = q[t:t + 1] * k[j:j + 1]
    dots = _head_sums(pr_scr[...], maskbd)
    kk = {tj: dots[r:r + 1] for r, tj in enumerate(pairs_k)}
    qk = {tj: dots[len(pairs_k) + r:len(pairs_k) + r + 1] for r, tj in enumerate(pairs_q)}

    rk_rows = [rk[t:t + 1] for t in range(n_tok)]
    rq_rows = [rq[t:t + 1] for t in range(n_tok)]
    u_scr[...] = jnp.zeros_like(u_scr)
    kd_scr[...] = jnp.zeros_like(kd_scr)
    o_scr[...] = jnp.zeros_like(o_scr)
    for t in range(n_tok):
        a_t = a[t:t + 1]
        u_t = beta[t:t + 1] * (v[t:t + 1] - a_t * rk_rows[t])
        for t2 in range(t + 1, n_tok):
            rk_rows[t2] = a_t * rk_rows[t2] + kk[(t2, t)] * u_t
        for t2 in range(t, n_tok):
            rq_rows[t2] = a_t * rq_rows[t2] + qk[(t2, t)] * u_t
        o_scr[t:t + 1, :] = rq_rows[t]
        u_scr[t:t + 1, :] = u_t
        tail = None
        for t2 in range(t + 1, n_tok):
            tail = a[t2:t2 + 1] if tail is None else tail * a[t2:t2 + 1]
        kd_scr[t:t + 1, :] = k[t:t + 1] if tail is None else k[t:t + 1] * tail
    a_all = a[0:1]
    for t in range(1, n_tok):
        a_all = a_all * a[t:t + 1]
    s_new = sbd0 * a_all + maskf * _dot_tn(kd_scr[...].astype(BF16), u_scr[...].astype(BF16))
    s_ref[...] = _dot3_r(s_new, tile_t_ref[...])

    o = o_scr[...]
    ms = _head_sums(o * o, maskbd) * (1.0 / DELTA_V_DIM)
    o_ref[...] = o * lax.rsqrt(ms + NORM_EPS) * gain_ref[...] * cz_ref[...]


def _delta_sample(proj, conv_state, state, layer, conv_w, gain, n_tok):
    n_samples = state.shape[1]
    cch = conv_state.shape[3]
    v_ = np.arange(DELTA_V_DIM)
    lane = np.arange(HEAD_W)
    tile_m = jnp.asarray((v_[:, None] == lane[None, :] % DELTA_V_DIM).astype(np.float32), BF16)
    tile_t = jnp.asarray((lane[:, None] % DELTA_V_DIM == v_[None, :]).astype(np.float32), BF16)
    lh = lane // 64
    maskbd = jnp.asarray((lh[:, None] == lh[None, :]).astype(np.float32), BF16)

    def tile(width):
        return pl.BlockSpec((SROWS, width), lambda b: (b, 0))

    return pl.pallas_call(
        functools.partial(_delta_sample_body, n_tok=n_tok),
        out_shape=(jax.ShapeDtypeStruct((n_samples * SROWS, HEAD_W), F32),
                   jax.ShapeDtypeStruct((n_samples, HEAD_W, DELTA_V_DIM), F32)),
        grid=(n_samples,),
        in_specs=[tile(cch),
                  pl.BlockSpec((None, None, CONV_WIDTH - 1, cch), lambda b: (layer, b, 0, 0)),
                  tile(HEAD_W), tile(HEAD_W), tile(HEAD_W),
                  _const_spec((CONV_WIDTH, cch)), _const_spec((1, HEAD_W)),
                  pl.BlockSpec((None, None, HEAD_W, DELTA_V_DIM), lambda b: (layer, b, 0, 0)),
                  _const_spec(tile_m.shape), _const_spec(tile_t.shape), _const_spec(maskbd.shape)],
        out_specs=(tile(HEAD_W), pl.BlockSpec((None, HEAD_W, DELTA_V_DIM), lambda b: (b, 0, 0))),
        scratch_shapes=[pltpu.VMEM((16, cch), F32), pltpu.VMEM((16, HEAD_W), F32), pltpu.VMEM((SROWS, HEAD_W), F32),
                        pltpu.VMEM((SROWS, HEAD_W), F32), pltpu.VMEM((SROWS, HEAD_W), F32)],
        compiler_params=_cparams("arbitrary"),
        name="delta_sample",
    )(proj["cqkv"], conv_state, proj["beta"], proj["g"], proj["cz"], conv_w.astype(F32),
      jnp.tile(gain.astype(F32), DELTA_HEADS).reshape(1, HEAD_W), state, tile_m, tile_t, maskbd)


def kernel(x_prompt, x_sample, cache_fox_k, cache_fox_v, cache_fox_logf, cache_diff_k, cache_diff_v, state_delta, state_conv, page_table, norm_ffn1, ffn1_wi, ffn1_wo, norm_mix, w_in, fox_f_bias, diff_lambda, diff_norm, delta_conv_w, delta_A_log, delta_dt_bias, delta_norm, w_branch, w_out, norm_ffn2, ffn2_wi, ffn2_wo, norm_final):
    n_seq, seq_len, d = x_prompt.shape
    n_smp, dec_seq, _ = x_sample.shape
    depth = w_in.shape[0]
    n_pool = cache_fox_k.shape[1]
    n_pages = page_table.shape[1]
    past_len = n_pages * PAGE_SIZE
    assert CONV_WIDTH - 1 <= dec_seq <= SROWS and seq_len % DELTA_CHUNK == 0

    wi1, wo1 = ffn1_wi.astype(BF16), ffn1_wo.astype(BF16)
    wi2, wo2 = ffn2_wi.astype(BF16), ffn2_wo.astype(BF16)
    w_p = _relayout_w_in(w_in, _PROMPT_COLS)
    w_s = _relayout_w_in(w_in, _SAMPLE_COLS)
    w_gates = w_in[:, :, w_in.shape[2] - N_BRANCH * d:].astype(BF16)
    wb, wo = w_branch.astype(BF16), w_out.astype(BF16)

    caches = (cache_fox_k.reshape(depth, n_pool, PAGE_SIZE, HEAD_W),
              cache_fox_v.reshape(depth, n_pool, PAGE_SIZE, HEAD_W),
              cache_fox_logf.reshape(depth, n_pool, 1, PAGE_SIZE * FOX_HEADS),
              cache_diff_k.reshape(depth, n_pool, PAGE_SIZE, HEAD_W),
              cache_diff_v.reshape(depth, n_pool, PAGE_SIZE, HEAD_W))
    s_delta = state_delta.reshape(depth, n_smp, HEAD_W, DELTA_V_DIM)

    xp = x_prompt.reshape(n_seq * seq_len, d)
    xs = jnp.pad(x_sample, ((0, 0), (0, SROWS - dec_seq), (0, 0))).reshape(n_smp * SROWS, d)
    cos_p, sin_p = _rope_tables(n_seq * seq_len, 0, seq_len)
    cos_s, sin_s = _rope_tables(n_smp * SROWS, past_len, SROWS)

    new_p, new_s = [], []
    for l in range(depth):
        lam_init = 0.8 - 0.6 * math.exp(-0.3 * l)
        xp = _ffn(xp, norm_ffn1[l], wi1, wo1, l)
        xs = _ffn(xs, norm_ffn1[l], wi1, wo1, l)
        pp = _inproj(xp, norm_mix[l], w_p, l, cos_p, sin_p, fox_f_bias[l], delta_A_log[l], delta_dt_bias[l],
                     prompt=True, seq_len=seq_len)
        ps = _inproj(xs, norm_mix[l], w_s, l, cos_s, sin_s, fox_f_bias[l], delta_A_log[l], delta_dt_bias[l],
                     prompt=False, seq_len=SROWS)

        of_p = _fox_prompt(pp["fq_pad"], pp["fk_pad"], pp["fv_pad"], n_seq, seq_len)
        od_p = _diff_prompt(pp["dq"], pp["dkb"], pp["dv_pad"], diff_lambda[l], diff_norm[l], lam_init, n_seq, seq_len)
        ol_p, sbd = _delta_prompt(pp["cqkv"], pp["beta"], pp["g"], pp["cz"], delta_conv_w[l], delta_norm[l],
                                  n_seq, seq_len)
        of_s, od_s = _sample_attn(page_table, caches, l, ps, diff_lambda[l], diff_norm[l], lam_init)
        ol_s, s_new = _delta_sample(ps, state_conv, s_delta, l, delta_conv_w[l], delta_norm[l], dec_seq)

        xp = _merge(xp, norm_mix[l], w_gates, wb, wo, l, (of_p, od_p, ol_p))
        xs = _merge(xs, norm_mix[l], w_gates, wb, wo, l, (of_s, od_s, ol_s))
        final = norm_final if l == depth - 1 else None
        xp = _ffn(xp, norm_ffn2[l], wi2, wo2, l, final)
        xs = _ffn(xs, norm_ffn2[l], wi2, wo2, l, final)

        def p3(a):
            return a.reshape(n_seq, seq_len, a.shape[1])

        def s3(a):
            return a.reshape(n_smp, SROWS, a.shape[1])[:, :dec_seq]

        new_p.append(dict(
            fox_k=p3(pp["fk"]).reshape(n_seq, seq_len, FOX_HEADS, FOX_DIM),
            fox_v=p3(pp["fv"]).reshape(n_seq, seq_len, FOX_HEADS, FOX_DIM),
            fox_logf=p3(pp["small"])[:, :, :FOX_HEADS],
            diff_k=p3(pp["dk"]).reshape(n_seq, seq_len, DIFF_HEADS, 2, DIFF_QK_DIM),
            diff_v=p3(pp["dv"]).reshape(n_seq, seq_len, DIFF_HEADS, DIFF_V_DIM),
            delta=jnp.stack([sbd[:, h * 64:(h + 1) * 64, h * 64:(h + 1) * 64] for h in range(DELTA_HEADS)], axis=1),
            conv=p3(pp["cqkv"])[:, seq_len - (CONV_WIDTH - 1):]))
        new_s.append(dict(
            fox_k=s3(ps["fk"]).reshape(n_smp, dec_seq, FOX_HEADS, FOX_DIM),
            fox_v=s3(ps["fv"]).reshape(n_smp, dec_seq, FOX_HEADS, FOX_DIM),
            fox_logf=s3(ps["small"])[:, :, :FOX_HEADS],
            diff_k=s3(ps["dk"]).reshape(n_smp, dec_seq, DIFF_HEADS, 2, DIFF_QK_DIM),
            diff_v=s3(ps["dv"]).reshape(n_smp, dec_seq, DIFF_HEADS, DIFF_V_DIM),
            delta=s_new.reshape(n_smp, DELTA_HEADS, DELTA_K_DIM, DELTA_V_DIM),
            conv=s3(ps["cqkv"])[:, dec_seq - (CONV_WIDTH - 1):]))

    def stk(lst, name):
        return jnp.stack([dd[name] for dd in lst])

    y_prompt = xp.reshape(n_seq, seq_len, d)
    y_sample = xs.reshape(n_smp, SROWS, d)[:, :dec_seq]
    return (y_prompt, y_sample,
            stk(new_p, "fox_k"), stk(new_s, "fox_k"),
            stk(new_p, "fox_v"), stk(new_s, "fox_v"),
            stk(new_p, "fox_logf"), stk(new_s, "fox_logf"),
            stk(new_p, "diff_k"), stk(new_s, "diff_k"),
            stk(new_p, "diff_v"), stk(new_s, "diff_v"),
            stk(new_p, "delta"), stk(new_s, "delta"),
            stk(new_p, "conv"), stk(new_s, "conv"))
```

```python
import functools
import math

import numpy as np
import jax
import jax.numpy as jnp
from jax import lax
from jax.experimental import pallas as pl
from jax.experimental.pallas import tpu as pltpu

F32 = jnp.float32
BF16 = jnp.bfloat16

FOX_HEADS = 4
FOX_DIM = 64
DIFF_HEADS = 4
DIFF_QK_DIM = 32
DIFF_V_DIM = 64
DELTA_HEADS = 4
DELTA_K_DIM = 64
DELTA_V_DIM = 64
CONV_WIDTH = 4
DELTA_CHUNK = 64
N_BRANCH = 3
HEAD_W = 256
PAGE_SIZE = 128
ROPE_THETA = 10000.0
NORM_EPS = 1e-6
LOG2E = 1.4426950408889634
FOX_QSCALE = FOX_DIM ** -0.5 * LOG2E
DIFF_QSCALE = DIFF_QK_DIM ** -0.5 * LOG2E

V7X_VMEM_LIMIT_BYTES = 56 * 1024 * 1024
AUG_C = 64
AUG_K = 67
ONES_LANE = 64


def _cparams(*sem):
    return pltpu.CompilerParams(dimension_semantics=sem, vmem_limit_bytes=V7X_VMEM_LIMIT_BYTES)


def _dot(a, b):
    return jnp.dot(a, b, preferred_element_type=F32)


def _dot_nt(a, b):
    return lax.dot_general(a, b, (((1,), (1,)), ((), ())), preferred_element_type=F32)


def _dot_tn(a, b):
    return lax.dot_general(a, b, (((0,), (0,)), ((), ())), preferred_element_type=F32)


def _split3(x):
    hi = x.astype(BF16)
    r = x - hi.astype(F32)
    mid = r.astype(BF16)
    lo = (r - mid.astype(F32)).astype(BF16)
    return hi, mid, lo


def _dot3_r(x, m):
    hi, mid, lo = _split3(x)
    return _dot(hi, m) + _dot(mid, m) + _dot(lo, m)


def _dot3_l(m, x):
    hi, mid, lo = _split3(x)
    return _dot(m, hi) + _dot(m, mid) + _dot(m, lo)


def _rms(x, gain):
    ms = jnp.mean(x * x, axis=-1, keepdims=True)
    return x * lax.rsqrt(ms + NORM_EPS) * gain


def _const_spec(shape):
    nd = len(shape)
    return pl.BlockSpec(shape, lambda *_: (0,) * nd, pipeline_mode=pl.Buffered(1))


def _ffn_body(x_ref, g_ref, wg_ref, wu_ref, wo_ref, gf_ref, o_ref, h_scr, acc_scr, *, final_norm):
    f = pl.program_id(1)

    @pl.when(f == 0)
    def _():
        h_scr[...] = _rms(x_ref[...], g_ref[...]).astype(BF16)
        acc_scr[...] = jnp.zeros_like(acc_scr)

    h = h_scr[...]
    gate = _dot(h, wg_ref[...])
    up = _dot(h, wu_ref[...])
    act = (gate * jax.nn.sigmoid(gate) * up).astype(BF16)
    acc_scr[...] += _dot(act, wo_ref[...])

    @pl.when(f == pl.num_programs(1) - 1)
    def _():
        y = x_ref[...] + 0.5 * acc_scr[...]
        if final_norm:
            y = _rms(y, gf_ref[...])
        o_ref[...] = y


def _ffn(x, gain, wi, wo, layer, final_gain=None):
    r, d = x.shape
    dff = wo.shape[1]
    tm = min(r, 512)
    tf = dff // 2 if (dff // 2) % 128 == 0 else dff
    nf = dff // tf
    final_norm = final_gain is not None
    gf = final_gain if final_norm else gain
    return pl.pallas_call(
        functools.partial(_ffn_body, final_norm=final_norm),
        out_shape=jax.ShapeDtypeStruct((r, d), F32),
        grid=(r // tm, nf),
        in_specs=[
            pl.BlockSpec((tm, d), lambda i, f: (i, 0)),
            pl.BlockSpec((1, d), lambda i, f: (0, 0)),
            pl.BlockSpec((None, d, tf), lambda i, f: (layer, 0, f)),
            pl.BlockSpec((None, d, tf), lambda i, f: (layer, 0, nf + f)),
            pl.BlockSpec((None, tf, d), lambda i, f: (layer, f, 0)),
            pl.BlockSpec((1, d), lambda i, f: (0, 0)),
        ],
        out_specs=pl.BlockSpec((tm, d), lambda i, f: (i, 0)),
        scratch_shapes=[pltpu.VMEM((tm, d), BF16), pltpu.VMEM((tm, d), F32)],
        compiler_params=_cparams("arbitrary", "arbitrary"),
        name="ffn",
    )(x, gain.reshape(1, d), wi, wi, wo, gf.reshape(1, d))


def _rope_body(inv_ref, cos_ref, sin_ref, *, tm, base, period):
    i = pl.program_id(0)
    row = lax.broadcasted_iota(jnp.int32, (tm, HEAD_W), 0) + i * tm
    lane = lax.broadcasted_iota(jnp.int32, (tm, HEAD_W), 1)
    pos = base + (row & (period - 1))
    ang = pos.astype(F32) * inv_ref[...]
    cos_ref[...] = jnp.cos(ang)
    s = jnp.sin(ang)
    first_half = (lane & (DIFF_QK_DIM - 1)) < DIFF_QK_DIM // 2
    sin_ref[...] = jnp.where(first_half, -s, s)


def _rope_tables(rows, base, period):
    assert period & (period - 1) == 0
    half = DIFF_QK_DIM // 2
    inv = ROPE_THETA ** (-jnp.arange(0, DIFF_QK_DIM, 2, dtype=F32) / DIFF_QK_DIM)
    inv_lanes = jnp.tile(inv, HEAD_W // half).reshape(1, HEAD_W)
    tm = min(rows, 512)
    return pl.pallas_call(
        functools.partial(_rope_body, tm=tm, base=base, period=period),
        out_shape=(jax.ShapeDtypeStruct((rows, HEAD_W), F32),) * 2,
        grid=(rows // tm,),
        in_specs=[pl.BlockSpec((1, HEAD_W), lambda i: (0, 0))],
        out_specs=(pl.BlockSpec((tm, HEAD_W), lambda i: (i, 0)),) * 2,
        compiler_params=_cparams("arbitrary"),
        name="rope_tables",
    )(inv_lanes)


_PROMPT_COLS = (("fq_pad", 512), ("fk_pad", 512), ("fk", 256), ("fv", 256),
                ("dq", 256), ("dqs", 256), ("dk", 256), ("dks", 256), ("dv", 256),
                ("cqkv", 768), ("cz", 256), ("cb_rep", 256), ("ca_rep", 256), ("small", 128))
_SAMPLE_COLS = (("fq", 256), ("fk", 256), ("fv", 256),
                ("dq", 256), ("dqs", 256), ("dk", 256), ("dks", 256), ("dv", 256),
                ("cqkv", 768), ("cz", 256), ("cb_rep", 256), ("ca_rep", 256), ("small", 128))


def _layout(cols):
    lay, off = {}, 0
    for name, width in cols:
        lay[name] = (off, width)
        off += width
    return lay, off


def _relayout_w_in(w_in, cols):
    depth, d, _ = w_in.shape
    sizes = (256, 256, 256, 4, 256, 256, 256, 768, 4, 4, 256)
    offs = np.concatenate([[0], np.cumsum(sizes)])
    fq, fk, fv, ff, dq, dk, dv, cqkv, cb, ca, cz = (w_in[:, :, int(offs[i]):int(offs[i + 1])] for i in range(11))

    def pad_heads(w):
        w = w.reshape(depth, d, 4, 64)
        return jnp.pad(w, ((0, 0), (0, 0), (0, 0), (0, 64))).reshape(depth, d, 512)

    def swap_halves(w):
        w = w.reshape(depth, d, DIFF_HEADS * 2, 2, DIFF_QK_DIM // 2)
        return w[:, :, :, ::-1, :].reshape(depth, d, 256)

    def rep(w):
        return jnp.repeat(w, 64, axis=2)

    table = {
        "fq": fq, "fk": fk, "fv": fv, "fq_pad": pad_heads(fq), "fk_pad": pad_heads(fk),
        "dq": dq, "dqs": swap_halves(dq), "dk": dk, "dks": swap_halves(dk), "dv": dv,
        "cqkv": cqkv, "cz": cz, "cb_rep": rep(cb), "ca_rep": rep(ca),
        "small": jnp.pad(ff, ((0, 0), (0, 0), (0, 124))),
    }
    return jnp.concatenate([table[name] for name, _ in cols], axis=2).astype(BF16)


def _relayout_w_values_t(w_in):
    depth, d, _ = w_in.shape

    def head_rows(w):
        wt = jnp.swapaxes(w, 1, 2).reshape(depth, 4, 64, d)
        return jnp.pad(wt, ((0, 0), (0, 0), (0, 64), (0, 0))).reshape(depth, 512, d)

    fv = w_in[:, :, 512:768]
    dv = w_in[:, :, 1284:1540]
    return jnp.concatenate([head_rows(fv), head_rows(dv)], axis=1).astype(BF16)


def _aug_place_mats():
    pq = np.zeros((384, 512), np.float32)
    pk = np.zeros((384, 512), np.float32)
    for piece in range(3):
        for h in range(FOX_HEADS):
            pq[piece * 128 + h, h * 128 + AUG_C + piece] = 1.0
            pk[piece * 128 + h, h * 128 + AUG_K + piece] = -1.0
    return jnp.asarray(pq, BF16), jnp.asarray(pk, BF16)


def _lane_ones(shape, lo, hi):
    lane = lax.broadcasted_iota(jnp.int32, shape, 1) & 127
    mid2 = lo + hi
    return jnp.where(jnp.abs(2 * lane - mid2) <= hi - lo, 1.0, 0.0).astype(F32)


def _inproj_body(*refs, lay, prompt, tiles_per_seq, tm):
    if prompt:
        (x_ref, g_ref, w_ref, cos_ref, sin_ref, fb_ref, alog_ref, dtb_ref, tri_ref, pq_ref, pk_ref, wvt_ref,
         fq_o, fk_pad_o, fvt_o, fk_o, fv_o, small_o, dq_o, dk_o, dkb_o, dv_o, dvt_o,
         cqkv_o, cz_o, beta_o, g_o, carry_scr) = refs
    else:
        (x_ref, g_ref, w_ref, cos_ref, sin_ref, fb_ref, alog_ref, dtb_ref,
         fq_o, fk_o, fv_o, small_o, dq_o, dk_o, dv_o, cqkv_o, cz_o, beta_o, g_o) = refs

    h = _rms(x_ref[...], g_ref[...]).astype(BF16)

    def proj(name):
        off, width = lay[name]
        return _dot(h, w_ref[:, off:off + width])

    logf = jax.nn.log_sigmoid(proj("small") + fb_ref[...])
    small_o[...] = logf

    if prompt:
        i = pl.program_id(0)

        @pl.when(i % tiles_per_seq == 0)
        def _():
            carry_scr[...] = jnp.zeros_like(carry_scr)

        c = _dot3_l(tri_ref[...], logf) + carry_scr[...]
        carry_scr[...] = c[tm - 1:tm, :]
        c1, c2, c3 = _split3(c * LOG2E)
        c123 = jnp.concatenate([c1, c2, c3], axis=1)
        shape = (tm, 512)
        q = proj("fq_pad") * FOX_QSCALE + _dot(c123, pq_ref[...]) + _lane_ones(shape, AUG_K, AUG_K + 2)
        fq_o[...] = q.astype(BF16)
        k = proj("fk_pad") + _dot(c123, pk_ref[...]) + _lane_ones(shape, AUG_C, AUG_C + 2)
        fk_pad_o[...] = k.astype(BF16)
        vt = _dot_nt(wvt_ref[...], h)
        row = lax.broadcasted_iota(jnp.int32, vt.shape, 0) & 127
        vt = (vt + jnp.where(row == ONES_LANE, 1.0, 0.0)).astype(BF16)
        fvt_o[...] = vt[0:512]
        dvt_o[...] = vt[512:1024]
    else:
        fq_o[...] = proj("fq") * FOX_QSCALE
    fk_o[...] = proj("fk")
    fv_o[...] = proj("fv")

    cos = cos_ref[...]
    sin = sin_ref[...]
    dq_o[...] = ((proj("dq") * cos + proj("dqs") * sin) * DIFF_QSCALE).astype(dq_o.dtype)
    dk = proj("dk") * cos + proj("dks") * sin
    dk_o[...] = dk
    if prompt:
        dkb_o[...] = dk.astype(BF16)
    dv_o[...] = proj("dv")

    cqkv_o[...] = proj("cqkv")
    cz = proj("cz")
    cz_o[...] = cz * jax.nn.sigmoid(cz)
    beta_o[...] = jax.nn.sigmoid(proj("cb_rep"))
    g_o[...] = -jnp.exp(alog_ref[...]) * jax.nn.softplus(proj("ca_rep") + dtb_ref[...])


def _inproj(x, gain, w, layer, cos, sin, fbias, alog, dtb, *, prompt, seq_len, w_vt=None):
    r, d = x.shape
    lay, nc = _layout(_PROMPT_COLS if prompt else _SAMPLE_COLS)
    assert w.shape[2] == nc
    tm = min(r, 512 if prompt else 256)
    tm = min(tm, seq_len) if prompt else tm
    fb = jnp.pad(fbias.astype(F32), (0, 124)).reshape(1, 128)
    alog_l = jnp.repeat(alog.astype(F32), 64).reshape(1, 256)
    dtb_l = jnp.repeat(dtb.astype(F32), 64).reshape(1, 256)

    def rows(width):
        return pl.BlockSpec((tm, width), lambda i: (i, 0))

    in_specs = [rows(d), _const_spec((1, d)),
                pl.BlockSpec((None, d, nc), lambda i: (layer, 0, 0), pipeline_mode=pl.Buffered(1)),
                rows(256), rows(256), _const_spec((1, 128)), _const_spec((1, 256)), _const_spec((1, 256))]
    args = [x, gain.reshape(1, d), w, cos, sin, fb, alog_l, dtb_l]
    if prompt:
        tri = jnp.asarray(np.tril(np.ones((tm, tm), np.float32)), BF16)
        pq, pk = _aug_place_mats()
        in_specs += [_const_spec((tm, tm)), _const_spec((384, 512)), _const_spec((384, 512)),
                     pl.BlockSpec((None, 1024, d), lambda i: (layer, 0, 0), pipeline_mode=pl.Buffered(1))]
        args += [tri, pq, pk, w_vt]
        outs = (("fq_pad", 512, BF16), ("fk_pad", 512, BF16), ("fvt", 0, BF16), ("fk", 256, F32),
                ("fv", 256, F32), ("small", 128, F32), ("dq", 256, BF16), ("dk", 256, F32), ("dkb", 256, BF16),
                ("dv", 256, F32), ("dvt", 0, BF16), ("cqkv", 768, F32), ("cz", 256, F32),
                ("beta", 256, F32), ("g", 256, F32))
        scratch = [pltpu.VMEM((1, 128), F32)]
    else:
        outs = (("fq", 256, F32), ("fk", 256, F32), ("fv", 256, F32), ("small", 128, F32), ("dq", 256, F32),
                ("dk", 256, F32), ("dv", 256, F32), ("cqkv", 768, F32), ("cz", 256, F32),
                ("beta", 256, F32), ("g", 256, F32))
        scratch = []
    res = pl.pallas_call(
        functools.partial(_inproj_body, lay=lay, prompt=prompt, tiles_per_seq=max(seq_len // tm, 1), tm=tm),
        out_shape=tuple(jax.ShapeDtypeStruct((r, wd) if wd else (512, r), dt) for _, wd, dt in outs),
        grid=(r // tm,),
        in_specs=in_specs,
        out_specs=tuple(rows(wd) if wd else pl.BlockSpec((512, tm), lambda i: (0, i)) for _, wd, _ in outs),
        scratch_shapes=scratch,
        compiler_params=_cparams("arbitrary"),
        name="inproj_prompt" if prompt else "inproj_sample",
    )(*args)
    return {name: a for (name, _, _), a in zip(outs, res)}


def _tri_schedule(nq):
    qi = np.concatenate([np.full(i + 1, i, np.int32) for i in range(nq)])
    kj = np.concatenate([np.arange(i + 1, dtype=np.int32) for i in range(nq)])
    return jnp.asarray(qi), jnp.asarray(kj)


def _online_softmax_step(st, vt, m_ref, acc_ref):
    m_old = m_ref[...]
    m_new = jnp.maximum(m_old, jnp.max(st, axis=0, keepdims=True))
    alpha = jnp.exp2(m_old - m_new)
    p = jnp.exp2(st - m_new)
    acc_ref[...] = alpha * acc_ref[...] + _dot(vt, p.astype(BF16))
    m_ref[...] = m_new


def _causal_mask(st):
    key = lax.broadcasted_iota(jnp.int32, st.shape, 0)
    qry = lax.broadcasted_iota(jnp.int32, st.shape, 1)
    return jnp.where(key <= qry, st, -jnp.inf)


def _fox_body(qi_ref, kj_ref, q_ref, k_ref, vt_ref, o_ref, m_scr, acc_scr):
    t = pl.program_id(1)
    qi = qi_ref[t]
    kj = kj_ref[t]

    @pl.when(kj == 0)
    def _():
        m_scr[...] = jnp.full_like(m_scr, -jnp.inf)
        acc_scr[...] = jnp.zeros_like(acc_scr)

    def scores(h):
        sl = slice(h * 128, (h + 1) * 128)
        return _dot_nt(k_ref[:, sl], q_ref[:, sl])

    def step(masked):
        st_next = scores(0)
        for h in range(FOX_HEADS):
            st = st_next
            if h + 1 < FOX_HEADS:
                st_next = scores(h + 1)
            if masked:
                st = _causal_mask(st)
            _online_softmax_step(st, vt_ref[h * 128:(h + 1) * 128, :], m_scr.at[h], acc_scr.at[h])

    @pl.when(kj < qi)
    def _():
        step(False)

    @pl.when(kj == qi)
    def _():
        step(True)
        for h in range(FOX_HEADS):
            a = acc_scr[h]
            o = a[:FOX_DIM] / a[ONES_LANE:ONES_LANE + 1]
            o_ref[:, h * FOX_DIM:(h + 1) * FOX_DIM] = o.T.astype(o_ref.dtype)


def _fox_prompt(q_pad, k_pad, vt_pad, n_seq, seq_len):
    tq = min(seq_len, 512)
    nq = seq_len // tq
    qi, kj = _tri_schedule(nq)
    gs = pltpu.PrefetchScalarGridSpec(
        num_scalar_prefetch=2,
        grid=(n_seq, int(qi.shape[0])),
        in_specs=[
            pl.BlockSpec((tq, 512), lambda n, t, qi, kj: (n * nq + qi[t], 0)),
            pl.BlockSpec((tq, 512), lambda n, t, qi, kj: (n * nq + kj[t], 0)),
            pl.BlockSpec((512, tq), lambda n, t, qi, kj: (0, n * nq + kj[t])),
        ],
        out_specs=pl.BlockSpec((tq, HEAD_W), lambda n, t, qi, kj: (n * nq + qi[t], 0)),
        scratch_shapes=[pltpu.VMEM((FOX_HEADS, 1, tq), F32), pltpu.VMEM((FOX_HEADS, 128, tq), F32)],
    )
    return pl.pallas_call(
        _fox_body,
        out_shape=jax.ShapeDtypeStruct((n_seq * seq_len, HEAD_W), BF16),
        grid_spec=gs,
        compiler_params=_cparams("arbitrary", "arbitrary"),
        name="fox_prompt",
    )(qi, kj, q_pad, k_pad, vt_pad)


def _diff_lambda(lam_ref, lam_init):
    lp = lam_ref[...]
    s1 = jnp.sum(lp[0:1] * lp[1:2], axis=1, keepdims=True)
    s2 = jnp.sum(lp[2:3] * lp[3:4], axis=1, keepdims=True)
    return jnp.exp(s1) - jnp.exp(s2) + lam_init


def _diff_body(qi_ref, kj_ref, q_ref, k_ref, vt_ref, lam_ref, gain_ref, o_ref, qm_scr, m_scr, acc_scr, *, lam_init):
    t = pl.program_id(1)
    qi = qi_ref[t]
    kj = kj_ref[t]
    n_maps = 2 * DIFF_HEADS

    @pl.when(kj == 0)
    def _():
        m_scr[...] = jnp.full_like(m_scr, -jnp.inf)
        acc_scr[...] = jnp.zeros_like(acc_scr)
        q = q_ref[...]
        lane_map = lax.broadcasted_iota(jnp.int32, q.shape, 1) // DIFF_QK_DIM
        for m in range(n_maps):
            qm_scr[m] = jnp.where(lane_map == m, q, jnp.zeros_like(q))

    def step(masked):
        k = k_ref[...]
        st_next = _dot_nt(k, qm_scr[0])
        for m in range(n_maps):
            h = m // 2
            st = st_next
            if m + 1 < n_maps:
                st_next = _dot_nt(k, qm_scr[m + 1])
            if masked:
                st = _causal_mask(st)
            _online_softmax_step(st, vt_ref[h * 128:(h + 1) * 128, :], m_scr.at[m], acc_scr.at[m])

    @pl.when(kj < qi)
    def _():
        step(False)

    @pl.when(kj == qi)
    def _():
        step(True)
        lam = _diff_lambda(lam_ref, lam_init)
        tq = o_ref.shape[0]
        gain = jnp.concatenate([gain_ref[...]] * (tq // 128), axis=1)
        for h in range(DIFF_HEADS):
            a0 = acc_scr[2 * h]
            a1 = acc_scr[2 * h + 1]
            o = (a0[:DIFF_V_DIM] / a0[ONES_LANE:ONES_LANE + 1]
                 - lam * (a1[:DIFF_V_DIM] / a1[ONES_LANE:ONES_LANE + 1]))
            ms = jnp.mean(o * o, axis=0, keepdims=True)
            o = o * lax.rsqrt(ms + NORM_EPS) * gain * (1.0 - lam_init)
            o_ref[:, h * DIFF_V_DIM:(h + 1) * DIFF_V_DIM] = o.T.astype(o_ref.dtype)


def _diff_prompt(q, k, vt_pad, lam_p, gain, lam_init, n_seq, seq_len):
    tq = min(seq_len, 512)
    nq = seq_len // tq
    qi, kj = _tri_schedule(nq)
    n_maps = 2 * DIFF_HEADS
    gs = pltpu.PrefetchScalarGridSpec(
        num_scalar_prefetch=2,
        grid=(n_seq, int(qi.shape[0])),
        in_specs=[
            pl.BlockSpec((tq, HEAD_W), lambda n, t, qi, kj: (n * nq + qi[t], 0)),
            pl.BlockSpec((tq, HEAD_W), lambda n, t, qi, kj: (n * nq + kj[t], 0)),
            pl.BlockSpec((512, tq), lambda n, t, qi, kj: (0, n * nq + kj[t])),
            pl.BlockSpec((4, DIFF_QK_DIM), lambda n, t, qi, kj: (0, 0)),
            pl.BlockSpec((DIFF_V_DIM, 128), lambda n, t, qi, kj: (0, 0)),
        ],
        out_specs=pl.BlockSpec((tq, HEAD_W), lambda n, t, qi, kj: (n * nq + qi[t], 0)),
        scratch_shapes=[pltpu.VMEM((n_maps, tq, HEAD_W), BF16), pltpu.VMEM((n_maps, 1, tq), F32),
                        pltpu.VMEM((n_maps, 128, tq), F32)],
    )
    gain_rows = jnp.broadcast_to(gain.astype(F32).reshape(DIFF_V_DIM, 1), (DIFF_V_DIM, 128))
    return pl.pallas_call(
        functools.partial(_diff_body, lam_init=lam_init),
        out_shape=jax.ShapeDtypeStruct((n_seq * seq_len, HEAD_W), BF16),
        grid_spec=gs,
        compiler_params=_cparams("arbitrary", "arbitrary"),
        name="diff_prompt",
    )(qi, kj, q, k, vt_pad, lam_p.astype(F32), gain_rows)


def _merge_body(x_ref, g_ref, wg_ref, wb_ref, wo_ref, b0_ref, b1_ref, b2_ref, o_ref):
    x = x_ref[...]
    d = x.shape[1]
    h = _rms(x, g_ref[...]).astype(BF16)
    mix = None
    for b, b_ref in enumerate((b0_ref, b1_ref, b2_ref)):
        gate = jax.nn.sigmoid(_dot(h, wg_ref[:, b * d:(b + 1) * d]))
        term = gate * _dot(b_ref[...].astype(BF16), wb_ref[b])
        mix = term if mix is None else mix + term
    o_ref[...] = x + _dot(mix.astype(BF16), wo_ref[...])


def _merge(x, gain, w_gates, w_branch, w_out, layer, branches):
    r, d = x.shape
    tm = min(r, 512)

    def rows(width):
        return pl.BlockSpec((tm, width), lambda i: (i, 0))

    return pl.pallas_call(
        _merge_body,
        out_shape=jax.ShapeDtypeStruct((r, d), F32),
        grid=(r // tm,),
        in_specs=[rows(d), _const_spec((1, d)),
                  pl.BlockSpec((None, d, N_BRANCH * d), lambda i: (layer, 0, 0), pipeline_mode=pl.Buffered(1)),
                  pl.BlockSpec((None, N_BRANCH, HEAD_W, d), lambda i: (layer, 0, 0, 0), pipeline_mode=pl.Buffered(1)),
                  pl.BlockSpec((None, d, d), lambda i: (layer, 0, 0), pipeline_mode=pl.Buffered(1)),
                  rows(HEAD_W), rows(HEAD_W), rows(HEAD_W)],
        out_specs=rows(d),
        compiler_params=_cparams("arbitrary"),
        name="merge",
    )(x, gain.reshape(1, d), w_gates, w_branch, w_out, *branches)


def _delta_consts():
    c = DELTA_CHUNK
    i = np.arange(c)
    lane = np.arange(HEAD_W)
    lh, lj = lane // c, lane % c
    tril = i[:, None] >= i[None, :]
    suffix = i[None, :] > i[:, None]
    lvl = []
    for k in range(1, 7):
        b = 1 << k
        lvl.append((i[:, None] // b == lj[None, :] // b) & (i[:, None] // (b // 2) != lj[None, :] // (b // 2)))
    f = lambda a, dt: jnp.asarray(np.asarray(a, np.float32), dt)
    return dict(
        lhsg=f(np.concatenate([tril, suffix], axis=0), BF16),
        ones=f(np.ones((c, c)), BF16),
        triu4=f(i[:, None] <= lj[None, :], F32),
        tril4=f(i[:, None] >= lj[None, :], F32),
        stril4=f(i[:, None] > lj[None, :], F32),
        eye4=f(i[:, None] == lj[None, :], F32),
        lvl=f(np.stack(lvl), F32),
        maskbd=f(lh[:, None] == lh[None, :], BF16),
    )


def _bd(x4, maskbd):
    return jnp.concatenate([x4, x4, x4, x4], axis=0) * maskbd


def _mm_hi(xs, ys, maskbd):
    xh = [x.astype(BF16) for x in xs]
    xl = [(x - h.astype(F32)).astype(BF16) for x, h in zip(xs, xh)]
    yh = [y.astype(BF16) for y in ys]
    yl = [(y - h.astype(F32)).astype(BF16) for y, h in zip(ys, yh)]
    ybh = [_bd(h, maskbd) for h in yh]
    ybl = [_bd(l, maskbd) for l in yl]
    return [_dot(a, bh) + _dot(al, bh) + _dot(a, bl) for a, al, bh, bl in zip(xh, xl, ybh, ybl)]


def _unit_lower_inverse(a4s, eye4, lvl_ref, maskbd):
    ts = [eye4 - a4 * lvl_ref[0] for a4 in a4s]
    for k in range(1, 6):
        ms = [a4 * lvl_ref[k] for a4 in a4s]
        xs = _mm_hi(ms, ts, maskbd)
        ts = [t - d for t, d in zip(ts, _mm_hi(ts, xs, maskbd))]
    return ts


def _head_sums(x, maskbd):
    return _dot3_r(x, maskbd)


def _delta_body(cqkv_ref, beta_ref, g_ref, cz_ref, cw_ref, gain_ref,
                lhsg_ref, ones_ref, triu4_ref, tril4_ref, stril4_ref, eye4_ref, lvl_ref, maskbd_ref,
                o_ref, s_ref, xbuf, ub_scr, w_scr, qk_scr, qe_scr, kd_scr, eg_scr, o_scr, sbd_scr, *, tc):
    j = pl.program_id(1)
    c = DELTA_CHUNK
    maskbd = maskbd_ref[...]
    maskf = maskbd.astype(F32)

    @pl.when(j == 0)
    def _():
        xbuf[0:8, :] = jnp.zeros((8, xbuf.shape[1]), F32)
        sbd_scr[...] = jnp.zeros_like(sbd_scr)

    xbuf[8:8 + tc, :] = cqkv_ref[...]
    cw = cw_ref[...]
    y = cw[3:4] * xbuf[8:8 + tc, :]
    for i in range(1, CONV_WIDTH):
        y = y + cw[3 - i:4 - i] * xbuf[8 - i:8 - i + tc, :]
    xbuf[0:8, :] = xbuf[tc:tc + 8, :]
    y = y * jax.nn.sigmoid(y)
    q = y[:, 0:HEAD_W]
    k = y[:, HEAD_W:2 * HEAD_W]
    qn = q * lax.rsqrt(_head_sums(q * q, maskbd) + 1e-6) * (DELTA_K_DIM ** -0.5)
    kn = k * lax.rsqrt(_head_sums(k * k, maskbd) + 1e-6)
    v = y[:, 2 * HEAD_W:3 * HEAD_W]
    beta = beta_ref[...]
    g = g_ref[...]

    cs = [slice(ci * c, (ci + 1) * c) for ci in range(tc // c)]
    qcs, kcs, vcs = [qn[s] for s in cs], [kn[s] for s in cs], [v[s] for s in cs]
    bcs, gcs = [beta[s] for s in cs], [g[s] for s in cs]
    gsums = [_dot3_l(lhsg_ref[...], gc) for gc in gcs]
    grows = [_dot3_l(ones_ref[...], gc * triu4_ref[...]) for gc in gcs]
    gcols, gsufs = [gs[0:c] for gs in gsums], [gs[c:2 * c] for gs in gsums]
    decays = [jnp.where(tril4_ref[...] > 0, jnp.exp(gcol - grow), 0.0) for gcol, grow in zip(gcols, grows)]
    exp_gs = [jnp.exp(gcol) for gcol in gcols]
    kbs = [kc * bc for kc, bc in zip(kcs, bcs)]
    kbds = [_bd(kc.astype(BF16), maskbd) for kc in kcs]
    a4s = [_dot_nt(kb.astype(BF16), kbd) * decay * stril4_ref[...] for kb, kbd, decay in zip(kbs, kbds, decays)]
    qk4s = [_dot_nt(qc.astype(BF16), kbd) * decay for qc, kbd, decay in zip(qcs, kbds, decays)]
    t16s = [t.astype(BF16) for t in _unit_lower_inverse(a4s, eye4_ref[...], lvl_ref, maskbd)]
    u_bases = [_dot(t16, _bd((vc * bc).astype(BF16), maskbd)) for t16, vc, bc in zip(t16s, vcs, bcs)]
    ws = [_dot(t16, _bd((kb * eg).astype(BF16), maskbd)) for t16, kb, eg in zip(t16s, kbs, exp_gs)]
    ub_scr[...] = jnp.concatenate(u_bases, axis=0)
    w_scr[...] = jnp.concatenate([w.astype(BF16) for w in ws], axis=0)
    qk_scr[...] = jnp.concatenate([x.astype(BF16) for x in qk4s], axis=0)
    qe_scr[...] = jnp.concatenate([(qc * eg).astype(BF16) for qc, eg in zip(qcs, exp_gs)], axis=0)
    kd_scr[...] = jnp.concatenate([(kc * jnp.exp(gs)).astype(BF16) for kc, gs in zip(kcs, gsufs)], axis=0)
    eg_scr[...] = jnp.concatenate([jnp.broadcast_to(eg[c - 1:c, :], (8, HEAD_W)) for eg in exp_gs], axis=0)

    def scan(ci, carry):
        sl = pl.ds(pl.multiple_of(ci * c, c), c)
        s_old = sbd_scr[...]
        s16 = s_old.astype(BF16)
        u16 = (ub_scr[sl, :] - _dot(w_scr[sl, :], s16)).astype(BF16)
        o_scr[sl, :] = _dot(qe_scr[sl, :], s16) + _dot(qk_scr[sl, :], _bd(u16, maskbd))
        exp_g_last = eg_scr[pl.ds(pl.multiple_of(ci * 8, 8), 8), :][0:1]
        sbd_scr[...] = s_old * exp_g_last + maskf * _dot_tn(kd_scr[sl, :], u16)
        return carry

    lax.fori_loop(0, tc // c, scan, 0)

    o = o_scr[...]
    ms = _head_sums(o * o, maskbd) * (1.0 / DELTA_V_DIM)
    o_ref[...] = (o * lax.rsqrt(ms + NORM_EPS) * gain_ref[...] * cz_ref[...]).astype(o_ref.dtype)

    @pl.when(j == pl.num_programs(1) - 1)
    def _():
        s_ref[...] = sbd_scr[...]


def _delta_prompt(cqkv, beta, g, cz, conv_w, gain, n_seq, seq_len):
    tc = min(seq_len, 512)
    nt = seq_len // tc
    cch = cqkv.shape[1]
    consts = _delta_consts()
    names = ("lhsg", "ones", "triu4", "tril4", "stril4", "eye4", "lvl", "maskbd")

    def rows(width):
        return pl.BlockSpec((tc, width), lambda n, j: (n * nt + j, 0))

    return pl.pallas_call(
        functools.partial(_delta_body, tc=tc),
        out_shape=(jax.ShapeDtypeStruct((n_seq * seq_len, HEAD_W), BF16),
                   jax.ShapeDtypeStruct((n_seq, HEAD_W, HEAD_W), F32)),
        grid=(n_seq, nt),
        in_specs=[rows(cch), rows(HEAD_W), rows(HEAD_W), rows(HEAD_W),
                  _const_spec((CONV_WIDTH, cch)), _const_spec((1, HEAD_W))]
                 + [_const_spec(consts[nm].shape) for nm in names],
        out_specs=(rows(HEAD_W), pl.BlockSpec((None, HEAD_W, HEAD_W), lambda n, j: (n, 0, 0))),
        scratch_shapes=[pltpu.VMEM((tc + 8, cch), F32), pltpu.VMEM((tc, HEAD_W), F32),
                        pltpu.VMEM((tc, HEAD_W), BF16), pltpu.VMEM((tc, HEAD_W), BF16),
                        pltpu.VMEM((tc, HEAD_W), BF16), pltpu.VMEM((tc, HEAD_W), BF16),
                        pltpu.VMEM((8 * (tc // DELTA_CHUNK), HEAD_W), F32),
                        pltpu.VMEM((tc, HEAD_W), F32), pltpu.VMEM((HEAD_W, HEAD_W), F32)],
        compiler_params=_cparams("arbitrary", "arbitrary"),
        name="delta_prompt",
    )(cqkv, beta, g, cz, conv_w.astype(F32), jnp.tile(gain.astype(F32), DELTA_HEADS).reshape(1, HEAD_W),
      *[consts[nm] for nm in names])


SROWS = 8


def _sample_attn_consts(n_pages):
    r = np.arange(PAGE_SIZE)
    m2 = r[:, None] > r[None, :]
    m2ones = np.concatenate([m2, np.ones((PAGE_SIZE, PAGE_SIZE), bool)], axis=1)
    idx = np.arange(4 * n_pages)
    u2 = (idx[None, :] // 4 > idx[:, None] // 4) & (idx[None, :] % 4 == idx[:, None] % 4)
    t = np.arange(SROWS)
    tri8 = t[None, :] <= t[:, None]
    sup = t[:, None] > r[None, :]
    lh = np.arange(HEAD_W) // 64
    f = lambda a, dt: jnp.asarray(np.asarray(a, np.float32), dt)
    return dict(m2ones=f(m2ones, BF16), u2=f(u2, BF16), tri8=f(tri8, BF16), sup=f(sup, F32),
                maskbd=f(lh[:, None] == lh[None, :], BF16))


def _pad_rows(x, rows):
    return jnp.concatenate([x, jnp.zeros((rows - x.shape[0], x.shape[1]), x.dtype)], axis=0)


def _block_rows(x, groups, lanes_per_group):
    xr = jnp.concatenate([x] * groups, axis=0)
    rg = lax.broadcasted_iota(jnp.int32, xr.shape, 0) // SROWS
    lg = lax.broadcasted_iota(jnp.int32, xr.shape, 1) // lanes_per_group
    return jnp.where(rg == lg, xr, 0.0).astype(BF16)


def _paged_softmax_pv(qbd, kt_pages, vt_pages, k_new, v_new, bias_pages, bias_new):
    tiles = []
    for p, kt_ref in enumerate(kt_pages):
        s = _dot(qbd, kt_ref[...].astype(BF16))
        tiles.append(s if bias_pages is None else s + bias_pages[p])
    s = _dot_nt(qbd, _pad_rows(k_new, PAGE_SIZE).astype(BF16))
    if bias_new is not None:
        s = s + bias_new
    row = lax.broadcasted_iota(jnp.int32, s.shape, 0) & (SROWS - 1)
    col = lax.broadcasted_iota(jnp.int32, s.shape, 1)
    tiles.append(jnp.where(col <= row, s, -jnp.inf))
    m = tiles[0].max(axis=1, keepdims=True)
    for s in tiles[1:]:
        m = jnp.maximum(m, s.max(axis=1, keepdims=True))
    acc = None
    den = None
    for i, s in enumerate(tiles):
        p = jnp.exp2(s - m)
        ps = p.sum(axis=1, keepdims=True)
        if i < len(vt_pages):
            pv = _dot_nt(p.astype(BF16), vt_pages[i][...].astype(BF16))
        else:
            pv = _dot(p.astype(BF16), _pad_rows(v_new, PAGE_SIZE).astype(BF16))
        acc = pv if acc is None else acc + pv
        den = ps if den is None else den + ps
    return acc / den


def _own_head_rows(x, stride):
    lh = lax.broadcasted_iota(jnp.int32, (SROWS, HEAD_W), 1) // 64
    out = None
    for h in range(4):
        part = jnp.where(lh == h, x[h * stride:h * stride + SROWS], 0.0)
        out = part if out is None else out + part
    return out


def _sample_attn_body(pt_ref, *refs, n_pages, lam_init):
    del pt_ref
    groups = [refs[i * n_pages:(i + 1) * n_pages] for i in range(5)]
    fk_pages, fv_pages, lf_pages, dk_pages, dv_pages = groups
    (fq_ref, fkn_ref, fvn_ref, ln_ref, dq_ref, dkn_ref, dvn_ref,
     m2_ref, u2_ref, tri8_ref, sup_ref, lam_ref, gain_ref, maskbd_ref,
     of_ref, od_ref, xm_scr) = refs[5 * n_pages:]

    for p in range(n_pages):
        xm_scr[4 * p:4 * p + 4, :] = lf_pages[p][...]
    wt = _dot3_r(xm_scr[...], m2_ref[...])
    within, totals = wt[:, :PAGE_SIZE], wt[:, PAGE_SIZE:]
    d_past = (within + _dot3_l(u2_ref[...], totals)) * LOG2E
    lnew = ln_ref[...]
    tri8 = tri8_ref[...]
    cn = _dot3_l(tri8, lnew) * LOG2E
    cn_h = [jnp.broadcast_to(cn[:, h:h + 1], (SROWS, PAGE_SIZE)) for h in range(FOX_HEADS)]
    bias_pages = [
        jnp.concatenate([d_past[4 * p + h:4 * p + h + 1, :] + cn_h[h] for h in range(FOX_HEADS)], axis=0)
        for p in range(n_pages)]
    sup = sup_ref[...]
    bias_new = jnp.concatenate(
        [_dot3_l(tri8, jnp.broadcast_to(lnew[:, h:h + 1], (SROWS, PAGE_SIZE)) * sup) * LOG2E
         for h in range(FOX_HEADS)], axis=0)
    qbd = _block_rows(fq_ref[...], FOX_HEADS, FOX_DIM)
    o = _paged_softmax_pv(qbd, fk_pages, fv_pages, fkn_ref[...], fvn_ref[...], bias_pages, bias_new)
    of_ref[...] = _own_head_rows(o, SROWS)

    qbd = _block_rows(dq_ref[...], 2 * DIFF_HEADS, DIFF_QK_DIM)
    o = _paged_softmax_pv(qbd, dk_pages, dv_pages, dkn_ref[...], dvn_ref[...], None, None)
    lam = _diff_lambda(lam_ref, lam_init)
    o = _own_head_rows(o, 2 * SROWS) - lam * _own_head_rows(o[SROWS:], 2 * SROWS)
    ms = _head_sums(o * o, maskbd_ref[...]) * (1.0 / DIFF_V_DIM)
    od_ref[...] = o * lax.rsqrt(ms + NORM_EPS) * gain_ref[...] * (1.0 - lam_init)


def _sample_attn(page_table, caches, layer, proj, lam_p, gain, lam_init):
    n_samples, n_pages = page_table.shape
    consts = _sample_attn_consts(n_pages)

    def page_spec(arr, p):
        return pl.BlockSpec((None, None) + arr.shape[2:], lambda b, pt: (layer, pt[b, p], 0, 0))

    def tile(width):
        return pl.BlockSpec((SROWS, width), lambda b, pt: (b, 0))

    def const(shape):
        nd = len(shape)
        return pl.BlockSpec(shape, lambda b, pt: (0,) * nd, pipeline_mode=pl.Buffered(1))

    in_specs, args = [], []
    for arr in caches:
        for p in range(n_pages):
            in_specs.append(page_spec(arr, p))
            args.append(arr)
    new = (proj["fq"], proj["fk"], proj["fv"], proj["small"], proj["dq"], proj["dk"], proj["dv"])
    in_specs += [tile(a.shape[1]) for a in new]
    args += list(new)
    cvals = (consts["m2ones"], consts["u2"], consts["tri8"], consts["sup"], lam_p.astype(F32),
             jnp.tile(gain.astype(F32), DIFF_HEADS).reshape(1, HEAD_W), consts["maskbd"])
    in_specs += [const(c.shape) for c in cvals]
    args += list(cvals)
    gs = pltpu.PrefetchScalarGridSpec(
        num_scalar_prefetch=1, grid=(n_samples,), in_specs=in_specs,
        out_specs=(tile(HEAD_W), tile(HEAD_W)),
        scratch_shapes=[pltpu.VMEM((4 * n_pages, PAGE_SIZE), F32)])
    return pl.pallas_call(
        functools.partial(_sample_attn_body, n_pages=n_pages, lam_init=lam_init),
        out_shape=(jax.ShapeDtypeStruct((n_samples * SROWS, HEAD_W), F32),) * 2,
        grid_spec=gs,
        compiler_params=_cparams("arbitrary"),
        name="sample_attn",
    )(page_table, *args)


def _delta_sample_body(cqkv_ref, buf_ref, beta_ref, g_ref, cz_ref, cw_ref, gain_ref, s0_ref,
                       tile_ref, tile_t_ref, maskbd_ref, o_ref, s_ref, xp_scr, pr_scr, u_scr, kd_scr, o_scr, *, n_tok):
    maskbd = maskbd_ref[...]
    cch = xp_scr.shape[1]
    xp_scr[0:8, :] = jnp.zeros((8, cch), F32)
    xp_scr[8 - (CONV_WIDTH - 1):8, :] = buf_ref[...]
    xp_scr[8:16, :] = cqkv_ref[...]
    cw = cw_ref[...]
    y = cw[3:4] * xp_scr[8:16, :]
    for i in range(1, CONV_WIDTH):
        y = y + cw[3 - i:4 - i] * xp_scr[8 - i:16 - i, :]
    y = y * jax.nn.sigmoid(y)
    q = y[:, 0:HEAD_W]
    k = y[:, HEAD_W:2 * HEAD_W]
    v = y[:, 2 * HEAD_W:3 * HEAD_W]
    q = q * lax.rsqrt(_head_sums(q * q, maskbd) + 1e-6) * (DELTA_K_DIM ** -0.5)
    k = k * lax.rsqrt(_head_sums(k * k, maskbd) + 1e-6)
    beta = beta_ref[...]
    a = jnp.exp(g_ref[...])

    maskf = maskbd.astype(F32)
    sbd0 = maskf * _dot3_r(s0_ref[...], tile_ref[...])
    s16 = sbd0.astype(BF16)
    rk = _dot(k.astype(BF16), s16)
    rq = _dot(q.astype(BF16), s16)

    pairs_k = [(t, j) for t in range(n_tok) for j in range(t)]
    pairs_q = [(t, j) for t in range(n_tok) for j in range(t + 1)]
    assert len(pairs_k) + len(pairs_q) <= pr_scr.shape[0]
    pr_scr[...] = jnp.zeros_like(pr_scr)
    for r, (t, j) in enumerate(pairs_k):
        pr_scr[r:r + 1, :] = k[t:t + 1] * k[j:j + 1]
    for r, (t, j) in enumerate(pairs_q):
        pr_scr[len(pairs_k) + r:len(pairs_k) + r + 1, :] = q[t:t + 1] * k[j:j + 1]
    dots = _head_sums(pr_scr[...], maskbd)
    kk = {tj: dots[r:r + 1] for r, tj in enumerate(pairs_k)}
    qk = {tj: dots[len(pairs_k) + r:len(pairs_k) + r + 1] for r, tj in enumerate(pairs_q)}

    rk_rows = [rk[t:t + 1] for t in range(n_tok)]
    rq_rows = [rq[t:t + 1] for t in range(n_tok)]
    u_scr[...] = jnp.zeros_like(u_scr)
    kd_scr[...] = jnp.zeros_like(kd_scr)
    o_scr[...] = jnp.zeros_like(o_scr)
    for t in range(n_tok):
        a_t = a[t:t + 1]
        u_t = beta[t:t + 1] * (v[t:t + 1] - a_t * rk_rows[t])
        for t2 in range(t + 1, n_tok):
            rk_rows[t2] = a_t * rk_rows[t2] + kk[(t2, t)] * u_t
        for t2 in range(t, n_tok):
            rq_rows[t2] = a_t * rq_rows[t2] + qk[(t2, t)] * u_t
        o_scr[t:t + 1, :] = rq_rows[t]
        u_scr[t:t + 1, :] = u_t
        tail = None
        for t2 in range(t + 1, n_tok):
            tail = a[t2:t2 + 1] if tail is None else tail * a[t2:t2 + 1]
        kd_scr[t:t + 1, :] = k[t:t + 1] if tail is None else k[t:t + 1] * tail
    a_all = a[0:1]
    for t in range(1, n_tok):
        a_all = a_all * a[t:t + 1]
    s_new = sbd0 * a_all + maskf * _dot_tn(kd_scr[...].astype(BF16), u_scr[...].astype(BF16))
    s_ref[...] = _dot3_r(s_new, tile_t_ref[...])

    o = o_scr[...]
    ms = _head_sums(o * o, maskbd) * (1.0 / DELTA_V_DIM)
    o_ref[...] = o * lax.rsqrt(ms + NORM_EPS) * gain_ref[...] * cz_ref[...]


def _delta_sample(proj, conv_state, state, layer, conv_w, gain, n_tok):
    n_samples = state.shape[1]
    cch = conv_state.shape[3]
    v_ = np.arange(DELTA_V_DIM)
    lane = np.arange(HEAD_W)
    tile_m = jnp.asarray((v_[:, None] == lane[None, :] % DELTA_V_DIM).astype(np.float32), BF16)
    tile_t = jnp.asarray((lane[:, None] % DELTA_V_DIM == v_[None, :]).astype(np.float32), BF16)
    lh = lane // 64
    maskbd = jnp.asarray((lh[:, None] == lh[None, :]).astype(np.float32), BF16)

    def tile(width):
        return pl.BlockSpec((SROWS, width), lambda b: (b, 0))

    return pl.pallas_call(
        functools.partial(_delta_sample_body, n_tok=n_tok),
        out_shape=(jax.ShapeDtypeStruct((n_samples * SROWS, HEAD_W), F32),
                   jax.ShapeDtypeStruct((n_samples, HEAD_W, DELTA_V_DIM), F32)),
        grid=(n_samples,),
        in_specs=[tile(cch),
                  pl.BlockSpec((None, None, CONV_WIDTH - 1, cch), lambda b: (layer, b, 0, 0)),
                  tile(HEAD_W), tile(HEAD_W), tile(HEAD_W),
                  _const_spec((CONV_WIDTH, cch)), _const_spec((1, HEAD_W)),
                  pl.BlockSpec((None, None, HEAD_W, DELTA_V_DIM), lambda b: (layer, b, 0, 0)),
                  _const_spec(tile_m.shape), _const_spec(tile_t.shape), _const_spec(maskbd.shape)],
        out_specs=(tile(HEAD_W), pl.BlockSpec((None, HEAD_W, DELTA_V_DIM), lambda b: (b, 0, 0))),
        scratch_shapes=[pltpu.VMEM((16, cch), F32), pltpu.VMEM((16, HEAD_W), F32), pltpu.VMEM((SROWS, HEAD_W), F32),
                        pltpu.VMEM((SROWS, HEAD_W), F32), pltpu.VMEM((SROWS, HEAD_W), F32)],
        compiler_params=_cparams("arbitrary"),
        name="delta_sample",
    )(proj["cqkv"], conv_state, proj["beta"], proj["g"], proj["cz"], conv_w.astype(F32),
      jnp.tile(gain.astype(F32), DELTA_HEADS).reshape(1, HEAD_W), state, tile_m, tile_t, maskbd)


def kernel(x_prompt, x_sample, cache_fox_k, cache_fox_v, cache_fox_logf, cache_diff_k, cache_diff_v, state_delta, state_conv, page_table, norm_ffn1, ffn1_wi, ffn1_wo, norm_mix, w_in, fox_f_bias, diff_lambda, diff_norm, delta_conv_w, delta_A_log, delta_dt_bias, delta_norm, w_branch, w_out, norm_ffn2, ffn2_wi, ffn2_wo, norm_final):
    n_seq, seq_len, d = x_prompt.shape
    n_smp, dec_seq, _ = x_sample.shape
    depth = w_in.shape[0]
    n_pool = cache_fox_k.shape[1]
    n_pages = page_table.shape[1]
    past_len = n_pages * PAGE_SIZE
    assert CONV_WIDTH - 1 <= dec_seq <= SROWS and seq_len % DELTA_CHUNK == 0

    wi1, wo1 = ffn1_wi.astype(BF16), ffn1_wo.astype(BF16)
    wi2, wo2 = ffn2_wi.astype(BF16), ffn2_wo.astype(BF16)
    w_p = _relayout_w_in(w_in, _PROMPT_COLS)
    w_s = _relayout_w_in(w_in, _SAMPLE_COLS)
    w_vt = _relayout_w_values_t(w_in)
    w_gates = w_in[:, :, w_in.shape[2] - N_BRANCH * d:].astype(BF16)
    wb, wo = w_branch.astype(BF16), w_out.astype(BF16)

    def pages_t(cache):
        c = cache.reshape(depth, n_pool, PAGE_SIZE, HEAD_W)
        return jnp.swapaxes(c, 2, 3)

    caches = (pages_t(cache_fox_k), pages_t(cache_fox_v), jnp.swapaxes(cache_fox_logf, 2, 3),
              pages_t(cache_diff_k), pages_t(cache_diff_v))
    s_delta = state_delta.reshape(depth, n_smp, HEAD_W, DELTA_V_DIM)

    xp = x_prompt.reshape(n_seq * seq_len, d)
    xs = jnp.pad(x_sample, ((0, 0), (0, SROWS - dec_seq), (0, 0))).reshape(n_smp * SROWS, d)
    cos_p, sin_p = _rope_tables(n_seq * seq_len, 0, seq_len)
    cos_s, sin_s = _rope_tables(n_smp * SROWS, past_len, SROWS)

    new_p, new_s = [], []
    for l in range(depth):
        lam_init = 0.8 - 0.6 * math.exp(-0.3 * l)
        xp = _ffn(xp, norm_ffn1[l], wi1, wo1, l)
        xs = _ffn(xs, norm_ffn1[l], wi1, wo1, l)
        pp = _inproj(xp, norm_mix[l], w_p, l, cos_p, sin_p, fox_f_bias[l], delta_A_log[l], delta_dt_bias[l],
                     prompt=True, seq_len=seq_len, w_vt=w_vt)
        ps = _inproj(xs, norm_mix[l], w_s, l, cos_s, sin_s, fox_f_bias[l], delta_A_log[l], delta_dt_bias[l],
                     prompt=False, seq_len=SROWS)

        of_p = _fox_prompt(pp["fq_pad"], pp["fk_pad"], pp["fvt"], n_seq, seq_len)
        od_p = _diff_prompt(pp["dq"], pp["dkb"], pp["dvt"], diff_lambda[l], diff_norm[l], lam_init, n_seq, seq_len)
        ol_p, sbd = _delta_prompt(pp["cqkv"], pp["beta"], pp["g"], pp["cz"], delta_conv_w[l], delta_norm[l],
                                  n_seq, seq_len)
        of_s, od_s = _sample_attn(page_table, caches, l, ps, diff_lambda[l], diff_norm[l], lam_init)
        ol_s, s_new = _delta_sample(ps, state_conv, s_delta, l, delta_conv_w[l], delta_norm[l], dec_seq)

        xp = _merge(xp, norm_mix[l], w_gates, wb, wo, l, (of_p, od_p, ol_p))
        xs = _merge(xs, norm_mix[l], w_gates, wb, wo, l, (of_s, od_s, ol_s))
        final = norm_final if l == depth - 1 else None
        xp = _ffn(xp, norm_ffn2[l], wi2, wo2, l, final)
        xs = _ffn(xs, norm_ffn2[l], wi2, wo2, l, final)

        def p3(a):
            return a.reshape(n_seq, seq_len, a.shape[1])

        def s3(a):
            return a.reshape(n_smp, SROWS, a.shape[1])[:, :dec_seq]

        new_p.append(dict(
            fox_k=p3(pp["fk"]).reshape(n_seq, seq_len, FOX_HEADS, FOX_DIM),
            fox_v=p3(pp["fv"]).reshape(n_seq, seq_len, FOX_HEADS, FOX_DIM),
            fox_logf=p3(pp["small"])[:, :, :FOX_HEADS],
            diff_k=p3(pp["dk"]).reshape(n_seq, seq_len, DIFF_HEADS, 2, DIFF_QK_DIM),
            diff_v=p3(pp["dv"]).reshape(n_seq, seq_len, DIFF_HEADS, DIFF_V_DIM),
            delta=jnp.stack([sbd[:, h * 64:(h + 1) * 64, h * 64:(h + 1) * 64] for h in range(DELTA_HEADS)], axis=1),
            conv=p3(pp["cqkv"])[:, seq_len - (CONV_WIDTH - 1):]))
        new_s.append(dict(
            fox_k=s3(ps["fk"]).reshape(n_smp, dec_seq, FOX_HEADS, FOX_DIM),
            fox_v=s3(ps["fv"]).reshape(n_smp, dec_seq, FOX_HEADS, FOX_DIM),
            fox_logf=s3(ps["small"])[:, :, :FOX_HEADS],
            diff_k=s3(ps["dk"]).reshape(n_smp, dec_seq, DIFF_HEADS, 2, DIFF_QK_DIM),
            diff_v=s3(ps["dv"]).reshape(n_smp, dec_seq, DIFF_HEADS, DIFF_V_DIM),
            delta=s_new.reshape(n_smp, DELTA_HEADS, DELTA_K_DIM, DELTA_V_DIM),
            conv=s3(ps["cqkv"])[:, dec_seq - (CONV_WIDTH - 1):]))

    def stk(lst, name):
        return jnp.stack([dd[name] for dd in lst])

    y_prompt = xp.reshape(n_seq, seq_len, d)
    y_sample = xs.reshape(n_smp, SROWS, d)[:, :dec_seq]
    return (y_prompt, y_sample,
            stk(new_p, "fox_k"), stk(new_s, "fox_k"),
            stk(new_p, "fox_v"), stk(new_s, "fox_v"),
            stk(new_p, "fox_logf"), stk(new_s, "fox_logf"),
            stk(new_p, "diff_k"), stk(new_s, "diff_k"),
            stk(new_p, "diff_v"), stk(new_s, "diff_v"),
            stk(new_p, "delta"), stk(new_s, "delta"),
            stk(new_p, "conv"), stk(new_s, "conv"))
```

```python
import functools
import math

import numpy as np
import jax
import jax.numpy as jnp
from jax import lax
from jax.experimental import pallas as pl
from jax.experimental.pallas import tpu as pltpu

F32 = jnp.float32
BF16 = jnp.bfloat16

FOX_HEADS = 4
FOX_DIM = 64
DIFF_HEADS = 4
DIFF_QK_DIM = 32
DIFF_V_DIM = 64
DELTA_HEADS = 4
DELTA_K_DIM = 64
DELTA_V_DIM = 64
CONV_WIDTH = 4
DELTA_CHUNK = 64
N_BRANCH = 3
HEAD_W = 256
PAGE_SIZE = 128
ROPE_THETA = 10000.0
NORM_EPS = 1e-6
LOG2E = 1.4426950408889634
FOX_QSCALE = FOX_DIM ** -0.5 * LOG2E
DIFF_QSCALE = DIFF_QK_DIM ** -0.5 * LOG2E

V7X_VMEM_LIMIT_BYTES = 56 * 1024 * 1024
AUG_C = 64
AUG_K = 67
ONES_LANE = 64


def _cparams(*sem):
    return pltpu.CompilerParams(dimension_semantics=sem, vmem_limit_bytes=V7X_VMEM_LIMIT_BYTES)


def _dot(a, b):
    return jnp.dot(a, b, preferred_element_type=F32)


def _dot_nt(a, b):
    return lax.dot_general(a, b, (((1,), (1,)), ((), ())), preferred_element_type=F32)


def _dot_tn(a, b):
    return lax.dot_general(a, b, (((0,), (0,)), ((), ())), preferred_element_type=F32)


def _split3(x):
    hi = x.astype(BF16)
    r = x - hi.astype(F32)
    mid = r.astype(BF16)
    lo = (r - mid.astype(F32)).astype(BF16)
    return hi, mid, lo


def _dot3_r(x, m):
    hi, mid, lo = _split3(x)
    return _dot(hi, m) + _dot(mid, m) + _dot(lo, m)


def _dot3_l(m, x):
    hi, mid, lo = _split3(x)
    return _dot(m, hi) + _dot(m, mid) + _dot(m, lo)


def _rms(x, gain):
    ms = jnp.mean(x * x, axis=-1, keepdims=True)
    return x * lax.rsqrt(ms + NORM_EPS) * gain


def _const_spec(shape):
    nd = len(shape)
    return pl.BlockSpec(shape, lambda *_: (0,) * nd, pipeline_mode=pl.Buffered(1))


def _ffn_body(x_ref, g_ref, wg_ref, wu_ref, wo_ref, gf_ref, o_ref, h_scr, acc_scr, *, final_norm):
    f = pl.program_id(1)

    @pl.when(f == 0)
    def _():
        h_scr[...] = _rms(x_ref[...], g_ref[...]).astype(BF16)
        acc_scr[...] = jnp.zeros_like(acc_scr)

    h = h_scr[...]
    gate = _dot(h, wg_ref[...])
    up = _dot(h, wu_ref[...])
    act = (gate * jax.nn.sigmoid(gate) * up).astype(BF16)
    acc_scr[...] += _dot(act, wo_ref[...])

    @pl.when(f == pl.num_programs(1) - 1)
    def _():
        y = x_ref[...] + 0.5 * acc_scr[...]
        if final_norm:
            y = _rms(y, gf_ref[...])
        o_ref[...] = y


def _ffn(x, gain, wi, wo, layer, final_gain=None):
    r, d = x.shape
    dff = wo.shape[1]
    tm = min(r, 512)
    tf = dff // 2 if (dff // 2) % 128 == 0 else dff
    nf = dff // tf
    final_norm = final_gain is not None
    gf = final_gain if final_norm else gain
    return pl.pallas_call(
        functools.partial(_ffn_body, final_norm=final_norm),
        out_shape=jax.ShapeDtypeStruct((r, d), F32),
        grid=(r // tm, nf),
        in_specs=[
            pl.BlockSpec((tm, d), lambda i, f: (i, 0)),
            pl.BlockSpec((1, d), lambda i, f: (0, 0)),
            pl.BlockSpec((None, d, tf), lambda i, f: (layer, 0, f)),
            pl.BlockSpec((None, d, tf), lambda i, f: (layer, 0, nf + f)),
            pl.BlockSpec((None, tf, d), lambda i, f: (layer, f, 0)),
            pl.BlockSpec((1, d), lambda i, f: (0, 0)),
        ],
        out_specs=pl.BlockSpec((tm, d), lambda i, f: (i, 0)),
        scratch_shapes=[pltpu.VMEM((tm, d), BF16), pltpu.VMEM((tm, d), F32)],
        compiler_params=_cparams("arbitrary", "arbitrary"),
        name="ffn",
    )(x, gain.reshape(1, d), wi, wi, wo, gf.reshape(1, d))


def _rope_body(inv_ref, cos_ref, sin_ref, *, tm, base, period):
    i = pl.program_id(0)
    row = lax.broadcasted_iota(jnp.int32, (tm, HEAD_W), 0) + i * tm
    lane = lax.broadcasted_iota(jnp.int32, (tm, HEAD_W), 1)
    pos = base + (row & (period - 1))
    ang = pos.astype(F32) * inv_ref[...]
    cos_ref[...] = jnp.cos(ang)
    s = jnp.sin(ang)
    first_half = (lane & (DIFF_QK_DIM - 1)) < DIFF_QK_DIM // 2
    sin_ref[...] = jnp.where(first_half, -s, s)


def _rope_tables(rows, base, period):
    assert period & (period - 1) == 0
    half = DIFF_QK_DIM // 2
    inv = ROPE_THETA ** (-jnp.arange(0, DIFF_QK_DIM, 2, dtype=F32) / DIFF_QK_DIM)
    inv_lanes = jnp.tile(inv, HEAD_W // half).reshape(1, HEAD_W)
    tm = min(rows, 512)
    return pl.pallas_call(
        functools.partial(_rope_body, tm=tm, base=base, period=period),
        out_shape=(jax.ShapeDtypeStruct((rows, HEAD_W), F32),) * 2,
        grid=(rows // tm,),
        in_specs=[pl.BlockSpec((1, HEAD_W), lambda i: (0, 0))],
        out_specs=(pl.BlockSpec((tm, HEAD_W), lambda i: (i, 0)),) * 2,
        compiler_params=_cparams("arbitrary"),
        name="rope_tables",
    )(inv_lanes)


_PROMPT_COLS = (("fq_pad", 512), ("fk_pad", 512), ("fk", 256),
                ("dq", 256), ("dqs", 256), ("dk", 256), ("dks", 256),
                ("cqkv", 768), ("cz", 256), ("cb_rep", 256), ("ca_rep", 256), ("small", 128))
_SAMPLE_COLS = (("fq", 256), ("fk", 256), ("fv", 256),
                ("dq", 256), ("dqs", 256), ("dk", 256), ("dks", 256), ("dv", 256),
                ("cqkv", 768), ("cz", 256), ("cb_rep", 256), ("ca_rep", 256), ("small", 128))


def _layout(cols):
    lay, off = {}, 0
    for name, width in cols:
        lay[name] = (off, width)
        off += width
    return lay, off


def _relayout_w_in(w_in, cols):
    depth, d, _ = w_in.shape
    sizes = (256, 256, 256, 4, 256, 256, 256, 768, 4, 4, 256)
    offs = np.concatenate([[0], np.cumsum(sizes)])
    fq, fk, fv, ff, dq, dk, dv, cqkv, cb, ca, cz = (w_in[:, :, int(offs[i]):int(offs[i + 1])] for i in range(11))

    def pad_heads(w):
        w = w.reshape(depth, d, 4, 64)
        return jnp.pad(w, ((0, 0), (0, 0), (0, 0), (0, 64))).reshape(depth, d, 512)

    def swap_halves(w):
        w = w.reshape(depth, d, DIFF_HEADS * 2, 2, DIFF_QK_DIM // 2)
        return w[:, :, :, ::-1, :].reshape(depth, d, 256)

    def rep(w):
        return jnp.repeat(w, 64, axis=2)

    table = {
        "fq": fq, "fk": fk, "fv": fv, "fq_pad": pad_heads(fq), "fk_pad": pad_heads(fk),
        "dq": dq, "dqs": swap_halves(dq), "dk": dk, "dks": swap_halves(dk), "dv": dv,
        "cqkv": cqkv, "cz": cz, "cb_rep": rep(cb), "ca_rep": rep(ca),
        "small": jnp.pad(ff, ((0, 0), (0, 0), (0, 124))),
    }
    return jnp.concatenate([table[name] for name, _ in cols], axis=2).astype(BF16)


def _relayout_w_values_t(w_in):
    depth, d, _ = w_in.shape

    def head_rows(w):
        wt = jnp.swapaxes(w, 1, 2).reshape(depth, 4, 64, d)
        return jnp.pad(wt, ((0, 0), (0, 0), (0, 64), (0, 0))).reshape(depth, 512, d)

    fv = w_in[:, :, 512:768]
    dv = w_in[:, :, 1284:1540]
    return jnp.concatenate([head_rows(fv), head_rows(dv)], axis=1).astype(BF16)


def _aug_place_mats():
    pq = np.zeros((384, 512), np.float32)
    pk = np.zeros((384, 512), np.float32)
    for piece in range(3):
        for h in range(FOX_HEADS):
            pq[piece * 128 + h, h * 128 + AUG_C + piece] = 1.0
            pk[piece * 128 + h, h * 128 + AUG_K + piece] = -1.0
    return jnp.asarray(pq, BF16), jnp.asarray(pk, BF16)


def _lane_ones(shape, lo, hi):
    lane = lax.broadcasted_iota(jnp.int32, shape, 1) & 127
    mid2 = lo + hi
    return jnp.where(jnp.abs(2 * lane - mid2) <= hi - lo, 1.0, 0.0).astype(F32)


def _inproj_body(*refs, lay, prompt, tiles_per_seq, tm, n_alias=0):
    if prompt:
        (x_ref, g_ref, w_ref, cos_ref, sin_ref, fb_ref, alog_ref, dtb_ref, tri_ref, pq_ref, pk_ref, wvt_ref) = refs[:12]
        (fq_o, fk_pad_o, fvt_o, small_o, dq_o, dkb_o, dvt_o, cqkv_o, cz_o, beta_o, g_o,
         fkt_o, fvt32_o, dkt_o, dvt32_o, carry_scr) = refs[12 + n_alias:]
    else:
        (x_ref, g_ref, w_ref, cos_ref, sin_ref, fb_ref, alog_ref, dtb_ref,
         fq_o, fk_o, fv_o, small_o, dq_o, dk_o, dv_o, cqkv_o, cz_o, beta_o, g_o) = refs

    h = _rms(x_ref[...], g_ref[...]).astype(BF16)

    def proj(name):
        off, width = lay[name]
        return _dot(h, w_ref[:, off:off + width])

    logf = jax.nn.log_sigmoid(proj("small") + fb_ref[...])
    small_o[...] = logf

    if prompt:
        i = pl.program_id(0)

        @pl.when(i % tiles_per_seq == 0)
        def _():
            carry_scr[...] = jnp.zeros_like(carry_scr)

        c = _dot3_l(tri_ref[...], logf) + carry_scr[...]
        carry_scr[...] = c[tm - 1:tm, :]
        c1, c2, c3 = _split3(c * LOG2E)
        c123 = jnp.concatenate([c1, c2, c3], axis=1)
        shape = (tm, 512)
        q = proj("fq_pad") * FOX_QSCALE + _dot(c123, pq_ref[...]) + _lane_ones(shape, AUG_K, AUG_K + 2)
        fq_o[...] = q.astype(BF16)
        k = proj("fk_pad") + _dot(c123, pk_ref[...]) + _lane_ones(shape, AUG_C, AUG_C + 2)
        fk_pad_o[...] = k.astype(BF16)
        vt = _dot_nt(wvt_ref[...], h)
        for hd in range(4):
            fvt32_o[hd * 64:(hd + 1) * 64, :] = vt[hd * 128:hd * 128 + 64]
            dvt32_o[hd * 64:(hd + 1) * 64, :] = vt[512 + hd * 128:512 + hd * 128 + 64]
        row = lax.broadcasted_iota(jnp.int32, vt.shape, 0) & 127
        vt = (vt + jnp.where(row == ONES_LANE, 1.0, 0.0)).astype(BF16)
        fvt_o[...] = vt[0:512]
        dvt_o[...] = vt[512:1024]
        fkt_o[...] = proj("fk").T
    else:
        fq_o[...] = proj("fq") * FOX_QSCALE
        fk_o[...] = proj("fk")
        fv_o[...] = proj("fv")
        dv_o[...] = proj("dv")

    cos = cos_ref[...]
    sin = sin_ref[...]
    dq_o[...] = ((proj("dq") * cos + proj("dqs") * sin) * DIFF_QSCALE).astype(dq_o.dtype)
    dk = proj("dk") * cos + proj("dks") * sin
    if prompt:
        dkb_o[...] = dk.astype(BF16)
        dkt_o[...] = dk.T
    else:
        dk_o[...] = dk

    cqkv_o[...] = proj("cqkv")
    cz = proj("cz")
    cz_o[...] = cz * jax.nn.sigmoid(cz)
    beta_o[...] = jax.nn.sigmoid(proj("cb_rep"))
    g_o[...] = -jnp.exp(alog_ref[...]) * jax.nn.softplus(proj("ca_rep") + dtb_ref[...])


_CACHE_OUTS = ("fkt", "fvt32", "dkt", "dvt32")


def _inproj(x, gain, w, layer, cos, sin, fbias, alog, dtb, *, prompt, seq_len, w_vt=None, cache_bufs=None):
    r, d = x.shape
    lay, nc = _layout(_PROMPT_COLS if prompt else _SAMPLE_COLS)
    assert w.shape[2] == nc
    tm = min(r, 512 if prompt else 256)
    tm = min(tm, seq_len) if prompt else tm
    fb = jnp.pad(fbias.astype(F32), (0, 124)).reshape(1, 128)
    alog_l = jnp.repeat(alog.astype(F32), 64).reshape(1, 256)
    dtb_l = jnp.repeat(dtb.astype(F32), 64).reshape(1, 256)

    def rows(width):
        return pl.BlockSpec((tm, width), lambda i: (i, 0))

    in_specs = [rows(d), _const_spec((1, d)),
                pl.BlockSpec((None, d, nc), lambda i: (layer, 0, 0), pipeline_mode=pl.Buffered(1)),
                rows(256), rows(256), _const_spec((1, 128)), _const_spec((1, 256)), _const_spec((1, 256))]
    args = [x, gain.reshape(1, d), w, cos, sin, fb, alog_l, dtb_l]
    if prompt:
        tri = jnp.asarray(np.tril(np.ones((tm, tm), np.float32)), BF16)
        pq, pk = _aug_place_mats()
        in_specs += [_const_spec((tm, tm)), _const_spec((384, 512)), _const_spec((384, 512)),
                     pl.BlockSpec((None, 1024, d), lambda i: (layer, 0, 0), pipeline_mode=pl.Buffered(1))]
        args += [tri, pq, pk, w_vt]
        outs = (("fq_pad", 512, BF16), ("fk_pad", 512, BF16), ("fvt", 0, BF16), ("small", 128, F32),
                ("dq", 256, BF16), ("dkb", 256, BF16), ("dvt", 0, BF16), ("cqkv", 768, F32), ("cz", 256, F32),
                ("beta", 256, F32), ("g", 256, F32))
        scratch = [pltpu.VMEM((1, 128), F32)]
    else:
        outs = (("fq", 256, F32), ("fk", 256, F32), ("fv", 256, F32), ("small", 128, F32), ("dq", 256, F32),
                ("dk", 256, F32), ("dv", 256, F32), ("cqkv", 768, F32), ("cz", 256, F32),
                ("beta", 256, F32), ("g", 256, F32))
        scratch = []
    tps = max(seq_len // tm, 1)
    out_shape = [jax.ShapeDtypeStruct((r, wd) if wd else (512, r), dt) for _, wd, dt in outs]
    out_specs = [rows(wd) if wd else pl.BlockSpec((512, tm), lambda i: (0, i)) for _, wd, _ in outs]
    names = [name for name, _, _ in outs]
    aliases = {}
    if prompt:
        for j, name in enumerate(_CACHE_OUTS):
            buf = cache_bufs[name]
            out_shape.append(jax.ShapeDtypeStruct(buf.shape, buf.dtype))
            out_specs.append(pl.BlockSpec((None, None, HEAD_W, tm), lambda i: (layer, i // tps, 0, i % tps)))
            names.append(name)
            aliases[len(args)] = len(outs) + j
            in_specs.append(pl.BlockSpec(memory_space=pl.ANY))
            args.append(buf)
    res = pl.pallas_call(
        functools.partial(_inproj_body, lay=lay, prompt=prompt, tiles_per_seq=tps, tm=tm, n_alias=len(aliases)),
        out_shape=tuple(out_shape),
        grid=(r // tm,),
        in_specs=in_specs,
        out_specs=tuple(out_specs),
        scratch_shapes=scratch,
        input_output_aliases=aliases,
        compiler_params=_cparams("arbitrary"),
        name="inproj_prompt" if prompt else "inproj_sample",
    )(*args)
    return dict(zip(names, res))


def _tri_schedule(nq):
    qi = np.concatenate([np.full(i + 1, i, np.int32) for i in range(nq)])
    kj = np.concatenate([np.arange(i + 1, dtype=np.int32) for i in range(nq)])
    return jnp.asarray(qi), jnp.asarray(kj)


def _online_softmax_step(st, vt, m_ref, acc_ref):
    m_old = m_ref[...]
    m_new = jnp.maximum(m_old, jnp.max(st, axis=0, keepdims=True))
    alpha = jnp.exp2(m_old - m_new)
    p = jnp.exp2(st - m_new)
    acc_ref[...] = alpha * acc_ref[...] + _dot(vt, p.astype(BF16))
    m_ref[...] = m_new


def _causal_mask(st):
    key = lax.broadcasted_iota(jnp.int32, st.shape, 0)
    qry = lax.broadcasted_iota(jnp.int32, st.shape, 1)
    return jnp.where(key <= qry, st, -jnp.inf)


def _diff_lambda(lam_ref, lam_init):
    lp = lam_ref[...]
    s1 = jnp.sum(lp[0:1] * lp[1:2], axis=1, keepdims=True)
    s2 = jnp.sum(lp[2:3] * lp[3:4], axis=1, keepdims=True)
    return jnp.exp(s1) - jnp.exp(s2) + lam_init


N_ATTN_MAPS = FOX_HEADS + 2 * DIFF_HEADS


def _attn_body(qi_ref, kj_ref, fq_ref, fk_ref, fvt_ref, dq_ref, dk_ref, dvt_ref, lam_ref, gain_ref,
               of_ref, od_ref, qm_scr, m_scr, acc_scr, *, lam_init):
    t = pl.program_id(1)
    qi = qi_ref[t]
    kj = kj_ref[t]
    n_diff = 2 * DIFF_HEADS

    @pl.when(kj == 0)
    def _():
        m_scr[...] = jnp.full_like(m_scr, -jnp.inf)
        acc_scr[...] = jnp.zeros_like(acc_scr)
        q = dq_ref[...]
        lane_map = lax.broadcasted_iota(jnp.int32, q.shape, 1) // DIFF_QK_DIM
        for m in range(n_diff):
            qm_scr[m] = jnp.where(lane_map == m, q, jnp.zeros_like(q))

    def scores(j):
        if j < FOX_HEADS:
            sl = slice(j * 128, (j + 1) * 128)
            return _dot_nt(fk_ref[:, sl], fq_ref[:, sl])
        return _dot_nt(dk_ref[...], qm_scr[j - FOX_HEADS])

    def values_t(j):
        if j < FOX_HEADS:
            return fvt_ref[j * 128:(j + 1) * 128, :]
        h = (j - FOX_HEADS) // 2
        return dvt_ref[h * 128:(h + 1) * 128, :]

    def step(masked):
        st_next = scores(0)
        for j in range(N_ATTN_MAPS):
            st = st_next
            if j + 1 < N_ATTN_MAPS:
                st_next = scores(j + 1)
            if masked:
                st = _causal_mask(st)
            _online_softmax_step(st, values_t(j), m_scr.at[j], acc_scr.at[j])

    @pl.when(kj < qi)
    def _():
        step(False)

    @pl.when(kj == qi)
    def _():
        step(True)
        for h in range(FOX_HEADS):
            a = acc_scr[h]
            o = a[:FOX_DIM] / a[ONES_LANE:ONES_LANE + 1]
            of_ref[:, h * FOX_DIM:(h + 1) * FOX_DIM] = o.T.astype(of_ref.dtype)
        lam = _diff_lambda(lam_ref, lam_init)
        tq = od_ref.shape[0]
        gain = jnp.concatenate([gain_ref[...]] * (tq // 128), axis=1)
        for h in range(DIFF_HEADS):
            a0 = acc_scr[FOX_HEADS + 2 * h]
            a1 = acc_scr[FOX_HEADS + 2 * h + 1]
            o = (a0[:DIFF_V_DIM] / a0[ONES_LANE:ONES_LANE + 1]
                 - lam * (a1[:DIFF_V_DIM] / a1[ONES_LANE:ONES_LANE + 1]))
            ms = jnp.mean(o * o, axis=0, keepdims=True)
            o = o * lax.rsqrt(ms + NORM_EPS) * gain * (1.0 - lam_init)
            od_ref[:, h * DIFF_V_DIM:(h + 1) * DIFF_V_DIM] = o.T.astype(od_ref.dtype)


def _attn_prompt(fq_pad, fk_pad, fvt_pad, dq, dk, dvt_pad, lam_p, gain, lam_init, n_seq, seq_len):
    tq = min(seq_len, 512)
    nq = seq_len // tq
    qi, kj = _tri_schedule(nq)

    def q_rows(width):
        return pl.BlockSpec((tq, width), lambda n, t, qi, kj: (n * nq + qi[t], 0))

    def k_rows(width):
        return pl.BlockSpec((tq, width), lambda n, t, qi, kj: (n * nq + kj[t], 0))

    def k_cols():
        return pl.BlockSpec((512, tq), lambda n, t, qi, kj: (0, n * nq + kj[t]))

    gs = pltpu.PrefetchScalarGridSpec(
        num_scalar_prefetch=2,
        grid=(n_seq, int(qi.shape[0])),
        in_specs=[q_rows(512), k_rows(512), k_cols(), q_rows(HEAD_W), k_rows(HEAD_W), k_cols(),
                  pl.BlockSpec((4, DIFF_QK_DIM), lambda n, t, qi, kj: (0, 0)),
                  pl.BlockSpec((DIFF_V_DIM, 128), lambda n, t, qi, kj: (0, 0))],
        out_specs=(q_rows(HEAD_W), q_rows(HEAD_W)),
        scratch_shapes=[pltpu.VMEM((2 * DIFF_HEADS, tq, HEAD_W), BF16), pltpu.VMEM((N_ATTN_MAPS, 1, tq), F32),
                        pltpu.VMEM((N_ATTN_MAPS, 128, tq), F32)],
    )
    gain_rows = jnp.broadcast_to(gain.astype(F32).reshape(DIFF_V_DIM, 1), (DIFF_V_DIM, 128))
    return pl.pallas_call(
        functools.partial(_attn_body, lam_init=lam_init),
        out_shape=(jax.ShapeDtypeStruct((n_seq * seq_len, HEAD_W), BF16),) * 2,
        grid_spec=gs,
        compiler_params=_cparams("arbitrary", "arbitrary"),
        name="attn_prompt",
    )(qi, kj, fq_pad, fk_pad, fvt_pad, dq, dk, dvt_pad, lam_p.astype(F32), gain_rows)


def _merge_body(x_ref, g_ref, wg_ref, wb_ref, wo_ref, b0_ref, b1_ref, b2_ref, o_ref):
    x = x_ref[...]
    d = x.shape[1]
    h = _rms(x, g_ref[...]).astype(BF16)
    mix = None
    for b, b_ref in enumerate((b0_ref, b1_ref, b2_ref)):
        gate = jax.nn.sigmoid(_dot(h, wg_ref[:, b * d:(b + 1) * d]))
        term = gate * _dot(b_ref[...].astype(BF16), wb_ref[b])
        mix = term if mix is None else mix + term
    o_ref[...] = x + _dot(mix.astype(BF16), wo_ref[...])


def _merge(x, gain, w_gates, w_branch, w_out, layer, branches):
    r, d = x.shape
    tm = min(r, 512)

    def rows(width):
        return pl.BlockSpec((tm, width), lambda i: (i, 0))

    return pl.pallas_call(
        _merge_body,
        out_shape=jax.ShapeDtypeStruct((r, d), F32),
        grid=(r // tm,),
        in_specs=[rows(d), _const_spec((1, d)),
                  pl.BlockSpec((None, d, N_BRANCH * d), lambda i: (layer, 0, 0), pipeline_mode=pl.Buffered(1)),
                  pl.BlockSpec((None, N_BRANCH, HEAD_W, d), lambda i: (layer, 0, 0, 0), pipeline_mode=pl.Buffered(1)),
                  pl.BlockSpec((None, d, d), lambda i: (layer, 0, 0), pipeline_mode=pl.Buffered(1)),
                  rows(HEAD_W), rows(HEAD_W), rows(HEAD_W)],
        out_specs=rows(d),
        compiler_params=_cparams("arbitrary"),
        name="merge",
    )(x, gain.reshape(1, d), w_gates, w_branch, w_out, *branches)


def _delta_consts():
    c = DELTA_CHUNK
    i = np.arange(c)
    lane = np.arange(HEAD_W)
    lh, lj = lane // c, lane % c
    tril = i[:, None] >= i[None, :]
    suffix = i[None, :] > i[:, None]
    lvl = []
    for k in range(1, 7):
        b = 1 << k
        lvl.append((i[:, None] // b == lj[None, :] // b) & (i[:, None] // (b // 2) != lj[None, :] // (b // 2)))
    f = lambda a, dt: jnp.asarray(np.asarray(a, np.float32), dt)
    return dict(
        lhsg=f(np.concatenate([tril, suffix], axis=0), BF16),
        ones=f(np.ones((c, c)), BF16),
        triu4=f(i[:, None] <= lj[None, :], F32),
        tril4=f(i[:, None] >= lj[None, :], F32),
        stril4=f(i[:, None] > lj[None, :], F32),
        eye4=f(i[:, None] == lj[None, :], F32),
        lvl=f(np.stack(lvl), F32),
        maskbd=f(lh[:, None] == lh[None, :], BF16),
    )


def _bd(x4, maskbd):
    return jnp.concatenate([x4, x4, x4, x4], axis=0) * maskbd


def _mm_hi(xs, ys, maskbd):
    xh = [x.astype(BF16) for x in xs]
    xl = [(x - h.astype(F32)).astype(BF16) for x, h in zip(xs, xh)]
    yh = [y.astype(BF16) for y in ys]
    yl = [(y - h.astype(F32)).astype(BF16) for y, h in zip(ys, yh)]
    ybh = [_bd(h, maskbd) for h in yh]
    ybl = [_bd(l, maskbd) for l in yl]
    return [_dot(a, bh) + _dot(al, bh) + _dot(a, bl) for a, al, bh, bl in zip(xh, xl, ybh, ybl)]


def _unit_lower_inverse(a4s, eye4, lvl_ref, maskbd):
    ts = [eye4 - a4 * lvl_ref[0] for a4 in a4s]
    for k in range(1, 6):
        ms = [a4 * lvl_ref[k] for a4 in a4s]
        xs = _mm_hi(ms, ts, maskbd)
        ts = [t - d for t, d in zip(ts, _mm_hi(ts, xs, maskbd))]
    return ts


def _head_sums(x, maskbd):
    return _dot3_r(x, maskbd)


def _delta_body(cqkv_ref, beta_ref, g_ref, cz_ref, cw_ref, gain_ref,
                lhsg_ref, ones_ref, triu4_ref, tril4_ref, stril4_ref, eye4_ref, lvl_ref, maskbd_ref,
                o_ref, s_ref, xbuf, ub_scr, w_scr, qk_scr, qe_scr, kd_scr, eg_scr, o_scr, sbd_scr, *, tc):
    j = pl.program_id(1)
    c = DELTA_CHUNK
    maskbd = maskbd_ref[...]
    maskf = maskbd.astype(F32)

    @pl.when(j == 0)
    def _():
        xbuf[0:8, :] = jnp.zeros((8, xbuf.shape[1]), F32)
        sbd_scr[...] = jnp.zeros_like(sbd_scr)

    xbuf[8:8 + tc, :] = cqkv_ref[...]
    cw = cw_ref[...]
    y = cw[3:4] * xbuf[8:8 + tc, :]
    for i in range(1, CONV_WIDTH):
        y = y + cw[3 - i:4 - i] * xbuf[8 - i:8 - i + tc, :]
    xbuf[0:8, :] = xbuf[tc:tc + 8, :]
    y = y * jax.nn.sigmoid(y)
    q = y[:, 0:HEAD_W]
    k = y[:, HEAD_W:2 * HEAD_W]
    qn = q * lax.rsqrt(_head_sums(q * q, maskbd) + 1e-6) * (DELTA_K_DIM ** -0.5)
    kn = k * lax.rsqrt(_head_sums(k * k, maskbd) + 1e-6)
    v = y[:, 2 * HEAD_W:3 * HEAD_W]
    beta = beta_ref[...]
    g = g_ref[...]

    cs = [slice(ci * c, (ci + 1) * c) for ci in range(tc // c)]
    qcs, kcs, vcs = [qn[s] for s in cs], [kn[s] for s in cs], [v[s] for s in cs]
    bcs, gcs = [beta[s] for s in cs], [g[s] for s in cs]
    gsums = [_dot3_l(lhsg_ref[...], gc) for gc in gcs]
    grows = [_dot3_l(ones_ref[...], gc * triu4_ref[...]) for gc in gcs]
    gcols, gsufs = [gs[0:c] for gs in gsums], [gs[c:2 * c] for gs in gsums]
    decays = [jnp.where(tril4_ref[...] > 0, jnp.exp(gcol - grow), 0.0) for gcol, grow in zip(gcols, grows)]
    exp_gs = [jnp.exp(gcol) for gcol in gcols]
    kbs = [kc * bc for kc, bc in zip(kcs, bcs)]
    kbds = [_bd(kc.astype(BF16), maskbd) for kc in kcs]
    a4s = [_dot_nt(kb.astype(BF16), kbd) * decay * stril4_ref[...] for kb, kbd, decay in zip(kbs, kbds, decays)]
    qk4s = [_dot_nt(qc.astype(BF16), kbd) * decay for qc, kbd, decay in zip(qcs, kbds, decays)]
    t16s = [t.astype(BF16) for t in _unit_lower_inverse(a4s, eye4_ref[...], lvl_ref, maskbd)]
    u_bases = [_dot(t16, _bd((vc * bc).astype(BF16), maskbd)) for t16, vc, bc in zip(t16s, vcs, bcs)]
    ws = [_dot(t16, _bd((kb * eg).astype(BF16), maskbd)) for t16, kb, eg in zip(t16s, kbs, exp_gs)]
    ub_scr[...] = jnp.concatenate(u_bases, axis=0)
    w_scr[...] = jnp.concatenate([w.astype(BF16) for w in ws], axis=0)
    qk_scr[...] = jnp.concatenate([x.astype(BF16) for x in qk4s], axis=0)
    qe_scr[...] = jnp.concatenate([(qc * eg).astype(BF16) for qc, eg in zip(qcs, exp_gs)], axis=0)
    kd_scr[...] = jnp.concatenate([(kc * jnp.exp(gs)).astype(BF16) for kc, gs in zip(kcs, gsufs)], axis=0)
    eg_scr[...] = jnp.concatenate([jnp.broadcast_to(eg[c - 1:c, :], (8, HEAD_W)) for eg in exp_gs], axis=0)

    def scan(ci, carry):
        sl = pl.ds(pl.multiple_of(ci * c, c), c)
        s_old = sbd_scr[...]
        s16 = s_old.astype(BF16)
        u16 = (ub_scr[sl, :] - _dot(w_scr[sl, :], s16)).astype(BF16)
        o_scr[sl, :] = _dot(qe_scr[sl, :], s16) + _dot(qk_scr[sl, :], _bd(u16, maskbd))
        exp_g_last = eg_scr[pl.ds(pl.multiple_of(ci * 8, 8), 8), :][0:1]
        sbd_scr[...] = s_old * exp_g_last + maskf * _dot_tn(kd_scr[sl, :], u16)
        return carry

    lax.fori_loop(0, tc // c, scan, 0)

    o = o_scr[...]
    ms = _head_sums(o * o, maskbd) * (1.0 / DELTA_V_DIM)
    o_ref[...] = (o * lax.rsqrt(ms + NORM_EPS) * gain_ref[...] * cz_ref[...]).astype(o_ref.dtype)

    @pl.when(j == pl.num_programs(1) - 1)
    def _():
        s_ref[...] = sbd_scr[...]


def _delta_prompt(cqkv, beta, g, cz, conv_w, gain, n_seq, seq_len):
    tc = min(seq_len, 512)
    nt = seq_len // tc
    cch = cqkv.shape[1]
    consts = _delta_consts()
    names = ("lhsg", "ones", "triu4", "tril4", "stril4", "eye4", "lvl", "maskbd")

    def rows(width):
        return pl.BlockSpec((tc, width), lambda n, j: (n * nt + j, 0))

    return pl.pallas_call(
        functools.partial(_delta_body, tc=tc),
        out_shape=(jax.ShapeDtypeStruct((n_seq * seq_len, HEAD_W), BF16),
                   jax.ShapeDtypeStruct((n_seq, HEAD_W, HEAD_W), F32)),
        grid=(n_seq, nt),
        in_specs=[rows(cch), rows(HEAD_W), rows(HEAD_W), rows(HEAD_W),
                  _const_spec((CONV_WIDTH, cch)), _const_spec((1, HEAD_W))]
                 + [_const_spec(consts[nm].shape) for nm in names],
        out_specs=(rows(HEAD_W), pl.BlockSpec((None, HEAD_W, HEAD_W), lambda n, j: (n, 0, 0))),
        scratch_shapes=[pltpu.VMEM((tc + 8, cch), F32), pltpu.VMEM((tc, HEAD_W), F32),
                        pltpu.VMEM((tc, HEAD_W), BF16), pltpu.VMEM((tc, HEAD_W), BF16),
                        pltpu.VMEM((tc, HEAD_W), BF16), pltpu.VMEM((tc, HEAD_W), BF16),
                        pltpu.VMEM((8 * (tc // DELTA_CHUNK), HEAD_W), F32),
                        pltpu.VMEM((tc, HEAD_W), F32), pltpu.VMEM((HEAD_W, HEAD_W), F32)],
        compiler_params=_cparams("arbitrary", "arbitrary"),
        name="delta_prompt",
    )(cqkv, beta, g, cz, conv_w.astype(F32), jnp.tile(gain.astype(F32), DELTA_HEADS).reshape(1, HEAD_W),
      *[consts[nm] for nm in names])


SROWS = 8


def _sample_attn_consts(n_pages):
    r = np.arange(PAGE_SIZE)
    m2 = r[:, None] > r[None, :]
    m2ones = np.concatenate([m2, np.ones((PAGE_SIZE, PAGE_SIZE), bool)], axis=1)
    idx = np.arange(4 * n_pages)
    u2 = (idx[None, :] // 4 > idx[:, None] // 4) & (idx[None, :] % 4 == idx[:, None] % 4)
    t = np.arange(SROWS)
    tri8 = t[None, :] <= t[:, None]
    sup = t[:, None] > r[None, :]
    lh = np.arange(HEAD_W) // 64
    f = lambda a, dt: jnp.asarray(np.asarray(a, np.float32), dt)
    return dict(m2ones=f(m2ones, BF16), u2=f(u2, BF16), tri8=f(tri8, BF16), sup=f(sup, F32),
                maskbd=f(lh[:, None] == lh[None, :], BF16))


def _pad_rows(x, rows):
    return jnp.concatenate([x, jnp.zeros((rows - x.shape[0], x.shape[1]), x.dtype)], axis=0)


def _block_rows(x, groups, lanes_per_group):
    xr = jnp.concatenate([x] * groups, axis=0)
    rg = lax.broadcasted_iota(jnp.int32, xr.shape, 0) // SROWS
    lg = lax.broadcasted_iota(jnp.int32, xr.shape, 1) // lanes_per_group
    return jnp.where(rg == lg, xr, 0.0).astype(BF16)


def _paged_softmax_pv(qbd, kt_pages, vt_pages, k_new, v_new, bias_pages, bias_new):
    tiles = []
    for p, kt_ref in enumerate(kt_pages):
        s = _dot(qbd, kt_ref[...].astype(BF16))
        tiles.append(s if bias_pages is None else s + bias_pages[p])
    s = _dot_nt(qbd, _pad_rows(k_new, PAGE_SIZE).astype(BF16))
    if bias_new is not None:
        s = s + bias_new
    row = lax.broadcasted_iota(jnp.int32, s.shape, 0) & (SROWS - 1)
    col = lax.broadcasted_iota(jnp.int32, s.shape, 1)
    tiles.append(jnp.where(col <= row, s, -jnp.inf))
    m = tiles[0].max(axis=1, keepdims=True)
    for s in tiles[1:]:
        m = jnp.maximum(m, s.max(axis=1, keepdims=True))
    acc = None
    den = None
    for i, s in enumerate(tiles):
        p = jnp.exp2(s - m)
        ps = p.sum(axis=1, keepdims=True)
        if i < len(vt_pages):
            pv = _dot_nt(p.astype(BF16), vt_pages[i][...].astype(BF16))
        else:
            pv = _dot(p.astype(BF16), _pad_rows(v_new, PAGE_SIZE).astype(BF16))
        acc = pv if acc is None else acc + pv
        den = ps if den is None else den + ps
    return acc / den


def _own_head_rows(x, stride):
    lh = lax.broadcasted_iota(jnp.int32, (SROWS, HEAD_W), 1) // 64
    out = None
    for h in range(4):
        part = jnp.where(lh == h, x[h * stride:h * stride + SROWS], 0.0)
        out = part if out is None else out + part
    return out


def _sample_attn_body(pt_ref, *refs, n_pages, lam_init):
    del pt_ref
    groups = [refs[i * n_pages:(i + 1) * n_pages] for i in range(5)]
    fk_pages, fv_pages, lf_pages, dk_pages, dv_pages = groups
    (fq_ref, fkn_ref, fvn_ref, ln_ref, dq_ref, dkn_ref, dvn_ref,
     m2_ref, u2_ref, tri8_ref, sup_ref, lam_ref, gain_ref, maskbd_ref,
     of_ref, od_ref, xm_scr) = refs[5 * n_pages:]

    for p in range(n_pages):
        xm_scr[4 * p:4 * p + 4, :] = lf_pages[p][...]
    wt = _dot3_r(xm_scr[...], m2_ref[...])
    within, totals = wt[:, :PAGE_SIZE], wt[:, PAGE_SIZE:]
    d_past = (within + _dot3_l(u2_ref[...], totals)) * LOG2E
    lnew = ln_ref[...]
    tri8 = tri8_ref[...]
    cn = _dot3_l(tri8, lnew) * LOG2E
    cn_h = [jnp.broadcast_to(cn[:, h:h + 1], (SROWS, PAGE_SIZE)) for h in range(FOX_HEADS)]
    bias_pages = [
        jnp.concatenate([d_past[4 * p + h:4 * p + h + 1, :] + cn_h[h] for h in range(FOX_HEADS)], axis=0)
        for p in range(n_pages)]
    sup = sup_ref[...]
    bias_new = jnp.concatenate(
        [_dot3_l(tri8, jnp.broadcast_to(lnew[:, h:h + 1], (SROWS, PAGE_SIZE)) * sup) * LOG2E
         for h in range(FOX_HEADS)], axis=0)
    qbd = _block_rows(fq_ref[...], FOX_HEADS, FOX_DIM)
    o = _paged_softmax_pv(qbd, fk_pages, fv_pages, fkn_ref[...], fvn_ref[...], bias_pages, bias_new)
    of_ref[...] = _own_head_rows(o, SROWS)

    qbd = _block_rows(dq_ref[...], 2 * DIFF_HEADS, DIFF_QK_DIM)
    o = _paged_softmax_pv(qbd, dk_pages, dv_pages, dkn_ref[...], dvn_ref[...], None, None)
    lam = _diff_lambda(lam_ref, lam_init)
    o = _own_head_rows(o, 2 * SROWS) - lam * _own_head_rows(o[SROWS:], 2 * SROWS)
    ms = _head_sums(o * o, maskbd_ref[...]) * (1.0 / DIFF_V_DIM)
    od_ref[...] = o * lax.rsqrt(ms + NORM_EPS) * gain_ref[...] * (1.0 - lam_init)


def _sample_attn(page_table, caches, layer, proj, lam_p, gain, lam_init):
    n_samples, n_pages = page_table.shape
    consts = _sample_attn_consts(n_pages)

    def page_spec(arr, p):
        return pl.BlockSpec((None, None) + arr.shape[2:], lambda b, pt: (layer, pt[b, p], 0, 0))

    def tile(width):
        return pl.BlockSpec((SROWS, width), lambda b, pt: (b, 0))

    def const(shape):
        nd = len(shape)
        return pl.BlockSpec(shape, lambda b, pt: (0,) * nd, pipeline_mode=pl.Buffered(1))

    in_specs, args = [], []
    for arr in caches:
        for p in range(n_pages):
            in_specs.append(page_spec(arr, p))
            args.append(arr)
    new = (proj["fq"], proj["fk"], proj["fv"], proj["small"], proj["dq"], proj["dk"], proj["dv"])
    in_specs += [tile(a.shape[1]) for a in new]
    args += list(new)
    cvals = (consts["m2ones"], consts["u2"], consts["tri8"], consts["sup"], lam_p.astype(F32),
             jnp.tile(gain.astype(F32), DIFF_HEADS).reshape(1, HEAD_W), consts["maskbd"])
    in_specs += [const(c.shape) for c in cvals]
    args += list(cvals)
    gs = pltpu.PrefetchScalarGridSpec(
        num_scalar_prefetch=1, grid=(n_samples,), in_specs=in_specs,
        out_specs=(tile(HEAD_W), tile(HEAD_W)),
        scratch_shapes=[pltpu.VMEM((4 * n_pages, PAGE_SIZE), F32)])
    return pl.pallas_call(
        functools.partial(_sample_attn_body, n_pages=n_pages, lam_init=lam_init),
        out_shape=(jax.ShapeDtypeStruct((n_samples * SROWS, HEAD_W), F32),) * 2,
        grid_spec=gs,
        compiler_params=_cparams("arbitrary"),
        name="sample_attn",
    )(page_table, *args)


def _delta_sample_body(cqkv_ref, buf_ref, beta_ref, g_ref, cz_ref, cw_ref, gain_ref, s0_ref,
                       tile_ref, tile_t_ref, maskbd_ref, o_ref, s_ref, xp_scr, pr_scr, u_scr, kd_scr, o_scr,
                       *, n_tok, group):
    maskbd = maskbd_ref[...]
    maskf = maskbd.astype(F32)
    cch = xp_scr.shape[1]
    cw = cw_ref[...]
    ys = []
    for s in range(group):
        base = 16 * s
        xp_scr[base:base + 8, :] = jnp.zeros((8, cch), F32)
        xp_scr[base + 8 - (CONV_WIDTH - 1):base + 8, :] = buf_ref[s]
        xp_scr[base + 8:base + 16, :] = cqkv_ref[SROWS * s:SROWS * (s + 1), :]
        y = cw[3:4] * xp_scr[base + 8:base + 16, :]
        for i in range(1, CONV_WIDTH):
            y = y + cw[3 - i:4 - i] * xp_scr[base + 8 - i:base + 16 - i, :]
        ys.append(y)
    y = jnp.concatenate(ys, axis=0)
    y = y * jax.nn.sigmoid(y)
    q = y[:, 0:HEAD_W]
    k = y[:, HEAD_W:2 * HEAD_W]
    v = y[:, 2 * HEAD_W:3 * HEAD_W]
    q = q * lax.rsqrt(_head_sums(q * q, maskbd) + 1e-6) * (DELTA_K_DIM ** -0.5)
    k = k * lax.rsqrt(_head_sums(k * k, maskbd) + 1e-6)
    beta = beta_ref[...]
    a = jnp.exp(g_ref[...])

    pairs_k = [(t, j) for t in range(n_tok) for j in range(t)]
    pairs_q = [(t, j) for t in range(n_tok) for j in range(t + 1)]
    assert len(pairs_k) + len(pairs_q) <= 16
    pr_scr[...] = jnp.zeros_like(pr_scr)
    for s in range(group):
        r0, p0 = SROWS * s, 16 * s
        for r, (t, j) in enumerate(pairs_k):
            pr_scr[p0 + r:p0 + r + 1, :] = k[r0 + t:r0 + t + 1] * k[r0 + j:r0 + j + 1]
        for r, (t, j) in enumerate(pairs_q):
            rr = p0 + len(pairs_k) + r
            pr_scr[rr:rr + 1, :] = q[r0 + t:r0 + t + 1] * k[r0 + j:r0 + j + 1]
    dots = _head_sums(pr_scr[...], maskbd)

    rows = [slice(SROWS * s, SROWS * (s + 1)) for s in range(group)]
    sbd0s = [maskf * _dot3_r(s0_ref[s], tile_ref[...]) for s in range(group)]
    s16s = [sb.astype(BF16) for sb in sbd0s]
    rks = [_dot(k[r].astype(BF16), s16) for r, s16 in zip(rows, s16s)]
    rqs = [_dot(q[r].astype(BF16), s16) for r, s16 in zip(rows, s16s)]

    u_scr[...] = jnp.zeros_like(u_scr)
    kd_scr[...] = jnp.zeros_like(kd_scr)
    o_scr[...] = jnp.zeros_like(o_scr)
    a_alls = []
    for s in range(group):
        r0, p0 = SROWS * s, 16 * s
        kk = {tj: dots[p0 + r:p0 + r + 1] for r, tj in enumerate(pairs_k)}
        qk = {tj: dots[p0 + len(pairs_k) + r:p0 + len(pairs_k) + r + 1] for r, tj in enumerate(pairs_q)}
        rk_rows = [rks[s][t:t + 1] for t in range(n_tok)]
        rq_rows = [rqs[s][t:t + 1] for t in range(n_tok)]
        for t in range(n_tok):
            a_t = a[r0 + t:r0 + t + 1]
            u_t = beta[r0 + t:r0 + t + 1] * (v[r0 + t:r0 + t + 1] - a_t * rk_rows[t])
            for t2 in range(t + 1, n_tok):
                rk_rows[t2] = a_t * rk_rows[t2] + kk[(t2, t)] * u_t
            for t2 in range(t, n_tok):
                rq_rows[t2] = a_t * rq_rows[t2] + qk[(t2, t)] * u_t
            o_scr[r0 + t:r0 + t + 1, :] = rq_rows[t]
            u_scr[r0 + t:r0 + t + 1, :] = u_t
            tail = None
            for t2 in range(t + 1, n_tok):
                tail = a[r0 + t2:r0 + t2 + 1] if tail is None else tail * a[r0 + t2:r0 + t2 + 1]
            kd_scr[r0 + t:r0 + t + 1, :] = k[r0 + t:r0 + t + 1] if tail is None else k[r0 + t:r0 + t + 1] * tail
        a_all = a[r0:r0 + 1]
        for t in range(1, n_tok):
            a_all = a_all * a[r0 + t:r0 + t + 1]
        a_alls.append(a_all)
    upd = [_dot_tn(kd_scr[r, :].astype(BF16), u_scr[r, :].astype(BF16)) for r in rows]
    s_news = [sb * a_all + maskf * u for sb, a_all, u in zip(sbd0s, a_alls, upd)]
    for s in range(group):
        s_ref[s] = _dot3_r(s_news[s], tile_t_ref[...])

    o = o_scr[...]
    ms = _head_sums(o * o, maskbd) * (1.0 / DELTA_V_DIM)
    o_ref[...] = o * lax.rsqrt(ms + NORM_EPS) * gain_ref[...] * cz_ref[...]


def _delta_sample(proj, conv_state, state, layer, conv_w, gain, n_tok):
    n_samples = state.shape[1]
    cch = conv_state.shape[3]
    v_ = np.arange(DELTA_V_DIM)
    lane = np.arange(HEAD_W)
    tile_m = jnp.asarray((v_[:, None] == lane[None, :] % DELTA_V_DIM).astype(np.float32), BF16)
    tile_t = jnp.asarray((lane[:, None] % DELTA_V_DIM == v_[None, :]).astype(np.float32), BF16)
    lh = lane // 64
    maskbd = jnp.asarray((lh[:, None] == lh[None, :]).astype(np.float32), BF16)

    group = math.gcd(n_samples, 8)
    rows = group * SROWS

    def tile(width):
        return pl.BlockSpec((rows, width), lambda b: (b, 0))

    return pl.pallas_call(
        functools.partial(_delta_sample_body, n_tok=n_tok, group=group),
        out_shape=(jax.ShapeDtypeStruct((n_samples * SROWS, HEAD_W), F32),
                   jax.ShapeDtypeStruct((n_samples, HEAD_W, DELTA_V_DIM), F32)),
        grid=(n_samples // group,),
        in_specs=[tile(cch),
                  pl.BlockSpec((None, group, CONV_WIDTH - 1, cch), lambda b: (layer, b, 0, 0)),
                  tile(HEAD_W), tile(HEAD_W), tile(HEAD_W),
                  _const_spec((CONV_WIDTH, cch)), _const_spec((1, HEAD_W)),
                  pl.BlockSpec((None, group, HEAD_W, DELTA_V_DIM), lambda b: (layer, b, 0, 0)),
                  _const_spec(tile_m.shape), _const_spec(tile_t.shape), _const_spec(maskbd.shape)],
        out_specs=(tile(HEAD_W), pl.BlockSpec((group, HEAD_W, DELTA_V_DIM), lambda b: (b, 0, 0))),
        scratch_shapes=[pltpu.VMEM((2 * rows, cch), F32), pltpu.VMEM((2 * rows, HEAD_W), F32),
                        pltpu.VMEM((rows, HEAD_W), F32), pltpu.VMEM((rows, HEAD_W), F32),
                        pltpu.VMEM((rows, HEAD_W), F32)],
        compiler_params=_cparams("arbitrary"),
        name="delta_sample",
    )(proj["cqkv"], conv_state, proj["beta"], proj["g"], proj["cz"], conv_w.astype(F32),
      jnp.tile(gain.astype(F32), DELTA_HEADS).reshape(1, HEAD_W), state, tile_m, tile_t, maskbd)


def kernel(x_prompt, x_sample, cache_fox_k, cache_fox_v, cache_fox_logf, cache_diff_k, cache_diff_v, state_delta, state_conv, page_table, norm_ffn1, ffn1_wi, ffn1_wo, norm_mix, w_in, fox_f_bias, diff_lambda, diff_norm, delta_conv_w, delta_A_log, delta_dt_bias, delta_norm, w_branch, w_out, norm_ffn2, ffn2_wi, ffn2_wo, norm_final):
    n_seq, seq_len, d = x_prompt.shape
    n_smp, dec_seq, _ = x_sample.shape
    depth = w_in.shape[0]
    n_pool = cache_fox_k.shape[1]
    n_pages = page_table.shape[1]
    past_len = n_pages * PAGE_SIZE
    assert CONV_WIDTH - 1 <= dec_seq <= SROWS and seq_len % DELTA_CHUNK == 0

    wi1, wo1 = ffn1_wi.astype(BF16), ffn1_wo.astype(BF16)
    wi2, wo2 = ffn2_wi.astype(BF16), ffn2_wo.astype(BF16)
    w_p = _relayout_w_in(w_in, _PROMPT_COLS)
    w_s = _relayout_w_in(w_in, _SAMPLE_COLS)
    w_vt = _relayout_w_values_t(w_in)
    w_gates = w_in[:, :, w_in.shape[2] - N_BRANCH * d:].astype(BF16)
    wb, wo = w_branch.astype(BF16), w_out.astype(BF16)

    def pages_t(cache):
        c = cache.reshape(depth, n_pool, PAGE_SIZE, HEAD_W)
        return jnp.swapaxes(c, 2, 3)

    caches = (pages_t(cache_fox_k), pages_t(cache_fox_v), jnp.swapaxes(cache_fox_logf, 2, 3),
              pages_t(cache_diff_k), pages_t(cache_diff_v))
    s_delta = state_delta.reshape(depth, n_smp, HEAD_W, DELTA_V_DIM)

    xp = x_prompt.reshape(n_seq * seq_len, d)
    xs = jnp.pad(x_sample, ((0, 0), (0, SROWS - dec_seq), (0, 0))).reshape(n_smp * SROWS, d)
    cos_p, sin_p = _rope_tables(n_seq * seq_len, 0, seq_len)
    cos_s, sin_s = _rope_tables(n_smp * SROWS, past_len, SROWS)

    cache_bufs = {name: jnp.zeros((depth, n_seq, HEAD_W, seq_len), F32) for name in _CACHE_OUTS}

    new_p, new_s = [], []
    for l in range(depth):
        lam_init = 0.8 - 0.6 * math.exp(-0.3 * l)
        xp = _ffn(xp, norm_ffn1[l], wi1, wo1, l)
        xs = _ffn(xs, norm_ffn1[l], wi1, wo1, l)
        pp = _inproj(xp, norm_mix[l], w_p, l, cos_p, sin_p, fox_f_bias[l], delta_A_log[l], delta_dt_bias[l],
                     prompt=True, seq_len=seq_len, w_vt=w_vt, cache_bufs=cache_bufs)
        cache_bufs = {name: pp[name] for name in _CACHE_OUTS}
        ps = _inproj(xs, norm_mix[l], w_s, l, cos_s, sin_s, fox_f_bias[l], delta_A_log[l], delta_dt_bias[l],
                     prompt=False, seq_len=SROWS)

        of_p, od_p = _attn_prompt(pp["fq_pad"], pp["fk_pad"], pp["fvt"], pp["dq"], pp["dkb"], pp["dvt"],
                                  diff_lambda[l], diff_norm[l], lam_init, n_seq, seq_len)
        ol_p, sbd = _delta_prompt(pp["cqkv"], pp["beta"], pp["g"], pp["cz"], delta_conv_w[l], delta_norm[l],
                                  n_seq, seq_len)
        of_s, od_s = _sample_attn(page_table, caches, l, ps, diff_lambda[l], diff_norm[l], lam_init)
        ol_s, s_new = _delta_sample(ps, state_conv, s_delta, l, delta_conv_w[l], delta_norm[l], dec_seq)

        xp = _merge(xp, norm_mix[l], w_gates, wb, wo, l, (of_p, od_p, ol_p))
        xs = _merge(xs, norm_mix[l], w_gates, wb, wo, l, (of_s, od_s, ol_s))
        final = norm_final if l == depth - 1 else None
        xp = _ffn(xp, norm_ffn2[l], wi2, wo2, l, final)
        xs = _ffn(xs, norm_ffn2[l], wi2, wo2, l, final)

        def p3(a):
            return a.reshape(n_seq, seq_len, a.shape[1])

        def s3(a):
            return a.reshape(n_smp, SROWS, a.shape[1])[:, :dec_seq]

        new_p.append(dict(
            fox_logf=p3(pp["small"])[:, :, :FOX_HEADS],
            delta=jnp.stack([sbd[:, h * 64:(h + 1) * 64, h * 64:(h + 1) * 64] for h in range(DELTA_HEADS)], axis=1),
            conv=p3(pp["cqkv"])[:, seq_len - (CONV_WIDTH - 1):]))
        new_s.append(dict(
            fox_k=s3(ps["fk"]).reshape(n_smp, dec_seq, FOX_HEADS, FOX_DIM),
            fox_v=s3(ps["fv"]).reshape(n_smp, dec_seq, FOX_HEADS, FOX_DIM),
            fox_logf=s3(ps["small"])[:, :, :FOX_HEADS],
            diff_k=s3(ps["dk"]).reshape(n_smp, dec_seq, DIFF_HEADS, 2, DIFF_QK_DIM),
            diff_v=s3(ps["dv"]).reshape(n_smp, dec_seq, DIFF_HEADS, DIFF_V_DIM),
            delta=s_new.reshape(n_smp, DELTA_HEADS, DELTA_K_DIM, DELTA_V_DIM),
            conv=s3(ps["cqkv"])[:, dec_seq - (CONV_WIDTH - 1):]))

    def stk(lst, name):
        return jnp.stack([dd[name] for dd in lst])

    def token_major(buf, *feat):
        nf = len(feat)
        b = buf.reshape((depth, n_seq) + feat + (seq_len,))
        return jnp.transpose(b, (0, 1, 2 + nf) + tuple(range(2, 2 + nf)))

    y_prompt = xp.reshape(n_seq, seq_len, d)
    y_sample = xs.reshape(n_smp, SROWS, d)[:, :dec_seq]
    return (y_prompt, y_sample,
            token_major(cache_bufs["fkt"], FOX_HEADS, FOX_DIM), stk(new_s, "fox_k"),
            token_major(cache_bufs["fvt32"], FOX_HEADS, FOX_DIM), stk(new_s, "fox_v"),
            stk(new_p, "fox_logf"), stk(new_s, "fox_logf"),
            token_major(cache_bufs["dkt"], DIFF_HEADS, 2, DIFF_QK_DIM), stk(new_s, "diff_k"),
            token_major(cache_bufs["dvt32"], DIFF_HEADS, DIFF_V_DIM), stk(new_s, "diff_v"),
            stk(new_p, "delta"), stk(new_s, "delta"),
            stk(new_p, "conv"), stk(new_s, "conv"))
```

```python
import functools
import math

import numpy as np
import jax
import jax.numpy as jnp
from jax import lax
from jax.experimental import pallas as pl
from jax.experimental.pallas import tpu as pltpu

F32 = jnp.float32
BF16 = jnp.bfloat16

FOX_HEADS = 4
FOX_DIM = 64
DIFF_HEADS = 4
DIFF_QK_DIM = 32
DIFF_V_DIM = 64
DELTA_HEADS = 4
DELTA_K_DIM = 64
DELTA_V_DIM = 64
CONV_WIDTH = 4
DELTA_CHUNK = 64
N_BRANCH = 3
HEAD_W = 256
PAGE_SIZE = 128
ROPE_THETA = 10000.0
NORM_EPS = 1e-6
LOG2E = 1.4426950408889634
FOX_QSCALE = FOX_DIM ** -0.5 * LOG2E
DIFF_QSCALE = DIFF_QK_DIM ** -0.5 * LOG2E

V7X_VMEM_LIMIT_BYTES = 56 * 1024 * 1024
AUG_C = 64
AUG_K = 67
ONES_LANE = 64


def _cparams(*sem):
    return pltpu.CompilerParams(dimension_semantics=sem, vmem_limit_bytes=V7X_VMEM_LIMIT_BYTES)


def _dot(a, b):
    return jnp.dot(a, b, preferred_element_type=F32)


def _dot_nt(a, b):
    return lax.dot_general(a, b, (((1,), (1,)), ((), ())), preferred_element_type=F32)


def _dot_tn(a, b):
    return lax.dot_general(a, b, (((0,), (0,)), ((), ())), preferred_element_type=F32)


def _split3(x):
    hi = x.astype(BF16)
    r = x - hi.astype(F32)
    mid = r.astype(BF16)
    lo = (r - mid.astype(F32)).astype(BF16)
    return hi, mid, lo


def _dot3_r(x, m):
    hi, mid, lo = _split3(x)
    return _dot(hi, m) + _dot(mid, m) + _dot(lo, m)


def _dot3_l(m, x):
    hi, mid, lo = _split3(x)
    return _dot(m, hi) + _dot(m, mid) + _dot(m, lo)


def _rms(x, gain):
    ms = jnp.mean(x * x, axis=-1, keepdims=True)
    return x * lax.rsqrt(ms + NORM_EPS) * gain


def _const_spec(shape):
    nd = len(shape)
    return pl.BlockSpec(shape, lambda *_: (0,) * nd, pipeline_mode=pl.Buffered(1))


V7X_MXU_DIM = 256


def _ffn_body(x_ref, g_ref, wi_ref, wo_ref, gf_ref, o_ref, *, final_norm, dff, tf):
    x = x_ref[...]
    h = _rms(x, g_ref[...]).astype(BF16)
    acc = None
    for c0 in range(0, dff, tf):
        gate = _dot(h, wi_ref[:, c0:c0 + tf])
        up = _dot(h, wi_ref[:, dff + c0:dff + c0 + tf])
        act = (gate * jax.nn.sigmoid(gate) * up).astype(BF16)
        part = _dot(act, wo_ref[c0:c0 + tf, :])
        acc = part if acc is None else acc + part
    y = x + 0.5 * acc
    if final_norm:
        y = _rms(y, gf_ref[...])
    o_ref[...] = y


def _ffn(x, gain, wi, wo, layer, final_gain=None):
    r, d = x.shape
    dff = wo.shape[1]
    tm = min(r, 512)
    tf = V7X_MXU_DIM if dff % V7X_MXU_DIM == 0 else dff
    final_norm = final_gain is not None
    gf = final_gain if final_norm else gain
    return pl.pallas_call(
        functools.partial(_ffn_body, final_norm=final_norm, dff=dff, tf=tf),
        out_shape=jax.ShapeDtypeStruct((r, d), F32),
        grid=(r // tm,),
        in_specs=[
            pl.BlockSpec((tm, d), lambda i: (i, 0)),
            _const_spec((1, d)),
            pl.BlockSpec((None, d, 2 * dff), lambda i: (layer, 0, 0), pipeline_mode=pl.Buffered(1)),
            pl.BlockSpec((None, dff, d), lambda i: (layer, 0, 0), pipeline_mode=pl.Buffered(1)),
            _const_spec((1, d)),
        ],
        out_specs=pl.BlockSpec((tm, d), lambda i: (i, 0)),
        compiler_params=_cparams("arbitrary"),
        name="ffn",
    )(x, gain.reshape(1, d), wi, wo, gf.reshape(1, d))


def _rope_body(inv_ref, cos_ref, sin_ref, *, tm, base, period):
    i = pl.program_id(0)
    row = lax.broadcasted_iota(jnp.int32, (tm, HEAD_W), 0) + i * tm
    lane = lax.broadcasted_iota(jnp.int32, (tm, HEAD_W), 1)
    pos = base + (row & (period - 1))
    ang = pos.astype(F32) * inv_ref[...]
    cos_ref[...] = jnp.cos(ang)
    s = jnp.sin(ang)
    first_half = (lane & (DIFF_QK_DIM - 1)) < DIFF_QK_DIM // 2
    sin_ref[...] = jnp.where(first_half, -s, s)


def _rope_tables(rows, base, period):
    assert period & (period - 1) == 0
    half = DIFF_QK_DIM // 2
    inv = ROPE_THETA ** (-jnp.arange(0, DIFF_QK_DIM, 2, dtype=F32) / DIFF_QK_DIM)
    inv_lanes = jnp.tile(inv, HEAD_W // half).reshape(1, HEAD_W)
    tm = min(rows, 512)
    return pl.pallas_call(
        functools.partial(_rope_body, tm=tm, base=base, period=period),
        out_shape=(jax.ShapeDtypeStruct((rows, HEAD_W), F32),) * 2,
        grid=(rows // tm,),
        in_specs=[pl.BlockSpec((1, HEAD_W), lambda i: (0, 0))],
        out_specs=(pl.BlockSpec((tm, HEAD_W), lambda i: (i, 0)),) * 2,
        compiler_params=_cparams("arbitrary"),
        name="rope_tables",
    )(inv_lanes)


_PROMPT_COLS = (("fq_pad", 512), ("fk_pad", 512), ("fk", 256),
                ("dq", 256), ("dqs", 256), ("dk", 256), ("dks", 256),
                ("cqkv", 768), ("cz", 256), ("cb_rep", 256), ("ca_rep", 256), ("small", 128))
_SAMPLE_COLS = (("fq", 256), ("fk", 256), ("fv", 256),
                ("dq", 256), ("dqs", 256), ("dk", 256), ("dks", 256), ("dv", 256),
                ("cqkv", 768), ("cz", 256), ("cb_rep", 256), ("ca_rep", 256), ("small", 128))


def _layout(cols):
    lay, off = {}, 0
    for name, width in cols:
        lay[name] = (off, width)
        off += width
    return lay, off


def _relayout_w_in(w_in, cols):
    depth, d, _ = w_in.shape
    sizes = (256, 256, 256, 4, 256, 256, 256, 768, 4, 4, 256)
    offs = np.concatenate([[0], np.cumsum(sizes)])
    fq, fk, fv, ff, dq, dk, dv, cqkv, cb, ca, cz = (w_in[:, :, int(offs[i]):int(offs[i + 1])] for i in range(11))

    def pad_heads(w):
        w = w.reshape(depth, d, 4, 64)
        return jnp.pad(w, ((0, 0), (0, 0), (0, 0), (0, 64))).reshape(depth, d, 512)

    def swap_halves(w):
        w = w.reshape(depth, d, DIFF_HEADS * 2, 2, DIFF_QK_DIM // 2)
        return w[:, :, :, ::-1, :].reshape(depth, d, 256)

    def rep(w):
        return jnp.repeat(w, 64, axis=2)

    table = {
        "fq": fq, "fk": fk, "fv": fv, "fq_pad": pad_heads(fq), "fk_pad": pad_heads(fk),
        "dq": dq, "dqs": swap_halves(dq), "dk": dk, "dks": swap_halves(dk), "dv": dv,
        "cqkv": cqkv, "cz": cz, "cb_rep": rep(cb), "ca_rep": rep(ca),
        "small": jnp.pad(ff, ((0, 0), (0, 0), (0, 124))),
    }
    return jnp.concatenate([table[name] for name, _ in cols], axis=2).astype(BF16)


def _relayout_w_values_t(w_in):
    depth, d, _ = w_in.shape

    def head_rows(w):
        wt = jnp.swapaxes(w, 1, 2).reshape(depth, 4, 64, d)
        return jnp.pad(wt, ((0, 0), (0, 0), (0, 64), (0, 0))).reshape(depth, 512, d)

    fv = w_in[:, :, 512:768]
    dv = w_in[:, :, 1284:1540]
    return jnp.concatenate([head_rows(fv), head_rows(dv)], axis=1).astype(BF16)


def _aug_place_mats():
    pq = np.zeros((384, 512), np.float32)
    pk = np.zeros((384, 512), np.float32)
    for piece in range(3):
        for h in range(FOX_HEADS):
            pq[piece * 128 + h, h * 128 + AUG_C + piece] = 1.0
            pk[piece * 128 + h, h * 128 + AUG_K + piece] = -1.0
    return jnp.asarray(pq, BF16), jnp.asarray(pk, BF16)


def _lane_ones(shape, lo, hi):
    lane = lax.broadcasted_iota(jnp.int32, shape, 1) & 127
    mid2 = lo + hi
    return jnp.where(jnp.abs(2 * lane - mid2) <= hi - lo, 1.0, 0.0).astype(F32)


def _inproj_body(*refs, lay, prompt, tiles_per_seq, tm, n_alias=0):
    if prompt:
        (x_ref, g_ref, w_ref, cos_ref, sin_ref, fb_ref, alog_ref, dtb_ref, tri_ref, pq_ref, pk_ref, wvt_ref) = refs[:12]
        (fq_o, fk_pad_o, fvt_o, small_o, dq_o, dkb_o, dvt_o, cqkv_o, cz_o, beta_o, g_o,
         fkt_o, fvt32_o, dkt_o, dvt32_o, carry_scr) = refs[12 + n_alias:]
    else:
        (x_ref, g_ref, w_ref, cos_ref, sin_ref, fb_ref, alog_ref, dtb_ref,
         fq_o, fk_o, fv_o, small_o, dq_o, dk_o, dv_o, cqkv_o, cz_o, beta_o, g_o) = refs

    h = _rms(x_ref[...], g_ref[...]).astype(BF16)

    def proj(name):
        off, width = lay[name]
        return _dot(h, w_ref[:, off:off + width])

    logf = jax.nn.log_sigmoid(proj("small") + fb_ref[...])
    small_o[...] = logf

    if prompt:
        i = pl.program_id(0)

        @pl.when(i % tiles_per_seq == 0)
        def _():
            carry_scr[...] = jnp.zeros_like(carry_scr)

        c = _dot3_l(tri_ref[...], logf) + carry_scr[...]
        carry_scr[...] = c[tm - 1:tm, :]
        c1, c2, c3 = _split3(c * LOG2E)
        c123 = jnp.concatenate([c1, c2, c3], axis=1)
        shape = (tm, 512)
        q = proj("fq_pad") * FOX_QSCALE + _dot(c123, pq_ref[...]) + _lane_ones(shape, AUG_K, AUG_K + 2)
        fq_o[...] = q.astype(BF16)
        k = proj("fk_pad") + _dot(c123, pk_ref[...]) + _lane_ones(shape, AUG_C, AUG_C + 2)
        fk_pad_o[...] = k.astype(BF16)
        vt = _dot_nt(wvt_ref[...], h)
        for hd in range(4):
            fvt32_o[hd * 64:(hd + 1) * 64, :] = vt[hd * 128:hd * 128 + 64]
            dvt32_o[hd * 64:(hd + 1) * 64, :] = vt[512 + hd * 128:512 + hd * 128 + 64]
        row = lax.broadcasted_iota(jnp.int32, vt.shape, 0) & 127
        vt = (vt + jnp.where(row == ONES_LANE, 1.0, 0.0)).astype(BF16)
        fvt_o[...] = vt[0:512]
        dvt_o[...] = vt[512:1024]
        fkt_o[...] = proj("fk").T
    else:
        fq_o[...] = proj("fq") * FOX_QSCALE
        fk_o[...] = proj("fk")
        fv_o[...] = proj("fv")
        dv_o[...] = proj("dv")

    cos = cos_ref[...]
    sin = sin_ref[...]
    dq_o[...] = ((proj("dq") * cos + proj("dqs") * sin) * DIFF_QSCALE).astype(dq_o.dtype)
    dk = proj("dk") * cos + proj("dks") * sin
    if prompt:
        dkb_o[...] = dk.astype(BF16)
        dkt_o[...] = dk.T
    else:
        dk_o[...] = dk

    cqkv_o[...] = proj("cqkv")
    cz = proj("cz")
    cz_o[...] = cz * jax.nn.sigmoid(cz)
    beta_o[...] = jax.nn.sigmoid(proj("cb_rep"))
    g_o[...] = -jnp.exp(alog_ref[...]) * jax.nn.softplus(proj("ca_rep") + dtb_ref[...])


_CACHE_OUTS = ("fkt", "fvt32", "dkt", "dvt32")


def _inproj(x, gain, w, layer, cos, sin, fbias, alog, dtb, *, prompt, seq_len, w_vt=None, cache_bufs=None):
    r, d = x.shape
    lay, nc = _layout(_PROMPT_COLS if prompt else _SAMPLE_COLS)
    assert w.shape[2] == nc
    tm = min(r, 512 if prompt else 256)
    tm = min(tm, seq_len) if prompt else tm
    fb = jnp.pad(fbias.astype(F32), (0, 124)).reshape(1, 128)
    alog_l = jnp.repeat(alog.astype(F32), 64).reshape(1, 256)
    dtb_l = jnp.repeat(dtb.astype(F32), 64).reshape(1, 256)

    def rows(width):
        return pl.BlockSpec((tm, width), lambda i: (i, 0))

    rope_tiles = cos.shape[0] // tm
    rope = pl.BlockSpec((tm, 256), lambda i: (i % rope_tiles, 0))
    in_specs = [rows(d), _const_spec((1, d)),
                pl.BlockSpec((None, d, nc), lambda i: (layer, 0, 0), pipeline_mode=pl.Buffered(1)),
                rope, rope, _const_spec((1, 128)), _const_spec((1, 256)), _const_spec((1, 256))]
    args = [x, gain.reshape(1, d), w, cos, sin, fb, alog_l, dtb_l]
    if prompt:
        tri = jnp.asarray(np.tril(np.ones((tm, tm), np.float32)), BF16)
        pq, pk = _aug_place_mats()
        in_specs += [_const_spec((tm, tm)), _const_spec((384, 512)), _const_spec((384, 512)),
                     pl.BlockSpec((None, 1024, d), lambda i: (layer, 0, 0), pipeline_mode=pl.Buffered(1))]
        args += [tri, pq, pk, w_vt]
        outs = (("fq_pad", 512, BF16), ("fk_pad", 512, BF16), ("fvt", 0, BF16), ("small", 128, F32),
                ("dq", 256, BF16), ("dkb", 256, BF16), ("dvt", 0, BF16), ("cqkv", 768, F32), ("cz", 256, F32),
                ("beta", 256, F32), ("g", 256, F32))
        scratch = [pltpu.VMEM((1, 128), F32)]
    else:
        outs = (("fq", 256, F32), ("fk", 256, F32), ("fv", 256, F32), ("small", 128, F32), ("dq", 256, F32),
                ("dk", 256, F32), ("dv", 256, F32), ("cqkv", 768, F32), ("cz", 256, F32),
                ("beta", 256, F32), ("g", 256, F32))
        scratch = []
    tps = max(seq_len // tm, 1)
    out_shape = [jax.ShapeDtypeStruct((r, wd) if wd else (512, r), dt) for _, wd, dt in outs]
    out_specs = [rows(wd) if wd else pl.BlockSpec((512, tm), lambda i: (0, i)) for _, wd, _ in outs]
    names = [name for name, _, _ in outs]
    aliases = {}
    if prompt:
        for j, name in enumerate(_CACHE_OUTS):
            buf = cache_bufs[name]
            out_shape.append(jax.ShapeDtypeStruct(buf.shape, buf.dtype))
            out_specs.append(pl.BlockSpec((None, None, HEAD_W, tm), lambda i: (layer, i // tps, 0, i % tps)))
            names.append(name)
            aliases[len(args)] = len(outs) + j
            in_specs.append(pl.BlockSpec(memory_space=pl.ANY))
            args.append(buf)
    res = pl.pallas_call(
        functools.partial(_inproj_body, lay=lay, prompt=prompt, tiles_per_seq=tps, tm=tm, n_alias=len(aliases)),
        out_shape=tuple(out_shape),
        grid=(r // tm,),
        in_specs=in_specs,
        out_specs=tuple(out_specs),
        scratch_shapes=scratch,
        input_output_aliases=aliases,
        compiler_params=_cparams("arbitrary"),
        name="inproj_prompt" if prompt else "inproj_sample",
    )(*args)
    return dict(zip(names, res))


def _tri_schedule(nq):
    qi = np.concatenate([np.full(i + 1, i, np.int32) for i in range(nq)])
    kj = np.concatenate([np.arange(i + 1, dtype=np.int32) for i in range(nq)])
    return jnp.asarray(qi), jnp.asarray(kj)


def _online_softmax_step(st, vt, m_ref, acc_ref):
    m_old = m_ref[...]
    m_new = jnp.maximum(m_old, jnp.max(st, axis=0, keepdims=True))
    alpha = jnp.exp2(m_old - m_new)
    p = jnp.exp2(st - m_new)
    acc_ref[...] = alpha * acc_ref[...] + _dot(vt, p.astype(BF16))
    m_ref[...] = m_new


def _causal_mask(st):
    key = lax.broadcasted_iota(jnp.int32, st.shape, 0)
    qry = lax.broadcasted_iota(jnp.int32, st.shape, 1)
    return jnp.where(key <= qry, st, -jnp.inf)


def _diff_lambda(lam_ref, lam_init):
    lp = lam_ref[...]
    s1 = jnp.sum(lp[0:1] * lp[1:2], axis=1, keepdims=True)
    s2 = jnp.sum(lp[2:3] * lp[3:4], axis=1, keepdims=True)
    return jnp.exp(s1) - jnp.exp(s2) + lam_init


N_ATTN_MAPS = FOX_HEADS + 2 * DIFF_HEADS
ATTN_LOOKAHEAD = 3


def _attn_body(qi_ref, kj_ref, fq_ref, fk_ref, fvt_ref, dq_ref, dk_ref, dvt_ref, lam_ref, gain_ref,
               of_ref, od_ref, qm_scr, m_scr, acc_scr, *, lam_init):
    t = pl.program_id(1)
    qi = qi_ref[t]
    kj = kj_ref[t]
    n_diff = 2 * DIFF_HEADS

    @pl.when(kj == 0)
    def _():
        m_scr[...] = jnp.full_like(m_scr, -jnp.inf)
        acc_scr[...] = jnp.zeros_like(acc_scr)
        q = dq_ref[...]
        lane_map = lax.broadcasted_iota(jnp.int32, q.shape, 1) // DIFF_QK_DIM
        for m in range(n_diff):
            qm_scr[m] = jnp.where(lane_map == m, q, jnp.zeros_like(q))

    def scores(j):
        if j < FOX_HEADS:
            sl = slice(j * 128, (j + 1) * 128)
            return _dot_nt(fk_ref[:, sl], fq_ref[:, sl])
        return _dot_nt(dk_ref[...], qm_scr[j - FOX_HEADS])

    def values_t(j):
        if j < FOX_HEADS:
            return fvt_ref[j * 128:(j + 1) * 128, :]
        h = (j - FOX_HEADS) // 2
        return dvt_ref[h * 128:(h + 1) * 128, :]

    def step(masked):
        pending = [scores(j) for j in range(ATTN_LOOKAHEAD)]
        for j in range(N_ATTN_MAPS):
            st = pending.pop(0)
            if j + ATTN_LOOKAHEAD < N_ATTN_MAPS:
                pending.append(scores(j + ATTN_LOOKAHEAD))
            if masked:
                st = _causal_mask(st)
            _online_softmax_step(st, values_t(j), m_scr.at[j], acc_scr.at[j])

    @pl.when(kj < qi)
    def _():
        step(False)

    @pl.when(kj == qi)
    def _():
        step(True)
        for h in range(FOX_HEADS):
            a = acc_scr[h]
            o = a[:FOX_DIM] / a[ONES_LANE:ONES_LANE + 1]
            of_ref[:, h * FOX_DIM:(h + 1) * FOX_DIM] = o.T.astype(of_ref.dtype)
        lam = _diff_lambda(lam_ref, lam_init)
        tq = od_ref.shape[0]
        gain = jnp.concatenate([gain_ref[...]] * (tq // 128), axis=1)
        for h in range(DIFF_HEADS):
            a0 = acc_scr[FOX_HEADS + 2 * h]
            a1 = acc_scr[FOX_HEADS + 2 * h + 1]
            o = (a0[:DIFF_V_DIM] / a0[ONES_LANE:ONES_LANE + 1]
                 - lam * (a1[:DIFF_V_DIM] / a1[ONES_LANE:ONES_LANE + 1]))
            ms = jnp.mean(o * o, axis=0, keepdims=True)
            o = o * lax.rsqrt(ms + NORM_EPS) * gain * (1.0 - lam_init)
            od_ref[:, h * DIFF_V_DIM:(h + 1) * DIFF_V_DIM] = o.T.astype(od_ref.dtype)


def _attn_prompt(fq_pad, fk_pad, fvt_pad, dq, dk, dvt_pad, lam_p, gain, lam_init, n_seq, seq_len):
    tq = min(seq_len, 512)
    nq = seq_len // tq
    qi, kj = _tri_schedule(nq)

    def q_rows(width):
        return pl.BlockSpec((tq, width), lambda n, t, qi, kj: (n * nq + qi[t], 0))

    def k_rows(width):
        return pl.BlockSpec((tq, width), lambda n, t, qi, kj: (n * nq + kj[t], 0))

    def k_cols():
        return pl.BlockSpec((512, tq), lambda n, t, qi, kj: (0, n * nq + kj[t]))

    gs = pltpu.PrefetchScalarGridSpec(
        num_scalar_prefetch=2,
        grid=(n_seq, int(qi.shape[0])),
        in_specs=[q_rows(512), k_rows(512), k_cols(), q_rows(HEAD_W), k_rows(HEAD_W), k_cols(),
                  pl.BlockSpec((4, DIFF_QK_DIM), lambda n, t, qi, kj: (0, 0)),
                  pl.BlockSpec((DIFF_V_DIM, 128), lambda n, t, qi, kj: (0, 0))],
        out_specs=(q_rows(HEAD_W), q_rows(HEAD_W)),
        scratch_shapes=[pltpu.VMEM((2 * DIFF_HEADS, tq, HEAD_W), BF16), pltpu.VMEM((N_ATTN_MAPS, 1, tq), F32),
                        pltpu.VMEM((N_ATTN_MAPS, 128, tq), F32)],
    )
    gain_rows = jnp.broadcast_to(gain.astype(F32).reshape(DIFF_V_DIM, 1), (DIFF_V_DIM, 128))
    return pl.pallas_call(
        functools.partial(_attn_body, lam_init=lam_init),
        out_shape=(jax.ShapeDtypeStruct((n_seq * seq_len, HEAD_W), BF16),) * 2,
        grid_spec=gs,
        compiler_params=_cparams("arbitrary", "arbitrary"),
        name="attn_prompt",
    )(qi, kj, fq_pad, fk_pad, fvt_pad, dq, dk, dvt_pad, lam_p.astype(F32), gain_rows)


def _merge_body(x_ref, g_ref, wg_ref, wb_ref, wo_ref, b0_ref, b1_ref, b2_ref, o_ref):
    x = x_ref[...]
    d = x.shape[1]
    h = _rms(x, g_ref[...]).astype(BF16)
    mix = None
    for b, b_ref in enumerate((b0_ref, b1_ref, b2_ref)):
        gate = jax.nn.sigmoid(_dot(h, wg_ref[:, b * d:(b + 1) * d]))
        term = gate * _dot(b_ref[...].astype(BF16), wb_ref[b])
        mix = term if mix is None else mix + term
    o_ref[...] = x + _dot(mix.astype(BF16), wo_ref[...])


def _merge(x, gain, w_gates, w_branch, w_out, layer, branches):
    r, d = x.shape
    tm = min(r, 512)

    def rows(width):
        return pl.BlockSpec((tm, width), lambda i: (i, 0))

    return pl.pallas_call(
        _merge_body,
        out_shape=jax.ShapeDtypeStruct((r, d), F32),
        grid=(r // tm,),
        in_specs=[rows(d), _const_spec((1, d)),
                  pl.BlockSpec((None, d, N_BRANCH * d), lambda i: (layer, 0, 0), pipeline_mode=pl.Buffered(1)),
                  pl.BlockSpec((None, N_BRANCH, HEAD_W, d), lambda i: (layer, 0, 0, 0), pipeline_mode=pl.Buffered(1)),
                  pl.BlockSpec((None, d, d), lambda i: (layer, 0, 0), pipeline_mode=pl.Buffered(1)),
                  rows(HEAD_W), rows(HEAD_W), rows(HEAD_W)],
        out_specs=rows(d),
        compiler_params=_cparams("arbitrary"),
        name="merge",
    )(x, gain.reshape(1, d), w_gates, w_branch, w_out, *branches)


def _delta_consts():
    c = DELTA_CHUNK
    i = np.arange(c)
    lane = np.arange(HEAD_W)
    lh, lj = lane // c, lane % c
    tril = i[:, None] >= i[None, :]
    suffix = i[None, :] > i[:, None]
    lvl = []
    for k in range(1, 7):
        b = 1 << k
        lvl.append((i[:, None] // b == lj[None, :] // b) & (i[:, None] // (b // 2) != lj[None, :] // (b // 2)))
    f = lambda a, dt: jnp.asarray(np.asarray(a, np.float32), dt)
    return dict(
        lhsg=f(np.concatenate([tril, suffix], axis=0), BF16),
        ones=f(np.ones((c, c)), BF16),
        triu4=f(i[:, None] <= lj[None, :], F32),
        tril4=f(i[:, None] >= lj[None, :], F32),
        stril4=f(i[:, None] > lj[None, :], F32),
        eye4=f(i[:, None] == lj[None, :], F32),
        lvl=f(np.stack(lvl), F32),
        maskbd=f(lh[:, None] == lh[None, :], BF16),
    )


def _bd(x4, maskbd):
    return jnp.concatenate([x4, x4, x4, x4], axis=0) * maskbd


def _mm_hi(xs, ys, maskbd):
    xh = [x.astype(BF16) for x in xs]
    xl = [(x - h.astype(F32)).astype(BF16) for x, h in zip(xs, xh)]
    yh = [y.astype(BF16) for y in ys]
    yl = [(y - h.astype(F32)).astype(BF16) for y, h in zip(ys, yh)]
    ybh = [_bd(h, maskbd) for h in yh]
    ybl = [_bd(l, maskbd) for l in yl]
    return [_dot(a, bh) + _dot(al, bh) + _dot(a, bl) for a, al, bh, bl in zip(xh, xl, ybh, ybl)]


def _unit_lower_inverse(a4s, eye4, lvl_ref, maskbd):
    ts = [eye4 - a4 * lvl_ref[0] for a4 in a4s]
    for k in range(1, 6):
        ms = [a4 * lvl_ref[k] for a4 in a4s]
        xs = _mm_hi(ms, ts, maskbd)
        ts = [t - d for t, d in zip(ts, _mm_hi(ts, xs, maskbd))]
    return ts


def _head_sums(x, maskbd):
    return _dot3_r(x, maskbd)


def _delta_body(cqkv_ref, beta_ref, g_ref, cz_ref, cw_ref, gain_ref,
                lhsg_ref, ones_ref, triu4_ref, tril4_ref, stril4_ref, eye4_ref, lvl_ref, maskbd_ref,
                o_ref, s_ref, xbuf, ub_scr, w_scr, qk_scr, qe_scr, kd_scr, eg_scr, o_scr, sbd_scr, *, tc):
    j = pl.program_id(1)
    c = DELTA_CHUNK
    maskbd = maskbd_ref[...]
    maskf = maskbd.astype(F32)

    @pl.when(j == 0)
    def _():
        xbuf[0:8, :] = jnp.zeros((8, xbuf.shape[1]), F32)
        sbd_scr[...] = jnp.zeros_like(sbd_scr)

    xbuf[8:8 + tc, :] = cqkv_ref[...]
    cw = cw_ref[...]
    y = cw[3:4] * xbuf[8:8 + tc, :]
    for i in range(1, CONV_WIDTH):
        y = y + cw[3 - i:4 - i] * xbuf[8 - i:8 - i + tc, :]
    xbuf[0:8, :] = xbuf[tc:tc + 8, :]
    y = y * jax.nn.sigmoid(y)
    q = y[:, 0:HEAD_W]
    k = y[:, HEAD_W:2 * HEAD_W]
    qn = q * lax.rsqrt(_head_sums(q * q, maskbd) + 1e-6) * (DELTA_K_DIM ** -0.5)
    kn = k * lax.rsqrt(_head_sums(k * k, maskbd) + 1e-6)
    v = y[:, 2 * HEAD_W:3 * HEAD_W]
    beta = beta_ref[...]
    g = g_ref[...]

    cs = [slice(ci * c, (ci + 1) * c) for ci in range(tc // c)]
    qcs, kcs, vcs = [qn[s] for s in cs], [kn[s] for s in cs], [v[s] for s in cs]
    bcs, gcs = [beta[s] for s in cs], [g[s] for s in cs]
    gsums = [_dot3_l(lhsg_ref[...], gc) for gc in gcs]
    grows = [_dot3_l(ones_ref[...], gc * triu4_ref[...]) for gc in gcs]
    gcols, gsufs = [gs[0:c] for gs in gsums], [gs[c:2 * c] for gs in gsums]
    decays = [jnp.where(tril4_ref[...] > 0, jnp.exp(gcol - grow), 0.0) for gcol, grow in zip(gcols, grows)]
    exp_gs = [jnp.exp(gcol) for gcol in gcols]
    kbs = [kc * bc for kc, bc in zip(kcs, bcs)]
    kbds = [_bd(kc.astype(BF16), maskbd) for kc in kcs]
    a4s = [_dot_nt(kb.astype(BF16), kbd) * decay * stril4_ref[...] for kb, kbd, decay in zip(kbs, kbds, decays)]
    qk4s = [_dot_nt(qc.astype(BF16), kbd) * decay for qc, kbd, decay in zip(qcs, kbds, decays)]
    t16s = [t.astype(BF16) for t in _unit_lower_inverse(a4s, eye4_ref[...], lvl_ref, maskbd)]
    u_bases = [_dot(t16, _bd((vc * bc).astype(BF16), maskbd)) for t16, vc, bc in zip(t16s, vcs, bcs)]
    ws = [_dot(t16, _bd((kb * eg).astype(BF16), maskbd)) for t16, kb, eg in zip(t16s, kbs, exp_gs)]
    ub_scr[...] = jnp.concatenate(u_bases, axis=0)
    w_scr[...] = jnp.concatenate([w.astype(BF16) for w in ws], axis=0)
    qk_scr[...] = jnp.concatenate([x.astype(BF16) for x in qk4s], axis=0)
    qe_scr[...] = jnp.concatenate([(qc * eg).astype(BF16) for qc, eg in zip(qcs, exp_gs)], axis=0)
    kd_scr[...] = jnp.concatenate([(kc * jnp.exp(gs)).astype(BF16) for kc, gs in zip(kcs, gsufs)], axis=0)
    eg_scr[...] = jnp.concatenate([jnp.broadcast_to(eg[c - 1:c, :], (8, HEAD_W)) for eg in exp_gs], axis=0)

    def scan(ci, carry):
        sl = pl.ds(pl.multiple_of(ci * c, c), c)
        s_old = sbd_scr[...]
        s16 = s_old.astype(BF16)
        u16 = (ub_scr[sl, :] - _dot(w_scr[sl, :], s16)).astype(BF16)
        o_scr[sl, :] = _dot(qe_scr[sl, :], s16) + _dot(qk_scr[sl, :], _bd(u16, maskbd))
        exp_g_last = eg_scr[pl.ds(pl.multiple_of(ci * 8, 8), 8), :][0:1]
        sbd_scr[...] = s_old * exp_g_last + maskf * _dot_tn(kd_scr[sl, :], u16)
        return carry

    lax.fori_loop(0, tc // c, scan, 0)

    o = o_scr[...]
    ms = _head_sums(o * o, maskbd) * (1.0 / DELTA_V_DIM)
    o_ref[...] = (o * lax.rsqrt(ms + NORM_EPS) * gain_ref[...] * cz_ref[...]).astype(o_ref.dtype)

    @pl.when(j == pl.num_programs(1) - 1)
    def _():
        s_ref[...] = sbd_scr[...]


def _delta_prompt(cqkv, beta, g, cz, conv_w, gain, n_seq, seq_len):
    tc = min(seq_len, 512)
    nt = seq_len // tc
    cch = cqkv.shape[1]
    consts = _delta_consts()
    names = ("lhsg", "ones", "triu4", "tril4", "stril4", "eye4", "lvl", "maskbd")

    def rows(width):
        return pl.BlockSpec((tc, width), lambda n, j: (n * nt + j, 0))

    return pl.pallas_call(
        functools.partial(_delta_body, tc=tc),
        out_shape=(jax.ShapeDtypeStruct((n_seq * seq_len, HEAD_W), BF16),
                   jax.ShapeDtypeStruct((n_seq, HEAD_W, HEAD_W), F32)),
        grid=(n_seq, nt),
        in_specs=[rows(cch), rows(HEAD_W), rows(HEAD_W), rows(HEAD_W),
                  _const_spec((CONV_WIDTH, cch)), _const_spec((1, HEAD_W))]
                 + [_const_spec(consts[nm].shape) for nm in names],
        out_specs=(rows(HEAD_W), pl.BlockSpec((None, HEAD_W, HEAD_W), lambda n, j: (n, 0, 0))),
        scratch_shapes=[pltpu.VMEM((tc + 8, cch), F32), pltpu.VMEM((tc, HEAD_W), F32),
                        pltpu.VMEM((tc, HEAD_W), BF16), pltpu.VMEM((tc, HEAD_W), BF16),
                        pltpu.VMEM((tc, HEAD_W), BF16), pltpu.VMEM((tc, HEAD_W), BF16),
                        pltpu.VMEM((8 * (tc // DELTA_CHUNK), HEAD_W), F32),
                        pltpu.VMEM((tc, HEAD_W), F32), pltpu.VMEM((HEAD_W, HEAD_W), F32)],
        compiler_params=_cparams("arbitrary", "arbitrary"),
        name="delta_prompt",
    )(cqkv, beta, g, cz, conv_w.astype(F32), jnp.tile(gain.astype(F32), DELTA_HEADS).reshape(1, HEAD_W),
      *[consts[nm] for nm in names])


SROWS = 8


def _sample_attn_consts(n_pages):
    r = np.arange(PAGE_SIZE)
    m2 = r[:, None] > r[None, :]
    m2ones = np.concatenate([m2, np.ones((PAGE_SIZE, PAGE_SIZE), bool)], axis=1)
    idx = np.arange(4 * n_pages)
    u2 = (idx[None, :] // 4 > idx[:, None] // 4) & (idx[None, :] % 4 == idx[:, None] % 4)
    t = np.arange(SROWS)
    tri8 = t[None, :] <= t[:, None]
    sup = t[:, None] > r[None, :]
    lh = np.arange(HEAD_W) // 64
    f = lambda a, dt: jnp.asarray(np.asarray(a, np.float32), dt)
    return dict(m2ones=f(m2ones, BF16), u2=f(u2, BF16), tri8=f(tri8, BF16), sup=f(sup, F32),
                maskbd=f(lh[:, None] == lh[None, :], BF16))


def _pad_rows(x, rows):
    return jnp.concatenate([x, jnp.zeros((rows - x.shape[0], x.shape[1]), x.dtype)], axis=0)


def _block_rows(x, groups, lanes_per_group):
    xr = jnp.concatenate([x] * groups, axis=0)
    rg = lax.broadcasted_iota(jnp.int32, xr.shape, 0) // SROWS
    lg = lax.broadcasted_iota(jnp.int32, xr.shape, 1) // lanes_per_group
    return jnp.where(rg == lg, xr, 0.0).astype(BF16)


def _paged_softmax_pv(qbd, kt_pages, vt_pages, k_new, v_new, bias_pages, bias_new):
    tiles = []
    for p, kt_ref in enumerate(kt_pages):
        s = _dot(qbd, kt_ref[...].astype(BF16))
        tiles.append(s if bias_pages is None else s + bias_pages[p])
    s = _dot_nt(qbd, _pad_rows(k_new, PAGE_SIZE).astype(BF16))
    if bias_new is not None:
        s = s + bias_new
    row = lax.broadcasted_iota(jnp.int32, s.shape, 0) & (SROWS - 1)
    col = lax.broadcasted_iota(jnp.int32, s.shape, 1)
    tiles.append(jnp.where(col <= row, s, -jnp.inf))
    m = tiles[0].max(axis=1, keepdims=True)
    for s in tiles[1:]:
        m = jnp.maximum(m, s.max(axis=1, keepdims=True))
    acc = None
    den = None
    for i, s in enumerate(tiles):
        p = jnp.exp2(s - m)
        ps = p.sum(axis=1, keepdims=True)
        if i < len(vt_pages):
            pv = _dot_nt(p.astype(BF16), vt_pages[i][...].astype(BF16))
        else:
            pv = _dot(p.astype(BF16), _pad_rows(v_new, PAGE_SIZE).astype(BF16))
        acc = pv if acc is None else acc + pv
        den = ps if den is None else den + ps
    return acc / den


def _own_head_rows(x, stride):
    lh = lax.broadcasted_iota(jnp.int32, (SROWS, HEAD_W), 1) // 64
    out = None
    for h in range(4):
        part = jnp.where(lh == h, x[h * stride:h * stride + SROWS], 0.0)
        out = part if out is None else out + part
    return out


def _sample_attn_body(pt_ref, *refs, n_pages, lam_init):
    del pt_ref
    groups = [refs[i * n_pages:(i + 1) * n_pages] for i in range(5)]
    fk_pages, fv_pages, lf_pages, dk_pages, dv_pages = groups
    (fq_ref, fkn_ref, fvn_ref, ln_ref, dq_ref, dkn_ref, dvn_ref,
     m2_ref, u2_ref, tri8_ref, sup_ref, lam_ref, gain_ref, maskbd_ref,
     of_ref, od_ref, xm_scr) = refs[5 * n_pages:]

    for p in range(n_pages):
        xm_scr[4 * p:4 * p + 4, :] = lf_pages[p][...]
    wt = _dot3_r(xm_scr[...], m2_ref[...])
    within, totals = wt[:, :PAGE_SIZE], wt[:, PAGE_SIZE:]
    d_past = (within + _dot3_l(u2_ref[...], totals)) * LOG2E
    lnew = ln_ref[...]
    tri8 = tri8_ref[...]
    cn = _dot3_l(tri8, lnew) * LOG2E
    cn_h = [jnp.broadcast_to(cn[:, h:h + 1], (SROWS, PAGE_SIZE)) for h in range(FOX_HEADS)]
    bias_pages = [
        jnp.concatenate([d_past[4 * p + h:4 * p + h + 1, :] + cn_h[h] for h in range(FOX_HEADS)], axis=0)
        for p in range(n_pages)]
    sup = sup_ref[...]
    bias_new = jnp.concatenate(
        [_dot3_l(tri8, jnp.broadcast_to(lnew[:, h:h + 1], (SROWS, PAGE_SIZE)) * sup) * LOG2E
         for h in range(FOX_HEADS)], axis=0)
    qbd = _block_rows(fq_ref[...], FOX_HEADS, FOX_DIM)
    o = _paged_softmax_pv(qbd, fk_pages, fv_pages, fkn_ref[...], fvn_ref[...], bias_pages, bias_new)
    of_ref[...] = _own_head_rows(o, SROWS)

    qbd = _block_rows(dq_ref[...], 2 * DIFF_HEADS, DIFF_QK_DIM)
    o = _paged_softmax_pv(qbd, dk_pages, dv_pages, dkn_ref[...], dvn_ref[...], None, None)
    lam = _diff_lambda(lam_ref, lam_init)
    o = _own_head_rows(o, 2 * SROWS) - lam * _own_head_rows(o[SROWS:], 2 * SROWS)
    ms = _head_sums(o * o, maskbd_ref[...]) * (1.0 / DIFF_V_DIM)
    od_ref[...] = o * lax.rsqrt(ms + NORM_EPS) * gain_ref[...] * (1.0 - lam_init)


def _sample_attn(page_table, caches, layer, proj, lam_p, gain, lam_init):
    n_samples, n_pages = page_table.shape
    consts = _sample_attn_consts(n_pages)

    def page_spec(arr, p):
        return pl.BlockSpec((None, None) + arr.shape[2:], lambda b, pt: (layer, pt[b, p], 0, 0))

    def tile(width):
        return pl.BlockSpec((SROWS, width), lambda b, pt: (b, 0))

    def const(shape):
        nd = len(shape)
        return pl.BlockSpec(shape, lambda b, pt: (0,) * nd, pipeline_mode=pl.Buffered(1))

    in_specs, args = [], []
    for arr in caches:
        for p in range(n_pages):
            in_specs.append(page_spec(arr, p))
            args.append(arr)
    new = (proj["fq"], proj["fk"], proj["fv"], proj["small"], proj["dq"], proj["dk"], proj["dv"])
    in_specs += [tile(a.shape[1]) for a in new]
    args += list(new)
    cvals = (consts["m2ones"], consts["u2"], consts["tri8"], consts["sup"], lam_p.astype(F32),
             jnp.tile(gain.astype(F32), DIFF_HEADS).reshape(1, HEAD_W), consts["maskbd"])
    in_specs += [const(c.shape) for c in cvals]
    args += list(cvals)
    gs = pltpu.PrefetchScalarGridSpec(
        num_scalar_prefetch=1, grid=(n_samples,), in_specs=in_specs,
        out_specs=(tile(HEAD_W), tile(HEAD_W)),
        scratch_shapes=[pltpu.VMEM((4 * n_pages, PAGE_SIZE), F32)])
    return pl.pallas_call(
        functools.partial(_sample_attn_body, n_pages=n_pages, lam_init=lam_init),
        out_shape=(jax.ShapeDtypeStruct((n_samples * SROWS, HEAD_W), F32),) * 2,
        grid_spec=gs,
        compiler_params=_cparams("arbitrary"),
        name="sample_attn",
    )(page_table, *args)


def _delta_sample_body(cqkv_ref, buf_ref, beta_ref, g_ref, cz_ref, cw_ref, gain_ref, s0_ref,
                       tile_ref, tile_t_ref, maskbd_ref, o_ref, s_ref, xp_scr, pr_scr, u_scr, kd_scr, o_scr,
                       *, n_tok, group):
    maskbd = maskbd_ref[...]
    maskf = maskbd.astype(F32)
    cch = xp_scr.shape[1]
    cw = cw_ref[...]
    ys = []
    for s in range(group):
        base = 16 * s
        xp_scr[base:base + 8, :] = jnp.zeros((8, cch), F32)
        xp_scr[base + 8 - (CONV_WIDTH - 1):base + 8, :] = buf_ref[s]
        xp_scr[base + 8:base + 16, :] = cqkv_ref[SROWS * s:SROWS * (s + 1), :]
        y = cw[3:4] * xp_scr[base + 8:base + 16, :]
        for i in range(1, CONV_WIDTH):
            y = y + cw[3 - i:4 - i] * xp_scr[base + 8 - i:base + 16 - i, :]
        ys.append(y)
    y = jnp.concatenate(ys, axis=0)
    y = y * jax.nn.sigmoid(y)
    q = y[:, 0:HEAD_W]
    k = y[:, HEAD_W:2 * HEAD_W]
    v = y[:, 2 * HEAD_W:3 * HEAD_W]
    q = q * lax.rsqrt(_head_sums(q * q, maskbd) + 1e-6) * (DELTA_K_DIM ** -0.5)
    k = k * lax.rsqrt(_head_sums(k * k, maskbd) + 1e-6)
    beta = beta_ref[...]
    a = jnp.exp(g_ref[...])

    pairs_k = [(t, j) for t in range(n_tok) for j in range(t)]
    pairs_q = [(t, j) for t in range(n_tok) for j in range(t + 1)]
    assert len(pairs_k) + len(pairs_q) <= 16
    pr_scr[...] = jnp.zeros_like(pr_scr)
    for s in range(group):
        r0, p0 = SROWS * s, 16 * s
        for r, (t, j) in enumerate(pairs_k):
            pr_scr[p0 + r:p0 + r + 1, :] = k[r0 + t:r0 + t + 1] * k[r0 + j:r0 + j + 1]
        for r, (t, j) in enumerate(pairs_q):
            rr = p0 + len(pairs_k) + r
            pr_scr[rr:rr + 1, :] = q[r0 + t:r0 + t + 1] * k[r0 + j:r0 + j + 1]
    dots = _head_sums(pr_scr[...], maskbd)

    rows = [slice(SROWS * s, SROWS * (s + 1)) for s in range(group)]
    sbd0s = [maskf * _dot3_r(s0_ref[s], tile_ref[...]) for s in range(group)]
    s16s = [sb.astype(BF16) for sb in sbd0s]
    rks = [_dot(k[r].astype(BF16), s16) for r, s16 in zip(rows, s16s)]
    rqs = [_dot(q[r].astype(BF16), s16) for r, s16 in zip(rows, s16s)]

    u_scr[...] = jnp.zeros_like(u_scr)
    kd_scr[...] = jnp.zeros_like(kd_scr)
    o_scr[...] = jnp.zeros_like(o_scr)
    a_alls = []
    for s in range(group):
        r0, p0 = SROWS * s, 16 * s
        kk = {tj: dots[p0 + r:p0 + r + 1] for r, tj in enumerate(pairs_k)}
        qk = {tj: dots[p0 + len(pairs_k) + r:p0 + len(pairs_k) + r + 1] for r, tj in enumerate(pairs_q)}
        rk_rows = [rks[s][t:t + 1] for t in range(n_tok)]
        rq_rows = [rqs[s][t:t + 1] for t in range(n_tok)]
        for t in range(n_tok):
            a_t = a[r0 + t:r0 + t + 1]
            u_t = beta[r0 + t:r0 + t + 1] * (v[r0 + t:r0 + t + 1] - a_t * rk_rows[t])
            for t2 in range(t + 1, n_tok):
                rk_rows[t2] = a_t * rk_rows[t2] + kk[(t2, t)] * u_t
            for t2 in range(t, n_tok):
                rq_rows[t2] = a_t * rq_rows[t2] + qk[(t2, t)] * u_t
            o_scr[r0 + t:r0 + t + 1, :] = rq_rows[t]
            u_scr[r0 + t:r0 + t + 1, :] = u_t
            tail = None
            for t2 in range(t + 1, n_tok):
                tail = a[r0 + t2:r0 + t2 + 1] if tail is None else tail * a[r0 + t2:r0 + t2 + 1]
            kd_scr[r0 + t:r0 + t + 1, :] = k[r0 + t:r0 + t + 1] if tail is None else k[r0 + t:r0 + t + 1] * tail
        a_all = a[r0:r0 + 1]
        for t in range(1, n_tok):
            a_all = a_all * a[r0 + t:r0 + t + 1]
        a_alls.append(a_all)
    upd = [_dot_tn(kd_scr[r, :].astype(BF16), u_scr[r, :].astype(BF16)) for r in rows]
    s_news = [sb * a_all + maskf * u for sb, a_all, u in zip(sbd0s, a_alls, upd)]
    for s in range(group):
        s_ref[s] = _dot3_r(s_news[s], tile_t_ref[...])

    o = o_scr[...]
    ms = _head_sums(o * o, maskbd) * (1.0 / DELTA_V_DIM)
    o_ref[...] = o * lax.rsqrt(ms + NORM_EPS) * gain_ref[...] * cz_ref[...]


def _delta_sample(proj, conv_state, state, layer, conv_w, gain, n_tok):
    n_samples = state.shape[1]
    cch = conv_state.shape[3]
    v_ = np.arange(DELTA_V_DIM)
    lane = np.arange(HEAD_W)
    tile_m = jnp.asarray((v_[:, None] == lane[None, :] % DELTA_V_DIM).astype(np.float32), BF16)
    tile_t = jnp.asarray((lane[:, None] % DELTA_V_DIM == v_[None, :]).astype(np.float32), BF16)
    lh = lane // 64
    maskbd = jnp.asarray((lh[:, None] == lh[None, :]).astype(np.float32), BF16)

    group = math.gcd(n_samples, 8)
    rows = group * SROWS

    def tile(width):
        return pl.BlockSpec((rows, width), lambda b: (b, 0))

    return pl.pallas_call(
        functools.partial(_delta_sample_body, n_tok=n_tok, group=group),
        out_shape=(jax.ShapeDtypeStruct((n_samples * SROWS, HEAD_W), F32),
                   jax.ShapeDtypeStruct((n_samples, HEAD_W, DELTA_V_DIM), F32)),
        grid=(n_samples // group,),
        in_specs=[tile(cch),
                  pl.BlockSpec((None, group, CONV_WIDTH - 1, cch), lambda b: (layer, b, 0, 0)),
                  tile(HEAD_W), tile(HEAD_W), tile(HEAD_W),
                  _const_spec((CONV_WIDTH, cch)), _const_spec((1, HEAD_W)),
                  pl.BlockSpec((None, group, HEAD_W, DELTA_V_DIM), lambda b: (layer, b, 0, 0)),
                  _const_spec(tile_m.shape), _const_spec(tile_t.shape), _const_spec(maskbd.shape)],
        out_specs=(tile(HEAD_W), pl.BlockSpec((group, HEAD_W, DELTA_V_DIM), lambda b: (b, 0, 0))),
        scratch_shapes=[pltpu.VMEM((2 * rows, cch), F32), pltpu.VMEM((2 * rows, HEAD_W), F32),
                        pltpu.VMEM((rows, HEAD_W), F32), pltpu.VMEM((rows, HEAD_W), F32),
                        pltpu.VMEM((rows, HEAD_W), F32)],
        compiler_params=_cparams("arbitrary"),
        name="delta_sample",
    )(proj["cqkv"], conv_state, proj["beta"], proj["g"], proj["cz"], conv_w.astype(F32),
      jnp.tile(gain.astype(F32), DELTA_HEADS).reshape(1, HEAD_W), state, tile_m, tile_t, maskbd)


def kernel(x_prompt, x_sample, cache_fox_k, cache_fox_v, cache_fox_logf, cache_diff_k, cache_diff_v, state_delta, state_conv, page_table, norm_ffn1, ffn1_wi, ffn1_wo, norm_mix, w_in, fox_f_bias, diff_lambda, diff_norm, delta_conv_w, delta_A_log, delta_dt_bias, delta_norm, w_branch, w_out, norm_ffn2, ffn2_wi, ffn2_wo, norm_final):
    n_seq, seq_len, d = x_prompt.shape
    n_smp, dec_seq, _ = x_sample.shape
    depth = w_in.shape[0]
    n_pool = cache_fox_k.shape[1]
    n_pages = page_table.shape[1]
    past_len = n_pages * PAGE_SIZE
    assert CONV_WIDTH - 1 <= dec_seq <= SROWS and seq_len % DELTA_CHUNK == 0

    wi1, wo1 = ffn1_wi.astype(BF16), ffn1_wo.astype(BF16)
    wi2, wo2 = ffn2_wi.astype(BF16), ffn2_wo.astype(BF16)
    w_p = _relayout_w_in(w_in, _PROMPT_COLS)
    w_s = _relayout_w_in(w_in, _SAMPLE_COLS)
    w_vt = _relayout_w_values_t(w_in)
    w_gates = w_in[:, :, w_in.shape[2] - N_BRANCH * d:].astype(BF16)
    wb, wo = w_branch.astype(BF16), w_out.astype(BF16)

    def pages_t(cache):
        c = cache.reshape(depth, n_pool, PAGE_SIZE, HEAD_W)
        return jnp.swapaxes(c, 2, 3)

    caches = (pages_t(cache_fox_k), pages_t(cache_fox_v), jnp.swapaxes(cache_fox_logf, 2, 3),
              pages_t(cache_diff_k), pages_t(cache_diff_v))
    s_delta = state_delta.reshape(depth, n_smp, HEAD_W, DELTA_V_DIM)

    xp = x_prompt.reshape(n_seq * seq_len, d)
    xs = jnp.pad(x_sample, ((0, 0), (0, SROWS - dec_seq), (0, 0))).reshape(n_smp * SROWS, d)
    cos_p, sin_p = _rope_tables(seq_len, 0, seq_len)
    cos_s, sin_s = _rope_tables(n_smp * SROWS, past_len, SROWS)

    cache_bufs = {name: jnp.zeros((depth, n_seq, HEAD_W, seq_len), F32) for name in _CACHE_OUTS}

    new_p, new_s = [], []
    for l in range(depth):
        lam_init = 0.8 - 0.6 * math.exp(-0.3 * l)
        xp = _ffn(xp, norm_ffn1[l], wi1, wo1, l)
        xs = _ffn(xs, norm_ffn1[l], wi1, wo1, l)
        pp = _inproj(xp, norm_mix[l], w_p, l, cos_p, sin_p, fox_f_bias[l], delta_A_log[l], delta_dt_bias[l],
                     prompt=True, seq_len=seq_len, w_vt=w_vt, cache_bufs=cache_bufs)
        cache_bufs = {name: pp[name] for name in _CACHE_OUTS}
        ps = _inproj(xs, norm_mix[l], w_s, l, cos_s, sin_s, fox_f_bias[l], delta_A_log[l], delta_dt_bias[l],
                     prompt=False, seq_len=SROWS)

        of_p, od_p = _attn_prompt(pp["fq_pad"], pp["fk_pad"], pp["fvt"], pp["dq"], pp["dkb"], pp["dvt"],
                                  diff_lambda[l], diff_norm[l], lam_init, n_seq, seq_len)
        ol_p, sbd = _delta_prompt(pp["cqkv"], pp["beta"], pp["g"], pp["cz"], delta_conv_w[l], delta_norm[l],
                                  n_seq, seq_len)
        of_s, od_s = _sample_attn(page_table, caches, l, ps, diff_lambda[l], diff_norm[l], lam_init)
        ol_s, s_new = _delta_sample(ps, state_conv, s_delta, l, delta_conv_w[l], delta_norm[l], dec_seq)

        xp = _merge(xp, norm_mix[l], w_gates, wb, wo, l, (of_p, od_p, ol_p))
        xs = _merge(xs, norm_mix[l], w_gates, wb, wo, l, (of_s, od_s, ol_s))
        final = norm_final if l == depth - 1 else None
        xp = _ffn(xp, norm_ffn2[l], wi2, wo2, l, final)
        xs = _ffn(xs, norm_ffn2[l], wi2, wo2, l, final)

        def p3(a):
            return a.reshape(n_seq, seq_len, a.shape[1])

        def s3(a):
            return a.reshape(n_smp, SROWS, a.shape[1])[:, :dec_seq]

        new_p.append(dict(
            fox_logf=p3(pp["small"])[:, :, :FOX_HEADS],
            delta=jnp.stack([sbd[:, h * 64:(h + 1) * 64, h * 64:(h + 1) * 64] for h in range(DELTA_HEADS)], axis=1),
            conv=p3(pp["cqkv"])[:, seq_len - (CONV_WIDTH - 1):]))
        new_s.append(dict(
            fox_k=s3(ps["fk"]).reshape(n_smp, dec_seq, FOX_HEADS, FOX_DIM),
            fox_v=s3(ps["fv"]).reshape(n_smp, dec_seq, FOX_HEADS, FOX_DIM),
            fox_logf=s3(ps["small"])[:, :, :FOX_HEADS],
            diff_k=s3(ps["dk"]).reshape(n_smp, dec_seq, DIFF_HEADS, 2, DIFF_QK_DIM),
            diff_v=s3(ps["dv"]).reshape(n_smp, dec_seq, DIFF_HEADS, DIFF_V_DIM),
            delta=s_new.reshape(n_smp, DELTA_HEADS, DELTA_K_DIM, DELTA_V_DIM),
            conv=s3(ps["cqkv"])[:, dec_seq - (CONV_WIDTH - 1):]))

    def stk(lst, name):
        return jnp.stack([dd[name] for dd in lst])

    def token_major(buf, *feat):
        nf = len(feat)
        b = buf.reshape((depth, n_seq) + feat + (seq_len,))
        return jnp.transpose(b, (0, 1, 2 + nf) + tuple(range(2, 2 + nf)))

    y_prompt = xp.reshape(n_seq, seq_len, d)
    y_sample = xs.reshape(n_smp, SROWS, d)[:, :dec_seq]
    return (y_prompt, y_sample,
            token_major(cache_bufs["fkt"], FOX_HEADS, FOX_DIM), stk(new_s, "fox_k"),
            token_major(cache_bufs["fvt32"], FOX_HEADS, FOX_DIM), stk(new_s, "fox_v"),
            stk(new_p, "fox_logf"), stk(new_s, "fox_logf"),
            token_major(cache_bufs["dkt"], DIFF_HEADS, 2, DIFF_QK_DIM), stk(new_s, "diff_k"),
            token_major(cache_bufs["dvt32"], DIFF_HEADS, DIFF_V_DIM), stk(new_s, "diff_v"),
            stk(new_p, "delta"), stk(new_s, "delta"),
            stk(new_p, "conv"), stk(new_s, "conv"))
```

```python
import functools
import math

import numpy as np
import jax
import jax.numpy as jnp
from jax import lax
from jax.experimental import pallas as pl
from jax.experimental.pallas import tpu as pltpu

F32 = jnp.float32
BF16 = jnp.bfloat16

FOX_HEADS = 4
FOX_DIM = 64
DIFF_HEADS = 4
DIFF_QK_DIM = 32
DIFF_V_DIM = 64
DELTA_HEADS = 4
DELTA_K_DIM = 64
DELTA_V_DIM = 64
CONV_WIDTH = 4
DELTA_CHUNK = 64
N_BRANCH = 3
HEAD_W = 256
PAGE_SIZE = 128
ROPE_THETA = 10000.0
NORM_EPS = 1e-6
LOG2E = 1.4426950408889634
FOX_QSCALE = FOX_DIM ** -0.5 * LOG2E
DIFF_QSCALE = DIFF_QK_DIM ** -0.5 * LOG2E

V7X_VMEM_LIMIT_BYTES = 56 * 1024 * 1024
AUG_C = 64
AUG_K = 67
ONES_LANE = 64


def _cparams(*sem):
    return pltpu.CompilerParams(dimension_semantics=sem, vmem_limit_bytes=V7X_VMEM_LIMIT_BYTES)


def _dot(a, b):
    return jnp.dot(a, b, preferred_element_type=F32)


def _dot_nt(a, b):
    return lax.dot_general(a, b, (((1,), (1,)), ((), ())), preferred_element_type=F32)


def _dot_tn(a, b):
    return lax.dot_general(a, b, (((0,), (0,)), ((), ())), preferred_element_type=F32)


def _split3(x):
    hi = x.astype(BF16)
    r = x - hi.astype(F32)
    mid = r.astype(BF16)
    lo = (r - mid.astype(F32)).astype(BF16)
    return hi, mid, lo


def _dot3_r(x, m):
    hi, mid, lo = _split3(x)
    return _dot(hi, m) + _dot(mid, m) + _dot(lo, m)


def _dot3_l(m, x):
    hi, mid, lo = _split3(x)
    return _dot(m, hi) + _dot(m, mid) + _dot(m, lo)


def _rms(x, gain):
    ms = jnp.mean(x * x, axis=-1, keepdims=True)
    return x * lax.rsqrt(ms + NORM_EPS) * gain


def _const_spec(shape):
    nd = len(shape)
    return pl.BlockSpec(shape, lambda *_: (0,) * nd, pipeline_mode=pl.Buffered(1))


V7X_MXU_DIM = 256


def _ffn_body(x_ref, g_ref, wi_ref, wo_ref, gf_ref, o_ref, *, final_norm, dff, tf):
    x = x_ref[...]
    h = _rms(x, g_ref[...]).astype(BF16)
    acc = None
    for c0 in range(0, dff, tf):
        gate = _dot(h, wi_ref[:, c0:c0 + tf])
        up = _dot(h, wi_ref[:, dff + c0:dff + c0 + tf])
        act = (gate * jax.nn.sigmoid(gate) * up).astype(BF16)
        part = _dot(act, wo_ref[c0:c0 + tf, :])
        acc = part if acc is None else acc + part
    y = x + 0.5 * acc
    if final_norm:
        y = _rms(y, gf_ref[...])
    o_ref[...] = y


def _ffn(x, gain, wi, wo, layer, final_gain=None):
    r, d = x.shape
    dff = wo.shape[1]
    tm = min(r, 512)
    tf = V7X_MXU_DIM if dff % V7X_MXU_DIM == 0 else dff
    final_norm = final_gain is not None
    gf = final_gain if final_norm else gain
    return pl.pallas_call(
        functools.partial(_ffn_body, final_norm=final_norm, dff=dff, tf=tf),
        out_shape=jax.ShapeDtypeStruct((r, d), F32),
        grid=(r // tm,),
        in_specs=[
            pl.BlockSpec((tm, d), lambda i: (i, 0)),
            _const_spec((1, d)),
            pl.BlockSpec((None, d, 2 * dff), lambda i: (layer, 0, 0), pipeline_mode=pl.Buffered(1)),
            pl.BlockSpec((None, dff, d), lambda i: (layer, 0, 0), pipeline_mode=pl.Buffered(1)),
            _const_spec((1, d)),
        ],
        out_specs=pl.BlockSpec((tm, d), lambda i: (i, 0)),
        compiler_params=_cparams("arbitrary"),
        name="ffn",
    )(x, gain.reshape(1, d), wi, wo, gf.reshape(1, d))


def _rope_body(inv_ref, cos_ref, sin_ref, *, tm, base, period):
    i = pl.program_id(0)
    row = lax.broadcasted_iota(jnp.int32, (tm, HEAD_W), 0) + i * tm
    lane = lax.broadcasted_iota(jnp.int32, (tm, HEAD_W), 1)
    pos = base + (row & (period - 1))
    ang = pos.astype(F32) * inv_ref[...]
    cos_ref[...] = jnp.cos(ang)
    s = jnp.sin(ang)
    first_half = (lane & (DIFF_QK_DIM - 1)) < DIFF_QK_DIM // 2
    sin_ref[...] = jnp.where(first_half, -s, s)


def _rope_tables(rows, base, period):
    assert period & (period - 1) == 0
    half = DIFF_QK_DIM // 2
    inv = ROPE_THETA ** (-jnp.arange(0, DIFF_QK_DIM, 2, dtype=F32) / DIFF_QK_DIM)
    inv_lanes = jnp.tile(inv, HEAD_W // half).reshape(1, HEAD_W)
    tm = min(rows, 512)
    return pl.pallas_call(
        functools.partial(_rope_body, tm=tm, base=base, period=period),
        out_shape=(jax.ShapeDtypeStruct((rows, HEAD_W), F32),) * 2,
        grid=(rows // tm,),
        in_specs=[pl.BlockSpec((1, HEAD_W), lambda i: (0, 0))],
        out_specs=(pl.BlockSpec((tm, HEAD_W), lambda i: (i, 0)),) * 2,
        compiler_params=_cparams("arbitrary"),
        name="rope_tables",
    )(inv_lanes)


_PROMPT_COLS = (("fq_pad", 512), ("fk_pad", 512), ("dq", 256), ("dk", 256),
                ("cqkv", 768), ("cz", 256), ("cb_rep", 256), ("ca_rep", 256), ("small", 128))
_SAMPLE_COLS = (("fq", 256), ("fk", 256), ("fv", 256), ("dq", 256), ("dk", 256), ("dv", 256),
                ("cqkv", 768), ("cz", 256), ("cb_rep", 256), ("ca_rep", 256), ("small", 128))


def _layout(cols):
    lay, off = {}, 0
    for name, width in cols:
        lay[name] = (off, width)
        off += width
    return lay, off


def _relayout_w_in(w_in, cols):
    depth, d, _ = w_in.shape
    sizes = (256, 256, 256, 4, 256, 256, 256, 768, 4, 4, 256)
    offs = np.concatenate([[0], np.cumsum(sizes)])
    fq, fk, fv, ff, dq, dk, dv, cqkv, cb, ca, cz = (w_in[:, :, int(offs[i]):int(offs[i + 1])] for i in range(11))

    def pad_heads(w):
        w = w.reshape(depth, d, 4, 64)
        return jnp.pad(w, ((0, 0), (0, 0), (0, 0), (0, 64))).reshape(depth, d, 512)

    def rep(w):
        return jnp.repeat(w, 64, axis=2)

    table = {
        "fq": fq, "fk": fk, "fv": fv, "fq_pad": pad_heads(fq), "fk_pad": pad_heads(fk),
        "dq": dq, "dk": dk, "dv": dv,
        "cqkv": cqkv, "cz": cz, "cb_rep": rep(cb), "ca_rep": rep(ca),
        "small": jnp.pad(ff, ((0, 0), (0, 0), (0, 124))),
    }
    return jnp.concatenate([table[name] for name, _ in cols], axis=2).astype(BF16)


def _relayout_w_values_t(w_in):
    depth, d, _ = w_in.shape

    def head_rows(w):
        wt = jnp.swapaxes(w, 1, 2).reshape(depth, 4, 64, d)
        return jnp.pad(wt, ((0, 0), (0, 0), (0, 64), (0, 0))).reshape(depth, 512, d)

    fv = w_in[:, :, 512:768]
    dv = w_in[:, :, 1284:1540]
    return jnp.concatenate([head_rows(fv), head_rows(dv)], axis=1).astype(BF16)


def _aug_place_mats():
    pq = np.zeros((384, 512), np.float32)
    pk = np.zeros((384, 512), np.float32)
    for piece in range(3):
        for h in range(FOX_HEADS):
            pq[piece * 128 + h, h * 128 + AUG_C + piece] = 1.0
            pk[piece * 128 + h, h * 128 + AUG_K + piece] = -1.0
    return jnp.asarray(pq, BF16), jnp.asarray(pk, BF16)


def _lane_ones(shape, lo, hi):
    lane = lax.broadcasted_iota(jnp.int32, shape, 1) & 127
    mid2 = lo + hi
    return jnp.where(jnp.abs(2 * lane - mid2) <= hi - lo, 1.0, 0.0).astype(F32)


def _swap_rope_halves(x):
    half = DIFF_QK_DIM // 2
    lane = lax.broadcasted_iota(jnp.int32, (x.shape[0], 128), 1)
    first = (lane & (DIFF_QK_DIM - 1)) < half
    parts = []
    for c in range(0, x.shape[1], 128):
        xc = x[:, c:c + 128]
        parts.append(jnp.where(first, pltpu.roll(xc, 128 - half, axis=1), pltpu.roll(xc, half, axis=1)))
    return jnp.concatenate(parts, axis=1)


def _inproj_body(*refs, lay, prompt, tiles_per_seq, tm, n_alias=0):
    if prompt:
        (x_ref, g_ref, w_ref, cos_ref, sin_ref, fb_ref, alog_ref, dtb_ref, tri_ref, pq_ref, pk_ref, wvt_ref) = refs[:12]
        (fq_o, fk_pad_o, fvt_o, small_o, dq_o, dkb_o, dvt_o, cqkv_o, cz_o, beta_o, g_o,
         fkt_o, fvt32_o, dkt_o, dvt32_o, carry_scr) = refs[12 + n_alias:]
    else:
        (x_ref, g_ref, w_ref, cos_ref, sin_ref, fb_ref, alog_ref, dtb_ref,
         fq_o, fk_o, fv_o, small_o, dq_o, dk_o, dv_o, cqkv_o, cz_o, beta_o, g_o) = refs

    h = _rms(x_ref[...], g_ref[...]).astype(BF16)

    def proj(name):
        off, width = lay[name]
        return _dot(h, w_ref[:, off:off + width])

    logf = jax.nn.log_sigmoid(proj("small") + fb_ref[...])
    small_o[...] = logf

    if prompt:
        i = pl.program_id(0)

        @pl.when(i % tiles_per_seq == 0)
        def _():
            carry_scr[...] = jnp.zeros_like(carry_scr)

        c = _dot3_l(tri_ref[...], logf) + carry_scr[...]
        carry_scr[...] = c[tm - 1:tm, :]
        c1, c2, c3 = _split3(c * LOG2E)
        c123 = jnp.concatenate([c1, c2, c3], axis=1)
        shape = (tm, 512)
        q = proj("fq_pad") * FOX_QSCALE + _dot(c123, pq_ref[...]) + _lane_ones(shape, AUG_K, AUG_K + 2)
        fq_o[...] = q.astype(BF16)
        k32 = proj("fk_pad")
        fk_pad_o[...] = (k32 + _dot(c123, pk_ref[...]) + _lane_ones(shape, AUG_C, AUG_C + 2)).astype(BF16)
        k32_t = k32.T
        for hd in range(FOX_HEADS):
            fkt_o[hd * 64:(hd + 1) * 64, :] = k32_t[hd * 128:hd * 128 + 64]
        vt = _dot_nt(wvt_ref[...], h)
        for hd in range(4):
            fvt32_o[hd * 64:(hd + 1) * 64, :] = vt[hd * 128:hd * 128 + 64]
            dvt32_o[hd * 64:(hd + 1) * 64, :] = vt[512 + hd * 128:512 + hd * 128 + 64]
        row = lax.broadcasted_iota(jnp.int32, vt.shape, 0) & 127
        vt = (vt + jnp.where(row == ONES_LANE, 1.0, 0.0)).astype(BF16)
        fvt_o[...] = vt[0:512]
        dvt_o[...] = vt[512:1024]
    else:
        fq_o[...] = proj("fq") * FOX_QSCALE
        fk_o[...] = proj("fk")
        fv_o[...] = proj("fv")
        dv_o[...] = proj("dv")

    cos = cos_ref[...]
    sin = sin_ref[...]
    dq = proj("dq")
    dq_o[...] = ((dq * cos + _swap_rope_halves(dq) * sin) * DIFF_QSCALE).astype(dq_o.dtype)
    dk = proj("dk")
    dk = dk * cos + _swap_rope_halves(dk) * sin
    if prompt:
        dkb_o[...] = dk.astype(BF16)
        dkt_o[...] = dk.T
    else:
        dk_o[...] = dk

    cqkv_o[...] = proj("cqkv")
    cz = proj("cz")
    cz_o[...] = cz * jax.nn.sigmoid(cz)
    beta_o[...] = jax.nn.sigmoid(proj("cb_rep"))
    g_o[...] = -jnp.exp(alog_ref[...]) * jax.nn.softplus(proj("ca_rep") + dtb_ref[...])


_CACHE_OUTS = ("fkt", "fvt32", "dkt", "dvt32")


def _inproj(x, gain, w, layer, cos, sin, fbias, alog, dtb, *, prompt, seq_len, w_vt=None, cache_bufs=None):
    r, d = x.shape
    lay, nc = _layout(_PROMPT_COLS if prompt else _SAMPLE_COLS)
    assert w.shape[2] == nc
    tm = min(r, 512 if prompt else 256)
    tm = min(tm, seq_len) if prompt else tm
    fb = jnp.pad(fbias.astype(F32), (0, 124)).reshape(1, 128)
    alog_l = jnp.repeat(alog.astype(F32), 64).reshape(1, 256)
    dtb_l = jnp.repeat(dtb.astype(F32), 64).reshape(1, 256)

    def rows(width):
        return pl.BlockSpec((tm, width), lambda i: (i, 0))

    rope_tiles = cos.shape[0] // tm
    rope = pl.BlockSpec((tm, 256), lambda i: (i % rope_tiles, 0))
    in_specs = [rows(d), _const_spec((1, d)),
                pl.BlockSpec((None, d, nc), lambda i: (layer, 0, 0), pipeline_mode=pl.Buffered(1)),
                rope, rope, _const_spec((1, 128)), _const_spec((1, 256)), _const_spec((1, 256))]
    args = [x, gain.reshape(1, d), w, cos, sin, fb, alog_l, dtb_l]
    if prompt:
        tri = jnp.asarray(np.tril(np.ones((tm, tm), np.float32)), BF16)
        pq, pk = _aug_place_mats()
        in_specs += [_const_spec((tm, tm)), _const_spec((384, 512)), _const_spec((384, 512)),
                     pl.BlockSpec((None, 1024, d), lambda i: (layer, 0, 0), pipeline_mode=pl.Buffered(1))]
        args += [tri, pq, pk, w_vt]
        outs = (("fq_pad", 512, BF16), ("fk_pad", 512, BF16), ("fvt", 0, BF16), ("small", 128, F32),
                ("dq", 256, BF16), ("dkb", 256, BF16), ("dvt", 0, BF16), ("cqkv", 768, F32), ("cz", 256, F32),
                ("beta", 256, F32), ("g", 256, F32))
        scratch = [pltpu.VMEM((1, 128), F32)]
    else:
        outs = (("fq", 256, F32), ("fk", 256, F32), ("fv", 256, F32), ("small", 128, F32), ("dq", 256, F32),
                ("dk", 256, F32), ("dv", 256, F32), ("cqkv", 768, F32), ("cz", 256, F32),
                ("beta", 256, F32), ("g", 256, F32))
        scratch = []
    tps = max(seq_len // tm, 1)
    out_shape = [jax.ShapeDtypeStruct((r, wd) if wd else (512, r), dt) for _, wd, dt in outs]
    out_specs = [rows(wd) if wd else pl.BlockSpec((512, tm), lambda i: (0, i)) for _, wd, _ in outs]
    names = [name for name, _, _ in outs]
    aliases = {}
    if prompt:
        for j, name in enumerate(_CACHE_OUTS):
            buf = cache_bufs[name]
            out_shape.append(jax.ShapeDtypeStruct(buf.shape, buf.dtype))
            out_specs.append(pl.BlockSpec((None, None, HEAD_W, tm), lambda i: (layer, i // tps, 0, i % tps)))
            names.append(name)
            aliases[len(args)] = len(outs) + j
            in_specs.append(pl.BlockSpec(memory_space=pl.ANY))
            args.append(buf)
    res = pl.pallas_call(
        functools.partial(_inproj_body, lay=lay, prompt=prompt, tiles_per_seq=tps, tm=tm, n_alias=len(aliases)),
        out_shape=tuple(out_shape),
        grid=(r // tm,),
        in_specs=in_specs,
        out_specs=tuple(out_specs),
        scratch_shapes=scratch,
        input_output_aliases=aliases,
        compiler_params=_cparams("arbitrary"),
        name="inproj_prompt" if prompt else "inproj_sample",
    )(*args)
    return dict(zip(names, res))


def _tri_schedule(nq):
    qi = np.concatenate([np.full(i + 1, i, np.int32) for i in range(nq)])
    kj = np.concatenate([np.arange(i + 1, dtype=np.int32) for i in range(nq)])
    return jnp.asarray(qi), jnp.asarray(kj)


def _online_softmax_step(st, vt, m_ref, acc_ref):
    m_old = m_ref[...]
    m_new = jnp.maximum(m_old, jnp.max(st, axis=0, keepdims=True))
    alpha = jnp.exp2(m_old - m_new)
    p = jnp.exp2(st - m_new)
    acc_ref[...] = alpha * acc_ref[...] + _dot(vt, p.astype(BF16))
    m_ref[...] = m_new


def _causal_mask(st):
    key = lax.broadcasted_iota(jnp.int32, st.shape, 0)
    qry = lax.broadcasted_iota(jnp.int32, st.shape, 1)
    return jnp.where(key <= qry, st, -jnp.inf)


def _diff_lambda(lam_ref, lam_init):
    lp = lam_ref[...]
    s1 = jnp.sum(lp[0:1] * lp[1:2], axis=1, keepdims=True)
    s2 = jnp.sum(lp[2:3] * lp[3:4], axis=1, keepdims=True)
    return jnp.exp(s1) - jnp.exp(s2) + lam_init


N_ATTN_MAPS = FOX_HEADS + 2 * DIFF_HEADS
ATTN_LOOKAHEAD = 3


def _attn_body(qi_ref, kj_ref, fq_ref, fk_ref, fvt_ref, dq_ref, dk_ref, dvt_ref, lam_ref, gain_ref,
               of_ref, od_ref, qm_scr, m_scr, acc_scr, *, lam_init):
    t = pl.program_id(1)
    qi = qi_ref[t]
    kj = kj_ref[t]
    n_diff = 2 * DIFF_HEADS

    @pl.when(kj == 0)
    def _():
        m_scr[...] = jnp.full_like(m_scr, -jnp.inf)
        acc_scr[...] = jnp.zeros_like(acc_scr)
        q = dq_ref[...]
        lane_map = lax.broadcasted_iota(jnp.int32, q.shape, 1) // DIFF_QK_DIM
        for m in range(n_diff):
            qm_scr[m] = jnp.where(lane_map == m, q, jnp.zeros_like(q))

    def scores(j):
        if j < FOX_HEADS:
            sl = slice(j * 128, (j + 1) * 128)
            return _dot_nt(fk_ref[:, sl], fq_ref[:, sl])
        return _dot_nt(dk_ref[...], qm_scr[j - FOX_HEADS])

    def values_t(j):
        if j < FOX_HEADS:
            return fvt_ref[j * 128:(j + 1) * 128, :]
        h = (j - FOX_HEADS) // 2
        return dvt_ref[h * 128:(h + 1) * 128, :]

    def step(masked):
        pending = [scores(j) for j in range(ATTN_LOOKAHEAD)]
        for j in range(N_ATTN_MAPS):
            st = pending.pop(0)
            if j + ATTN_LOOKAHEAD < N_ATTN_MAPS:
                pending.append(scores(j + ATTN_LOOKAHEAD))
            if masked:
                st = _causal_mask(st)
            _online_softmax_step(st, values_t(j), m_scr.at[j], acc_scr.at[j])

    @pl.when(kj < qi)
    def _():
        step(False)

    @pl.when(kj == qi)
    def _():
        step(True)
        for h in range(FOX_HEADS):
            a = acc_scr[h]
            o = a[:FOX_DIM] / a[ONES_LANE:ONES_LANE + 1]
            of_ref[:, h * FOX_DIM:(h + 1) * FOX_DIM] = o.T.astype(of_ref.dtype)
        lam = _diff_lambda(lam_ref, lam_init)
        tq = od_ref.shape[0]
        gain = jnp.concatenate([gain_ref[...]] * (tq // 128), axis=1)
        for h in range(DIFF_HEADS):
            a0 = acc_scr[FOX_HEADS + 2 * h]
            a1 = acc_scr[FOX_HEADS + 2 * h + 1]
            o = (a0[:DIFF_V_DIM] / a0[ONES_LANE:ONES_LANE + 1]
                 - lam * (a1[:DIFF_V_DIM] / a1[ONES_LANE:ONES_LANE + 1]))
            ms = jnp.mean(o * o, axis=0, keepdims=True)
            o = o * lax.rsqrt(ms + NORM_EPS) * gain * (1.0 - lam_init)
            od_ref[:, h * DIFF_V_DIM:(h + 1) * DIFF_V_DIM] = o.T.astype(od_ref.dtype)


def _attn_prompt(fq_pad, fk_pad, fvt_pad, dq, dk, dvt_pad, lam_p, gain, lam_init, n_seq, seq_len):
    tq = min(seq_len, 512)
    nq = seq_len // tq
    qi, kj = _tri_schedule(nq)

    def q_rows(width):
        return pl.BlockSpec((tq, width), lambda n, t, qi, kj: (n * nq + qi[t], 0))

    def k_rows(width):
        return pl.BlockSpec((tq, width), lambda n, t, qi, kj: (n * nq + kj[t], 0))

    def k_cols():
        return pl.BlockSpec((512, tq), lambda n, t, qi, kj: (0, n * nq + kj[t]))

    gs = pltpu.PrefetchScalarGridSpec(
        num_scalar_prefetch=2,
        grid=(n_seq, int(qi.shape[0])),
        in_specs=[q_rows(512), k_rows(512), k_cols(), q_rows(HEAD_W), k_rows(HEAD_W), k_cols(),
                  pl.BlockSpec((4, DIFF_QK_DIM), lambda n, t, qi, kj: (0, 0)),
                  pl.BlockSpec((DIFF_V_DIM, 128), lambda n, t, qi, kj: (0, 0))],
        out_specs=(q_rows(HEAD_W), q_rows(HEAD_W)),
        scratch_shapes=[pltpu.VMEM((2 * DIFF_HEADS, tq, HEAD_W), BF16), pltpu.VMEM((N_ATTN_MAPS, 1, tq), F32),
                        pltpu.VMEM((N_ATTN_MAPS, 128, tq), F32)],
    )
    gain_rows = jnp.broadcast_to(gain.astype(F32).reshape(DIFF_V_DIM, 1), (DIFF_V_DIM, 128))
    return pl.pallas_call(
        functools.partial(_attn_body, lam_init=lam_init),
        out_shape=(jax.ShapeDtypeStruct((n_seq * seq_len, HEAD_W), BF16),) * 2,
        grid_spec=gs,
        compiler_params=_cparams("arbitrary", "arbitrary"),
        name="attn_prompt",
    )(qi, kj, fq_pad, fk_pad, fvt_pad, dq, dk, dvt_pad, lam_p.astype(F32), gain_rows)


def _merge_body(x_ref, g_ref, wg_ref, wb_ref, wo_ref, b0_ref, b1_ref, b2_ref, o_ref):
    x = x_ref[...]
    d = x.shape[1]
    h = _rms(x, g_ref[...]).astype(BF16)
    mix = None
    for b, b_ref in enumerate((b0_ref, b1_ref, b2_ref)):
        gate = jax.nn.sigmoid(_dot(h, wg_ref[:, b * d:(b + 1) * d]))
        term = gate * _dot(b_ref[...].astype(BF16), wb_ref[b])
        mix = term if mix is None else mix + term
    o_ref[...] = x + _dot(mix.astype(BF16), wo_ref[...])


def _merge(x, gain, w_gates, w_branch, w_out, layer, branches):
    r, d = x.shape
    tm = min(r, 512)

    def rows(width):
        return pl.BlockSpec((tm, width), lambda i: (i, 0))

    return pl.pallas_call(
        _merge_body,
        out_shape=jax.ShapeDtypeStruct((r, d), F32),
        grid=(r // tm,),
        in_specs=[rows(d), _const_spec((1, d)),
                  pl.BlockSpec((None, d, N_BRANCH * d), lambda i: (layer, 0, 0), pipeline_mode=pl.Buffered(1)),
                  pl.BlockSpec((None, N_BRANCH, HEAD_W, d), lambda i: (layer, 0, 0, 0), pipeline_mode=pl.Buffered(1)),
                  pl.BlockSpec((None, d, d), lambda i: (layer, 0, 0), pipeline_mode=pl.Buffered(1)),
                  rows(HEAD_W), rows(HEAD_W), rows(HEAD_W)],
        out_specs=rows(d),
        compiler_params=_cparams("arbitrary"),
        name="merge",
    )(x, gain.reshape(1, d), w_gates, w_branch, w_out, *branches)


def _delta_consts():
    c = DELTA_CHUNK
    i = np.arange(c)
    lane = np.arange(HEAD_W)
    lh, lj = lane // c, lane % c
    tril = i[:, None] >= i[None, :]
    suffix = i[None, :] > i[:, None]
    lvl = []
    for k in range(1, 7):
        b = 1 << k
        lvl.append((i[:, None] // b == lj[None, :] // b) & (i[:, None] // (b // 2) != lj[None, :] // (b // 2)))
    f = lambda a, dt: jnp.asarray(np.asarray(a, np.float32), dt)
    return dict(
        lhsg=f(np.concatenate([tril, suffix], axis=0), BF16),
        ones=f(np.ones((c, c)), BF16),
        triu4=f(i[:, None] <= lj[None, :], F32),
        tril4=f(i[:, None] >= lj[None, :], F32),
        stril4=f(i[:, None] > lj[None, :], F32),
        eye4=f(i[:, None] == lj[None, :], F32),
        lvl=f(np.stack(lvl), F32),
        maskbd=f(lh[:, None] == lh[None, :], BF16),
    )


def _bd(x4, maskbd):
    return jnp.concatenate([x4, x4, x4, x4], axis=0) * maskbd


def _mm_hi(xs, ys, maskbd):
    xh = [x.astype(BF16) for x in xs]
    xl = [(x - h.astype(F32)).astype(BF16) for x, h in zip(xs, xh)]
    yh = [y.astype(BF16) for y in ys]
    yl = [(y - h.astype(F32)).astype(BF16) for y, h in zip(ys, yh)]
    ybh = [_bd(h, maskbd) for h in yh]
    ybl = [_bd(l, maskbd) for l in yl]
    return [_dot(a, bh) + _dot(al, bh) + _dot(a, bl) for a, al, bh, bl in zip(xh, xl, ybh, ybl)]


def _unit_lower_inverse(a4s, eye4, lvl_ref, maskbd):
    ts = [eye4 - a4 * lvl_ref[0] for a4 in a4s]
    for k in range(1, 6):
        ms = [a4 * lvl_ref[k] for a4 in a4s]
        xs = _mm_hi(ms, ts, maskbd)
        ts = [t - d for t, d in zip(ts, _mm_hi(ts, xs, maskbd))]
    return ts


def _head_sums(x, maskbd):
    hi = x.astype(BF16)
    lo = (x - hi.astype(F32)).astype(BF16)
    return _dot(hi, maskbd) + _dot(lo, maskbd)


def _delta_body(cqkv_ref, beta_ref, g_ref, cz_ref, cw_ref, gain_ref,
                lhsg_ref, ones_ref, triu4_ref, tril4_ref, stril4_ref, eye4_ref, lvl_ref, maskbd_ref,
                o_ref, s_ref, xbuf, ub_scr, w_scr, qk_scr, qe_scr, kd_scr, eg_scr, o_scr, sbd_scr, *, tc):
    j = pl.program_id(1)
    c = DELTA_CHUNK
    maskbd = maskbd_ref[...]
    maskf = maskbd.astype(F32)

    @pl.when(j == 0)
    def _():
        xbuf[0:8, :] = jnp.zeros((8, xbuf.shape[1]), F32)
        sbd_scr[...] = jnp.zeros_like(sbd_scr)

    xbuf[8:8 + tc, :] = cqkv_ref[...]
    cw = cw_ref[...]
    y = cw[3:4] * xbuf[8:8 + tc, :]
    for i in range(1, CONV_WIDTH):
        y = y + cw[3 - i:4 - i] * xbuf[8 - i:8 - i + tc, :]
    xbuf[0:8, :] = xbuf[tc:tc + 8, :]
    y = y * jax.nn.sigmoid(y)
    q = y[:, 0:HEAD_W]
    k = y[:, HEAD_W:2 * HEAD_W]
    qn = q * lax.rsqrt(_head_sums(q * q, maskbd) + 1e-6) * (DELTA_K_DIM ** -0.5)
    kn = k * lax.rsqrt(_head_sums(k * k, maskbd) + 1e-6)
    v = y[:, 2 * HEAD_W:3 * HEAD_W]
    beta = beta_ref[...]
    g = g_ref[...]

    cs = [slice(ci * c, (ci + 1) * c) for ci in range(tc // c)]
    qcs, kcs, vcs = [qn[s] for s in cs], [kn[s] for s in cs], [v[s] for s in cs]
    bcs, gcs = [beta[s] for s in cs], [g[s] for s in cs]
    gsums = [_dot3_l(lhsg_ref[...], gc) for gc in gcs]
    grows = [_dot3_l(ones_ref[...], gc * triu4_ref[...]) for gc in gcs]
    gcols, gsufs = [gs[0:c] for gs in gsums], [gs[c:2 * c] for gs in gsums]
    decays = [jnp.where(tril4_ref[...] > 0, jnp.exp(gcol - grow), 0.0) for gcol, grow in zip(gcols, grows)]
    exp_gs = [jnp.exp(gcol) for gcol in gcols]
    kbs = [kc * bc for kc, bc in zip(kcs, bcs)]
    kbds = [_bd(kc.astype(BF16), maskbd) for kc in kcs]
    a4s = [_dot_nt(kb.astype(BF16), kbd) * decay * stril4_ref[...] for kb, kbd, decay in zip(kbs, kbds, decays)]
    qk4s = [_dot_nt(qc.astype(BF16), kbd) * decay for qc, kbd, decay in zip(qcs, kbds, decays)]
    t16s = [t.astype(BF16) for t in _unit_lower_inverse(a4s, eye4_ref[...], lvl_ref, maskbd)]
    u_bases = [_dot(t16, _bd((vc * bc).astype(BF16), maskbd)) for t16, vc, bc in zip(t16s, vcs, bcs)]
    ws = [_dot(t16, _bd((kb * eg).astype(BF16), maskbd)) for t16, kb, eg in zip(t16s, kbs, exp_gs)]
    ub_scr[...] = jnp.concatenate(u_bases, axis=0)
    w_scr[...] = jnp.concatenate([w.astype(BF16) for w in ws], axis=0)
    qk_scr[...] = jnp.concatenate([x.astype(BF16) for x in qk4s], axis=0)
    qe_scr[...] = jnp.concatenate([(qc * eg).astype(BF16) for qc, eg in zip(qcs, exp_gs)], axis=0)
    kd_scr[...] = jnp.concatenate([(kc * jnp.exp(gs)).astype(BF16) for kc, gs in zip(kcs, gsufs)], axis=0)
    eg_scr[...] = jnp.concatenate([jnp.broadcast_to(eg[c - 1:c, :], (8, HEAD_W)) for eg in exp_gs], axis=0)

    def scan(ci, carry):
        sl = pl.ds(pl.multiple_of(ci * c, c), c)
        s_old = sbd_scr[...]
        s16 = s_old.astype(BF16)
        u16 = (ub_scr[sl, :] - _dot(w_scr[sl, :], s16)).astype(BF16)
        o_scr[sl, :] = _dot(qe_scr[sl, :], s16) + _dot(qk_scr[sl, :], _bd(u16, maskbd))
        exp_g_last = eg_scr[pl.ds(pl.multiple_of(ci * 8, 8), 8), :][0:1]
        sbd_scr[...] = s_old * exp_g_last + maskf * _dot_tn(kd_scr[sl, :], u16)
        return carry

    lax.fori_loop(0, tc // c, scan, 0)

    o = o_scr[...]
    ms = _head_sums(o * o, maskbd) * (1.0 / DELTA_V_DIM)
    o_ref[...] = (o * lax.rsqrt(ms + NORM_EPS) * gain_ref[...] * cz_ref[...]).astype(o_ref.dtype)

    @pl.when(j == pl.num_programs(1) - 1)
    def _():
        s_ref[...] = sbd_scr[...]


def _delta_prompt(cqkv, beta, g, cz, conv_w, gain, n_seq, seq_len):
    tc = min(seq_len, 512)
    nt = seq_len // tc
    cch = cqkv.shape[1]
    consts = _delta_consts()
    names = ("lhsg", "ones", "triu4", "tril4", "stril4", "eye4", "lvl", "maskbd")

    def rows(width):
        return pl.BlockSpec((tc, width), lambda n, j: (n * nt + j, 0))

    return pl.pallas_call(
        functools.partial(_delta_body, tc=tc),
        out_shape=(jax.ShapeDtypeStruct((n_seq * seq_len, HEAD_W), BF16),
                   jax.ShapeDtypeStruct((n_seq, HEAD_W, HEAD_W), F32)),
        grid=(n_seq, nt),
        in_specs=[rows(cch), rows(HEAD_W), rows(HEAD_W), rows(HEAD_W),
                  _const_spec((CONV_WIDTH, cch)), _const_spec((1, HEAD_W))]
                 + [_const_spec(consts[nm].shape) for nm in names],
        out_specs=(rows(HEAD_W), pl.BlockSpec((None, HEAD_W, HEAD_W), lambda n, j: (n, 0, 0))),
        scratch_shapes=[pltpu.VMEM((tc + 8, cch), F32), pltpu.VMEM((tc, HEAD_W), F32),
                        pltpu.VMEM((tc, HEAD_W), BF16), pltpu.VMEM((tc, HEAD_W), BF16),
                        pltpu.VMEM((tc, HEAD_W), BF16), pltpu.VMEM((tc, HEAD_W), BF16),
                        pltpu.VMEM((8 * (tc // DELTA_CHUNK), HEAD_W), F32),
                        pltpu.VMEM((tc, HEAD_W), F32), pltpu.VMEM((HEAD_W, HEAD_W), F32)],
        compiler_params=_cparams("arbitrary", "arbitrary"),
        name="delta_prompt",
    )(cqkv, beta, g, cz, conv_w.astype(F32), jnp.tile(gain.astype(F32), DELTA_HEADS).reshape(1, HEAD_W),
      *[consts[nm] for nm in names])


SROWS = 8


def _sample_attn_consts(n_pages):
    r = np.arange(PAGE_SIZE)
    m2 = r[:, None] > r[None, :]
    m2ones = np.concatenate([m2, np.ones((PAGE_SIZE, PAGE_SIZE), bool)], axis=1)
    idx = np.arange(4 * n_pages)
    u2 = (idx[None, :] // 4 > idx[:, None] // 4) & (idx[None, :] % 4 == idx[:, None] % 4)
    t = np.arange(SROWS)
    tri8 = t[None, :] <= t[:, None]
    sup = t[:, None] > r[None, :]
    lh = np.arange(HEAD_W) // 64
    f = lambda a, dt: jnp.asarray(np.asarray(a, np.float32), dt)
    return dict(m2ones=f(m2ones, BF16), u2=f(u2, BF16), tri8=f(tri8, BF16), sup=f(sup, F32),
                maskbd=f(lh[:, None] == lh[None, :], BF16))


def _pad_rows(x, rows):
    return jnp.concatenate([x, jnp.zeros((rows - x.shape[0], x.shape[1]), x.dtype)], axis=0)


def _block_rows(x, groups, lanes_per_group):
    xr = jnp.concatenate([x] * groups, axis=0)
    rg = lax.broadcasted_iota(jnp.int32, xr.shape, 0) // SROWS
    lg = lax.broadcasted_iota(jnp.int32, xr.shape, 1) // lanes_per_group
    return jnp.where(rg == lg, xr, 0.0).astype(BF16)


def _paged_softmax_pv(qbd, kt_pages, vt_pages, k_new, v_new, bias_pages, bias_new):
    n_pages = len(kt_pages)
    assert n_pages % 2 == 0
    tiles = []
    for p in range(0, n_pages, 2):
        kt2 = jnp.concatenate([kt_pages[p][...], kt_pages[p + 1][...]], axis=1).astype(BF16)
        s = _dot(qbd, kt2)
        if bias_pages is not None:
            s = s + jnp.concatenate([bias_pages[p], bias_pages[p + 1]], axis=1)
        tiles.append(s)
    s = _dot_nt(qbd, _pad_rows(k_new, PAGE_SIZE).astype(BF16))
    if bias_new is not None:
        s = s + bias_new
    row = lax.broadcasted_iota(jnp.int32, s.shape, 0) & (SROWS - 1)
    col = lax.broadcasted_iota(jnp.int32, s.shape, 1)
    tiles.append(jnp.where(col <= row, s, -jnp.inf))
    m = tiles[0].max(axis=1, keepdims=True)
    for s in tiles[1:]:
        m = jnp.maximum(m, s.max(axis=1, keepdims=True))
    acc = None
    den = None
    for i, s in enumerate(tiles):
        p = jnp.exp2(s - m)
        ps = p.sum(axis=1, keepdims=True)
        if 2 * i < n_pages:
            vt2 = jnp.concatenate([vt_pages[2 * i][...], vt_pages[2 * i + 1][...]], axis=1).astype(BF16)
            pv = _dot_nt(p.astype(BF16), vt2)
        else:
            pv = _dot(p.astype(BF16), _pad_rows(v_new, PAGE_SIZE).astype(BF16))
        acc = pv if acc is None else acc + pv
        den = ps if den is None else den + ps
    return acc / den


def _own_head_rows(x, stride):
    lh = lax.broadcasted_iota(jnp.int32, (SROWS, HEAD_W), 1) // 64
    out = None
    for h in range(4):
        part = jnp.where(lh == h, x[h * stride:h * stride + SROWS], 0.0)
        out = part if out is None else out + part
    return out


def _sample_attn_body(pt_ref, fk_hbm, fv_hbm, lf_hbm, dk_hbm, dv_hbm,
                      fq_ref, fkn_ref, fvn_ref, ln_ref, dq_ref, dkn_ref, dvn_ref,
                      m2_ref, u2_ref, tri8_ref, sup_ref, lam_ref, gain_ref, maskbd_ref,
                      of_ref, od_ref, fk_buf, fv_buf, lf_buf, dk_buf, dv_buf, sems, xm_scr, *, n_pages, layer, lam_init):
    b = pl.program_id(0)
    slot = b % 2
    streams = ((fk_hbm, fk_buf), (fv_hbm, fv_buf), (lf_hbm, lf_buf), (dk_hbm, dk_buf), (dv_hbm, dv_buf))

    def page_copies(sample, sl):
        cps = []
        for p in range(n_pages):
            page = pt_ref[sample, p]
            for hbm, buf in streams:
                cps.append(pltpu.make_async_copy(hbm.at[layer, page], buf.at[sl, p], sems.at[sl]))
        return cps

    @pl.when(b == 0)
    def _():
        for cp in page_copies(0, 0):
            cp.start()

    @pl.when(b + 1 < pl.num_programs(0))
    def _():
        for cp in page_copies(b + 1, 1 - slot):
            cp.start()

    for cp in page_copies(b, slot):
        cp.wait()

    fk_pages = [fk_buf.at[slot, p] for p in range(n_pages)]
    fv_pages = [fv_buf.at[slot, p] for p in range(n_pages)]
    dk_pages = [dk_buf.at[slot, p] for p in range(n_pages)]
    dv_pages = [dv_buf.at[slot, p] for p in range(n_pages)]

    for p in range(n_pages):
        xm_scr[4 * p:4 * p + 4, :] = lf_buf[slot, p]
    wt = _dot3_r(xm_scr[...], m2_ref[...])
    within, totals = wt[:, :PAGE_SIZE], wt[:, PAGE_SIZE:]
    d_past = (within + _dot3_l(u2_ref[...], totals)) * LOG2E
    lnew = ln_ref[...]
    tri8 = tri8_ref[...]
    cn = _dot3_l(tri8, lnew) * LOG2E
    cn_h = [jnp.broadcast_to(cn[:, h:h + 1], (SROWS, PAGE_SIZE)) for h in range(FOX_HEADS)]
    bias_pages = [
        jnp.concatenate([d_past[4 * p + h:4 * p + h + 1, :] + cn_h[h] for h in range(FOX_HEADS)], axis=0)
        for p in range(n_pages)]
    sup = sup_ref[...]
    bias_new = jnp.concatenate(
        [_dot3_l(tri8, jnp.broadcast_to(lnew[:, h:h + 1], (SROWS, PAGE_SIZE)) * sup) * LOG2E
         for h in range(FOX_HEADS)], axis=0)
    qbd = _block_rows(fq_ref[...], FOX_HEADS, FOX_DIM)
    o = _paged_softmax_pv(qbd, fk_pages, fv_pages, fkn_ref[...], fvn_ref[...], bias_pages, bias_new)
    of_ref[...] = _own_head_rows(o, SROWS)

    qbd = _block_rows(dq_ref[...], 2 * DIFF_HEADS, DIFF_QK_DIM)
    o = _paged_softmax_pv(qbd, dk_pages, dv_pages, dkn_ref[...], dvn_ref[...], None, None)
    lam = _diff_lambda(lam_ref, lam_init)
    o = _own_head_rows(o, 2 * SROWS) - lam * _own_head_rows(o[SROWS:], 2 * SROWS)
    ms = _head_sums(o * o, maskbd_ref[...]) * (1.0 / DIFF_V_DIM)
    od_ref[...] = o * lax.rsqrt(ms + NORM_EPS) * gain_ref[...] * (1.0 - lam_init)


def _sample_attn(page_table, caches, layer, proj, lam_p, gain, lam_init):
    n_samples, n_pages = page_table.shape
    consts = _sample_attn_consts(n_pages)

    def tile(width):
        return pl.BlockSpec((SROWS, width), lambda b, pt: (b, 0))

    def const(shape):
        nd = len(shape)
        return pl.BlockSpec(shape, lambda b, pt: (0,) * nd, pipeline_mode=pl.Buffered(1))

    in_specs = [pl.BlockSpec(memory_space=pl.ANY) for _ in caches]
    args = list(caches)
    new = (proj["fq"], proj["fk"], proj["fv"], proj["small"], proj["dq"], proj["dk"], proj["dv"])
    in_specs += [tile(a.shape[1]) for a in new]
    args += list(new)
    cvals = (consts["m2ones"], consts["u2"], consts["tri8"], consts["sup"], lam_p.astype(F32),
             jnp.tile(gain.astype(F32), DIFF_HEADS).reshape(1, HEAD_W), consts["maskbd"])
    in_specs += [const(c.shape) for c in cvals]
    args += list(cvals)
    gs = pltpu.PrefetchScalarGridSpec(
        num_scalar_prefetch=1, grid=(n_samples,), in_specs=in_specs,
        out_specs=(tile(HEAD_W), tile(HEAD_W)),
        scratch_shapes=[pltpu.VMEM((2, n_pages) + c.shape[2:], c.dtype) for c in caches]
                       + [pltpu.SemaphoreType.DMA((2,)), pltpu.VMEM((4 * n_pages, PAGE_SIZE), F32)])
    return pl.pallas_call(
        functools.partial(_sample_attn_body, n_pages=n_pages, layer=layer, lam_init=lam_init),
        out_shape=(jax.ShapeDtypeStruct((n_samples * SROWS, HEAD_W), F32),) * 2,
        grid_spec=gs,
        compiler_params=_cparams("arbitrary"),
        name="sample_attn",
    )(page_table, *args)


def _delta_sample_body(cqkv_ref, buf_ref, beta_ref, g_ref, cz_ref, cw_ref, gain_ref, s0_ref,
                       tile_ref, tile_t_ref, maskbd_ref, o_ref, s_ref, xp_scr, pr_scr, u_scr, kd_scr, o_scr,
                       *, n_tok, group):
    maskbd = maskbd_ref[...]
    maskf = maskbd.astype(F32)
    cch = xp_scr.shape[1]
    cw = cw_ref[...]
    ys = []
    for s in range(group):
        base = 16 * s
        xp_scr[base:base + 8, :] = jnp.zeros((8, cch), F32)
        xp_scr[base + 8 - (CONV_WIDTH - 1):base + 8, :] = buf_ref[s]
        xp_scr[base + 8:base + 16, :] = cqkv_ref[SROWS * s:SROWS * (s + 1), :]
        y = cw[3:4] * xp_scr[base + 8:base + 16, :]
        for i in range(1, CONV_WIDTH):
            y = y + cw[3 - i:4 - i] * xp_scr[base + 8 - i:base + 16 - i, :]
        ys.append(y)
    y = jnp.concatenate(ys, axis=0)
    y = y * jax.nn.sigmoid(y)
    q = y[:, 0:HEAD_W]
    k = y[:, HEAD_W:2 * HEAD_W]
    v = y[:, 2 * HEAD_W:3 * HEAD_W]
    q = q * lax.rsqrt(_head_sums(q * q, maskbd) + 1e-6) * (DELTA_K_DIM ** -0.5)
    k = k * lax.rsqrt(_head_sums(k * k, maskbd) + 1e-6)
    beta = beta_ref[...]
    a = jnp.exp(g_ref[...])

    pairs_k = [(t, j) for t in range(n_tok) for j in range(t)]
    pairs_q = [(t, j) for t in range(n_tok) for j in range(t + 1)]
    assert len(pairs_k) + len(pairs_q) <= 16
    pr_scr[...] = jnp.zeros_like(pr_scr)
    for s in range(group):
        r0, p0 = SROWS * s, 16 * s
        for r, (t, j) in enumerate(pairs_k):
            pr_scr[p0 + r:p0 + r + 1, :] = k[r0 + t:r0 + t + 1] * k[r0 + j:r0 + j + 1]
        for r, (t, j) in enumerate(pairs_q):
            rr = p0 + len(pairs_k) + r
            pr_scr[rr:rr + 1, :] = q[r0 + t:r0 + t + 1] * k[r0 + j:r0 + j + 1]
    dots = _head_sums(pr_scr[...], maskbd)

    rows = [slice(SROWS * s, SROWS * (s + 1)) for s in range(group)]
    sbd0s = [maskf * _dot3_r(s0_ref[s], tile_ref[...]) for s in range(group)]
    s16s = [sb.astype(BF16) for sb in sbd0s]
    rks = [_dot(k[r].astype(BF16), s16) for r, s16 in zip(rows, s16s)]
    rqs = [_dot(q[r].astype(BF16), s16) for r, s16 in zip(rows, s16s)]

    u_scr[...] = jnp.zeros_like(u_scr)
    kd_scr[...] = jnp.zeros_like(kd_scr)
    o_scr[...] = jnp.zeros_like(o_scr)
    a_alls = []
    for s in range(group):
        r0, p0 = SROWS * s, 16 * s
        kk = {tj: dots[p0 + r:p0 + r + 1] for r, tj in enumerate(pairs_k)}
        qk = {tj: dots[p0 + len(pairs_k) + r:p0 + len(pairs_k) + r + 1] for r, tj in enumerate(pairs_q)}
        rk_rows = [rks[s][t:t + 1] for t in range(n_tok)]
        rq_rows = [rqs[s][t:t + 1] for t in range(n_tok)]
        for t in range(n_tok):
            a_t = a[r0 + t:r0 + t + 1]
            u_t = beta[r0 + t:r0 + t + 1] * (v[r0 + t:r0 + t + 1] - a_t * rk_rows[t])
            for t2 in range(t + 1, n_tok):
                rk_rows[t2] = a_t * rk_rows[t2] + kk[(t2, t)] * u_t
            for t2 in range(t, n_tok):
                rq_rows[t2] = a_t * rq_rows[t2] + qk[(t2, t)] * u_t
            o_scr[r0 + t:r0 + t + 1, :] = rq_rows[t]
            u_scr[r0 + t:r0 + t + 1, :] = u_t
            tail = None
            for t2 in range(t + 1, n_tok):
                tail = a[r0 + t2:r0 + t2 + 1] if tail is None else tail * a[r0 + t2:r0 + t2 + 1]
            kd_scr[r0 + t:r0 + t + 1, :] = k[r0 + t:r0 + t + 1] if tail is None else k[r0 + t:r0 + t + 1] * tail
        a_all = a[r0:r0 + 1]
        for t in range(1, n_tok):
            a_all = a_all * a[r0 + t:r0 + t + 1]
        a_alls.append(a_all)
    upd = [_dot_tn(kd_scr[r, :].astype(BF16), u_scr[r, :].astype(BF16)) for r in rows]
    s_news = [sb * a_all + maskf * u for sb, a_all, u in zip(sbd0s, a_alls, upd)]
    for s in range(group):
        s_ref[s] = _dot3_r(s_news[s], tile_t_ref[...])

    o = o_scr[...]
    ms = _head_sums(o * o, maskbd) * (1.0 / DELTA_V_DIM)
    o_ref[...] = o * lax.rsqrt(ms + NORM_EPS) * gain_ref[...] * cz_ref[...]


def _delta_sample(proj, conv_state, state, layer, conv_w, gain, n_tok):
    n_samples = state.shape[1]
    cch = conv_state.shape[3]
    v_ = np.arange(DELTA_V_DIM)
    lane = np.arange(HEAD_W)
    tile_m = jnp.asarray((v_[:, None] == lane[None, :] % DELTA_V_DIM).astype(np.float32), BF16)
    tile_t = jnp.asarray((lane[:, None] % DELTA_V_DIM == v_[None, :]).astype(np.float32), BF16)
    lh = lane // 64
    maskbd = jnp.asarray((lh[:, None] == lh[None, :]).astype(np.float32), BF16)

    group = math.gcd(n_samples, 8)
    rows = group * SROWS

    def tile(width):
        return pl.BlockSpec((rows, width), lambda b: (b, 0))

    return pl.pallas_call(
        functools.partial(_delta_sample_body, n_tok=n_tok, group=group),
        out_shape=(jax.ShapeDtypeStruct((n_samples * SROWS, HEAD_W), F32),
                   jax.ShapeDtypeStruct((n_samples, HEAD_W, DELTA_V_DIM), F32)),
        grid=(n_samples // group,),
        in_specs=[tile(cch),
                  pl.BlockSpec((None, group, CONV_WIDTH - 1, cch), lambda b: (layer, b, 0, 0)),
                  tile(HEAD_W), tile(HEAD_W), tile(HEAD_W),
                  _const_spec((CONV_WIDTH, cch)), _const_spec((1, HEAD_W)),
                  pl.BlockSpec((None, group, HEAD_W, DELTA_V_DIM), lambda b: (layer, b, 0, 0)),
                  _const_spec(tile_m.shape), _const_spec(tile_t.shape), _const_spec(maskbd.shape)],
        out_specs=(tile(HEAD_W), pl.BlockSpec((group, HEAD_W, DELTA_V_DIM), lambda b: (b, 0, 0))),
        scratch_shapes=[pltpu.VMEM((2 * rows, cch), F32), pltpu.VMEM((2 * rows, HEAD_W), F32),
                        pltpu.VMEM((rows, HEAD_W), F32), pltpu.VMEM((rows, HEAD_W), F32),
                        pltpu.VMEM((rows, HEAD_W), F32)],
        compiler_params=_cparams("arbitrary"),
        name="delta_sample",
    )(proj["cqkv"], conv_state, proj["beta"], proj["g"], proj["cz"], conv_w.astype(F32),
      jnp.tile(gain.astype(F32), DELTA_HEADS).reshape(1, HEAD_W), state, tile_m, tile_t, maskbd)


def kernel(x_prompt, x_sample, cache_fox_k, cache_fox_v, cache_fox_logf, cache_diff_k, cache_diff_v, state_delta, state_conv, page_table, norm_ffn1, ffn1_wi, ffn1_wo, norm_mix, w_in, fox_f_bias, diff_lambda, diff_norm, delta_conv_w, delta_A_log, delta_dt_bias, delta_norm, w_branch, w_out, norm_ffn2, ffn2_wi, ffn2_wo, norm_final):
    n_seq, seq_len, d = x_prompt.shape
    n_smp, dec_seq, _ = x_sample.shape
    depth = w_in.shape[0]
    n_pool = cache_fox_k.shape[1]
    n_pages = page_table.shape[1]
    past_len = n_pages * PAGE_SIZE
    assert CONV_WIDTH - 1 <= dec_seq <= SROWS and seq_len % DELTA_CHUNK == 0

    wi1, wo1 = ffn1_wi.astype(BF16), ffn1_wo.astype(BF16)
    wi2, wo2 = ffn2_wi.astype(BF16), ffn2_wo.astype(BF16)
    w_p = _relayout_w_in(w_in, _PROMPT_COLS)
    w_s = _relayout_w_in(w_in, _SAMPLE_COLS)
    w_vt = _relayout_w_values_t(w_in)
    w_gates = w_in[:, :, w_in.shape[2] - N_BRANCH * d:].astype(BF16)
    wb, wo = w_branch.astype(BF16), w_out.astype(BF16)

    def pages_t(cache):
        c = cache.reshape(depth, n_pool, PAGE_SIZE, HEAD_W)
        return jnp.swapaxes(c, 2, 3)

    caches = (pages_t(cache_fox_k), pages_t(cache_fox_v), jnp.swapaxes(cache_fox_logf, 2, 3),
              pages_t(cache_diff_k), pages_t(cache_diff_v))
    s_delta = state_delta.reshape(depth, n_smp, HEAD_W, DELTA_V_DIM)

    xp = x_prompt.reshape(n_seq * seq_len, d)
    xs = jnp.pad(x_sample, ((0, 0), (0, SROWS - dec_seq), (0, 0))).reshape(n_smp * SROWS, d)
    cos_p, sin_p = _rope_tables(seq_len, 0, seq_len)
    cos_s, sin_s = _rope_tables(n_smp * SROWS, past_len, SROWS)

    cache_bufs = {name: jnp.zeros((depth, n_seq, HEAD_W, seq_len), F32) for name in _CACHE_OUTS}

    new_p, new_s = [], []
    for l in range(depth):
        lam_init = 0.8 - 0.6 * math.exp(-0.3 * l)
        xp = _ffn(xp, norm_ffn1[l], wi1, wo1, l)
        xs = _ffn(xs, norm_ffn1[l], wi1, wo1, l)
        pp = _inproj(xp, norm_mix[l], w_p, l, cos_p, sin_p, fox_f_bias[l], delta_A_log[l], delta_dt_bias[l],
                     prompt=True, seq_len=seq_len, w_vt=w_vt, cache_bufs=cache_bufs)
        cache_bufs = {name: pp[name] for name in _CACHE_OUTS}
        ps = _inproj(xs, norm_mix[l], w_s, l, cos_s, sin_s, fox_f_bias[l], delta_A_log[l], delta_dt_bias[l],
                     prompt=False, seq_len=SROWS)

        of_p, od_p = _attn_prompt(pp["fq_pad"], pp["fk_pad"], pp["fvt"], pp["dq"], pp["dkb"], pp["dvt"],
                                  diff_lambda[l], diff_norm[l], lam_init, n_seq, seq_len)
        ol_p, sbd = _delta_prompt(pp["cqkv"], pp["beta"], pp["g"], pp["cz"], delta_conv_w[l], delta_norm[l],
                                  n_seq, seq_len)
        of_s, od_s = _sample_attn(page_table, caches, l, ps, diff_lambda[l], diff_norm[l], lam_init)
        ol_s, s_new = _delta_sample(ps, state_conv, s_delta, l, delta_conv_w[l], delta_norm[l], dec_seq)

        xp = _merge(xp, norm_mix[l], w_gates, wb, wo, l, (of_p, od_p, ol_p))
        xs = _merge(xs, norm_mix[l], w_gates, wb, wo, l, (of_s, od_s, ol_s))
        final = norm_final if l == depth - 1 else None
        xp = _ffn(xp, norm_ffn2[l], wi2, wo2, l, final)
        xs = _ffn(xs, norm_ffn2[l], wi2, wo2, l, final)

        def p3(a):
            return a.reshape(n_seq, seq_len, a.shape[1])

        def s3(a):
            return a.reshape(n_smp, SROWS, a.shape[1])[:, :dec_seq]

        new_p.append(dict(
            fox_logf=p3(pp["small"])[:, :, :FOX_HEADS],
            delta=jnp.stack([sbd[:, h * 64:(h + 1) * 64, h * 64:(h + 1) * 64] for h in range(DELTA_HEADS)], axis=1),
            conv=p3(pp["cqkv"])[:, seq_len - (CONV_WIDTH - 1):]))
        new_s.append(dict(
            fox_k=s3(ps["fk"]).reshape(n_smp, dec_seq, FOX_HEADS, FOX_DIM),
            fox_v=s3(ps["fv"]).reshape(n_smp, dec_seq, FOX_HEADS, FOX_DIM),
            fox_logf=s3(ps["small"])[:, :, :FOX_HEADS],
            diff_k=s3(ps["dk"]).reshape(n_smp, dec_seq, DIFF_HEADS, 2, DIFF_QK_DIM),
            diff_v=s3(ps["dv"]).reshape(n_smp, dec_seq, DIFF_HEADS, DIFF_V_DIM),
            delta=s_new.reshape(n_smp, DELTA_HEADS, DELTA_K_DIM, DELTA_V_DIM),
            conv=s3(ps["cqkv"])[:, dec_seq - (CONV_WIDTH - 1):]))

    def stk(lst, name):
        return jnp.stack([dd[name] for dd in lst])

    def token_major(buf, *feat):
        nf = len(feat)
        b = buf.reshape((depth, n_seq) + feat + (seq_len,))
        return jnp.transpose(b, (0, 1, 2 + nf) + tuple(range(2, 2 + nf)))

    y_prompt = xp.reshape(n_seq, seq_len, d)
    y_sample = xs.reshape(n_smp, SROWS, d)[:, :dec_seq]
    return (y_prompt, y_sample,
            token_major(cache_bufs["fkt"], FOX_HEADS, FOX_DIM), stk(new_s, "fox_k"),
            token_major(cache_bufs["fvt32"], FOX_HEADS, FOX_DIM), stk(new_s, "fox_v"),
            stk(new_p, "fox_logf"), stk(new_s, "fox_logf"),
            token_major(cache_bufs["dkt"], DIFF_HEADS, 2, DIFF_QK_DIM), stk(new_s, "diff_k"),
            token_major(cache_bufs["dvt32"], DIFF_HEADS, DIFF_V_DIM), stk(new_s, "diff_v"),
            stk(new_p, "delta"), stk(new_s, "delta"),
            stk(new_p, "conv"), stk(new_s, "conv"))
```

```python
import functools
import math

import numpy as np
import jax
import jax.numpy as jnp
from jax import lax
from jax.experimental import pallas as pl
from jax.experimental.pallas import tpu as pltpu

F32 = jnp.float32
BF16 = jnp.bfloat16

FOX_HEADS = 4
FOX_DIM = 64
DIFF_HEADS = 4
DIFF_QK_DIM = 32
DIFF_V_DIM = 64
DELTA_HEADS = 4
DELTA_K_DIM = 64
DELTA_V_DIM = 64
CONV_WIDTH = 4
DELTA_CHUNK = 64
N_BRANCH = 3
HEAD_W = 256
PAGE_SIZE = 128
ROPE_THETA = 10000.0
NORM_EPS = 1e-6
LOG2E = 1.4426950408889634
FOX_QSCALE = FOX_DIM ** -0.5 * LOG2E
DIFF_QSCALE = DIFF_QK_DIM ** -0.5 * LOG2E

V7X_VMEM_LIMIT_BYTES = 56 * 1024 * 1024
AUG_C = 64
AUG_K = 67
ONES_LANE = 64


def _cparams(*sem):
    return pltpu.CompilerParams(dimension_semantics=sem, vmem_limit_bytes=V7X_VMEM_LIMIT_BYTES)


def _dot(a, b):
    return jnp.dot(a, b, preferred_element_type=F32)


def _dot_nt(a, b):
    return lax.dot_general(a, b, (((1,), (1,)), ((), ())), preferred_element_type=F32)


def _dot_tn(a, b):
    return lax.dot_general(a, b, (((0,), (0,)), ((), ())), preferred_element_type=F32)


def _split3(x):
    hi = x.astype(BF16)
    r = x - hi.astype(F32)
    mid = r.astype(BF16)
    lo = (r - mid.astype(F32)).astype(BF16)
    return hi, mid, lo


def _dot3_r(x, m):
    hi, mid, lo = _split3(x)
    return _dot(hi, m) + _dot(mid, m) + _dot(lo, m)


def _dot3_l(m, x):
    hi, mid, lo = _split3(x)
    return _dot(m, hi) + _dot(m, mid) + _dot(m, lo)


def _rms(x, gain):
    ms = jnp.mean(x * x, axis=-1, keepdims=True)
    return x * lax.rsqrt(ms + NORM_EPS) * gain


def _const_spec(shape):
    nd = len(shape)
    return pl.BlockSpec(shape, lambda *_: (0,) * nd, pipeline_mode=pl.Buffered(1))


V7X_MXU_DIM = 256


def _swiglu_residual(x, gain, wi_ref, wo_ref, dff, tf):
    h = _rms(x, gain).astype(BF16)
    acc = None
    for c0 in range(0, dff, tf):
        gate = _dot(h, wi_ref[:, c0:c0 + tf])
        up = _dot(h, wi_ref[:, dff + c0:dff + c0 + tf])
        act = (gate * jax.nn.sigmoid(gate) * up).astype(BF16)
        part = _dot(act, wo_ref[c0:c0 + tf, :])
        acc = part if acc is None else acc + part
    return x + 0.5 * acc


def _ffn_body(x_ref, g_ref, wi_ref, wo_ref, gf_ref, o_ref, *, final_norm, dff, tf):
    y = _swiglu_residual(x_ref[...], g_ref[...], wi_ref, wo_ref, dff, tf)
    if final_norm:
        y = _rms(y, gf_ref[...])
    o_ref[...] = y


def _ffn(x, gain, wi, wo, layer, final_gain=None):
    r, d = x.shape
    dff = wo.shape[1]
    tm = min(r, 512)
    tf = V7X_MXU_DIM if dff % V7X_MXU_DIM == 0 else dff
    final_norm = final_gain is not None
    gf = final_gain if final_norm else gain
    return pl.pallas_call(
        functools.partial(_ffn_body, final_norm=final_norm, dff=dff, tf=tf),
        out_shape=jax.ShapeDtypeStruct((r, d), F32),
        grid=(r // tm,),
        in_specs=[
            pl.BlockSpec((tm, d), lambda i: (i, 0)),
            _const_spec((1, d)),
            pl.BlockSpec((None, d, 2 * dff), lambda i: (layer, 0, 0), pipeline_mode=pl.Buffered(1)),
            pl.BlockSpec((None, dff, d), lambda i: (layer, 0, 0), pipeline_mode=pl.Buffered(1)),
            _const_spec((1, d)),
        ],
        out_specs=pl.BlockSpec((tm, d), lambda i: (i, 0)),
        compiler_params=_cparams("arbitrary"),
        name="ffn",
    )(x, gain.reshape(1, d), wi, wo, gf.reshape(1, d))


def _rope_body(inv_ref, cos_ref, sin_ref, *, tm, base, period):
    i = pl.program_id(0)
    row = lax.broadcasted_iota(jnp.int32, (tm, HEAD_W), 0) + i * tm
    lane = lax.broadcasted_iota(jnp.int32, (tm, HEAD_W), 1)
    pos = base + (row & (period - 1))
    ang = pos.astype(F32) * inv_ref[...]
    cos_ref[...] = jnp.cos(ang)
    s = jnp.sin(ang)
    first_half = (lane & (DIFF_QK_DIM - 1)) < DIFF_QK_DIM // 2
    sin_ref[...] = jnp.where(first_half, -s, s)


def _rope_tables(rows, base, period):
    assert period & (period - 1) == 0
    half = DIFF_QK_DIM // 2
    inv = ROPE_THETA ** (-jnp.arange(0, DIFF_QK_DIM, 2, dtype=F32) / DIFF_QK_DIM)
    inv_lanes = jnp.tile(inv, HEAD_W // half).reshape(1, HEAD_W)
    tm = min(rows, 512)
    return pl.pallas_call(
        functools.partial(_rope_body, tm=tm, base=base, period=period),
        out_shape=(jax.ShapeDtypeStruct((rows, HEAD_W), F32),) * 2,
        grid=(rows // tm,),
        in_specs=[pl.BlockSpec((1, HEAD_W), lambda i: (0, 0))],
        out_specs=(pl.BlockSpec((tm, HEAD_W), lambda i: (i, 0)),) * 2,
        compiler_params=_cparams("arbitrary"),
        name="rope_tables",
    )(inv_lanes)


_PROMPT_COLS = (("fq_pad", 512), ("fk_pad", 512), ("dq", 256), ("dk", 256),
                ("cqkv", 768), ("cz", 256), ("cb_rep", 256), ("ca_rep", 256), ("small", 128))
_SAMPLE_COLS = (("fq", 256), ("fk", 256), ("fv", 256), ("dq", 256), ("dk", 256), ("dv", 256),
                ("cqkv", 768), ("cz", 256), ("cb_rep", 256), ("ca_rep", 256), ("small", 128))


def _layout(cols):
    lay, off = {}, 0
    for name, width in cols:
        lay[name] = (off, width)
        off += width
    return lay, off


_W_IN_SIZES = (256, 256, 256, 4, 256, 256, 256, 768, 4, 4, 256)


def _split_w_in_t(w_in_t):
    offs = np.concatenate([[0], np.cumsum(_W_IN_SIZES)])
    return tuple(w_in_t[:, int(offs[i]):int(offs[i + 1]), :] for i in range(len(_W_IN_SIZES)))


def _pad_head_rows(w):
    depth, _, d = w.shape
    w = w.reshape(depth, 4, 64, d)
    return jnp.pad(w, ((0, 0), (0, 0), (0, 64), (0, 0))).reshape(depth, 512, d)


def _relayout_w_in(w_in_t, cols):
    fq, fk, fv, ff, dq, dk, dv, cqkv, cb, ca, cz = _split_w_in_t(w_in_t)

    def rep(w):
        return jnp.repeat(w, 64, axis=1)

    table = {
        "fq": fq, "fk": fk, "fv": fv, "fq_pad": _pad_head_rows(fq), "fk_pad": _pad_head_rows(fk),
        "dq": dq, "dk": dk, "dv": dv,
        "cqkv": cqkv, "cz": cz, "cb_rep": rep(cb), "ca_rep": rep(ca),
        "small": jnp.pad(ff, ((0, 0), (0, 124), (0, 0))),
    }
    return jnp.concatenate([table[name] for name, _ in cols], axis=1).astype(BF16)


def _relayout_w_values_t(w_in_t):
    _, _, fv, _, _, _, dv, _, _, _, _ = _split_w_in_t(w_in_t)
    return jnp.concatenate([_pad_head_rows(fv), _pad_head_rows(dv)], axis=1).astype(BF16)


def _aug_place_mats():
    pq = np.zeros((384, 512), np.float32)
    pk = np.zeros((384, 512), np.float32)
    for piece in range(3):
        for h in range(FOX_HEADS):
            pq[piece * 128 + h, h * 128 + AUG_C + piece] = 1.0
            pk[piece * 128 + h, h * 128 + AUG_K + piece] = -1.0
    return jnp.asarray(pq, BF16), jnp.asarray(pk, BF16)


def _lane_ones(shape, lo, hi):
    lane = lax.broadcasted_iota(jnp.int32, shape, 1) & 127
    mid2 = lo + hi
    return jnp.where(jnp.abs(2 * lane - mid2) <= hi - lo, 1.0, 0.0).astype(F32)


def _swap_rope_halves(x):
    half = DIFF_QK_DIM // 2
    lane = lax.broadcasted_iota(jnp.int32, (x.shape[0], 128), 1)
    first = (lane & (DIFF_QK_DIM - 1)) < half
    parts = []
    for c in range(0, x.shape[1], 128):
        xc = x[:, c:c + 128]
        parts.append(jnp.where(first, pltpu.roll(xc, 128 - half, axis=1), pltpu.roll(xc, half, axis=1)))
    return jnp.concatenate(parts, axis=1)


def _inproj_body(*refs, lay, prompt, tiles_per_seq, tm, n_alias=0):
    if prompt:
        (x_ref, g_ref, w_ref, cos_ref, sin_ref, fb_ref, alog_ref, dtb_ref, tri_ref, pq_ref, pk_ref, wvt_ref) = refs[:12]
        (fq_o, fk_pad_o, fvt_o, small_o, dq_o, dkb_o, dvt_o, cqkv_o, cz_o, beta_o, g_o,
         fkt_o, fvt32_o, dkt_o, dvt32_o, carry_scr) = refs[12 + n_alias:]
    else:
        (x_ref, g_ref, w_ref, cos_ref, sin_ref, fb_ref, alog_ref, dtb_ref,
         fq_o, fk_o, fv_o, small_o, dq_o, dk_o, dv_o, cqkv_o, cz_o, beta_o, g_o) = refs

    h = _rms(x_ref[...], g_ref[...]).astype(BF16)

    def proj(name):
        off, width = lay[name]
        return _dot_nt(h, w_ref[off:off + width, :])

    logf = jax.nn.log_sigmoid(proj("small") + fb_ref[...])
    small_o[...] = logf

    if prompt:
        i = pl.program_id(0)

        @pl.when(i % tiles_per_seq == 0)
        def _():
            carry_scr[...] = jnp.zeros_like(carry_scr)

        c = _dot3_l(tri_ref[...], logf) + carry_scr[...]
        carry_scr[...] = c[tm - 1:tm, :]
        c1, c2, c3 = _split3(c * LOG2E)
        c123 = jnp.concatenate([c1, c2, c3], axis=1)
        shape = (tm, 512)
        q = proj("fq_pad") * FOX_QSCALE + _dot(c123, pq_ref[...]) + _lane_ones(shape, AUG_K, AUG_K + 2)
        fq_o[...] = q.astype(BF16)
        k32 = proj("fk_pad")
        fk_pad_o[...] = (k32 + _dot(c123, pk_ref[...]) + _lane_ones(shape, AUG_C, AUG_C + 2)).astype(BF16)
        k32_t = k32.T
        for hd in range(FOX_HEADS):
            fkt_o[hd * 64:(hd + 1) * 64, :] = k32_t[hd * 128:hd * 128 + 64]
        vt = _dot_nt(wvt_ref[...], h)
        for hd in range(4):
            fvt32_o[hd * 64:(hd + 1) * 64, :] = vt[hd * 128:hd * 128 + 64]
            dvt32_o[hd * 64:(hd + 1) * 64, :] = vt[512 + hd * 128:512 + hd * 128 + 64]
        row = lax.broadcasted_iota(jnp.int32, vt.shape, 0) & 127
        vt = (vt + jnp.where(row == ONES_LANE, 1.0, 0.0)).astype(BF16)
        fvt_o[...] = vt[0:512]
        dvt_o[...] = vt[512:1024]
    else:
        fq_o[...] = proj("fq") * FOX_QSCALE
        fk_o[...] = proj("fk")
        fv_o[...] = proj("fv")
        dv_o[...] = proj("dv")

    cos = cos_ref[...]
    sin = sin_ref[...]
    dq = proj("dq")
    dq_o[...] = ((dq * cos + _swap_rope_halves(dq) * sin) * DIFF_QSCALE).astype(dq_o.dtype)
    dk = proj("dk")
    dk = dk * cos + _swap_rope_halves(dk) * sin
    if prompt:
        dkb_o[...] = dk.astype(BF16)
        dkt_o[...] = dk.T
    else:
        dk_o[...] = dk

    cqkv_o[...] = proj("cqkv")
    cz = proj("cz")
    cz_o[...] = cz * jax.nn.sigmoid(cz)
    beta_o[...] = jax.nn.sigmoid(proj("cb_rep"))
    g_o[...] = -jnp.exp(alog_ref[...]) * jax.nn.softplus(proj("ca_rep") + dtb_ref[...])


_CACHE_OUTS = ("fkt", "fvt32", "dkt", "dvt32")


def _inproj(x, gain, w, layer, cos, sin, fbias, alog, dtb, *, prompt, seq_len, w_vt=None, cache_bufs=None):
    r, d = x.shape
    lay, nc = _layout(_PROMPT_COLS if prompt else _SAMPLE_COLS)
    assert w.shape[1] == nc
    tm = min(r, 512 if prompt else 256)
    tm = min(tm, seq_len) if prompt else tm
    fb = jnp.pad(fbias.astype(F32), (0, 124)).reshape(1, 128)
    alog_l = jnp.repeat(alog.astype(F32), 64).reshape(1, 256)
    dtb_l = jnp.repeat(dtb.astype(F32), 64).reshape(1, 256)

    def rows(width):
        return pl.BlockSpec((tm, width), lambda i: (i, 0))

    rope_tiles = cos.shape[0] // tm
    rope = pl.BlockSpec((tm, 256), lambda i: (i % rope_tiles, 0))
    in_specs = [rows(d), _const_spec((1, d)),
                pl.BlockSpec((None, nc, d), lambda i: (layer, 0, 0), pipeline_mode=pl.Buffered(1)),
                rope, rope, _const_spec((1, 128)), _const_spec((1, 256)), _const_spec((1, 256))]
    args = [x, gain.reshape(1, d), w, cos, sin, fb, alog_l, dtb_l]
    if prompt:
        tri = jnp.asarray(np.tril(np.ones((tm, tm), np.float32)), BF16)
        pq, pk = _aug_place_mats()
        in_specs += [_const_spec((tm, tm)), _const_spec((384, 512)), _const_spec((384, 512)),
                     pl.BlockSpec((None, 1024, d), lambda i: (layer, 0, 0), pipeline_mode=pl.Buffered(1))]
        args += [tri, pq, pk, w_vt]
        outs = (("fq_pad", 512, BF16), ("fk_pad", 512, BF16), ("fvt", 0, BF16), ("small", 128, F32),
                ("dq", 256, BF16), ("dkb", 256, BF16), ("dvt", 0, BF16), ("cqkv", 768, F32), ("cz", 256, F32),
                ("beta", 256, F32), ("g", 256, F32))
        scratch = [pltpu.VMEM((1, 128), F32)]
    else:
        outs = (("fq", 256, F32), ("fk", 256, F32), ("fv", 256, F32), ("small", 128, F32), ("dq", 256, F32),
                ("dk", 256, F32), ("dv", 256, F32), ("cqkv", 768, F32), ("cz", 256, F32),
                ("beta", 256, F32), ("g", 256, F32))
        scratch = []
    tps = max(seq_len // tm, 1)
    out_shape = [jax.ShapeDtypeStruct((r, wd) if wd else (512, r), dt) for _, wd, dt in outs]
    out_specs = [rows(wd) if wd else pl.BlockSpec((512, tm), lambda i: (0, i)) for _, wd, _ in outs]
    names = [name for name, _, _ in outs]
    aliases = {}
    if prompt:
        for j, name in enumerate(_CACHE_OUTS):
            buf = cache_bufs[name]
            out_shape.append(jax.ShapeDtypeStruct(buf.shape, buf.dtype))
            out_specs.append(pl.BlockSpec((None, None, HEAD_W, tm), lambda i: (layer, i // tps, 0, i % tps)))
            names.append(name)
            aliases[len(args)] = len(outs) + j
            in_specs.append(pl.BlockSpec(memory_space=pl.ANY))
            args.append(buf)
    res = pl.pallas_call(
        functools.partial(_inproj_body, lay=lay, prompt=prompt, tiles_per_seq=tps, tm=tm, n_alias=len(aliases)),
        out_shape=tuple(out_shape),
        grid=(r // tm,),
        in_specs=in_specs,
        out_specs=tuple(out_specs),
        scratch_shapes=scratch,
        input_output_aliases=aliases,
        compiler_params=_cparams("arbitrary"),
        name="inproj_prompt" if prompt else "inproj_sample",
    )(*args)
    return dict(zip(names, res))


def _tri_schedule(nq):
    qi = np.concatenate([np.full(i + 1, i, np.int32) for i in range(nq)])
    kj = np.concatenate([np.arange(i + 1, dtype=np.int32) for i in range(nq)])
    return jnp.asarray(qi), jnp.asarray(kj)


def _online_softmax_step(st, vt, m_ref, acc_ref):
    m_old = m_ref[...]
    m_new = jnp.maximum(m_old, jnp.max(st, axis=0, keepdims=True))
    alpha = jnp.exp2(m_old - m_new)
    p = jnp.exp2(st - m_new)
    acc_ref[...] = alpha * acc_ref[...] + _dot(vt, p.astype(BF16))
    m_ref[...] = m_new


def _causal_mask(st):
    key = lax.broadcasted_iota(jnp.int32, st.shape, 0)
    qry = lax.broadcasted_iota(jnp.int32, st.shape, 1)
    return jnp.where(key <= qry, st, -jnp.inf)


def _diff_lambda(lam_ref, lam_init):
    lp = lam_ref[...]
    s1 = jnp.sum(lp[0:1] * lp[1:2], axis=1, keepdims=True)
    s2 = jnp.sum(lp[2:3] * lp[3:4], axis=1, keepdims=True)
    return jnp.exp(s1) - jnp.exp(s2) + lam_init


N_ATTN_MAPS = FOX_HEADS + 2 * DIFF_HEADS
ATTN_LOOKAHEAD = 3


def _attn_body(qi_ref, kj_ref, fq_ref, fk_ref, fvt_ref, dq_ref, dk_ref, dvt_ref, lam_ref, gain_ref,
               of_ref, od_ref, qm_scr, m_scr, acc_scr, *, lam_init):
    t = pl.program_id(1)
    qi = qi_ref[t]
    kj = kj_ref[t]
    n_diff = 2 * DIFF_HEADS

    @pl.when(kj == 0)
    def _():
        m_scr[...] = jnp.full_like(m_scr, -jnp.inf)
        acc_scr[...] = jnp.zeros_like(acc_scr)
        q = dq_ref[...]
        lane_map = lax.broadcasted_iota(jnp.int32, q.shape, 1) // DIFF_QK_DIM
        for m in range(n_diff):
            qm_scr[m] = jnp.where(lane_map == m, q, jnp.zeros_like(q))

    def scores(j):
        if j < FOX_HEADS:
            sl = slice(j * 128, (j + 1) * 128)
            return _dot_nt(fk_ref[:, sl], fq_ref[:, sl])
        return _dot_nt(dk_ref[...], qm_scr[j - FOX_HEADS])

    def values_t(j):
        if j < FOX_HEADS:
            return fvt_ref[j * 128:(j + 1) * 128, :]
        h = (j - FOX_HEADS) // 2
        return dvt_ref[h * 128:(h + 1) * 128, :]

    def step(masked):
        pending = [scores(j) for j in range(ATTN_LOOKAHEAD)]
        for j in range(N_ATTN_MAPS):
            st = pending.pop(0)
            if j + ATTN_LOOKAHEAD < N_ATTN_MAPS:
                pending.append(scores(j + ATTN_LOOKAHEAD))
            if masked:
                st = _causal_mask(st)
            _online_softmax_step(st, values_t(j), m_scr.at[j], acc_scr.at[j])

    @pl.when(kj < qi)
    def _():
        step(False)

    @pl.when(kj == qi)
    def _():
        step(True)
        for h in range(FOX_HEADS):
            a = acc_scr[h]
            o = a[:FOX_DIM] / a[ONES_LANE:ONES_LANE + 1]
            of_ref[:, h * FOX_DIM:(h + 1) * FOX_DIM] = o.T.astype(of_ref.dtype)
        lam = _diff_lambda(lam_ref, lam_init)
        tq = od_ref.shape[0]
        gain = jnp.concatenate([gain_ref[...]] * (tq // 128), axis=1)
        for h in range(DIFF_HEADS):
            a0 = acc_scr[FOX_HEADS + 2 * h]
            a1 = acc_scr[FOX_HEADS + 2 * h + 1]
            o = (a0[:DIFF_V_DIM] / a0[ONES_LANE:ONES_LANE + 1]
                 - lam * (a1[:DIFF_V_DIM] / a1[ONES_LANE:ONES_LANE + 1]))
            ms = jnp.mean(o * o, axis=0, keepdims=True)
            o = o * lax.rsqrt(ms + NORM_EPS) * gain * (1.0 - lam_init)
            od_ref[:, h * DIFF_V_DIM:(h + 1) * DIFF_V_DIM] = o.T.astype(od_ref.dtype)


def _attn_prompt(fq_pad, fk_pad, fvt_pad, dq, dk, dvt_pad, lam_p, gain, lam_init, n_seq, seq_len):
    tq = min(seq_len, 512)
    nq = seq_len // tq
    qi, kj = _tri_schedule(nq)

    def q_rows(width):
        return pl.BlockSpec((tq, width), lambda n, t, qi, kj: (n * nq + qi[t], 0))

    def k_rows(width):
        return pl.BlockSpec((tq, width), lambda n, t, qi, kj: (n * nq + kj[t], 0))

    def k_cols():
        return pl.BlockSpec((512, tq), lambda n, t, qi, kj: (0, n * nq + kj[t]))

    gs = pltpu.PrefetchScalarGridSpec(
        num_scalar_prefetch=2,
        grid=(n_seq, int(qi.shape[0])),
        in_specs=[q_rows(512), k_rows(512), k_cols(), q_rows(HEAD_W), k_rows(HEAD_W), k_cols(),
                  pl.BlockSpec((4, DIFF_QK_DIM), lambda n, t, qi, kj: (0, 0)),
                  pl.BlockSpec((DIFF_V_DIM, 128), lambda n, t, qi, kj: (0, 0))],
        out_specs=(q_rows(HEAD_W), q_rows(HEAD_W)),
        scratch_shapes=[pltpu.VMEM((2 * DIFF_HEADS, tq, HEAD_W), BF16), pltpu.VMEM((N_ATTN_MAPS, 1, tq), F32),
                        pltpu.VMEM((N_ATTN_MAPS, 128, tq), F32)],
    )
    gain_rows = jnp.broadcast_to(gain.astype(F32).reshape(DIFF_V_DIM, 1), (DIFF_V_DIM, 128))
    return pl.pallas_call(
        functools.partial(_attn_body, lam_init=lam_init),
        out_shape=(jax.ShapeDtypeStruct((n_seq * seq_len, HEAD_W), BF16),) * 2,
        grid_spec=gs,
        compiler_params=_cparams("arbitrary", "arbitrary"),
        name="attn_prompt",
    )(qi, kj, fq_pad, fk_pad, fvt_pad, dq, dk, dvt_pad, lam_p.astype(F32), gain_rows)


def _merge_ffn_body(x_ref, g_ref, wg_ref, wb_ref, wo_ref, b0_ref, b1_ref, b2_ref,
                    g2_ref, wi_ref, wo2_ref, gf_ref, o_ref, *, final_norm, dff, tf):
    x = x_ref[...]
    d = x.shape[1]
    h = _rms(x, g_ref[...]).astype(BF16)
    mix = None
    for b, b_ref in enumerate((b0_ref, b1_ref, b2_ref)):
        gate = jax.nn.sigmoid(_dot_nt(h, wg_ref[b * d:(b + 1) * d, :]))
        term = gate * _dot(b_ref[...].astype(BF16), wb_ref[b])
        mix = term if mix is None else mix + term
    x = x + _dot(mix.astype(BF16), wo_ref[...])
    y = _swiglu_residual(x, g2_ref[...], wi_ref, wo2_ref, dff, tf)
    if final_norm:
        y = _rms(y, gf_ref[...])
    o_ref[...] = y


def _merge_ffn(x, gain, w_gates, w_branch, w_out, layer, branches, gain2, wi, wo2, final_gain=None):
    r, d = x.shape
    dff = wo2.shape[1]
    tm = min(r, 512)
    tf = V7X_MXU_DIM if dff % V7X_MXU_DIM == 0 else dff
    final_norm = final_gain is not None
    gf = final_gain if final_norm else gain2

    def rows(width):
        return pl.BlockSpec((tm, width), lambda i: (i, 0))

    def resident(*shape):
        nd = len(shape)
        return pl.BlockSpec((None,) + shape, lambda i: (layer,) + (0,) * nd, pipeline_mode=pl.Buffered(1))

    return pl.pallas_call(
        functools.partial(_merge_ffn_body, final_norm=final_norm, dff=dff, tf=tf),
        out_shape=jax.ShapeDtypeStruct((r, d), F32),
        grid=(r // tm,),
        in_specs=[rows(d), _const_spec((1, d)),
                  resident(N_BRANCH * d, d), resident(N_BRANCH, HEAD_W, d), resident(d, d),
                  rows(HEAD_W), rows(HEAD_W), rows(HEAD_W),
                  _const_spec((1, d)), resident(d, 2 * dff), resident(dff, d), _const_spec((1, d))],
        out_specs=rows(d),
        compiler_params=_cparams("arbitrary"),
        name="merge_ffn",
    )(x, gain.reshape(1, d), w_gates, w_branch, w_out, *branches,
      gain2.reshape(1, d), wi, wo2, gf.reshape(1, d))


def _delta_consts():
    c = DELTA_CHUNK
    i = np.arange(c)
    lane = np.arange(HEAD_W)
    lh, lj = lane // c, lane % c
    tril = i[:, None] >= i[None, :]
    suffix = i[None, :] > i[:, None]
    lvl = []
    for k in range(1, 7):
        b = 1 << k
        lvl.append((i[:, None] // b == lj[None, :] // b) & (i[:, None] // (b // 2) != lj[None, :] // (b // 2)))
    f = lambda a, dt: jnp.asarray(np.asarray(a, np.float32), dt)
    return dict(
        lhsg=f(np.concatenate([tril, suffix], axis=0), BF16),
        ones=f(np.ones((c, c)), BF16),
        triu4=f(i[:, None] <= lj[None, :], F32),
        tril4=f(i[:, None] >= lj[None, :], F32),
        stril4=f(i[:, None] > lj[None, :], F32),
        eye4=f(i[:, None] == lj[None, :], F32),
        lvl=f(np.stack(lvl), F32),
        maskbd=f(lh[:, None] == lh[None, :], BF16),
    )


def _bd(x4, maskbd):
    return jnp.concatenate([x4, x4, x4, x4], axis=0) * maskbd


def _mm_hi(xs, ys, maskbd):
    xh = [x.astype(BF16) for x in xs]
    xl = [(x - h.astype(F32)).astype(BF16) for x, h in zip(xs, xh)]
    yh = [y.astype(BF16) for y in ys]
    yl = [(y - h.astype(F32)).astype(BF16) for y, h in zip(ys, yh)]
    ybh = [_bd(h, maskbd) for h in yh]
    ybl = [_bd(l, maskbd) for l in yl]
    return [_dot(a, bh) + _dot(al, bh) + _dot(a, bl) for a, al, bh, bl in zip(xh, xl, ybh, ybl)]


def _unit_lower_inverse(a4s, eye4, lvl_ref, maskbd):
    ts = [eye4 - a4 * lvl_ref[0] for a4 in a4s]
    for k in range(1, 6):
        ms = [a4 * lvl_ref[k] for a4 in a4s]
        xs = _mm_hi(ms, ts, maskbd)
        ts = [t - d for t, d in zip(ts, _mm_hi(ts, xs, maskbd))]
    return ts


def _head_sums(x, maskbd):
    hi = x.astype(BF16)
    lo = (x - hi.astype(F32)).astype(BF16)
    return _dot(hi, maskbd) + _dot(lo, maskbd)


def _delta_body(cqkv_ref, beta_ref, g_ref, cz_ref, cw_ref, gain_ref,
                lhsg_ref, ones_ref, triu4_ref, tril4_ref, stril4_ref, eye4_ref, lvl_ref, maskbd_ref,
                o_ref, s_ref, xbuf, ub_scr, w_scr, qk_scr, qe_scr, kd_scr, eg_scr, o_scr, sbd_scr, *, tc):
    j = pl.program_id(1)
    c = DELTA_CHUNK
    maskbd = maskbd_ref[...]
    maskf = maskbd.astype(F32)

    @pl.when(j == 0)
    def _():
        xbuf[0:8, :] = jnp.zeros((8, xbuf.shape[1]), F32)
        sbd_scr[...] = jnp.zeros_like(sbd_scr)

    xbuf[8:8 + tc, :] = cqkv_ref[...]
    cw = cw_ref[...]
    y = cw[3:4] * xbuf[8:8 + tc, :]
    for i in range(1, CONV_WIDTH):
        y = y + cw[3 - i:4 - i] * xbuf[8 - i:8 - i + tc, :]
    xbuf[0:8, :] = xbuf[tc:tc + 8, :]
    y = y * jax.nn.sigmoid(y)
    q = y[:, 0:HEAD_W]
    k = y[:, HEAD_W:2 * HEAD_W]
    qn = q * lax.rsqrt(_head_sums(q * q, maskbd) + 1e-6) * (DELTA_K_DIM ** -0.5)
    kn = k * lax.rsqrt(_head_sums(k * k, maskbd) + 1e-6)
    v = y[:, 2 * HEAD_W:3 * HEAD_W]
    beta = beta_ref[...]
    g = g_ref[...]

    cs = [slice(ci * c, (ci + 1) * c) for ci in range(tc // c)]
    qcs, kcs, vcs = [qn[s] for s in cs], [kn[s] for s in cs], [v[s] for s in cs]
    bcs, gcs = [beta[s] for s in cs], [g[s] for s in cs]
    gsums = [_dot3_l(lhsg_ref[...], gc) for gc in gcs]
    grows = [_dot3_l(ones_ref[...], gc * triu4_ref[...]) for gc in gcs]
    gcols, gsufs = [gs[0:c] for gs in gsums], [gs[c:2 * c] for gs in gsums]
    decays = [jnp.where(tril4_ref[...] > 0, jnp.exp(gcol - grow), 0.0) for gcol, grow in zip(gcols, grows)]
    exp_gs = [jnp.exp(gcol) for gcol in gcols]
    kbs = [kc * bc for kc, bc in zip(kcs, bcs)]
    kbds = [_bd(kc.astype(BF16), maskbd) for kc in kcs]
    kks = [_dot_nt(jnp.concatenate([kb, qc], axis=0).astype(BF16), kbd) for kb, qc, kbd in zip(kbs, qcs, kbds)]
    a4s = [kk[0:c] * decay * stril4_ref[...] for kk, decay in zip(kks, decays)]
    qk4s = [kk[c:2 * c] * decay for kk, decay in zip(kks, decays)]
    t16s = [t.astype(BF16) for t in _unit_lower_inverse(a4s, eye4_ref[...], lvl_ref, maskbd)]
    u_bases = [_dot(t16, _bd((vc * bc).astype(BF16), maskbd)) for t16, vc, bc in zip(t16s, vcs, bcs)]
    ws = [_dot(t16, _bd((kb * eg).astype(BF16), maskbd)) for t16, kb, eg in zip(t16s, kbs, exp_gs)]
    ub_scr[...] = jnp.concatenate(u_bases, axis=0)
    w_scr[...] = jnp.concatenate([w.astype(BF16) for w in ws], axis=0)
    qk_scr[...] = jnp.concatenate([x.astype(BF16) for x in qk4s], axis=0)
    qe_scr[...] = jnp.concatenate([(qc * eg).astype(BF16) for qc, eg in zip(qcs, exp_gs)], axis=0)
    kd_scr[...] = jnp.concatenate([(kc * jnp.exp(gs)).astype(BF16) for kc, gs in zip(kcs, gsufs)], axis=0)
    eg_scr[...] = jnp.concatenate([jnp.broadcast_to(eg[c - 1:c, :], (8, HEAD_W)) for eg in exp_gs], axis=0)

    def scan(ci, carry):
        sl = pl.ds(pl.multiple_of(ci * c, c), c)
        s_old = sbd_scr[...]
        s16 = s_old.astype(BF16)
        u16 = (ub_scr[sl, :] - _dot(w_scr[sl, :], s16)).astype(BF16)
        o_scr[sl, :] = _dot(qe_scr[sl, :], s16) + _dot(qk_scr[sl, :], _bd(u16, maskbd))
        exp_g_last = eg_scr[pl.ds(pl.multiple_of(ci * 8, 8), 8), :][0:1]
        sbd_scr[...] = s_old * exp_g_last + maskf * _dot_tn(kd_scr[sl, :], u16)
        return carry

    lax.fori_loop(0, tc // c, scan, 0)

    o = o_scr[...]
    ms = _head_sums(o * o, maskbd) * (1.0 / DELTA_V_DIM)
    o_ref[...] = (o * lax.rsqrt(ms + NORM_EPS) * gain_ref[...] * cz_ref[...]).astype(o_ref.dtype)

    @pl.when(j == pl.num_programs(1) - 1)
    def _():
        s_ref[...] = sbd_scr[...]


def _delta_prompt(cqkv, beta, g, cz, conv_w, gain, n_seq, seq_len):
    tc = min(seq_len, 512)
    nt = seq_len // tc
    cch = cqkv.shape[1]
    consts = _delta_consts()
    names = ("lhsg", "ones", "triu4", "tril4", "stril4", "eye4", "lvl", "maskbd")

    def rows(width):
        return pl.BlockSpec((tc, width), lambda n, j: (n * nt + j, 0))

    return pl.pallas_call(
        functools.partial(_delta_body, tc=tc),
        out_shape=(jax.ShapeDtypeStruct((n_seq * seq_len, HEAD_W), BF16),
                   jax.ShapeDtypeStruct((n_seq, HEAD_W, HEAD_W), F32)),
        grid=(n_seq, nt),
        in_specs=[rows(cch), rows(HEAD_W), rows(HEAD_W), rows(HEAD_W),
                  _const_spec((CONV_WIDTH, cch)), _const_spec((1, HEAD_W))]
                 + [_const_spec(consts[nm].shape) for nm in names],
        out_specs=(rows(HEAD_W), pl.BlockSpec((None, HEAD_W, HEAD_W), lambda n, j: (n, 0, 0))),
        scratch_shapes=[pltpu.VMEM((tc + 8, cch), F32), pltpu.VMEM((tc, HEAD_W), F32),
                        pltpu.VMEM((tc, HEAD_W), BF16), pltpu.VMEM((tc, HEAD_W), BF16),
                        pltpu.VMEM((tc, HEAD_W), BF16), pltpu.VMEM((tc, HEAD_W), BF16),
                        pltpu.VMEM((8 * (tc // DELTA_CHUNK), HEAD_W), F32),
                        pltpu.VMEM((tc, HEAD_W), F32), pltpu.VMEM((HEAD_W, HEAD_W), F32)],
        compiler_params=_cparams("arbitrary", "arbitrary"),
        name="delta_prompt",
    )(cqkv, beta, g, cz, conv_w.astype(F32), jnp.tile(gain.astype(F32), DELTA_HEADS).reshape(1, HEAD_W),
      *[consts[nm] for nm in names])


SROWS = 8


def _sample_attn_consts(n_pages):
    r = np.arange(PAGE_SIZE)
    m2 = r[:, None] > r[None, :]
    m2ones = np.concatenate([m2, np.ones((PAGE_SIZE, PAGE_SIZE), bool)], axis=1)
    idx = np.arange(4 * n_pages)
    u2 = (idx[None, :] // 4 > idx[:, None] // 4) & (idx[None, :] % 4 == idx[:, None] % 4)
    t = np.arange(SROWS)
    tri8 = t[None, :] <= t[:, None]
    sup = t[:, None] > r[None, :]
    lh = np.arange(HEAD_W) // 64
    f = lambda a, dt: jnp.asarray(np.asarray(a, np.float32), dt)
    return dict(m2ones=f(m2ones, BF16), u2=f(u2, BF16), tri8=f(tri8, BF16), sup=f(sup, F32),
                maskbd=f(lh[:, None] == lh[None, :], BF16))


def _pad_rows(x, rows):
    return jnp.concatenate([x, jnp.zeros((rows - x.shape[0], x.shape[1]), x.dtype)], axis=0)


def _block_rows(x, groups, lanes_per_group):
    xr = jnp.concatenate([x] * groups, axis=0)
    rg = lax.broadcasted_iota(jnp.int32, xr.shape, 0) // SROWS
    lg = lax.broadcasted_iota(jnp.int32, xr.shape, 1) // lanes_per_group
    return jnp.where(rg == lg, xr, 0.0).astype(BF16)


def _paged_softmax_pv(qbd, kt_pages, vt_pages, k_new, v_new, bias_pages, bias_new):
    n_pages = len(kt_pages)
    assert n_pages % 2 == 0
    tiles = []
    for p in range(0, n_pages, 2):
        kt2 = jnp.concatenate([kt_pages[p][...], kt_pages[p + 1][...]], axis=1).astype(BF16)
        s = _dot(qbd, kt2)
        if bias_pages is not None:
            s = s + jnp.concatenate([bias_pages[p], bias_pages[p + 1]], axis=1)
        tiles.append(s)
    s = _dot_nt(qbd, _pad_rows(k_new, PAGE_SIZE).astype(BF16))
    if bias_new is not None:
        s = s + bias_new
    row = lax.broadcasted_iota(jnp.int32, s.shape, 0) & (SROWS - 1)
    col = lax.broadcasted_iota(jnp.int32, s.shape, 1)
    tiles.append(jnp.where(col <= row, s, -jnp.inf))
    m = tiles[0].max(axis=1, keepdims=True)
    for s in tiles[1:]:
        m = jnp.maximum(m, s.max(axis=1, keepdims=True))
    acc = None
    den = None
    for i, s in enumerate(tiles):
        p = jnp.exp2(s - m)
        ps = p.sum(axis=1, keepdims=True)
        if 2 * i < n_pages:
            vt2 = jnp.concatenate([vt_pages[2 * i][...], vt_pages[2 * i + 1][...]], axis=1).astype(BF16)
            pv = _dot_nt(p.astype(BF16), vt2)
        else:
            pv = _dot(p.astype(BF16), _pad_rows(v_new, PAGE_SIZE).astype(BF16))
        acc = pv if acc is None else acc + pv
        den = ps if den is None else den + ps
    return acc / den


def _own_head_rows(x, stride):
    lh = lax.broadcasted_iota(jnp.int32, (SROWS, HEAD_W), 1) // 64
    out = None
    for h in range(4):
        part = jnp.where(lh == h, x[h * stride:h * stride + SROWS], 0.0)
        out = part if out is None else out + part
    return out


def _sample_attn_body(pt_ref, fk_hbm, fv_hbm, lf_hbm, dk_hbm, dv_hbm,
                      fq_ref, fkn_ref, fvn_ref, ln_ref, dq_ref, dkn_ref, dvn_ref,
                      m2_ref, u2_ref, tri8_ref, sup_ref, lam_ref, gain_ref, maskbd_ref,
                      of_ref, od_ref, fk_buf, fv_buf, lf_buf, dk_buf, dv_buf, sems, xm_scr, *, n_pages, layer, lam_init):
    b = pl.program_id(0)
    slot = b % 2
    streams = ((fk_hbm, fk_buf), (fv_hbm, fv_buf), (lf_hbm, lf_buf), (dk_hbm, dk_buf), (dv_hbm, dv_buf))

    def page_copies(sample, sl):
        cps = []
        for p in range(n_pages):
            page = pt_ref[sample, p]
            for hbm, buf in streams:
                cps.append(pltpu.make_async_copy(hbm.at[layer, page], buf.at[sl, p], sems.at[sl]))
        return cps

    @pl.when(b == 0)
    def _():
        for cp in page_copies(0, 0):
            cp.start()

    @pl.when(b + 1 < pl.num_programs(0))
    def _():
        for cp in page_copies(b + 1, 1 - slot):
            cp.start()

    for cp in page_copies(b, slot):
        cp.wait()

    fk_pages = [fk_buf.at[slot, p] for p in range(n_pages)]
    fv_pages = [fv_buf.at[slot, p] for p in range(n_pages)]
    dk_pages = [dk_buf.at[slot, p] for p in range(n_pages)]
    dv_pages = [dv_buf.at[slot, p] for p in range(n_pages)]

    for p in range(n_pages):
        xm_scr[4 * p:4 * p + 4, :] = lf_buf[slot, p]
    wt = _dot3_r(xm_scr[...], m2_ref[...])
    within, totals = wt[:, :PAGE_SIZE], wt[:, PAGE_SIZE:]
    d_past = (within + _dot3_l(u2_ref[...], totals)) * LOG2E
    lnew = ln_ref[...]
    tri8 = tri8_ref[...]
    cn = _dot3_l(tri8, lnew) * LOG2E
    cn_h = [jnp.broadcast_to(cn[:, h:h + 1], (SROWS, PAGE_SIZE)) for h in range(FOX_HEADS)]
    bias_pages = [
        jnp.concatenate([d_past[4 * p + h:4 * p + h + 1, :] + cn_h[h] for h in range(FOX_HEADS)], axis=0)
        for p in range(n_pages)]
    sup = sup_ref[...]
    bias_new = jnp.concatenate(
        [_dot3_l(tri8, jnp.broadcast_to(lnew[:, h:h + 1], (SROWS, PAGE_SIZE)) * sup) * LOG2E
         for h in range(FOX_HEADS)], axis=0)
    qbd = _block_rows(fq_ref[...], FOX_HEADS, FOX_DIM)
    o = _paged_softmax_pv(qbd, fk_pages, fv_pages, fkn_ref[...], fvn_ref[...], bias_pages, bias_new)
    of_ref[...] = _own_head_rows(o, SROWS)

    qbd = _block_rows(dq_ref[...], 2 * DIFF_HEADS, DIFF_QK_DIM)
    o = _paged_softmax_pv(qbd, dk_pages, dv_pages, dkn_ref[...], dvn_ref[...], None, None)
    lam = _diff_lambda(lam_ref, lam_init)
    o = _own_head_rows(o, 2 * SROWS) - lam * _own_head_rows(o[SROWS:], 2 * SROWS)
    ms = _head_sums(o * o, maskbd_ref[...]) * (1.0 / DIFF_V_DIM)
    od_ref[...] = o * lax.rsqrt(ms + NORM_EPS) * gain_ref[...] * (1.0 - lam_init)


def _sample_attn(page_table, caches, layer, proj, lam_p, gain, lam_init):
    n_samples, n_pages = page_table.shape
    consts = _sample_attn_consts(n_pages)

    def tile(width):
        return pl.BlockSpec((SROWS, width), lambda b, pt: (b, 0))

    def const(shape):
        nd = len(shape)
        return pl.BlockSpec(shape, lambda b, pt: (0,) * nd, pipeline_mode=pl.Buffered(1))

    in_specs = [pl.BlockSpec(memory_space=pl.ANY) for _ in caches]
    args = list(caches)
    new = (proj["fq"], proj["fk"], proj["fv"], proj["small"], proj["dq"], proj["dk"], proj["dv"])
    in_specs += [tile(a.shape[1]) for a in new]
    args += list(new)
    cvals = (consts["m2ones"], consts["u2"], consts["tri8"], consts["sup"], lam_p.astype(F32),
             jnp.tile(gain.astype(F32), DIFF_HEADS).reshape(1, HEAD_W), consts["maskbd"])
    in_specs += [const(c.shape) for c in cvals]
    args += list(cvals)
    gs = pltpu.PrefetchScalarGridSpec(
        num_scalar_prefetch=1, grid=(n_samples,), in_specs=in_specs,
        out_specs=(tile(HEAD_W), tile(HEAD_W)),
        scratch_shapes=[pltpu.VMEM((2, n_pages) + c.shape[2:], c.dtype) for c in caches]
                       + [pltpu.SemaphoreType.DMA((2,)), pltpu.VMEM((4 * n_pages, PAGE_SIZE), F32)])
    return pl.pallas_call(
        functools.partial(_sample_attn_body, n_pages=n_pages, layer=layer, lam_init=lam_init),
        out_shape=(jax.ShapeDtypeStruct((n_samples * SROWS, HEAD_W), F32),) * 2,
        grid_spec=gs,
        compiler_params=_cparams("arbitrary"),
        name="sample_attn",
    )(page_table, *args)


def _delta_sample_body(cqkv_ref, buf_ref, beta_ref, g_ref, cz_ref, cw_ref, gain_ref, s0_ref,
                       tile_ref, tile_t_ref, maskbd_ref, o_ref, s_ref, xp_scr, pr_scr, u_scr, kd_scr, o_scr,
                       *, n_tok, group):
    maskbd = maskbd_ref[...]
    maskf = maskbd.astype(F32)
    cch = xp_scr.shape[1]
    cw = cw_ref[...]
    ys = []
    for s in range(group):
        base = 16 * s
        xp_scr[base:base + 8, :] = jnp.zeros((8, cch), F32)
        xp_scr[base + 8 - (CONV_WIDTH - 1):base + 8, :] = buf_ref[s]
        xp_scr[base + 8:base + 16, :] = cqkv_ref[SROWS * s:SROWS * (s + 1), :]
        y = cw[3:4] * xp_scr[base + 8:base + 16, :]
        for i in range(1, CONV_WIDTH):
            y = y + cw[3 - i:4 - i] * xp_scr[base + 8 - i:base + 16 - i, :]
        ys.append(y)
    y = jnp.concatenate(ys, axis=0)
    y = y * jax.nn.sigmoid(y)
    q = y[:, 0:HEAD_W]
    k = y[:, HEAD_W:2 * HEAD_W]
    v = y[:, 2 * HEAD_W:3 * HEAD_W]
    q = q * lax.rsqrt(_head_sums(q * q, maskbd) + 1e-6) * (DELTA_K_DIM ** -0.5)
    k = k * lax.rsqrt(_head_sums(k * k, maskbd) + 1e-6)
    beta = beta_ref[...]
    a = jnp.exp(g_ref[...])

    pairs_k = [(t, j) for t in range(n_tok) for j in range(t)]
    pairs_q = [(t, j) for t in range(n_tok) for j in range(t + 1)]
    assert len(pairs_k) + len(pairs_q) <= 16
    pr_scr[...] = jnp.zeros_like(pr_scr)
    for s in range(group):
        r0, p0 = SROWS * s, 16 * s
        for r, (t, j) in enumerate(pairs_k):
            pr_scr[p0 + r:p0 + r + 1, :] = k[r0 + t:r0 + t + 1] * k[r0 + j:r0 + j + 1]
        for r, (t, j) in enumerate(pairs_q):
            rr = p0 + len(pairs_k) + r
            pr_scr[rr:rr + 1, :] = q[r0 + t:r0 + t + 1] * k[r0 + j:r0 + j + 1]
    dots = _head_sums(pr_scr[...], maskbd)

    rows = [slice(SROWS * s, SROWS * (s + 1)) for s in range(group)]
    sbd0s = [maskf * _dot3_r(s0_ref[s], tile_ref[...]) for s in range(group)]
    s16s = [sb.astype(BF16) for sb in sbd0s]
    rks = [_dot(k[r].astype(BF16), s16) for r, s16 in zip(rows, s16s)]
    rqs = [_dot(q[r].astype(BF16), s16) for r, s16 in zip(rows, s16s)]

    u_scr[...] = jnp.zeros_like(u_scr)
    kd_scr[...] = jnp.zeros_like(kd_scr)
    o_scr[...] = jnp.zeros_like(o_scr)
    a_alls = []
    for s in range(group):
        r0, p0 = SROWS * s, 16 * s
        kk = {tj: dots[p0 + r:p0 + r + 1] for r, tj in enumerate(pairs_k)}
        qk = {tj: dots[p0 + len(pairs_k) + r:p0 + len(pairs_k) + r + 1] for r, tj in enumerate(pairs_q)}
        rk_rows = [rks[s][t:t + 1] for t in range(n_tok)]
        rq_rows = [rqs[s][t:t + 1] for t in range(n_tok)]
        for t in range(n_tok):
            a_t = a[r0 + t:r0 + t + 1]
            u_t = beta[r0 + t:r0 + t + 1] * (v[r0 + t:r0 + t + 1] - a_t * rk_rows[t])
            for t2 in range(t + 1, n_tok):
                rk_rows[t2] = a_t * rk_rows[t2] + kk[(t2, t)] * u_t
            for t2 in range(t, n_tok):
                rq_rows[t2] = a_t * rq_rows[t2] + qk[(t2, t)] * u_t
            o_scr[r0 + t:r0 + t + 1, :] = rq_rows[t]
            u_scr[r0 + t:r0 + t + 1, :] = u_t
            tail = None
            for t2 in range(t + 1, n_tok):
                tail = a[r0 + t2:r0 + t2 + 1] if tail is None else tail * a[r0 + t2:r0 + t2 + 1]
            kd_scr[r0 + t:r0 + t + 1, :] = k[r0 + t:r0 + t + 1] if tail is None else k[r0 + t:r0 + t + 1] * tail
        a_all = a[r0:r0 + 1]
        for t in range(1, n_tok):
            a_all = a_all * a[r0 + t:r0 + t + 1]
        a_alls.append(a_all)
    upd = [_dot_tn(kd_scr[r, :].astype(BF16), u_scr[r, :].astype(BF16)) for r in rows]
    s_news = [sb * a_all + maskf * u for sb, a_all, u in zip(sbd0s, a_alls, upd)]
    for s in range(group):
        s_ref[s] = _dot3_r(s_news[s], tile_t_ref[...])

    o = o_scr[...]
    ms = _head_sums(o * o, maskbd) * (1.0 / DELTA_V_DIM)
    o_ref[...] = o * lax.rsqrt(ms + NORM_EPS) * gain_ref[...] * cz_ref[...]


def _delta_sample(proj, conv_state, state, layer, conv_w, gain, n_tok):
    n_samples = state.shape[1]
    cch = conv_state.shape[3]
    v_ = np.arange(DELTA_V_DIM)
    lane = np.arange(HEAD_W)
    tile_m = jnp.asarray((v_[:, None] == lane[None, :] % DELTA_V_DIM).astype(np.float32), BF16)
    tile_t = jnp.asarray((lane[:, None] % DELTA_V_DIM == v_[None, :]).astype(np.float32), BF16)
    lh = lane // 64
    maskbd = jnp.asarray((lh[:, None] == lh[None, :]).astype(np.float32), BF16)

    group = math.gcd(n_samples, 8)
    rows = group * SROWS

    def tile(width):
        return pl.BlockSpec((rows, width), lambda b: (b, 0))

    return pl.pallas_call(
        functools.partial(_delta_sample_body, n_tok=n_tok, group=group),
        out_shape=(jax.ShapeDtypeStruct((n_samples * SROWS, HEAD_W), F32),
                   jax.ShapeDtypeStruct((n_samples, HEAD_W, DELTA_V_DIM), F32)),
        grid=(n_samples // group,),
        in_specs=[tile(cch),
                  pl.BlockSpec((None, group, CONV_WIDTH - 1, cch), lambda b: (layer, b, 0, 0)),
                  tile(HEAD_W), tile(HEAD_W), tile(HEAD_W),
                  _const_spec((CONV_WIDTH, cch)), _const_spec((1, HEAD_W)),
                  pl.BlockSpec((None, group, HEAD_W, DELTA_V_DIM), lambda b: (layer, b, 0, 0)),
                  _const_spec(tile_m.shape), _const_spec(tile_t.shape), _const_spec(maskbd.shape)],
        out_specs=(tile(HEAD_W), pl.BlockSpec((group, HEAD_W, DELTA_V_DIM), lambda b: (b, 0, 0))),
        scratch_shapes=[pltpu.VMEM((2 * rows, cch), F32), pltpu.VMEM((2 * rows, HEAD_W), F32),
                        pltpu.VMEM((rows, HEAD_W), F32), pltpu.VMEM((rows, HEAD_W), F32),
                        pltpu.VMEM((rows, HEAD_W), F32)],
        compiler_params=_cparams("arbitrary"),
        name="delta_sample",
    )(proj["cqkv"], conv_state, proj["beta"], proj["g"], proj["cz"], conv_w.astype(F32),
      jnp.tile(gain.astype(F32), DELTA_HEADS).reshape(1, HEAD_W), state, tile_m, tile_t, maskbd)


def kernel(x_prompt, x_sample, cache_fox_k, cache_fox_v, cache_fox_logf, cache_diff_k, cache_diff_v, state_delta, state_conv, page_table, norm_ffn1, ffn1_wi, ffn1_wo, norm_mix, w_in, fox_f_bias, diff_lambda, diff_norm, delta_conv_w, delta_A_log, delta_dt_bias, delta_norm, w_branch, w_out, norm_ffn2, ffn2_wi, ffn2_wo, norm_final):
    n_seq, seq_len, d = x_prompt.shape
    n_smp, dec_seq, _ = x_sample.shape
    depth = w_in.shape[0]
    n_pool = cache_fox_k.shape[1]
    n_pages = page_table.shape[1]
    past_len = n_pages * PAGE_SIZE
    assert CONV_WIDTH - 1 <= dec_seq <= SROWS and seq_len % DELTA_CHUNK == 0

    wi1, wo1 = ffn1_wi.astype(BF16), ffn1_wo.astype(BF16)
    wi2, wo2 = ffn2_wi.astype(BF16), ffn2_wo.astype(BF16)
    w_in_t = jnp.swapaxes(w_in, 1, 2)
    w_p = _relayout_w_in(w_in_t, _PROMPT_COLS)
    w_s = _relayout_w_in(w_in_t, _SAMPLE_COLS)
    w_vt = _relayout_w_values_t(w_in_t)
    w_gates = w_in_t[:, w_in_t.shape[1] - N_BRANCH * d:, :].astype(BF16)
    wb, wo = w_branch.astype(BF16), w_out.astype(BF16)

    def pages_t(cache):
        c = cache.reshape(depth, n_pool, PAGE_SIZE, HEAD_W)
        return jnp.swapaxes(c, 2, 3)

    caches = (pages_t(cache_fox_k), pages_t(cache_fox_v), jnp.swapaxes(cache_fox_logf, 2, 3),
              pages_t(cache_diff_k), pages_t(cache_diff_v))
    s_delta = state_delta.reshape(depth, n_smp, HEAD_W, DELTA_V_DIM)

    xp = x_prompt.reshape(n_seq * seq_len, d)
    xs = jnp.pad(x_sample, ((0, 0), (0, SROWS - dec_seq), (0, 0))).reshape(n_smp * SROWS, d)
    cos_p, sin_p = _rope_tables(seq_len, 0, seq_len)
    cos_s, sin_s = _rope_tables(n_smp * SROWS, past_len, SROWS)

    cache_bufs = {name: jnp.zeros((depth, n_seq, HEAD_W, seq_len), F32) for name in _CACHE_OUTS}

    new_p, new_s = [], []
    for l in range(depth):
        lam_init = 0.8 - 0.6 * math.exp(-0.3 * l)
        xp = _ffn(xp, norm_ffn1[l], wi1, wo1, l)
        xs = _ffn(xs, norm_ffn1[l], wi1, wo1, l)
        pp = _inproj(xp, norm_mix[l], w_p, l, cos_p, sin_p, fox_f_bias[l], delta_A_log[l], delta_dt_bias[l],
                     prompt=True, seq_len=seq_len, w_vt=w_vt, cache_bufs=cache_bufs)
        cache_bufs = {name: pp[name] for name in _CACHE_OUTS}
        ps = _inproj(xs, norm_mix[l], w_s, l, cos_s, sin_s, fox_f_bias[l], delta_A_log[l], delta_dt_bias[l],
                     prompt=False, seq_len=SROWS)

        of_p, od_p = _attn_prompt(pp["fq_pad"], pp["fk_pad"], pp["fvt"], pp["dq"], pp["dkb"], pp["dvt"],
                                  diff_lambda[l], diff_norm[l], lam_init, n_seq, seq_len)
        ol_p, sbd = _delta_prompt(pp["cqkv"], pp["beta"], pp["g"], pp["cz"], delta_conv_w[l], delta_norm[l],
                                  n_seq, seq_len)
        of_s, od_s = _sample_attn(page_table, caches, l, ps, diff_lambda[l], diff_norm[l], lam_init)
        ol_s, s_new = _delta_sample(ps, state_conv, s_delta, l, delta_conv_w[l], delta_norm[l], dec_seq)

        final = norm_final if l == depth - 1 else None
        xp = _merge_ffn(xp, norm_mix[l], w_gates, wb, wo, l, (of_p, od_p, ol_p), norm_ffn2[l], wi2, wo2, final)
        xs = _merge_ffn(xs, norm_mix[l], w_gates, wb, wo, l, (of_s, od_s, ol_s), norm_ffn2[l], wi2, wo2, final)

        def p3(a):
            return a.reshape(n_seq, seq_len, a.shape[1])

        def s3(a):
            return a.reshape(n_smp, SROWS, a.shape[1])[:, :dec_seq]

        new_p.append(dict(
            fox_logf=p3(pp["small"])[:, :, :FOX_HEADS],
            delta=jnp.stack([sbd[:, h * 64:(h + 1) * 64, h * 64:(h + 1) * 64] for h in range(DELTA_HEADS)], axis=1),
            conv=p3(pp["cqkv"])[:, seq_len - (CONV_WIDTH - 1):]))
        new_s.append(dict(
            fox_k=s3(ps["fk"]).reshape(n_smp, dec_seq, FOX_HEADS, FOX_DIM),
            fox_v=s3(ps["fv"]).reshape(n_smp, dec_seq, FOX_HEADS, FOX_DIM),
            fox_logf=s3(ps["small"])[:, :, :FOX_HEADS],
            diff_k=s3(ps["dk"]).reshape(n_smp, dec_seq, DIFF_HEADS, 2, DIFF_QK_DIM),
            diff_v=s3(ps["dv"]).reshape(n_smp, dec_seq, DIFF_HEADS, DIFF_V_DIM),
            delta=s_new.reshape(n_smp, DELTA_HEADS, DELTA_K_DIM, DELTA_V_DIM),
            conv=s3(ps["cqkv"])[:, dec_seq - (CONV_WIDTH - 1):]))

    def stk(lst, name):
        return jnp.stack([dd[name] for dd in lst])

    def token_major(buf, *feat):
        nf = len(feat)
        b = buf.reshape((depth, n_seq) + feat + (seq_len,))
        return jnp.transpose(b, (0, 1, 2 + nf) + tuple(range(2, 2 + nf)))

    y_prompt = xp.reshape(n_seq, seq_len, d)
    y_sample = xs.reshape(n_smp, SROWS, d)[:, :dec_seq]
    return (y_prompt, y_sample,
            token_major(cache_bufs["fkt"], FOX_HEADS, FOX_DIM), stk(new_s, "fox_k"),
            token_major(cache_bufs["fvt32"], FOX_HEADS, FOX_DIM), stk(new_s, "fox_v"),
            stk(new_p, "fox_logf"), stk(new_s, "fox_logf"),
            token_major(cache_bufs["dkt"], DIFF_HEADS, 2, DIFF_QK_DIM), stk(new_s, "diff_k"),
            token_major(cache_bufs["dvt32"], DIFF_HEADS, DIFF_V_DIM), stk(new_s, "diff_v"),
            stk(new_p, "delta"), stk(new_s, "delta"),
            stk(new_p, "conv"), stk(new_s, "conv"))
```

```python
import functools
import math

import numpy as np
import jax
import jax.numpy as jnp
from jax import lax
from jax.experimental import pallas as pl
from jax.experimental.pallas import tpu as pltpu

F32 = jnp.float32
BF16 = jnp.bfloat16

FOX_HEADS = 4
FOX_DIM = 64
DIFF_HEADS = 4
DIFF_QK_DIM = 32
DIFF_V_DIM = 64
DELTA_HEADS = 4
DELTA_K_DIM = 64
DELTA_V_DIM = 64
CONV_WIDTH = 4
DELTA_CHUNK = 64
N_BRANCH = 3
HEAD_W = 256
PAGE_SIZE = 128
ROPE_THETA = 10000.0
NORM_EPS = 1e-6
LOG2E = 1.4426950408889634
FOX_QSCALE = FOX_DIM ** -0.5 * LOG2E
DIFF_QSCALE = DIFF_QK_DIM ** -0.5 * LOG2E

V7X_VMEM_LIMIT_BYTES = 56 * 1024 * 1024
AUG_C = 64
AUG_K = 67
ONES_LANE = 64


def _cparams(*sem):
    return pltpu.CompilerParams(dimension_semantics=sem, vmem_limit_bytes=V7X_VMEM_LIMIT_BYTES)


def _dot(a, b):
    return jnp.dot(a, b, preferred_element_type=F32)


def _dot_nt(a, b):
    return lax.dot_general(a, b, (((1,), (1,)), ((), ())), preferred_element_type=F32)


def _dot_tn(a, b):
    return lax.dot_general(a, b, (((0,), (0,)), ((), ())), preferred_element_type=F32)


def _split3(x):
    hi = x.astype(BF16)
    r = x - hi.astype(F32)
    mid = r.astype(BF16)
    lo = (r - mid.astype(F32)).astype(BF16)
    return hi, mid, lo


def _dot3_r(x, m):
    hi, mid, lo = _split3(x)
    return _dot(hi, m) + _dot(mid, m) + _dot(lo, m)


def _dot3_l(m, x):
    hi, mid, lo = _split3(x)
    return _dot(m, hi) + _dot(m, mid) + _dot(m, lo)


def _rms(x, gain):
    ms = jnp.mean(x * x, axis=-1, keepdims=True)
    return x * lax.rsqrt(ms + NORM_EPS) * gain


def _const_spec(shape):
    nd = len(shape)
    return pl.BlockSpec(shape, lambda *_: (0,) * nd, pipeline_mode=pl.Buffered(1))


V7X_MXU_DIM = 256


def _swiglu_residual(x, gain, wi_ref, wo_ref, dff, tf):
    h = _rms(x, gain).astype(BF16)
    acc = None
    for c0 in range(0, dff, tf):
        gate = _dot(h, wi_ref[:, c0:c0 + tf])
        up = _dot(h, wi_ref[:, dff + c0:dff + c0 + tf])
        act = (gate * jax.nn.sigmoid(gate) * up).astype(BF16)
        part = _dot(act, wo_ref[c0:c0 + tf, :])
        acc = part if acc is None else acc + part
    return x + 0.5 * acc


def _ffn_body(x_ref, g_ref, wi_ref, wo_ref, gf_ref, o_ref, *, final_norm, dff, tf):
    y = _swiglu_residual(x_ref[...], g_ref[...], wi_ref, wo_ref, dff, tf)
    if final_norm:
        y = _rms(y, gf_ref[...])
    o_ref[...] = y


def _ffn(x, gain, wi, wo, layer, final_gain=None):
    r, d = x.shape
    dff = wo.shape[1]
    tm = min(r, 512)
    tf = V7X_MXU_DIM if dff % V7X_MXU_DIM == 0 else dff
    final_norm = final_gain is not None
    gf = final_gain if final_norm else gain
    return pl.pallas_call(
        functools.partial(_ffn_body, final_norm=final_norm, dff=dff, tf=tf),
        out_shape=jax.ShapeDtypeStruct((r, d), F32),
        grid=(r // tm,),
        in_specs=[
            pl.BlockSpec((tm, d), lambda i: (i, 0)),
            _const_spec((1, d)),
            pl.BlockSpec((None, d, 2 * dff), lambda i: (layer, 0, 0), pipeline_mode=pl.Buffered(1)),
            pl.BlockSpec((None, dff, d), lambda i: (layer, 0, 0), pipeline_mode=pl.Buffered(1)),
            _const_spec((1, d)),
        ],
        out_specs=pl.BlockSpec((tm, d), lambda i: (i, 0)),
        compiler_params=_cparams("arbitrary"),
        name="ffn",
    )(x, gain.reshape(1, d), wi, wo, gf.reshape(1, d))


def _rope_body(inv_ref, cos_ref, sin_ref, *, tm, base, period):
    i = pl.program_id(0)
    row = lax.broadcasted_iota(jnp.int32, (tm, HEAD_W), 0) + i * tm
    lane = lax.broadcasted_iota(jnp.int32, (tm, HEAD_W), 1)
    pos = base + (row & (period - 1))
    ang = pos.astype(F32) * inv_ref[...]
    cos_ref[...] = jnp.cos(ang)
    s = jnp.sin(ang)
    first_half = (lane & (DIFF_QK_DIM - 1)) < DIFF_QK_DIM // 2
    sin_ref[...] = jnp.where(first_half, -s, s)


def _rope_tables(rows, base, period):
    assert period & (period - 1) == 0
    half = DIFF_QK_DIM // 2
    inv = ROPE_THETA ** (-jnp.arange(0, DIFF_QK_DIM, 2, dtype=F32) / DIFF_QK_DIM)
    inv_lanes = jnp.tile(inv, HEAD_W // half).reshape(1, HEAD_W)
    tm = min(rows, 512)
    return pl.pallas_call(
        functools.partial(_rope_body, tm=tm, base=base, period=period),
        out_shape=(jax.ShapeDtypeStruct((rows, HEAD_W), F32),) * 2,
        grid=(rows // tm,),
        in_specs=[pl.BlockSpec((1, HEAD_W), lambda i: (0, 0))],
        out_specs=(pl.BlockSpec((tm, HEAD_W), lambda i: (i, 0)),) * 2,
        compiler_params=_cparams("arbitrary"),
        name="rope_tables",
    )(inv_lanes)


_PROMPT_COLS = (("fq_pad", 512), ("fk_pad", 512), ("dq", 256), ("dk", 256),
                ("cqkv", 768), ("cz", 256), ("cb_rep", 256), ("ca_rep", 256), ("small", 128))
_SAMPLE_COLS = (("fq", 256), ("fk", 256), ("fv", 256), ("dq", 256), ("dk", 256), ("dv", 256),
                ("cqkv", 768), ("cz", 256), ("cb_rep", 256), ("ca_rep", 256), ("small", 128))


def _layout(cols):
    lay, off = {}, 0
    for name, width in cols:
        lay[name] = (off, width)
        off += width
    return lay, off


_W_IN_SIZES = (256, 256, 256, 4, 256, 256, 256, 768, 4, 4, 256)


def _split_w_in_t(w_in_t):
    offs = np.concatenate([[0], np.cumsum(_W_IN_SIZES)])
    return tuple(w_in_t[:, int(offs[i]):int(offs[i + 1]), :] for i in range(len(_W_IN_SIZES)))


def _pad_head_rows(w):
    depth, _, d = w.shape
    w = w.reshape(depth, 4, 64, d)
    return jnp.pad(w, ((0, 0), (0, 0), (0, 64), (0, 0))).reshape(depth, 512, d)


def _relayout_w_in(w_in_t, cols):
    fq, fk, fv, ff, dq, dk, dv, cqkv, cb, ca, cz = _split_w_in_t(w_in_t)

    def rep(w):
        return jnp.repeat(w, 64, axis=1)

    table = {
        "fq": fq, "fk": fk, "fv": fv, "fq_pad": _pad_head_rows(fq), "fk_pad": _pad_head_rows(fk),
        "dq": dq, "dk": dk, "dv": dv,
        "cqkv": cqkv, "cz": cz, "cb_rep": rep(cb), "ca_rep": rep(ca),
        "small": jnp.pad(ff, ((0, 0), (0, 124), (0, 0))),
    }
    return jnp.concatenate([table[name] for name, _ in cols], axis=1).astype(BF16)


def _relayout_w_values_t(w_in_t):
    _, _, fv, _, _, _, dv, _, _, _, _ = _split_w_in_t(w_in_t)
    return jnp.concatenate([_pad_head_rows(fv), _pad_head_rows(dv)], axis=1).astype(BF16)


def _aug_place_mats():
    pq = np.zeros((384, 512), np.float32)
    pk = np.zeros((384, 512), np.float32)
    for piece in range(3):
        for h in range(FOX_HEADS):
            pq[piece * 128 + h, h * 128 + AUG_C + piece] = 1.0
            pk[piece * 128 + h, h * 128 + AUG_K + piece] = -1.0
    return jnp.asarray(pq, BF16), jnp.asarray(pk, BF16)


def _lane_ones(shape, lo, hi):
    lane = lax.broadcasted_iota(jnp.int32, shape, 1) & 127
    mid2 = lo + hi
    return jnp.where(jnp.abs(2 * lane - mid2) <= hi - lo, 1.0, 0.0).astype(F32)


def _swap_rope_halves(x):
    half = DIFF_QK_DIM // 2
    lane = lax.broadcasted_iota(jnp.int32, (x.shape[0], 128), 1)
    first = (lane & (DIFF_QK_DIM - 1)) < half
    parts = []
    for c in range(0, x.shape[1], 128):
        xc = x[:, c:c + 128]
        parts.append(jnp.where(first, pltpu.roll(xc, 128 - half, axis=1), pltpu.roll(xc, half, axis=1)))
    return jnp.concatenate(parts, axis=1)


def _inproj_body(*refs, lay, prompt, tiles_per_seq, tm, n_alias=0):
    if prompt:
        (x_ref, g_ref, w_ref, cos_ref, sin_ref, fb_ref, alog_ref, dtb_ref, tri_ref, pq_ref, pk_ref, wvt_ref) = refs[:12]
        (qcat_o, kcat_o, vcat_o, small_o, cqkv_o, cz_o, beta_o, g_o,
         fkt_o, fvt32_o, dkt_o, dvt32_o, carry_scr) = refs[12 + n_alias:]
        fq_o, dq_o = qcat_o.at[:, 0:512], qcat_o.at[:, 512:768]
        fk_pad_o, dkb_o = kcat_o.at[:, 0:512], kcat_o.at[:, 512:768]
        fvt_o, dvt_o = vcat_o.at[0:512, :], vcat_o.at[512:1024, :]
    else:
        (x_ref, g_ref, w_ref, cos_ref, sin_ref, fb_ref, alog_ref, dtb_ref,
         fq_o, fk_o, fv_o, small_o, dq_o, dk_o, dv_o, cqkv_o, cz_o, beta_o, g_o) = refs

    h = _rms(x_ref[...], g_ref[...]).astype(BF16)

    def proj(name):
        off, width = lay[name]
        return _dot_nt(h, w_ref[off:off + width, :])

    logf = jax.nn.log_sigmoid(proj("small") + fb_ref[...])
    small_o[...] = logf

    if prompt:
        i = pl.program_id(0)

        @pl.when(i % tiles_per_seq == 0)
        def _():
            carry_scr[...] = jnp.zeros_like(carry_scr)

        c = _dot3_l(tri_ref[...], logf) + carry_scr[...]
        carry_scr[...] = c[tm - 1:tm, :]
        c1, c2, c3 = _split3(c * LOG2E)
        c123 = jnp.concatenate([c1, c2, c3], axis=1)
        shape = (tm, 512)
        q = proj("fq_pad") * FOX_QSCALE + _dot(c123, pq_ref[...]) + _lane_ones(shape, AUG_K, AUG_K + 2)
        fq_o[...] = q.astype(BF16)
        k32 = proj("fk_pad")
        fk_pad_o[...] = (k32 + _dot(c123, pk_ref[...]) + _lane_ones(shape, AUG_C, AUG_C + 2)).astype(BF16)
        k32_t = k32.T
        for hd in range(FOX_HEADS):
            fkt_o[hd * 64:(hd + 1) * 64, :] = k32_t[hd * 128:hd * 128 + 64]
        vt = _dot_nt(wvt_ref[...], h)
        for hd in range(4):
            fvt32_o[hd * 64:(hd + 1) * 64, :] = vt[hd * 128:hd * 128 + 64]
            dvt32_o[hd * 64:(hd + 1) * 64, :] = vt[512 + hd * 128:512 + hd * 128 + 64]
        row = lax.broadcasted_iota(jnp.int32, vt.shape, 0) & 127
        vt = (vt + jnp.where(row == ONES_LANE, 1.0, 0.0)).astype(BF16)
        fvt_o[...] = vt[0:512]
        dvt_o[...] = vt[512:1024]
    else:
        fq_o[...] = proj("fq") * FOX_QSCALE
        fk_o[...] = proj("fk")
        fv_o[...] = proj("fv")
        dv_o[...] = proj("dv")

    cos = cos_ref[...]
    sin = sin_ref[...]
    dq = proj("dq")
    dq_o[...] = ((dq * cos + _swap_rope_halves(dq) * sin) * DIFF_QSCALE).astype(BF16 if prompt else F32)
    dk = proj("dk")
    dk = dk * cos + _swap_rope_halves(dk) * sin
    if prompt:
        dkb_o[...] = dk.astype(BF16)
        dkt_o[...] = dk.T
    else:
        dk_o[...] = dk

    cqkv_o[...] = proj("cqkv")
    cz = proj("cz")
    cz_o[...] = cz * jax.nn.sigmoid(cz)
    beta_o[...] = jax.nn.sigmoid(proj("cb_rep"))
    g_o[...] = -jnp.exp(alog_ref[...]) * jax.nn.softplus(proj("ca_rep") + dtb_ref[...])


_CACHE_OUTS = ("fkt", "fvt32", "dkt", "dvt32")


def _inproj(x, gain, w, layer, cos, sin, fbias, alog, dtb, *, prompt, seq_len, w_vt=None, cache_bufs=None):
    r, d = x.shape
    lay, nc = _layout(_PROMPT_COLS if prompt else _SAMPLE_COLS)
    assert w.shape[1] == nc
    tm = min(r, 512 if prompt else 256)
    tm = min(tm, seq_len) if prompt else tm
    fb = jnp.pad(fbias.astype(F32), (0, 124)).reshape(1, 128)
    alog_l = jnp.repeat(alog.astype(F32), 64).reshape(1, 256)
    dtb_l = jnp.repeat(dtb.astype(F32), 64).reshape(1, 256)

    def rows(width):
        return pl.BlockSpec((tm, width), lambda i: (i, 0))

    rope_tiles = cos.shape[0] // tm
    rope = pl.BlockSpec((tm, 256), lambda i: (i % rope_tiles, 0))
    in_specs = [rows(d), _const_spec((1, d)),
                pl.BlockSpec((None, nc, d), lambda i: (layer, 0, 0), pipeline_mode=pl.Buffered(1)),
                rope, rope, _const_spec((1, 128)), _const_spec((1, 256)), _const_spec((1, 256))]
    args = [x, gain.reshape(1, d), w, cos, sin, fb, alog_l, dtb_l]
    if prompt:
        tri = jnp.asarray(np.tril(np.ones((tm, tm), np.float32)), BF16)
        pq, pk = _aug_place_mats()
        in_specs += [_const_spec((tm, tm)), _const_spec((384, 512)), _const_spec((384, 512)),
                     pl.BlockSpec((None, 1024, d), lambda i: (layer, 0, 0), pipeline_mode=pl.Buffered(1))]
        args += [tri, pq, pk, w_vt]
        outs = (("qcat", 768, BF16), ("kcat", 768, BF16), ("vcat", -1024, BF16), ("small", 128, F32),
                ("cqkv", 768, F32), ("cz", 256, F32), ("beta", 256, F32), ("g", 256, F32))
        scratch = [pltpu.VMEM((1, 128), F32)]
    else:
        outs = (("fq", 256, F32), ("fk", 256, F32), ("fv", 256, F32), ("small", 128, F32), ("dq", 256, F32),
                ("dk", 256, F32), ("dv", 256, F32), ("cqkv", 768, F32), ("cz", 256, F32),
                ("beta", 256, F32), ("g", 256, F32))
        scratch = []
    tps = max(seq_len // tm, 1)
    out_shape = [jax.ShapeDtypeStruct((r, wd) if wd > 0 else (-wd, r), dt) for _, wd, dt in outs]
    out_specs = [rows(wd) if wd > 0 else pl.BlockSpec((-wd, tm), lambda i: (0, i)) for _, wd, _ in outs]
    names = [name for name, _, _ in outs]
    aliases = {}
    if prompt:
        for j, name in enumerate(_CACHE_OUTS):
            buf = cache_bufs[name]
            out_shape.append(jax.ShapeDtypeStruct(buf.shape, buf.dtype))
            out_specs.append(pl.BlockSpec((None, None, HEAD_W, tm), lambda i: (layer, i // tps, 0, i % tps)))
            names.append(name)
            aliases[len(args)] = len(outs) + j
            in_specs.append(pl.BlockSpec(memory_space=pl.ANY))
            args.append(buf)
    res = pl.pallas_call(
        functools.partial(_inproj_body, lay=lay, prompt=prompt, tiles_per_seq=tps, tm=tm, n_alias=len(aliases)),
        out_shape=tuple(out_shape),
        grid=(r // tm,),
        in_specs=in_specs,
        out_specs=tuple(out_specs),
        scratch_shapes=scratch,
        input_output_aliases=aliases,
        compiler_params=_cparams("arbitrary"),
        name="inproj_prompt" if prompt else "inproj_sample",
    )(*args)
    return dict(zip(names, res))


def _tri_schedule(nq):
    qi = np.concatenate([np.full(i + 1, i, np.int32) for i in range(nq)])
    kj = np.concatenate([np.arange(i + 1, dtype=np.int32) for i in range(nq)])
    return jnp.asarray(qi), jnp.asarray(kj)


def _online_softmax_step(st, vt, m_ref, acc_ref):
    m_old = m_ref[...]
    m_new = jnp.maximum(m_old, jnp.max(st, axis=0, keepdims=True))
    alpha = jnp.exp2(m_old - m_new)
    p = jnp.exp2(st - m_new)
    acc_ref[...] = alpha * acc_ref[...] + _dot(vt, p.astype(BF16))
    m_ref[...] = m_new


def _causal_mask(st):
    key = lax.broadcasted_iota(jnp.int32, st.shape, 0)
    qry = lax.broadcasted_iota(jnp.int32, st.shape, 1)
    return jnp.where(key <= qry, st, -jnp.inf)


def _diff_lambda(lam_ref, lam_init):
    lp = lam_ref[...]
    s1 = jnp.sum(lp[0:1] * lp[1:2], axis=1, keepdims=True)
    s2 = jnp.sum(lp[2:3] * lp[3:4], axis=1, keepdims=True)
    return jnp.exp(s1) - jnp.exp(s2) + lam_init


N_ATTN_MAPS = FOX_HEADS + 2 * DIFF_HEADS
ATTN_LOOKAHEAD = 3


def _attn_body(qi_ref, kj_ref, q_ref, k_ref, vt_ref, lam_ref, gain_ref,
               of_ref, od_ref, qm_scr, m_scr, acc_scr, *, lam_init):
    t = pl.program_id(1)
    qi = qi_ref[t]
    kj = kj_ref[t]
    n_diff = 2 * DIFF_HEADS
    fq_ref, dq_ref = q_ref.at[:, 0:512], q_ref.at[:, 512:768]
    fk_ref, dk_ref = k_ref.at[:, 0:512], k_ref.at[:, 512:768]
    fvt_ref, dvt_ref = vt_ref.at[0:512, :], vt_ref.at[512:1024, :]

    @pl.when(kj == 0)
    def _():
        m_scr[...] = jnp.full_like(m_scr, -jnp.inf)
        acc_scr[...] = jnp.zeros_like(acc_scr)
        q = dq_ref[...]
        lane_map = lax.broadcasted_iota(jnp.int32, q.shape, 1) // DIFF_QK_DIM
        for m in range(n_diff):
            qm_scr[m] = jnp.where(lane_map == m, q, jnp.zeros_like(q))

    def scores(j):
        if j < FOX_HEADS:
            sl = slice(j * 128, (j + 1) * 128)
            return _dot_nt(fk_ref[:, sl], fq_ref[:, sl])
        return _dot_nt(dk_ref[...], qm_scr[j - FOX_HEADS])

    def values_t(j):
        if j < FOX_HEADS:
            return fvt_ref[j * 128:(j + 1) * 128, :]
        h = (j - FOX_HEADS) // 2
        return dvt_ref[h * 128:(h + 1) * 128, :]

    def step(masked):
        pending = [scores(j) for j in range(ATTN_LOOKAHEAD)]
        for j in range(N_ATTN_MAPS):
            st = pending.pop(0)
            if j + ATTN_LOOKAHEAD < N_ATTN_MAPS:
                pending.append(scores(j + ATTN_LOOKAHEAD))
            if masked:
                st = _causal_mask(st)
            _online_softmax_step(st, values_t(j), m_scr.at[j], acc_scr.at[j])

    @pl.when(kj < qi)
    def _():
        step(False)

    @pl.when(kj == qi)
    def _():
        step(True)
        for h in range(FOX_HEADS):
            a = acc_scr[h]
            o = a[:FOX_DIM] / a[ONES_LANE:ONES_LANE + 1]
            of_ref[:, h * FOX_DIM:(h + 1) * FOX_DIM] = o.T.astype(of_ref.dtype)
        lam = _diff_lambda(lam_ref, lam_init)
        tq = od_ref.shape[0]
        gain = jnp.concatenate([gain_ref[...]] * (tq // 128), axis=1)
        for h in range(DIFF_HEADS):
            a0 = acc_scr[FOX_HEADS + 2 * h]
            a1 = acc_scr[FOX_HEADS + 2 * h + 1]
            o = (a0[:DIFF_V_DIM] / a0[ONES_LANE:ONES_LANE + 1]
                 - lam * (a1[:DIFF_V_DIM] / a1[ONES_LANE:ONES_LANE + 1]))
            ms = jnp.mean(o * o, axis=0, keepdims=True)
            o = o * lax.rsqrt(ms + NORM_EPS) * gain * (1.0 - lam_init)
            od_ref[:, h * DIFF_V_DIM:(h + 1) * DIFF_V_DIM] = o.T.astype(od_ref.dtype)


def _attn_prompt(qcat, kcat, vcat, lam_p, gain, lam_init, n_seq, seq_len):
    tq = min(seq_len, 512)
    nq = seq_len // tq
    qi, kj = _tri_schedule(nq)

    def q_rows(width):
        return pl.BlockSpec((tq, width), lambda n, t, qi, kj: (n * nq + qi[t], 0))

    def k_rows(width):
        return pl.BlockSpec((tq, width), lambda n, t, qi, kj: (n * nq + kj[t], 0))

    gs = pltpu.PrefetchScalarGridSpec(
        num_scalar_prefetch=2,
        grid=(n_seq, int(qi.shape[0])),
        in_specs=[q_rows(768), k_rows(768),
                  pl.BlockSpec((1024, tq), lambda n, t, qi, kj: (0, n * nq + kj[t])),
                  pl.BlockSpec((4, DIFF_QK_DIM), lambda n, t, qi, kj: (0, 0)),
                  pl.BlockSpec((DIFF_V_DIM, 128), lambda n, t, qi, kj: (0, 0))],
        out_specs=(q_rows(HEAD_W), q_rows(HEAD_W)),
        scratch_shapes=[pltpu.VMEM((2 * DIFF_HEADS, tq, HEAD_W), BF16), pltpu.VMEM((N_ATTN_MAPS, 1, tq), F32),
                        pltpu.VMEM((N_ATTN_MAPS, 128, tq), F32)],
    )
    gain_rows = jnp.broadcast_to(gain.astype(F32).reshape(DIFF_V_DIM, 1), (DIFF_V_DIM, 128))
    return pl.pallas_call(
        functools.partial(_attn_body, lam_init=lam_init),
        out_shape=(jax.ShapeDtypeStruct((n_seq * seq_len, HEAD_W), BF16),) * 2,
        grid_spec=gs,
        compiler_params=_cparams("arbitrary", "arbitrary"),
        name="attn_prompt",
    )(qi, kj, qcat, kcat, vcat, lam_p.astype(F32), gain_rows)


def _merge_ffn_body(x_ref, g_ref, wg_ref, wb_ref, wo_ref, b0_ref, b1_ref, b2_ref,
                    g2_ref, wi_ref, wo2_ref, gf_ref, o_ref, *, final_norm, dff, tf):
    x = x_ref[...]
    d = x.shape[1]
    h = _rms(x, g_ref[...]).astype(BF16)
    mix = None
    for b, b_ref in enumerate((b0_ref, b1_ref, b2_ref)):
        gate = jax.nn.sigmoid(_dot_nt(h, wg_ref[b * d:(b + 1) * d, :]))
        term = gate * _dot(b_ref[...].astype(BF16), wb_ref[b])
        mix = term if mix is None else mix + term
    x = x + _dot(mix.astype(BF16), wo_ref[...])
    y = _swiglu_residual(x, g2_ref[...], wi_ref, wo2_ref, dff, tf)
    if final_norm:
        y = _rms(y, gf_ref[...])
    o_ref[...] = y


def _merge_ffn(x, gain, w_gates, w_branch, w_out, layer, branches, gain2, wi, wo2, final_gain=None):
    r, d = x.shape
    dff = wo2.shape[1]
    tm = min(r, 512)
    tf = V7X_MXU_DIM if dff % V7X_MXU_DIM == 0 else dff
    final_norm = final_gain is not None
    gf = final_gain if final_norm else gain2

    def rows(width):
        return pl.BlockSpec((tm, width), lambda i: (i, 0))

    def resident(*shape):
        nd = len(shape)
        return pl.BlockSpec((None,) + shape, lambda i: (layer,) + (0,) * nd, pipeline_mode=pl.Buffered(1))

    return pl.pallas_call(
        functools.partial(_merge_ffn_body, final_norm=final_norm, dff=dff, tf=tf),
        out_shape=jax.ShapeDtypeStruct((r, d), F32),
        grid=(r // tm,),
        in_specs=[rows(d), _const_spec((1, d)),
                  resident(N_BRANCH * d, d), resident(N_BRANCH, HEAD_W, d), resident(d, d),
                  rows(HEAD_W), rows(HEAD_W), rows(HEAD_W),
                  _const_spec((1, d)), resident(d, 2 * dff), resident(dff, d), _const_spec((1, d))],
        out_specs=rows(d),
        compiler_params=_cparams("arbitrary"),
        name="merge_ffn",
    )(x, gain.reshape(1, d), w_gates, w_branch, w_out, *branches,
      gain2.reshape(1, d), wi, wo2, gf.reshape(1, d))


def _delta_consts():
    c = DELTA_CHUNK
    i = np.arange(c)
    lane = np.arange(HEAD_W)
    lh, lj = lane // c, lane % c
    tril = i[:, None] >= i[None, :]
    suffix = i[None, :] > i[:, None]
    lvl = []
    for k in range(1, 7):
        b = 1 << k
        lvl.append((i[:, None] // b == lj[None, :] // b) & (i[:, None] // (b // 2) != lj[None, :] // (b // 2)))
    f = lambda a, dt: jnp.asarray(np.asarray(a, np.float32), dt)
    return dict(
        lhsg=f(np.concatenate([tril, suffix], axis=0), BF16),
        ones=f(np.ones((c, c)), BF16),
        triu4=f(i[:, None] <= lj[None, :], F32),
        tril4=f(i[:, None] >= lj[None, :], F32),
        stril4=f(i[:, None] > lj[None, :], F32),
        eye4=f(i[:, None] == lj[None, :], F32),
        lvl=f(np.stack(lvl), F32),
        maskbd=f(lh[:, None] == lh[None, :], BF16),
    )


def _bd(x4, maskbd):
    return jnp.concatenate([x4, x4, x4, x4], axis=0) * maskbd


def _mm_heads(xs, ys, maskbd):
    return [_dot(x.astype(BF16), _bd(y.astype(BF16), maskbd)) for x, y in zip(xs, ys)]


def _unit_lower_inverse(a4s, eye4, lvl_ref, maskbd):
    ts = [eye4 - a4 * lvl_ref[0] for a4 in a4s]
    for k in range(1, 6):
        ms = [a4 * lvl_ref[k] for a4 in a4s]
        xs = _mm_heads(ms, ts, maskbd)
        ts = [t - d for t, d in zip(ts, _mm_heads(ts, xs, maskbd))]
    return ts


def _head_sums(x, maskbd):
    hi = x.astype(BF16)
    lo = (x - hi.astype(F32)).astype(BF16)
    return _dot(hi, maskbd) + _dot(lo, maskbd)


def _delta_body(cqkv_ref, beta_ref, g_ref, cz_ref, cw_ref, gain_ref,
                lhsg_ref, ones_ref, triu4_ref, tril4_ref, stril4_ref, eye4_ref, lvl_ref, maskbd_ref,
                o_ref, s_ref, xbuf, ub_scr, w_scr, qk_scr, qe_scr, kd_scr, eg_scr, o_scr, sbd_scr, *, tc):
    j = pl.program_id(1)
    c = DELTA_CHUNK
    maskbd = maskbd_ref[...]
    maskf = maskbd.astype(F32)

    @pl.when(j == 0)
    def _():
        xbuf[0:8, :] = jnp.zeros((8, xbuf.shape[1]), F32)
        sbd_scr[...] = jnp.zeros_like(sbd_scr)

    xbuf[8:8 + tc, :] = cqkv_ref[...]
    cw = cw_ref[...]
    y = cw[3:4] * xbuf[8:8 + tc, :]
    for i in range(1, CONV_WIDTH):
        y = y + cw[3 - i:4 - i] * xbuf[8 - i:8 - i + tc, :]
    xbuf[0:8, :] = xbuf[tc:tc + 8, :]
    y = y * jax.nn.sigmoid(y)
    q = y[:, 0:HEAD_W]
    k = y[:, HEAD_W:2 * HEAD_W]
    qn = q * lax.rsqrt(_head_sums(q * q, maskbd) + 1e-6) * (DELTA_K_DIM ** -0.5)
    kn = k * lax.rsqrt(_head_sums(k * k, maskbd) + 1e-6)
    v = y[:, 2 * HEAD_W:3 * HEAD_W]
    beta = beta_ref[...]
    g = g_ref[...]

    cs = [slice(ci * c, (ci + 1) * c) for ci in range(tc // c)]
    qcs, kcs, vcs = [qn[s] for s in cs], [kn[s] for s in cs], [v[s] for s in cs]
    bcs, gcs = [beta[s] for s in cs], [g[s] for s in cs]
    gsums = [_dot3_l(lhsg_ref[...], gc) for gc in gcs]
    grows = [_dot3_l(ones_ref[...], gc * triu4_ref[...]) for gc in gcs]
    gcols, gsufs = [gs[0:c] for gs in gsums], [gs[c:2 * c] for gs in gsums]
    decays = [jnp.where(tril4_ref[...] > 0, jnp.exp(gcol - grow), 0.0) for gcol, grow in zip(gcols, grows)]
    exp_gs = [jnp.exp(gcol) for gcol in gcols]
    kbs = [kc * bc for kc, bc in zip(kcs, bcs)]
    kbds = [_bd(kc.astype(BF16), maskbd) for kc in kcs]
    kks = [_dot_nt(jnp.concatenate([kb, qc], axis=0).astype(BF16), kbd) for kb, qc, kbd in zip(kbs, qcs, kbds)]
    a4s = [kk[0:c] * decay * stril4_ref[...] for kk, decay in zip(kks, decays)]
    qk4s = [kk[c:2 * c] * decay for kk, decay in zip(kks, decays)]
    t16s = [t.astype(BF16) for t in _unit_lower_inverse(a4s, eye4_ref[...], lvl_ref, maskbd)]
    u_bases = [_dot(t16, _bd((vc * bc).astype(BF16), maskbd)) for t16, vc, bc in zip(t16s, vcs, bcs)]
    ws = [_dot(t16, _bd((kb * eg).astype(BF16), maskbd)) for t16, kb, eg in zip(t16s, kbs, exp_gs)]
    ub_scr[...] = jnp.concatenate(u_bases, axis=0)
    w_scr[...] = jnp.concatenate([w.astype(BF16) for w in ws], axis=0)
    qk_scr[...] = jnp.concatenate([x.astype(BF16) for x in qk4s], axis=0)
    qe_scr[...] = jnp.concatenate([(qc * eg).astype(BF16) for qc, eg in zip(qcs, exp_gs)], axis=0)
    kd_scr[...] = jnp.concatenate([(kc * jnp.exp(gs)).astype(BF16) for kc, gs in zip(kcs, gsufs)], axis=0)
    eg_scr[...] = jnp.concatenate([jnp.broadcast_to(eg[c - 1:c, :], (8, HEAD_W)) for eg in exp_gs], axis=0)

    def scan(ci, carry):
        sl = pl.ds(pl.multiple_of(ci * c, c), c)
        s_old = sbd_scr[...]
        s16 = s_old.astype(BF16)
        u16 = (ub_scr[sl, :] - _dot(w_scr[sl, :], s16)).astype(BF16)
        o_scr[sl, :] = _dot(qe_scr[sl, :], s16) + _dot(qk_scr[sl, :], _bd(u16, maskbd))
        exp_g_last = eg_scr[pl.ds(pl.multiple_of(ci * 8, 8), 8), :][0:1]
        sbd_scr[...] = s_old * exp_g_last + maskf * _dot_tn(kd_scr[sl, :], u16)
        return carry

    lax.fori_loop(0, tc // c, scan, 0)

    o = o_scr[...]
    ms = _head_sums(o * o, maskbd) * (1.0 / DELTA_V_DIM)
    o_ref[...] = (o * lax.rsqrt(ms + NORM_EPS) * gain_ref[...] * cz_ref[...]).astype(o_ref.dtype)

    @pl.when(j == pl.num_programs(1) - 1)
    def _():
        s_ref[...] = sbd_scr[...]


def _delta_prompt(cqkv, beta, g, cz, conv_w, gain, n_seq, seq_len):
    tc = min(seq_len, 512)
    nt = seq_len // tc
    cch = cqkv.shape[1]
    consts = _delta_consts()
    names = ("lhsg", "ones", "triu4", "tril4", "stril4", "eye4", "lvl", "maskbd")

    def rows(width):
        return pl.BlockSpec((tc, width), lambda n, j: (n * nt + j, 0))

    return pl.pallas_call(
        functools.partial(_delta_body, tc=tc),
        out_shape=(jax.ShapeDtypeStruct((n_seq * seq_len, HEAD_W), BF16),
                   jax.ShapeDtypeStruct((n_seq, HEAD_W, HEAD_W), F32)),
        grid=(n_seq, nt),
        in_specs=[rows(cch), rows(HEAD_W), rows(HEAD_W), rows(HEAD_W),
                  _const_spec((CONV_WIDTH, cch)), _const_spec((1, HEAD_W))]
                 + [_const_spec(consts[nm].shape) for nm in names],
        out_specs=(rows(HEAD_W), pl.BlockSpec((None, HEAD_W, HEAD_W), lambda n, j: (n, 0, 0))),
        scratch_shapes=[pltpu.VMEM((tc + 8, cch), F32), pltpu.VMEM((tc, HEAD_W), F32),
                        pltpu.VMEM((tc, HEAD_W), BF16), pltpu.VMEM((tc, HEAD_W), BF16),
                        pltpu.VMEM((tc, HEAD_W), BF16), pltpu.VMEM((tc, HEAD_W), BF16),
                        pltpu.VMEM((8 * (tc // DELTA_CHUNK), HEAD_W), F32),
                        pltpu.VMEM((tc, HEAD_W), F32), pltpu.VMEM((HEAD_W, HEAD_W), F32)],
        compiler_params=_cparams("arbitrary", "arbitrary"),
        name="delta_prompt",
    )(cqkv, beta, g, cz, conv_w.astype(F32), jnp.tile(gain.astype(F32), DELTA_HEADS).reshape(1, HEAD_W),
      *[consts[nm] for nm in names])


SROWS = 8


def _sample_attn_consts(n_pages):
    r = np.arange(PAGE_SIZE)
    m2 = r[:, None] > r[None, :]
    m2ones = np.concatenate([m2, np.ones((PAGE_SIZE, PAGE_SIZE), bool)], axis=1)
    idx = np.arange(4 * n_pages)
    u2 = (idx[None, :] // 4 > idx[:, None] // 4) & (idx[None, :] % 4 == idx[:, None] % 4)
    t = np.arange(SROWS)
    tri8 = t[None, :] <= t[:, None]
    sup = t[:, None] > r[None, :]
    lh = np.arange(HEAD_W) // 64
    f = lambda a, dt: jnp.asarray(np.asarray(a, np.float32), dt)
    return dict(m2ones=f(m2ones, BF16), u2=f(u2, BF16), tri8=f(tri8, BF16), sup=f(sup, F32),
                maskbd=f(lh[:, None] == lh[None, :], BF16))


def _pad_rows(x, rows):
    return jnp.concatenate([x, jnp.zeros((rows - x.shape[0], x.shape[1]), x.dtype)], axis=0)


def _block_rows(x, groups, lanes_per_group):
    xr = jnp.concatenate([x] * groups, axis=0)
    rg = lax.broadcasted_iota(jnp.int32, xr.shape, 0) // SROWS
    lg = lax.broadcasted_iota(jnp.int32, xr.shape, 1) // lanes_per_group
    return jnp.where(rg == lg, xr, 0.0).astype(BF16)


def _paged_softmax_pv(qbd, kt_pages, vt_pages, k_new, v_new, bias_pages, bias_new):
    n_pages = len(kt_pages)
    assert n_pages % 2 == 0
    tiles = []
    for p in range(0, n_pages, 2):
        kt2 = jnp.concatenate([kt_pages[p][...], kt_pages[p + 1][...]], axis=1).astype(BF16)
        s = _dot(qbd, kt2)
        if bias_pages is not None:
            s = s + jnp.concatenate([bias_pages[p], bias_pages[p + 1]], axis=1)
        tiles.append(s)
    s = _dot_nt(qbd, _pad_rows(k_new, PAGE_SIZE).astype(BF16))
    if bias_new is not None:
        s = s + bias_new
    row = lax.broadcasted_iota(jnp.int32, s.shape, 0) & (SROWS - 1)
    col = lax.broadcasted_iota(jnp.int32, s.shape, 1)
    tiles.append(jnp.where(col <= row, s, -jnp.inf))
    m = tiles[0].max(axis=1, keepdims=True)
    for s in tiles[1:]:
        m = jnp.maximum(m, s.max(axis=1, keepdims=True))
    acc = None
    den = None
    for i, s in enumerate(tiles):
        p = jnp.exp2(s - m)
        ps = p.sum(axis=1, keepdims=True)
        if 2 * i < n_pages:
            vt2 = jnp.concatenate([vt_pages[2 * i][...], vt_pages[2 * i + 1][...]], axis=1).astype(BF16)
            pv = _dot_nt(p.astype(BF16), vt2)
        else:
            pv = _dot(p.astype(BF16), _pad_rows(v_new, PAGE_SIZE).astype(BF16))
        acc = pv if acc is None else acc + pv
        den = ps if den is None else den + ps
    return acc / den


def _own_head_rows(x, stride):
    lh = lax.broadcasted_iota(jnp.int32, (SROWS, HEAD_W), 1) // 64
    out = None
    for h in range(4):
        part = jnp.where(lh == h, x[h * stride:h * stride + SROWS], 0.0)
        out = part if out is None else out + part
    return out


def _sample_attn_body(pt_ref, fk_hbm, fv_hbm, lf_hbm, dk_hbm, dv_hbm,
                      fq_ref, fkn_ref, fvn_ref, ln_ref, dq_ref, dkn_ref, dvn_ref,
                      m2_ref, u2_ref, tri8_ref, sup_ref, lam_ref, gain_ref, maskbd_ref,
                      of_ref, od_ref, fk_buf, fv_buf, lf_buf, dk_buf, dv_buf, sems, xm_scr, *, n_pages, layer, lam_init):
    b = pl.program_id(0)
    slot = b % 2
    streams = ((fk_hbm, fk_buf), (fv_hbm, fv_buf), (lf_hbm, lf_buf), (dk_hbm, dk_buf), (dv_hbm, dv_buf))

    def page_copies(sample, sl):
        cps = []
        for p in range(n_pages):
            page = pt_ref[sample, p]
            for hbm, buf in streams:
                cps.append(pltpu.make_async_copy(hbm.at[layer, page], buf.at[sl, p], sems.at[sl]))
        return cps

    @pl.when(b == 0)
    def _():
        for cp in page_copies(0, 0):
            cp.start()

    @pl.when(b + 1 < pl.num_programs(0))
    def _():
        for cp in page_copies(b + 1, 1 - slot):
            cp.start()

    for cp in page_copies(b, slot):
        cp.wait()

    fk_pages = [fk_buf.at[slot, p] for p in range(n_pages)]
    fv_pages = [fv_buf.at[slot, p] for p in range(n_pages)]
    dk_pages = [dk_buf.at[slot, p] for p in range(n_pages)]
    dv_pages = [dv_buf.at[slot, p] for p in range(n_pages)]

    for p in range(n_pages):
        xm_scr[4 * p:4 * p + 4, :] = lf_buf[slot, p]
    wt = _dot3_r(xm_scr[...], m2_ref[...])
    within, totals = wt[:, :PAGE_SIZE], wt[:, PAGE_SIZE:]
    d_past = (within + _dot3_l(u2_ref[...], totals)) * LOG2E
    lnew = ln_ref[...]
    tri8 = tri8_ref[...]
    cn = _dot3_l(tri8, lnew) * LOG2E
    cn_h = [jnp.broadcast_to(cn[:, h:h + 1], (SROWS, PAGE_SIZE)) for h in range(FOX_HEADS)]
    bias_pages = [
        jnp.concatenate([d_past[4 * p + h:4 * p + h + 1, :] + cn_h[h] for h in range(FOX_HEADS)], axis=0)
        for p in range(n_pages)]
    sup = sup_ref[...]
    bias_new = jnp.concatenate(
        [_dot3_l(tri8, jnp.broadcast_to(lnew[:, h:h + 1], (SROWS, PAGE_SIZE)) * sup) * LOG2E
         for h in range(FOX_HEADS)], axis=0)
    qbd = _block_rows(fq_ref[...], FOX_HEADS, FOX_DIM)
    o = _paged_softmax_pv(qbd, fk_pages, fv_pages, fkn_ref[...], fvn_ref[...], bias_pages, bias_new)
    of_ref[...] = _own_head_rows(o, SROWS)

    qbd = _block_rows(dq_ref[...], 2 * DIFF_HEADS, DIFF_QK_DIM)
    o = _paged_softmax_pv(qbd, dk_pages, dv_pages, dkn_ref[...], dvn_ref[...], None, None)
    lam = _diff_lambda(lam_ref, lam_init)
    o = _own_head_rows(o, 2 * SROWS) - lam * _own_head_rows(o[SROWS:], 2 * SROWS)
    ms = _head_sums(o * o, maskbd_ref[...]) * (1.0 / DIFF_V_DIM)
    od_ref[...] = o * lax.rsqrt(ms + NORM_EPS) * gain_ref[...] * (1.0 - lam_init)


def _sample_attn(page_table, caches, layer, proj, lam_p, gain, lam_init):
    n_samples, n_pages = page_table.shape
    consts = _sample_attn_consts(n_pages)

    def tile(width):
        return pl.BlockSpec((SROWS, width), lambda b, pt: (b, 0))

    def const(shape):
        nd = len(shape)
        return pl.BlockSpec(shape, lambda b, pt: (0,) * nd, pipeline_mode=pl.Buffered(1))

    in_specs = [pl.BlockSpec(memory_space=pl.ANY) for _ in caches]
    args = list(caches)
    new = (proj["fq"], proj["fk"], proj["fv"], proj["small"], proj["dq"], proj["dk"], proj["dv"])
    in_specs += [tile(a.shape[1]) for a in new]
    args += list(new)
    cvals = (consts["m2ones"], consts["u2"], consts["tri8"], consts["sup"], lam_p.astype(F32),
             jnp.tile(gain.astype(F32), DIFF_HEADS).reshape(1, HEAD_W), consts["maskbd"])
    in_specs += [const(c.shape) for c in cvals]
    args += list(cvals)
    gs = pltpu.PrefetchScalarGridSpec(
        num_scalar_prefetch=1, grid=(n_samples,), in_specs=in_specs,
        out_specs=(tile(HEAD_W), tile(HEAD_W)),
        scratch_shapes=[pltpu.VMEM((2, n_pages) + c.shape[2:], c.dtype) for c in caches]
                       + [pltpu.SemaphoreType.DMA((2,)), pltpu.VMEM((4 * n_pages, PAGE_SIZE), F32)])
    return pl.pallas_call(
        functools.partial(_sample_attn_body, n_pages=n_pages, layer=layer, lam_init=lam_init),
        out_shape=(jax.ShapeDtypeStruct((n_samples * SROWS, HEAD_W), F32),) * 2,
        grid_spec=gs,
        compiler_params=_cparams("arbitrary"),
        name="sample_attn",
    )(page_table, *args)


def _delta_sample_body(cqkv_ref, buf_ref, beta_ref, g_ref, cz_ref, cw_ref, gain_ref, s0_ref,
                       tile_ref, tile_t_ref, maskbd_ref, o_ref, s_ref, xp_scr, pr_scr, u_scr, kd_scr, o_scr,
                       *, n_tok, group):
    maskbd = maskbd_ref[...]
    maskf = maskbd.astype(F32)
    cch = xp_scr.shape[1]
    cw = cw_ref[...]
    ys = []
    for s in range(group):
        base = 16 * s
        xp_scr[base:base + 8, :] = jnp.zeros((8, cch), F32)
        xp_scr[base + 8 - (CONV_WIDTH - 1):base + 8, :] = buf_ref[s]
        xp_scr[base + 8:base + 16, :] = cqkv_ref[SROWS * s:SROWS * (s + 1), :]
        y = cw[3:4] * xp_scr[base + 8:base + 16, :]
        for i in range(1, CONV_WIDTH):
            y = y + cw[3 - i:4 - i] * xp_scr[base + 8 - i:base + 16 - i, :]
        ys.append(y)
    y = jnp.concatenate(ys, axis=0)
    y = y * jax.nn.sigmoid(y)
    q = y[:, 0:HEAD_W]
    k = y[:, HEAD_W:2 * HEAD_W]
    v = y[:, 2 * HEAD_W:3 * HEAD_W]
    q = q * lax.rsqrt(_head_sums(q * q, maskbd) + 1e-6) * (DELTA_K_DIM ** -0.5)
    k = k * lax.rsqrt(_head_sums(k * k, maskbd) + 1e-6)
    beta = beta_ref[...]
    a = jnp.exp(g_ref[...])

    pairs_k = [(t, j) for t in range(n_tok) for j in range(t)]
    pairs_q = [(t, j) for t in range(n_tok) for j in range(t + 1)]
    assert len(pairs_k) + len(pairs_q) <= 16
    pr_scr[...] = jnp.zeros_like(pr_scr)
    for s in range(group):
        r0, p0 = SROWS * s, 16 * s
        for r, (t, j) in enumerate(pairs_k):
            pr_scr[p0 + r:p0 + r + 1, :] = k[r0 + t:r0 + t + 1] * k[r0 + j:r0 + j + 1]
        for r, (t, j) in enumerate(pairs_q):
            rr = p0 + len(pairs_k) + r
            pr_scr[rr:rr + 1, :] = q[r0 + t:r0 + t + 1] * k[r0 + j:r0 + j + 1]
    dots = _head_sums(pr_scr[...], maskbd)

    rows = [slice(SROWS * s, SROWS * (s + 1)) for s in range(group)]
    sbd0s = [maskf * _dot3_r(s0_ref[s], tile_ref[...]) for s in range(group)]
    s16s = [sb.astype(BF16) for sb in sbd0s]
    rks = [_dot(k[r].astype(BF16), s16) for r, s16 in zip(rows, s16s)]
    rqs = [_dot(q[r].astype(BF16), s16) for r, s16 in zip(rows, s16s)]

    u_scr[...] = jnp.zeros_like(u_scr)
    kd_scr[...] = jnp.zeros_like(kd_scr)
    o_scr[...] = jnp.zeros_like(o_scr)
    a_alls = []
    for s in range(group):
        r0, p0 = SROWS * s, 16 * s
        kk = {tj: dots[p0 + r:p0 + r + 1] for r, tj in enumerate(pairs_k)}
        qk = {tj: dots[p0 + len(pairs_k) + r:p0 + len(pairs_k) + r + 1] for r, tj in enumerate(pairs_q)}
        rk_rows = [rks[s][t:t + 1] for t in range(n_tok)]
        rq_rows = [rqs[s][t:t + 1] for t in range(n_tok)]
        for t in range(n_tok):
            a_t = a[r0 + t:r0 + t + 1]
            u_t = beta[r0 + t:r0 + t + 1] * (v[r0 + t:r0 + t + 1] - a_t * rk_rows[t])
            for t2 in range(t + 1, n_tok):
                rk_rows[t2] = a_t * rk_rows[t2] + kk[(t2, t)] * u_t
            for t2 in range(t, n_tok):
                rq_rows[t2] = a_t * rq_rows[t2] + qk[(t2, t)] * u_t
            o_scr[r0 + t:r0 + t + 1, :] = rq_rows[t]
            u_scr[r0 + t:r0 + t + 1, :] = u_t
            tail = None
            for t2 in range(t + 1, n_tok):
                tail = a[r0 + t2:r0 + t2 + 1] if tail is None else tail * a[r0 + t2:r0 + t2 + 1]
            kd_scr[r0 + t:r0 + t + 1, :] = k[r0 + t:r0 + t + 1] if tail is None else k[r0 + t:r0 + t + 1] * tail
        a_all = a[r0:r0 + 1]
        for t in range(1, n_tok):
            a_all = a_all * a[r0 + t:r0 + t + 1]
        a_alls.append(a_all)
    upd = [_dot_tn(kd_scr[r, :].astype(BF16), u_scr[r, :].astype(BF16)) for r in rows]
    s_news = [sb * a_all + maskf * u for sb, a_all, u in zip(sbd0s, a_alls, upd)]
    for s in range(group):
        s_ref[s] = _dot3_r(s_news[s], tile_t_ref[...])

    o = o_scr[...]
    ms = _head_sums(o * o, maskbd) * (1.0 / DELTA_V_DIM)
    o_ref[...] = o * lax.rsqrt(ms + NORM_EPS) * gain_ref[...] * cz_ref[...]


def _delta_sample(proj, conv_state, state, layer, conv_w, gain, n_tok):
    n_samples = state.shape[1]
    cch = conv_state.shape[3]
    v_ = np.arange(DELTA_V_DIM)
    lane = np.arange(HEAD_W)
    tile_m = jnp.asarray((v_[:, None] == lane[None, :] % DELTA_V_DIM).astype(np.float32), BF16)
    tile_t = jnp.asarray((lane[:, None] % DELTA_V_DIM == v_[None, :]).astype(np.float32), BF16)
    lh = lane // 64
    maskbd = jnp.asarray((lh[:, None] == lh[None, :]).astype(np.float32), BF16)

    group = math.gcd(n_samples, 8)
    rows = group * SROWS

    def tile(width):
        return pl.BlockSpec((rows, width), lambda b: (b, 0))

    return pl.pallas_call(
        functools.partial(_delta_sample_body, n_tok=n_tok, group=group),
        out_shape=(jax.ShapeDtypeStruct((n_samples * SROWS, HEAD_W), F32),
                   jax.ShapeDtypeStruct((n_samples, HEAD_W, DELTA_V_DIM), F32)),
        grid=(n_samples // group,),
        in_specs=[tile(cch),
                  pl.BlockSpec((None, group, CONV_WIDTH - 1, cch), lambda b: (layer, b, 0, 0)),
                  tile(HEAD_W), tile(HEAD_W), tile(HEAD_W),
                  _const_spec((CONV_WIDTH, cch)), _const_spec((1, HEAD_W)),
                  pl.BlockSpec((None, group, HEAD_W, DELTA_V_DIM), lambda b: (layer, b, 0, 0)),
                  _const_spec(tile_m.shape), _const_spec(tile_t.shape), _const_spec(maskbd.shape)],
        out_specs=(tile(HEAD_W), pl.BlockSpec((group, HEAD_W, DELTA_V_DIM), lambda b: (b, 0, 0))),
        scratch_shapes=[pltpu.VMEM((2 * rows, cch), F32), pltpu.VMEM((2 * rows, HEAD_W), F32),
                        pltpu.VMEM((rows, HEAD_W), F32), pltpu.VMEM((rows, HEAD_W), F32),
                        pltpu.VMEM((rows, HEAD_W), F32)],
        compiler_params=_cparams("arbitrary"),
        name="delta_sample",
    )(proj["cqkv"], conv_state, proj["beta"], proj["g"], proj["cz"], conv_w.astype(F32),
      jnp.tile(gain.astype(F32), DELTA_HEADS).reshape(1, HEAD_W), state, tile_m, tile_t, maskbd)


def kernel(x_prompt, x_sample, cache_fox_k, cache_fox_v, cache_fox_logf, cache_diff_k, cache_diff_v, state_delta, state_conv, page_table, norm_ffn1, ffn1_wi, ffn1_wo, norm_mix, w_in, fox_f_bias, diff_lambda, diff_norm, delta_conv_w, delta_A_log, delta_dt_bias, delta_norm, w_branch, w_out, norm_ffn2, ffn2_wi, ffn2_wo, norm_final):
    n_seq, seq_len, d = x_prompt.shape
    n_smp, dec_seq, _ = x_sample.shape
    depth = w_in.shape[0]
    n_pool = cache_fox_k.shape[1]
    n_pages = page_table.shape[1]
    past_len = n_pages * PAGE_SIZE
    assert CONV_WIDTH - 1 <= dec_seq <= SROWS and seq_len % DELTA_CHUNK == 0

    wi1, wo1 = ffn1_wi.astype(BF16), ffn1_wo.astype(BF16)
    wi2, wo2 = ffn2_wi.astype(BF16), ffn2_wo.astype(BF16)
    w_in_t = jnp.swapaxes(w_in, 1, 2)
    w_p = _relayout_w_in(w_in_t, _PROMPT_COLS)
    w_s = _relayout_w_in(w_in_t, _SAMPLE_COLS)
    w_vt = _relayout_w_values_t(w_in_t)
    w_gates = w_in_t[:, w_in_t.shape[1] - N_BRANCH * d:, :].astype(BF16)
    wb, wo = w_branch.astype(BF16), w_out.astype(BF16)

    def pages_t(cache):
        c = cache.reshape(depth, n_pool, PAGE_SIZE, HEAD_W)
        return jnp.swapaxes(c, 2, 3)

    caches = (pages_t(cache_fox_k), pages_t(cache_fox_v), jnp.swapaxes(cache_fox_logf, 2, 3),
              pages_t(cache_diff_k), pages_t(cache_diff_v))
    s_delta = state_delta.reshape(depth, n_smp, HEAD_W, DELTA_V_DIM)

    xp = x_prompt.reshape(n_seq * seq_len, d)
    xs = jnp.pad(x_sample, ((0, 0), (0, SROWS - dec_seq), (0, 0))).reshape(n_smp * SROWS, d)
    cos_p, sin_p = _rope_tables(seq_len, 0, seq_len)
    cos_s, sin_s = _rope_tables(n_smp * SROWS, past_len, SROWS)

    cache_bufs = {name: jnp.zeros((depth, n_seq, HEAD_W, seq_len), F32) for name in _CACHE_OUTS}

    new_p, new_s = [], []
    for l in range(depth):
        lam_init = 0.8 - 0.6 * math.exp(-0.3 * l)
        xp = _ffn(xp, norm_ffn1[l], wi1, wo1, l)
        xs = _ffn(xs, norm_ffn1[l], wi1, wo1, l)
        pp = _inproj(xp, norm_mix[l], w_p, l, cos_p, sin_p, fox_f_bias[l], delta_A_log[l], delta_dt_bias[l],
                     prompt=True, seq_len=seq_len, w_vt=w_vt, cache_bufs=cache_bufs)
        cache_bufs = {name: pp[name] for name in _CACHE_OUTS}
        ps = _inproj(xs, norm_mix[l], w_s, l, cos_s, sin_s, fox_f_bias[l], delta_A_log[l], delta_dt_bias[l],
                     prompt=False, seq_len=SROWS)

        of_p, od_p = _attn_prompt(pp["qcat"], pp["kcat"], pp["vcat"],
                                  diff_lambda[l], diff_norm[l], lam_init, n_seq, seq_len)
        ol_p, sbd = _delta_prompt(pp["cqkv"], pp["beta"], pp["g"], pp["cz"], delta_conv_w[l], delta_norm[l],
                                  n_seq, seq_len)
        of_s, od_s = _sample_attn(page_table, caches, l, ps, diff_lambda[l], diff_norm[l], lam_init)
        ol_s, s_new = _delta_sample(ps, state_conv, s_delta, l, delta_conv_w[l], delta_norm[l], dec_seq)

        final = norm_final if l == depth - 1 else None
        xp = _merge_ffn(xp, norm_mix[l], w_gates, wb, wo, l, (of_p, od_p, ol_p), norm_ffn2[l], wi2, wo2, final)
        xs = _merge_ffn(xs, norm_mix[l], w_gates, wb, wo, l, (of_s, od_s, ol_s), norm_ffn2[l], wi2, wo2, final)

        def p3(a):
            return a.reshape(n_seq, seq_len, a.shape[1])

        def s3(a):
            return a.reshape(n_smp, SROWS, a.shape[1])[:, :dec_seq]

        new_p.append(dict(
            fox_logf=p3(pp["small"])[:, :, :FOX_HEADS],
            delta=jnp.stack([sbd[:, h * 64:(h + 1) * 64, h * 64:(h + 1) * 64] for h in range(DELTA_HEADS)], axis=1),
            conv=p3(pp["cqkv"])[:, seq_len - (CONV_WIDTH - 1):]))
        new_s.append(dict(
            fox_k=s3(ps["fk"]).reshape(n_smp, dec_seq, FOX_HEADS, FOX_DIM),
            fox_v=s3(ps["fv"]).reshape(n_smp, dec_seq, FOX_HEADS, FOX_DIM),
            fox_logf=s3(ps["small"])[:, :, :FOX_HEADS],
            diff_k=s3(ps["dk"]).reshape(n_smp, dec_seq, DIFF_HEADS, 2, DIFF_QK_DIM),
            diff_v=s3(ps["dv"]).reshape(n_smp, dec_seq, DIFF_HEADS, DIFF_V_DIM),
            delta=s_new.reshape(n_smp, DELTA_HEADS, DELTA_K_DIM, DELTA_V_DIM),
            conv=s3(ps["cqkv"])[:, dec_seq - (CONV_WIDTH - 1):]))

    def stk(lst, name):
        return jnp.stack([dd[name] for dd in lst])

    def token_major(buf, *feat):
        nf = len(feat)
        b = buf.reshape((depth, n_seq) + feat + (seq_len,))
        return jnp.transpose(b, (0, 1, 2 + nf) + tuple(range(2, 2 + nf)))

    y_prompt = xp.reshape(n_seq, seq_len, d)
    y_sample = xs.reshape(n_smp, SROWS, d)[:, :dec_seq]
    return (y_prompt, y_sample,
            token_major(cache_bufs["fkt"], FOX_HEADS, FOX_DIM), stk(new_s, "fox_k"),
            token_major(cache_bufs["fvt32"], FOX_HEADS, FOX_DIM), stk(new_s, "fox_v"),
            stk(new_p, "fox_logf"), stk(new_s, "fox_logf"),
            token_major(cache_bufs["dkt"], DIFF_HEADS, 2, DIFF_QK_DIM), stk(new_s, "diff_k"),
            token_major(cache_bufs["dvt32"], DIFF_HEADS, DIFF_V_DIM), stk(new_s, "diff_v"),
            stk(new_p, "delta"), stk(new_s, "delta"),
            stk(new_p, "conv"), stk(new_s, "conv"))
```

```python
import functools
import math

import numpy as np
import jax
import jax.numpy as jnp
from jax import lax
from jax.experimental import pallas as pl
from jax.experimental.pallas import tpu as pltpu

F32 = jnp.float32
BF16 = jnp.bfloat16

FOX_HEADS = 4
FOX_DIM = 64
DIFF_HEADS = 4
DIFF_QK_DIM = 32
DIFF_V_DIM = 64
DELTA_HEADS = 4
DELTA_K_DIM = 64
DELTA_V_DIM = 64
CONV_WIDTH = 4
DELTA_CHUNK = 64
N_BRANCH = 3
HEAD_W = 256
PAGE_SIZE = 128
ROPE_THETA = 10000.0
NORM_EPS = 1e-6
LOG2E = 1.4426950408889634
FOX_QSCALE = FOX_DIM ** -0.5 * LOG2E
DIFF_QSCALE = DIFF_QK_DIM ** -0.5 * LOG2E

V7X_VMEM_LIMIT_BYTES = 56 * 1024 * 1024
AUG_C = 64
AUG_K = 67
ONES_LANE = 64


def _cparams(*sem):
    return pltpu.CompilerParams(dimension_semantics=sem, vmem_limit_bytes=V7X_VMEM_LIMIT_BYTES)


def _dot(a, b):
    return jnp.dot(a, b, preferred_element_type=F32)


def _dot_nt(a, b):
    return lax.dot_general(a, b, (((1,), (1,)), ((), ())), preferred_element_type=F32)


def _dot_tn(a, b):
    return lax.dot_general(a, b, (((0,), (0,)), ((), ())), preferred_element_type=F32)


def _split3(x):
    hi = x.astype(BF16)
    r = x - hi.astype(F32)
    mid = r.astype(BF16)
    lo = (r - mid.astype(F32)).astype(BF16)
    return hi, mid, lo


def _dot3_r(x, m):
    hi, mid, lo = _split3(x)
    return _dot(hi, m) + _dot(mid, m) + _dot(lo, m)


def _dot3_l(m, x):
    hi, mid, lo = _split3(x)
    return _dot(m, hi) + _dot(m, mid) + _dot(m, lo)


def _rms(x, gain):
    ms = jnp.mean(x * x, axis=-1, keepdims=True)
    return x * lax.rsqrt(ms + NORM_EPS) * gain


def _const_spec(shape):
    nd = len(shape)
    return pl.BlockSpec(shape, lambda *_: (0,) * nd, pipeline_mode=pl.Buffered(1))


V7X_MXU_DIM = 256


def _swiglu_residual(x, gain, wi_ref, wo_ref, dff, tf):
    h = _rms(x, gain).astype(BF16)
    acc = None
    for c0 in range(0, dff, tf):
        gate = _dot(h, wi_ref[:, c0:c0 + tf])
        up = _dot(h, wi_ref[:, dff + c0:dff + c0 + tf])
        act = (gate * jax.nn.sigmoid(gate) * up).astype(BF16)
        part = _dot(act, wo_ref[c0:c0 + tf, :])
        acc = part if acc is None else acc + part
    return x + 0.5 * acc


def _ffn_body(x_ref, g_ref, wi_ref, wo_ref, gf_ref, o_ref, *, final_norm, dff, tf):
    y = _swiglu_residual(x_ref[...], g_ref[...], wi_ref, wo_ref, dff, tf)
    if final_norm:
        y = _rms(y, gf_ref[...])
    o_ref[...] = y


def _ffn(x, gain, wi, wo, layer, final_gain=None):
    r, d = x.shape
    dff = wo.shape[1]
    tm = min(r, 512)
    tf = V7X_MXU_DIM if dff % V7X_MXU_DIM == 0 else dff
    final_norm = final_gain is not None
    gf = final_gain if final_norm else gain
    return pl.pallas_call(
        functools.partial(_ffn_body, final_norm=final_norm, dff=dff, tf=tf),
        out_shape=jax.ShapeDtypeStruct((r, d), F32),
        grid=(r // tm,),
        in_specs=[
            pl.BlockSpec((tm, d), lambda i: (i, 0)),
            _const_spec((1, d)),
            pl.BlockSpec((None, d, 2 * dff), lambda i: (layer, 0, 0), pipeline_mode=pl.Buffered(1)),
            pl.BlockSpec((None, dff, d), lambda i: (layer, 0, 0), pipeline_mode=pl.Buffered(1)),
            _const_spec((1, d)),
        ],
        out_specs=pl.BlockSpec((tm, d), lambda i: (i, 0)),
        compiler_params=_cparams("arbitrary"),
        name="ffn",
    )(x, gain.reshape(1, d), wi, wo, gf.reshape(1, d))


def _rope_body(inv_ref, cos_ref, sin_ref, *, tm, base, period):
    i = pl.program_id(0)
    row = lax.broadcasted_iota(jnp.int32, (tm, HEAD_W), 0) + i * tm
    lane = lax.broadcasted_iota(jnp.int32, (tm, HEAD_W), 1)
    pos = base + (row & (period - 1))
    ang = pos.astype(F32) * inv_ref[...]
    cos_ref[...] = jnp.cos(ang)
    s = jnp.sin(ang)
    first_half = (lane & (DIFF_QK_DIM - 1)) < DIFF_QK_DIM // 2
    sin_ref[...] = jnp.where(first_half, -s, s)


def _rope_tables(rows, base, period):
    assert period & (period - 1) == 0
    half = DIFF_QK_DIM // 2
    inv = ROPE_THETA ** (-jnp.arange(0, DIFF_QK_DIM, 2, dtype=F32) / DIFF_QK_DIM)
    inv_lanes = jnp.tile(inv, HEAD_W // half).reshape(1, HEAD_W)
    tm = min(rows, 512)
    return pl.pallas_call(
        functools.partial(_rope_body, tm=tm, base=base, period=period),
        out_shape=(jax.ShapeDtypeStruct((rows, HEAD_W), F32),) * 2,
        grid=(rows // tm,),
        in_specs=[pl.BlockSpec((1, HEAD_W), lambda i: (0, 0))],
        out_specs=(pl.BlockSpec((tm, HEAD_W), lambda i: (i, 0)),) * 2,
        compiler_params=_cparams("arbitrary"),
        name="rope_tables",
    )(inv_lanes)


_PROMPT_COLS = (("fq_pad", 512), ("fk_pad", 512), ("dq", 256), ("dk", 256),
                ("cqkv", 768), ("cz", 256), ("cb_rep", 256), ("ca_rep", 256), ("small", 128))
_SAMPLE_COLS = (("fq", 256), ("fk", 256), ("fv", 256), ("dq", 256), ("dk", 256), ("dv", 256),
                ("cqkv", 768), ("cz", 256), ("cb_rep", 256), ("ca_rep", 256), ("small", 128))


def _layout(cols):
    lay, off = {}, 0
    for name, width in cols:
        lay[name] = (off, width)
        off += width
    return lay, off


_W_IN_SIZES = (256, 256, 256, 4, 256, 256, 256, 768, 4, 4, 256)


def _split_w_in_t(w_in_t):
    offs = np.concatenate([[0], np.cumsum(_W_IN_SIZES)])
    return tuple(w_in_t[:, int(offs[i]):int(offs[i + 1]), :] for i in range(len(_W_IN_SIZES)))


def _pad_head_rows(w):
    depth, _, d = w.shape
    w = w.reshape(depth, 4, 64, d)
    return jnp.pad(w, ((0, 0), (0, 0), (0, 64), (0, 0))).reshape(depth, 512, d)


def _relayout_w_in(w_in_t, cols):
    fq, fk, fv, ff, dq, dk, dv, cqkv, cb, ca, cz = _split_w_in_t(w_in_t)

    def rep(w):
        return jnp.repeat(w, 64, axis=1)

    table = {
        "fq": fq, "fk": fk, "fv": fv, "fq_pad": _pad_head_rows(fq), "fk_pad": _pad_head_rows(fk),
        "dq": dq, "dk": dk, "dv": dv,
        "cqkv": cqkv, "cz": cz, "cb_rep": rep(cb), "ca_rep": rep(ca),
        "small": jnp.pad(ff, ((0, 0), (0, 124), (0, 0))),
    }
    return jnp.concatenate([table[name] for name, _ in cols], axis=1).astype(BF16)


def _relayout_w_values_t(w_in_t):
    _, _, fv, _, _, _, dv, _, _, _, _ = _split_w_in_t(w_in_t)
    return jnp.concatenate([_pad_head_rows(fv), _pad_head_rows(dv)], axis=1).astype(BF16)


def _aug_place_mats():
    pq = np.zeros((384, 512), np.float32)
    pk = np.zeros((384, 512), np.float32)
    for piece in range(3):
        for h in range(FOX_HEADS):
            pq[piece * 128 + h, h * 128 + AUG_C + piece] = 1.0
            pk[piece * 128 + h, h * 128 + AUG_K + piece] = -1.0
    return jnp.asarray(pq, BF16), jnp.asarray(pk, BF16)


def _lane_ones(shape, lo, hi):
    lane = lax.broadcasted_iota(jnp.int32, shape, 1) & 127
    mid2 = lo + hi
    return jnp.where(jnp.abs(2 * lane - mid2) <= hi - lo, 1.0, 0.0).astype(F32)


def _swap_rope_halves(x):
    half = DIFF_QK_DIM // 2
    lane = lax.broadcasted_iota(jnp.int32, (x.shape[0], 128), 1)
    first = (lane & (DIFF_QK_DIM - 1)) < half
    parts = []
    for c in range(0, x.shape[1], 128):
        xc = x[:, c:c + 128]
        parts.append(jnp.where(first, pltpu.roll(xc, 128 - half, axis=1), pltpu.roll(xc, half, axis=1)))
    return jnp.concatenate(parts, axis=1)


def _inproj_body(*refs, lay, prompt, tiles_per_seq, tm, n_alias=0):
    if prompt:
        (x_ref, g_ref, w_ref, cos_ref, sin_ref, fb_ref, alog_ref, dtb_ref, tri_ref, pq_ref, pk_ref, wvt_ref) = refs[:12]
        (qcat_o, kcat_o, vcat_o, small_o, cqkv_o, cz_o, beta_o, g_o,
         fkt_o, fvt32_o, dkt_o, dvt32_o, carry_scr) = refs[12 + n_alias:]
        fq_o, dq_o = qcat_o.at[:, 0:512], qcat_o.at[:, 512:768]
        fk_pad_o, dkb_o = kcat_o.at[:, 0:512], kcat_o.at[:, 512:768]
        fvt_o, dvt_o = vcat_o.at[0:512, :], vcat_o.at[512:1024, :]
    else:
        (x_ref, g_ref, w_ref, cos_ref, sin_ref, fb_ref, alog_ref, dtb_ref,
         fq_o, fk_o, fv_o, small_o, dq_o, dk_o, dv_o, cqkv_o, cz_o, beta_o, g_o) = refs

    h = _rms(x_ref[...], g_ref[...]).astype(BF16)

    def proj(name):
        off, width = lay[name]
        return _dot_nt(h, w_ref[off:off + width, :])

    logf = jax.nn.log_sigmoid(proj("small") + fb_ref[...])
    small_o[...] = logf

    if prompt:
        i = pl.program_id(0)

        @pl.when(i % tiles_per_seq == 0)
        def _():
            carry_scr[...] = jnp.zeros_like(carry_scr)

        c = _dot3_l(tri_ref[...], logf) + carry_scr[...]
        carry_scr[...] = c[tm - 1:tm, :]
        c1, c2, c3 = _split3(c * LOG2E)
        c123 = jnp.concatenate([c1, c2, c3], axis=1)
        shape = (tm, 512)
        q = proj("fq_pad") * FOX_QSCALE + _dot(c123, pq_ref[...]) + _lane_ones(shape, AUG_K, AUG_K + 2)
        fq_o[...] = q.astype(BF16)
        k32 = proj("fk_pad")
        fk_pad_o[...] = (k32 + _dot(c123, pk_ref[...]) + _lane_ones(shape, AUG_C, AUG_C + 2)).astype(BF16)
        k32_t = k32.T
        for hd in range(FOX_HEADS):
            fkt_o[hd * 64:(hd + 1) * 64, :] = k32_t[hd * 128:hd * 128 + 64]
        vt = _dot_nt(wvt_ref[...], h)
        for hd in range(4):
            fvt32_o[hd * 64:(hd + 1) * 64, :] = vt[hd * 128:hd * 128 + 64]
            dvt32_o[hd * 64:(hd + 1) * 64, :] = vt[512 + hd * 128:512 + hd * 128 + 64]
        row = lax.broadcasted_iota(jnp.int32, vt.shape, 0) & 127
        vt = (vt + jnp.where(row == ONES_LANE, 1.0, 0.0)).astype(BF16)
        fvt_o[...] = vt[0:512]
        dvt_o[...] = vt[512:1024]
    else:
        fq_o[...] = proj("fq") * FOX_QSCALE
        fk_o[...] = proj("fk")
        fv_o[...] = proj("fv")
        dv_o[...] = proj("dv")

    cos = cos_ref[...]
    sin = sin_ref[...]
    dq = proj("dq")
    dq_o[...] = ((dq * cos + _swap_rope_halves(dq) * sin) * DIFF_QSCALE).astype(BF16 if prompt else F32)
    dk = proj("dk")
    dk = dk * cos + _swap_rope_halves(dk) * sin
    if prompt:
        dkb_o[...] = dk.astype(BF16)
        dkt_o[...] = dk.T
    else:
        dk_o[...] = dk

    cqkv_o[...] = proj("cqkv")
    cz = proj("cz")
    cz_o[...] = cz * jax.nn.sigmoid(cz)
    beta_o[...] = jax.nn.sigmoid(proj("cb_rep"))
    g_o[...] = -jnp.exp(alog_ref[...]) * jax.nn.softplus(proj("ca_rep") + dtb_ref[...])


_CACHE_OUTS = ("fkt", "fvt32", "dkt", "dvt32")


def _inproj(x, gain, w, layer, cos, sin, fbias, alog, dtb, *, prompt, seq_len, w_vt=None, cache_bufs=None):
    r, d = x.shape
    lay, nc = _layout(_PROMPT_COLS if prompt else _SAMPLE_COLS)
    assert w.shape[1] == nc
    tm = min(r, 512 if prompt else 256)
    tm = min(tm, seq_len) if prompt else tm
    fb = jnp.pad(fbias.astype(F32), (0, 124)).reshape(1, 128)
    alog_l = jnp.repeat(alog.astype(F32), 64).reshape(1, 256)
    dtb_l = jnp.repeat(dtb.astype(F32), 64).reshape(1, 256)

    def rows(width):
        return pl.BlockSpec((tm, width), lambda i: (i, 0))

    rope_tiles = cos.shape[0] // tm
    rope = pl.BlockSpec((tm, 256), lambda i: (i % rope_tiles, 0))
    in_specs = [rows(d), _const_spec((1, d)),
                pl.BlockSpec((None, nc, d), lambda i: (layer, 0, 0), pipeline_mode=pl.Buffered(1)),
                rope, rope, _const_spec((1, 128)), _const_spec((1, 256)), _const_spec((1, 256))]
    args = [x, gain.reshape(1, d), w, cos, sin, fb, alog_l, dtb_l]
    if prompt:
        tri = jnp.asarray(np.tril(np.ones((tm, tm), np.float32)), BF16)
        pq, pk = _aug_place_mats()
        in_specs += [_const_spec((tm, tm)), _const_spec((384, 512)), _const_spec((384, 512)),
                     pl.BlockSpec((None, 1024, d), lambda i: (layer, 0, 0), pipeline_mode=pl.Buffered(1))]
        args += [tri, pq, pk, w_vt]
        outs = (("qcat", 768, BF16), ("kcat", 768, BF16), ("vcat", -1024, BF16), ("small", 128, F32),
                ("cqkv", 768, F32), ("cz", 256, F32), ("beta", 256, F32), ("g", 256, F32))
        scratch = [pltpu.VMEM((1, 128), F32)]
    else:
        outs = (("fq", 256, F32), ("fk", 256, F32), ("fv", 256, F32), ("small", 128, F32), ("dq", 256, F32),
                ("dk", 256, F32), ("dv", 256, F32), ("cqkv", 768, F32), ("cz", 256, F32),
                ("beta", 256, F32), ("g", 256, F32))
        scratch = []
    tps = max(seq_len // tm, 1)
    out_shape = [jax.ShapeDtypeStruct((r, wd) if wd > 0 else (-wd, r), dt) for _, wd, dt in outs]
    out_specs = [rows(wd) if wd > 0 else pl.BlockSpec((-wd, tm), lambda i: (0, i)) for _, wd, _ in outs]
    names = [name for name, _, _ in outs]
    aliases = {}
    if prompt:
        for j, name in enumerate(_CACHE_OUTS):
            buf = cache_bufs[name]
            out_shape.append(jax.ShapeDtypeStruct(buf.shape, buf.dtype))
            out_specs.append(pl.BlockSpec((None, None, HEAD_W, tm), lambda i: (layer, i // tps, 0, i % tps)))
            names.append(name)
            aliases[len(args)] = len(outs) + j
            in_specs.append(pl.BlockSpec(memory_space=pl.ANY))
            args.append(buf)
    res = pl.pallas_call(
        functools.partial(_inproj_body, lay=lay, prompt=prompt, tiles_per_seq=tps, tm=tm, n_alias=len(aliases)),
        out_shape=tuple(out_shape),
        grid=(r // tm,),
        in_specs=in_specs,
        out_specs=tuple(out_specs),
        scratch_shapes=scratch,
        input_output_aliases=aliases,
        compiler_params=_cparams("arbitrary"),
        name="inproj_prompt" if prompt else "inproj_sample",
    )(*args)
    return dict(zip(names, res))


def _online_softmax_step(st, vt, m_ref, acc_ref):
    m_old = m_ref[...]
    m_new = jnp.maximum(m_old, jnp.max(st, axis=0, keepdims=True))
    alpha = jnp.exp2(m_old - m_new)
    p = jnp.exp2(st - m_new)
    acc_ref[...] = alpha * acc_ref[...] + _dot(vt, p.astype(BF16))
    m_ref[...] = m_new


def _causal_mask(st):
    key = lax.broadcasted_iota(jnp.int32, st.shape, 0)
    qry = lax.broadcasted_iota(jnp.int32, st.shape, 1)
    return jnp.where(key <= qry, st, -jnp.inf)


def _diff_lambda(lam_ref, lam_init):
    lp = lam_ref[...]
    s1 = jnp.sum(lp[0:1] * lp[1:2], axis=1, keepdims=True)
    s2 = jnp.sum(lp[2:3] * lp[3:4], axis=1, keepdims=True)
    return jnp.exp(s1) - jnp.exp(s2) + lam_init


N_ATTN_MAPS = FOX_HEADS + 2 * DIFF_HEADS
ATTN_LOOKAHEAD = 3


def _attn_body(q_ref, k_hbm, vt_hbm, lam_ref, gain_ref, of_ref, od_ref,
               k_buf, vt_buf, sems, qm_scr, m_scr, acc_scr, *, lam_init, nq, tq):
    n = pl.program_id(0)
    qi = pl.program_id(1)
    n_diff = 2 * DIFF_HEADS
    fq_ref, dq_ref = q_ref.at[:, 0:512], q_ref.at[:, 512:768]

    def tile_copies(kj, slot):
        off = pl.multiple_of((n * nq + kj) * tq, tq)
        return (pltpu.make_async_copy(k_hbm.at[pl.ds(off, tq), :], k_buf.at[slot], sems.at[0, slot]),
                pltpu.make_async_copy(vt_hbm.at[:, pl.ds(off, tq)], vt_buf.at[slot], sems.at[1, slot]))

    for cp in tile_copies(0, 0):
        cp.start()

    m_scr[...] = jnp.full_like(m_scr, -jnp.inf)
    acc_scr[...] = jnp.zeros_like(acc_scr)
    q = dq_ref[...]
    lane_map = lax.broadcasted_iota(jnp.int32, q.shape, 1) // DIFF_QK_DIM
    for m in range(n_diff):
        qm_scr[m] = jnp.where(lane_map == m, q, jnp.zeros_like(q))

    def step(slot, masked):
        k_ref, vt_ref = k_buf.at[slot], vt_buf.at[slot]

        def scores(j):
            if j < FOX_HEADS:
                sl = slice(j * 128, (j + 1) * 128)
                return _dot_nt(k_ref[:, sl], fq_ref[:, sl])
            return _dot_nt(k_ref[:, 512:768], qm_scr[j - FOX_HEADS])

        def values_t(j):
            if j < FOX_HEADS:
                return vt_ref[j * 128:(j + 1) * 128, :]
            h = (j - FOX_HEADS) // 2
            return vt_ref[512 + h * 128:512 + (h + 1) * 128, :]

        pending = [scores(j) for j in range(ATTN_LOOKAHEAD)]
        for j in range(N_ATTN_MAPS):
            st = pending.pop(0)
            if j + ATTN_LOOKAHEAD < N_ATTN_MAPS:
                pending.append(scores(j + ATTN_LOOKAHEAD))
            if masked:
                st = _causal_mask(st)
            _online_softmax_step(st, values_t(j), m_scr.at[j], acc_scr.at[j])

    def below_diagonal(kj, carry):
        slot = kj % 2
        for cp in tile_copies(kj, slot):
            cp.wait()
        for cp in tile_copies(kj + 1, 1 - slot):
            cp.start()
        step(slot, False)
        return carry

    lax.fori_loop(0, qi, below_diagonal, 0)
    slot = qi % 2
    for cp in tile_copies(qi, slot):
        cp.wait()
    step(slot, True)

    for h in range(FOX_HEADS):
        a = acc_scr[h]
        o = a[:FOX_DIM] / a[ONES_LANE:ONES_LANE + 1]
        of_ref[:, h * FOX_DIM:(h + 1) * FOX_DIM] = o.T.astype(of_ref.dtype)
    lam = _diff_lambda(lam_ref, lam_init)
    gain = jnp.concatenate([gain_ref[...]] * (tq // 128), axis=1)
    for h in range(DIFF_HEADS):
        a0 = acc_scr[FOX_HEADS + 2 * h]
        a1 = acc_scr[FOX_HEADS + 2 * h + 1]
        o = (a0[:DIFF_V_DIM] / a0[ONES_LANE:ONES_LANE + 1]
             - lam * (a1[:DIFF_V_DIM] / a1[ONES_LANE:ONES_LANE + 1]))
        ms = jnp.mean(o * o, axis=0, keepdims=True)
        o = o * lax.rsqrt(ms + NORM_EPS) * gain * (1.0 - lam_init)
        od_ref[:, h * DIFF_V_DIM:(h + 1) * DIFF_V_DIM] = o.T.astype(od_ref.dtype)


def _attn_prompt(qcat, kcat, vcat, lam_p, gain, lam_init, n_seq, seq_len):
    tq = min(seq_len, 512)
    nq = seq_len // tq

    def q_rows(width):
        return pl.BlockSpec((tq, width), lambda n, qi: (n * nq + qi, 0))

    gain_rows = jnp.broadcast_to(gain.astype(F32).reshape(DIFF_V_DIM, 1), (DIFF_V_DIM, 128))
    return pl.pallas_call(
        functools.partial(_attn_body, lam_init=lam_init, nq=nq, tq=tq),
        out_shape=(jax.ShapeDtypeStruct((n_seq * seq_len, HEAD_W), BF16),) * 2,
        grid=(n_seq, nq),
        in_specs=[q_rows(768), pl.BlockSpec(memory_space=pl.ANY), pl.BlockSpec(memory_space=pl.ANY),
                  _const_spec((4, DIFF_QK_DIM)), _const_spec((DIFF_V_DIM, 128))],
        out_specs=(q_rows(HEAD_W), q_rows(HEAD_W)),
        scratch_shapes=[pltpu.VMEM((2, tq, 768), BF16), pltpu.VMEM((2, 1024, tq), BF16),
                        pltpu.SemaphoreType.DMA((2, 2)),
                        pltpu.VMEM((2 * DIFF_HEADS, tq, HEAD_W), BF16), pltpu.VMEM((N_ATTN_MAPS, 1, tq), F32),
                        pltpu.VMEM((N_ATTN_MAPS, 128, tq), F32)],
        compiler_params=_cparams("arbitrary", "arbitrary"),
        name="attn_prompt",
    )(qcat, kcat, vcat, lam_p.astype(F32), gain_rows)


def _merge_ffn_body(x_ref, g_ref, wg_ref, wb_ref, wo_ref, b0_ref, b1_ref, b2_ref,
                    g2_ref, wi_ref, wo2_ref, gf_ref, o_ref, *, final_norm, dff, tf):
    x = x_ref[...]
    d = x.shape[1]
    h = _rms(x, g_ref[...]).astype(BF16)
    mix = None
    for b, b_ref in enumerate((b0_ref, b1_ref, b2_ref)):
        gate = jax.nn.sigmoid(_dot_nt(h, wg_ref[b * d:(b + 1) * d, :]))
        term = gate * _dot(b_ref[...].astype(BF16), wb_ref[b])
        mix = term if mix is None else mix + term
    x = x + _dot(mix.astype(BF16), wo_ref[...])
    y = _swiglu_residual(x, g2_ref[...], wi_ref, wo2_ref, dff, tf)
    if final_norm:
        y = _rms(y, gf_ref[...])
    o_ref[...] = y


def _merge_ffn(x, gain, w_gates, w_branch, w_out, layer, branches, gain2, wi, wo2, final_gain=None):
    r, d = x.shape
    dff = wo2.shape[1]
    tm = min(r, 512)
    tf = V7X_MXU_DIM if dff % V7X_MXU_DIM == 0 else dff
    final_norm = final_gain is not None
    gf = final_gain if final_norm else gain2

    def rows(width):
        return pl.BlockSpec((tm, width), lambda i: (i, 0))

    def resident(*shape):
        nd = len(shape)
        return pl.BlockSpec((None,) + shape, lambda i: (layer,) + (0,) * nd, pipeline_mode=pl.Buffered(1))

    return pl.pallas_call(
        functools.partial(_merge_ffn_body, final_norm=final_norm, dff=dff, tf=tf),
        out_shape=jax.ShapeDtypeStruct((r, d), F32),
        grid=(r // tm,),
        in_specs=[rows(d), _const_spec((1, d)),
                  resident(N_BRANCH * d, d), resident(N_BRANCH, HEAD_W, d), resident(d, d),
                  rows(HEAD_W), rows(HEAD_W), rows(HEAD_W),
                  _const_spec((1, d)), resident(d, 2 * dff), resident(dff, d), _const_spec((1, d))],
        out_specs=rows(d),
        compiler_params=_cparams("arbitrary"),
        name="merge_ffn",
    )(x, gain.reshape(1, d), w_gates, w_branch, w_out, *branches,
      gain2.reshape(1, d), wi, wo2, gf.reshape(1, d))


def _delta_consts():
    c = DELTA_CHUNK
    i = np.arange(c)
    lane = np.arange(HEAD_W)
    lh, lj = lane // c, lane % c
    tril = i[:, None] >= i[None, :]
    suffix = i[None, :] > i[:, None]
    lvl = []
    for k in range(1, 7):
        b = 1 << k
        lvl.append((i[:, None] // b == lj[None, :] // b) & (i[:, None] // (b // 2) != lj[None, :] // (b // 2)))
    f = lambda a, dt: jnp.asarray(np.asarray(a, np.float32), dt)
    return dict(
        lhsg=f(np.concatenate([tril, suffix], axis=0), BF16),
        ones=f(np.ones((c, c)), BF16),
        triu4=f(i[:, None] <= lj[None, :], F32),
        tril4=f(i[:, None] >= lj[None, :], F32),
        stril4=f(i[:, None] > lj[None, :], F32),
        eye4=f(i[:, None] == lj[None, :], F32),
        lvl=f(np.stack(lvl), F32),
        maskbd=f(lh[:, None] == lh[None, :], BF16),
    )


def _bd(x4, maskbd):
    return jnp.concatenate([x4, x4, x4, x4], axis=0) * maskbd


def _mm_heads(xs, ys, maskbd):
    return [_dot(x.astype(BF16), _bd(y.astype(BF16), maskbd)) for x, y in zip(xs, ys)]


def _unit_lower_inverse(a4s, eye4, lvl_ref, maskbd):
    ts = [eye4 - a4 * lvl_ref[0] for a4 in a4s]
    for k in range(1, 6):
        ms = [a4 * lvl_ref[k] for a4 in a4s]
        xs = _mm_heads(ms, ts, maskbd)
        ts = [t - d for t, d in zip(ts, _mm_heads(ts, xs, maskbd))]
    return ts


def _head_sums(x, maskbd):
    hi = x.astype(BF16)
    lo = (x - hi.astype(F32)).astype(BF16)
    return _dot(hi, maskbd) + _dot(lo, maskbd)


def _delta_body(cqkv_ref, beta_ref, g_ref, cz_ref, cw_ref, gain_ref,
                lhsg_ref, ones_ref, triu4_ref, tril4_ref, stril4_ref, eye4_ref, lvl_ref, maskbd_ref,
                o_ref, s_ref, xbuf, ub_scr, w_scr, qk_scr, qe_scr, kd_scr, eg_scr, o_scr, sbd_scr, *, tc):
    j = pl.program_id(1)
    c = DELTA_CHUNK
    maskbd = maskbd_ref[...]
    maskf = maskbd.astype(F32)

    @pl.when(j == 0)
    def _():
        xbuf[0:8, :] = jnp.zeros((8, xbuf.shape[1]), F32)
        sbd_scr[...] = jnp.zeros_like(sbd_scr)

    xbuf[8:8 + tc, :] = cqkv_ref[...]
    cw = cw_ref[...]
    y = cw[3:4] * xbuf[8:8 + tc, :]
    for i in range(1, CONV_WIDTH):
        y = y + cw[3 - i:4 - i] * xbuf[8 - i:8 - i + tc, :]
    xbuf[0:8, :] = xbuf[tc:tc + 8, :]
    y = y * jax.nn.sigmoid(y)
    q = y[:, 0:HEAD_W]
    k = y[:, HEAD_W:2 * HEAD_W]
    qn = q * lax.rsqrt(_head_sums(q * q, maskbd) + 1e-6) * (DELTA_K_DIM ** -0.5)
    kn = k * lax.rsqrt(_head_sums(k * k, maskbd) + 1e-6)
    v = y[:, 2 * HEAD_W:3 * HEAD_W]
    beta = beta_ref[...]
    g = g_ref[...]

    cs = [slice(ci * c, (ci + 1) * c) for ci in range(tc // c)]
    qcs, kcs, vcs = [qn[s] for s in cs], [kn[s] for s in cs], [v[s] for s in cs]
    bcs, gcs = [beta[s] for s in cs], [g[s] for s in cs]
    gsums = [_dot3_l(lhsg_ref[...], gc) for gc in gcs]
    grows = [_dot3_l(ones_ref[...], gc * triu4_ref[...]) for gc in gcs]
    gcols, gsufs = [gs[0:c] for gs in gsums], [gs[c:2 * c] for gs in gsums]
    decays = [jnp.where(tril4_ref[...] > 0, jnp.exp(gcol - grow), 0.0) for gcol, grow in zip(gcols, grows)]
    exp_gs = [jnp.exp(gcol) for gcol in gcols]
    kbs = [kc * bc for kc, bc in zip(kcs, bcs)]
    kbds = [_bd(kc.astype(BF16), maskbd) for kc in kcs]
    kks = [_dot_nt(jnp.concatenate([kb, qc], axis=0).astype(BF16), kbd) for kb, qc, kbd in zip(kbs, qcs, kbds)]
    a4s = [kk[0:c] * decay * stril4_ref[...] for kk, decay in zip(kks, decays)]
    qk4s = [kk[c:2 * c] * decay for kk, decay in zip(kks, decays)]
    t16s = [t.astype(BF16) for t in _unit_lower_inverse(a4s, eye4_ref[...], lvl_ref, maskbd)]
    u_bases = [_dot(t16, _bd((vc * bc).astype(BF16), maskbd)) for t16, vc, bc in zip(t16s, vcs, bcs)]
    ws = [_dot(t16, _bd((kb * eg).astype(BF16), maskbd)) for t16, kb, eg in zip(t16s, kbs, exp_gs)]
    ub_scr[...] = jnp.concatenate(u_bases, axis=0)
    w_scr[...] = jnp.concatenate([w.astype(BF16) for w in ws], axis=0)
    qk_scr[...] = jnp.concatenate([x.astype(BF16) for x in qk4s], axis=0)
    qe_scr[...] = jnp.concatenate([(qc * eg).astype(BF16) for qc, eg in zip(qcs, exp_gs)], axis=0)
    kd_scr[...] = jnp.concatenate([(kc * jnp.exp(gs)).astype(BF16) for kc, gs in zip(kcs, gsufs)], axis=0)
    eg_scr[...] = jnp.concatenate([jnp.broadcast_to(eg[c - 1:c, :], (8, HEAD_W)) for eg in exp_gs], axis=0)

    def scan(ci, carry):
        sl = pl.ds(pl.multiple_of(ci * c, c), c)
        s_old = sbd_scr[...]
        s16 = s_old.astype(BF16)
        u16 = (ub_scr[sl, :] - _dot(w_scr[sl, :], s16)).astype(BF16)
        o_scr[sl, :] = _dot(qe_scr[sl, :], s16) + _dot(qk_scr[sl, :], _bd(u16, maskbd))
        exp_g_last = eg_scr[pl.ds(pl.multiple_of(ci * 8, 8), 8), :][0:1]
        sbd_scr[...] = s_old * exp_g_last + maskf * _dot_tn(kd_scr[sl, :], u16)
        return carry

    lax.fori_loop(0, tc // c, scan, 0)

    o = o_scr[...]
    ms = _head_sums(o * o, maskbd) * (1.0 / DELTA_V_DIM)
    o_ref[...] = (o * lax.rsqrt(ms + NORM_EPS) * gain_ref[...] * cz_ref[...]).astype(o_ref.dtype)

    @pl.when(j == pl.num_programs(1) - 1)
    def _():
        s_ref[...] = sbd_scr[...]


def _delta_prompt(cqkv, beta, g, cz, conv_w, gain, n_seq, seq_len):
    tc = min(seq_len, 512)
    nt = seq_len // tc
    cch = cqkv.shape[1]
    consts = _delta_consts()
    names = ("lhsg", "ones", "triu4", "tril4", "stril4", "eye4", "lvl", "maskbd")

    def rows(width):
        return pl.BlockSpec((tc, width), lambda n, j: (n * nt + j, 0))

    return pl.pallas_call(
        functools.partial(_delta_body, tc=tc),
        out_shape=(jax.ShapeDtypeStruct((n_seq * seq_len, HEAD_W), BF16),
                   jax.ShapeDtypeStruct((n_seq, HEAD_W, HEAD_W), F32)),
        grid=(n_seq, nt),
        in_specs=[rows(cch), rows(HEAD_W), rows(HEAD_W), rows(HEAD_W),
                  _const_spec((CONV_WIDTH, cch)), _const_spec((1, HEAD_W))]
                 + [_const_spec(consts[nm].shape) for nm in names],
        out_specs=(rows(HEAD_W), pl.BlockSpec((None, HEAD_W, HEAD_W), lambda n, j: (n, 0, 0))),
        scratch_shapes=[pltpu.VMEM((tc + 8, cch), F32), pltpu.VMEM((tc, HEAD_W), F32),
                        pltpu.VMEM((tc, HEAD_W), BF16), pltpu.VMEM((tc, HEAD_W), BF16),
                        pltpu.VMEM((tc, HEAD_W), BF16), pltpu.VMEM((tc, HEAD_W), BF16),
                        pltpu.VMEM((8 * (tc // DELTA_CHUNK), HEAD_W), F32),
                        pltpu.VMEM((tc, HEAD_W), F32), pltpu.VMEM((HEAD_W, HEAD_W), F32)],
        compiler_params=_cparams("arbitrary", "arbitrary"),
        name="delta_prompt",
    )(cqkv, beta, g, cz, conv_w.astype(F32), jnp.tile(gain.astype(F32), DELTA_HEADS).reshape(1, HEAD_W),
      *[consts[nm] for nm in names])


SROWS = 8


def _sample_attn_consts(n_pages):
    r = np.arange(PAGE_SIZE)
    m2 = r[:, None] > r[None, :]
    m2ones = np.concatenate([m2, np.ones((PAGE_SIZE, PAGE_SIZE), bool)], axis=1)
    idx = np.arange(4 * n_pages)
    u2 = (idx[None, :] // 4 > idx[:, None] // 4) & (idx[None, :] % 4 == idx[:, None] % 4)
    t = np.arange(SROWS)
    tri8 = t[None, :] <= t[:, None]
    sup = t[:, None] > r[None, :]
    lh = np.arange(HEAD_W) // 64
    f = lambda a, dt: jnp.asarray(np.asarray(a, np.float32), dt)
    return dict(m2ones=f(m2ones, BF16), u2=f(u2, BF16), tri8=f(tri8, BF16), sup=f(sup, F32),
                maskbd=f(lh[:, None] == lh[None, :], BF16))


def _pad_rows(x, rows):
    return jnp.concatenate([x, jnp.zeros((rows - x.shape[0], x.shape[1]), x.dtype)], axis=0)


def _block_rows(x, groups, lanes_per_group):
    xr = jnp.concatenate([x] * groups, axis=0)
    rg = lax.broadcasted_iota(jnp.int32, xr.shape, 0) // SROWS
    lg = lax.broadcasted_iota(jnp.int32, xr.shape, 1) // lanes_per_group
    return jnp.where(rg == lg, xr, 0.0).astype(BF16)


def _paged_softmax_pv(qbd, kt_pages, vt_pages, k_new, v_new, bias_pages, bias_new):
    n_pages = len(kt_pages)
    assert n_pages % 2 == 0
    tiles = []
    for p in range(0, n_pages, 2):
        kt2 = jnp.concatenate([kt_pages[p][...], kt_pages[p + 1][...]], axis=1).astype(BF16)
        s = _dot(qbd, kt2)
        if bias_pages is not None:
            s = s + jnp.concatenate([bias_pages[p], bias_pages[p + 1]], axis=1)
        tiles.append(s)
    s = _dot_nt(qbd, _pad_rows(k_new, PAGE_SIZE).astype(BF16))
    if bias_new is not None:
        s = s + bias_new
    row = lax.broadcasted_iota(jnp.int32, s.shape, 0) & (SROWS - 1)
    col = lax.broadcasted_iota(jnp.int32, s.shape, 1)
    tiles.append(jnp.where(col <= row, s, -jnp.inf))
    m = tiles[0].max(axis=1, keepdims=True)
    for s in tiles[1:]:
        m = jnp.maximum(m, s.max(axis=1, keepdims=True))
    acc = None
    den = None
    for i, s in enumerate(tiles):
        p = jnp.exp2(s - m)
        ps = p.sum(axis=1, keepdims=True)
        if 2 * i < n_pages:
            vt2 = jnp.concatenate([vt_pages[2 * i][...], vt_pages[2 * i + 1][...]], axis=1).astype(BF16)
            pv = _dot_nt(p.astype(BF16), vt2)
        else:
            pv = _dot(p.astype(BF16), _pad_rows(v_new, PAGE_SIZE).astype(BF16))
        acc = pv if acc is None else acc + pv
        den = ps if den is None else den + ps
    return acc / den


def _own_head_rows(x, stride):
    lh = lax.broadcasted_iota(jnp.int32, (SROWS, HEAD_W), 1) // 64
    out = None
    for h in range(4):
        part = jnp.where(lh == h, x[h * stride:h * stride + SROWS], 0.0)
        out = part if out is None else out + part
    return out


def _sample_attn_body(pt_ref, fk_hbm, fv_hbm, lf_hbm, dk_hbm, dv_hbm,
                      fq_ref, fkn_ref, fvn_ref, ln_ref, dq_ref, dkn_ref, dvn_ref,
                      m2_ref, u2_ref, tri8_ref, sup_ref, lam_ref, gain_ref, maskbd_ref,
                      of_ref, od_ref, fk_buf, fv_buf, lf_buf, dk_buf, dv_buf, sems, xm_scr, *, n_pages, layer, lam_init):
    b = pl.program_id(0)
    slot = b % 2
    streams = ((fk_hbm, fk_buf), (fv_hbm, fv_buf), (lf_hbm, lf_buf), (dk_hbm, dk_buf), (dv_hbm, dv_buf))

    def page_copies(sample, sl):
        cps = []
        for p in range(n_pages):
            page = pt_ref[sample, p]
            for hbm, buf in streams:
                cps.append(pltpu.make_async_copy(hbm.at[layer, page], buf.at[sl, p], sems.at[sl]))
        return cps

    @pl.when(b == 0)
    def _():
        for cp in page_copies(0, 0):
            cp.start()

    @pl.when(b + 1 < pl.num_programs(0))
    def _():
        for cp in page_copies(b + 1, 1 - slot):
            cp.start()

    for cp in page_copies(b, slot):
        cp.wait()

    fk_pages = [fk_buf.at[slot, p] for p in range(n_pages)]
    fv_pages = [fv_buf.at[slot, p] for p in range(n_pages)]
    dk_pages = [dk_buf.at[slot, p] for p in range(n_pages)]
    dv_pages = [dv_buf.at[slot, p] for p in range(n_pages)]

    for p in range(n_pages):
        xm_scr[4 * p:4 * p + 4, :] = lf_buf[slot, p]
    wt = _dot3_r(xm_scr[...], m2_ref[...])
    within, totals = wt[:, :PAGE_SIZE], wt[:, PAGE_SIZE:]
    d_past = (within + _dot3_l(u2_ref[...], totals)) * LOG2E
    lnew = ln_ref[...]
    tri8 = tri8_ref[...]
    cn = _dot3_l(tri8, lnew) * LOG2E
    cn_h = [jnp.broadcast_to(cn[:, h:h + 1], (SROWS, PAGE_SIZE)) for h in range(FOX_HEADS)]
    bias_pages = [
        jnp.concatenate([d_past[4 * p + h:4 * p + h + 1, :] + cn_h[h] for h in range(FOX_HEADS)], axis=0)
        for p in range(n_pages)]
    sup = sup_ref[...]
    bias_new = jnp.concatenate(
        [_dot3_l(tri8, jnp.broadcast_to(lnew[:, h:h + 1], (SROWS, PAGE_SIZE)) * sup) * LOG2E
         for h in range(FOX_HEADS)], axis=0)
    qbd = _block_rows(fq_ref[...], FOX_HEADS, FOX_DIM)
    o = _paged_softmax_pv(qbd, fk_pages, fv_pages, fkn_ref[...], fvn_ref[...], bias_pages, bias_new)
    of_ref[...] = _own_head_rows(o, SROWS)

    qbd = _block_rows(dq_ref[...], 2 * DIFF_HEADS, DIFF_QK_DIM)
    o = _paged_softmax_pv(qbd, dk_pages, dv_pages, dkn_ref[...], dvn_ref[...], None, None)
    lam = _diff_lambda(lam_ref, lam_init)
    o = _own_head_rows(o, 2 * SROWS) - lam * _own_head_rows(o[SROWS:], 2 * SROWS)
    ms = _head_sums(o * o, maskbd_ref[...]) * (1.0 / DIFF_V_DIM)
    od_ref[...] = o * lax.rsqrt(ms + NORM_EPS) * gain_ref[...] * (1.0 - lam_init)


def _sample_attn(page_table, caches, layer, proj, lam_p, gain, lam_init):
    n_samples, n_pages = page_table.shape
    consts = _sample_attn_consts(n_pages)

    def tile(width):
        return pl.BlockSpec((SROWS, width), lambda b, pt: (b, 0))

    def const(shape):
        nd = len(shape)
        return pl.BlockSpec(shape, lambda b, pt: (0,) * nd, pipeline_mode=pl.Buffered(1))

    in_specs = [pl.BlockSpec(memory_space=pl.ANY) for _ in caches]
    args = list(caches)
    new = (proj["fq"], proj["fk"], proj["fv"], proj["small"], proj["dq"], proj["dk"], proj["dv"])
    in_specs += [tile(a.shape[1]) for a in new]
    args += list(new)
    cvals = (consts["m2ones"], consts["u2"], consts["tri8"], consts["sup"], lam_p.astype(F32),
             jnp.tile(gain.astype(F32), DIFF_HEADS).reshape(1, HEAD_W), consts["maskbd"])
    in_specs += [const(c.shape) for c in cvals]
    args += list(cvals)
    gs = pltpu.PrefetchScalarGridSpec(
        num_scalar_prefetch=1, grid=(n_samples,), in_specs=in_specs,
        out_specs=(tile(HEAD_W), tile(HEAD_W)),
        scratch_shapes=[pltpu.VMEM((2, n_pages) + c.shape[2:], c.dtype) for c in caches]
                       + [pltpu.SemaphoreType.DMA((2,)), pltpu.VMEM((4 * n_pages, PAGE_SIZE), F32)])
    return pl.pallas_call(
        functools.partial(_sample_attn_body, n_pages=n_pages, layer=layer, lam_init=lam_init),
        out_shape=(jax.ShapeDtypeStruct((n_samples * SROWS, HEAD_W), F32),) * 2,
        grid_spec=gs,
        compiler_params=_cparams("arbitrary"),
        name="sample_attn",
    )(page_table, *args)


def _delta_sample_body(cqkv_ref, buf_ref, beta_ref, g_ref, cz_ref, cw_ref, gain_ref, s0_ref,
                       tile_ref, tile_t_ref, maskbd_ref, o_ref, s_ref, xp_scr, pr_scr, u_scr, kd_scr, o_scr,
                       *, n_tok, group):
    maskbd = maskbd_ref[...]
    maskf = maskbd.astype(F32)
    cch = xp_scr.shape[1]
    cw = cw_ref[...]
    ys = []
    for s in range(group):
        base = 16 * s
        xp_scr[base:base + 8, :] = jnp.zeros((8, cch), F32)
        xp_scr[base + 8 - (CONV_WIDTH - 1):base + 8, :] = buf_ref[s]
        xp_scr[base + 8:base + 16, :] = cqkv_ref[SROWS * s:SROWS * (s + 1), :]
        y = cw[3:4] * xp_scr[base + 8:base + 16, :]
        for i in range(1, CONV_WIDTH):
            y = y + cw[3 - i:4 - i] * xp_scr[base + 8 - i:base + 16 - i, :]
        ys.append(y)
    y = jnp.concatenate(ys, axis=0)
    y = y * jax.nn.sigmoid(y)
    q = y[:, 0:HEAD_W]
    k = y[:, HEAD_W:2 * HEAD_W]
    v = y[:, 2 * HEAD_W:3 * HEAD_W]
    q = q * lax.rsqrt(_head_sums(q * q, maskbd) + 1e-6) * (DELTA_K_DIM ** -0.5)
    k = k * lax.rsqrt(_head_sums(k * k, maskbd) + 1e-6)
    beta = beta_ref[...]
    a = jnp.exp(g_ref[...])

    pairs_k = [(t, j) for t in range(n_tok) for j in range(t)]
    pairs_q = [(t, j) for t in range(n_tok) for j in range(t + 1)]
    assert len(pairs_k) + len(pairs_q) <= 16
    pr_scr[...] = jnp.zeros_like(pr_scr)
    for s in range(group):
        r0, p0 = SROWS * s, 16 * s
        for r, (t, j) in enumerate(pairs_k):
            pr_scr[p0 + r:p0 + r + 1, :] = k[r0 + t:r0 + t + 1] * k[r0 + j:r0 + j + 1]
        for r, (t, j) in enumerate(pairs_q):
            rr = p0 + len(pairs_k) + r
            pr_scr[rr:rr + 1, :] = q[r0 + t:r0 + t + 1] * k[r0 + j:r0 + j + 1]
    dots = _head_sums(pr_scr[...], maskbd)

    rows = [slice(SROWS * s, SROWS * (s + 1)) for s in range(group)]
    sbd0s = [maskf * _dot3_r(s0_ref[s], tile_ref[...]) for s in range(group)]
    s16s = [sb.astype(BF16) for sb in sbd0s]
    rks = [_dot(k[r].astype(BF16), s16) for r, s16 in zip(rows, s16s)]
    rqs = [_dot(q[r].astype(BF16), s16) for r, s16 in zip(rows, s16s)]

    u_scr[...] = jnp.zeros_like(u_scr)
    kd_scr[...] = jnp.zeros_like(kd_scr)
    o_scr[...] = jnp.zeros_like(o_scr)
    a_alls = []
    for s in range(group):
        r0, p0 = SROWS * s, 16 * s
        kk = {tj: dots[p0 + r:p0 + r + 1] for r, tj in enumerate(pairs_k)}
        qk = {tj: dots[p0 + len(pairs_k) + r:p0 + len(pairs_k) + r + 1] for r, tj in enumerate(pairs_q)}
        rk_rows = [rks[s][t:t + 1] for t in range(n_tok)]
        rq_rows = [rqs[s][t:t + 1] for t in range(n_tok)]
        for t in range(n_tok):
            a_t = a[r0 + t:r0 + t + 1]
            u_t = beta[r0 + t:r0 + t + 1] * (v[r0 + t:r0 + t + 1] - a_t * rk_rows[t])
            for t2 in range(t + 1, n_tok):
                rk_rows[t2] = a_t * rk_rows[t2] + kk[(t2, t)] * u_t
            for t2 in range(t, n_tok):
                rq_rows[t2] = a_t * rq_rows[t2] + qk[(t2, t)] * u_t
            o_scr[r0 + t:r0 + t + 1, :] = rq_rows[t]
            u_scr[r0 + t:r0 + t + 1, :] = u_t
            tail = None
            for t2 in range(t + 1, n_tok):
                tail = a[r0 + t2:r0 + t2 + 1] if tail is None else tail * a[r0 + t2:r0 + t2 + 1]
            kd_scr[r0 + t:r0 + t + 1, :] = k[r0 + t:r0 + t + 1] if tail is None else k[r0 + t:r0 + t + 1] * tail
        a_all = a[r0:r0 + 1]
        for t in range(1, n_tok):
            a_all = a_all * a[r0 + t:r0 + t + 1]
        a_alls.append(a_all)
    upd = [_dot_tn(kd_scr[r, :].astype(BF16), u_scr[r, :].astype(BF16)) for r in rows]
    s_news = [sb * a_all + maskf * u for sb, a_all, u in zip(sbd0s, a_alls, upd)]
    for s in range(group):
        s_ref[s] = _dot3_r(s_news[s], tile_t_ref[...])

    o = o_scr[...]
    ms = _head_sums(o * o, maskbd) * (1.0 / DELTA_V_DIM)
    o_ref[...] = o * lax.rsqrt(ms + NORM_EPS) * gain_ref[...] * cz_ref[...]


def _delta_sample(proj, conv_state, state, layer, conv_w, gain, n_tok):
    n_samples = state.shape[1]
    cch = conv_state.shape[3]
    v_ = np.arange(DELTA_V_DIM)
    lane = np.arange(HEAD_W)
    tile_m = jnp.asarray((v_[:, None] == lane[None, :] % DELTA_V_DIM).astype(np.float32), BF16)
    tile_t = jnp.asarray((lane[:, None] % DELTA_V_DIM == v_[None, :]).astype(np.float32), BF16)
    lh = lane // 64
    maskbd = jnp.asarray((lh[:, None] == lh[None, :]).astype(np.float32), BF16)

    group = math.gcd(n_samples, 8)
    rows = group * SROWS

    def tile(width):
        return pl.BlockSpec((rows, width), lambda b: (b, 0))

    return pl.pallas_call(
        functools.partial(_delta_sample_body, n_tok=n_tok, group=group),
        out_shape=(jax.ShapeDtypeStruct((n_samples * SROWS, HEAD_W), F32),
                   jax.ShapeDtypeStruct((n_samples, HEAD_W, DELTA_V_DIM), F32)),
        grid=(n_samples // group,),
        in_specs=[tile(cch),
                  pl.BlockSpec((None, group, CONV_WIDTH - 1, cch), lambda b: (layer, b, 0, 0)),
                  tile(HEAD_W), tile(HEAD_W), tile(HEAD_W),
                  _const_spec((CONV_WIDTH, cch)), _const_spec((1, HEAD_W)),
                  pl.BlockSpec((None, group, HEAD_W, DELTA_V_DIM), lambda b: (layer, b, 0, 0)),
                  _const_spec(tile_m.shape), _const_spec(tile_t.shape), _const_spec(maskbd.shape)],
        out_specs=(tile(HEAD_W), pl.BlockSpec((group, HEAD_W, DELTA_V_DIM), lambda b: (b, 0, 0))),
        scratch_shapes=[pltpu.VMEM((2 * rows, cch), F32), pltpu.VMEM((2 * rows, HEAD_W), F32),
                        pltpu.VMEM((rows, HEAD_W), F32), pltpu.VMEM((rows, HEAD_W), F32),
                        pltpu.VMEM((rows, HEAD_W), F32)],
        compiler_params=_cparams("arbitrary"),
        name="delta_sample",
    )(proj["cqkv"], conv_state, proj["beta"], proj["g"], proj["cz"], conv_w.astype(F32),
      jnp.tile(gain.astype(F32), DELTA_HEADS).reshape(1, HEAD_W), state, tile_m, tile_t, maskbd)


def kernel(x_prompt, x_sample, cache_fox_k, cache_fox_v, cache_fox_logf, cache_diff_k, cache_diff_v, state_delta, state_conv, page_table, norm_ffn1, ffn1_wi, ffn1_wo, norm_mix, w_in, fox_f_bias, diff_lambda, diff_norm, delta_conv_w, delta_A_log, delta_dt_bias, delta_norm, w_branch, w_out, norm_ffn2, ffn2_wi, ffn2_wo, norm_final):
    n_seq, seq_len, d = x_prompt.shape
    n_smp, dec_seq, _ = x_sample.shape
    depth = w_in.shape[0]
    n_pool = cache_fox_k.shape[1]
    n_pages = page_table.shape[1]
    past_len = n_pages * PAGE_SIZE
    assert CONV_WIDTH - 1 <= dec_seq <= SROWS and seq_len % DELTA_CHUNK == 0

    wi1, wo1 = ffn1_wi.astype(BF16), ffn1_wo.astype(BF16)
    wi2, wo2 = ffn2_wi.astype(BF16), ffn2_wo.astype(BF16)
    w_in_t = jnp.swapaxes(w_in, 1, 2)
    w_p = _relayout_w_in(w_in_t, _PROMPT_COLS)
    w_s = _relayout_w_in(w_in_t, _SAMPLE_COLS)
    w_vt = _relayout_w_values_t(w_in_t)
    w_gates = w_in_t[:, w_in_t.shape[1] - N_BRANCH * d:, :].astype(BF16)
    wb, wo = w_branch.astype(BF16), w_out.astype(BF16)

    def pages_t(cache):
        c = cache.reshape(depth, n_pool, PAGE_SIZE, HEAD_W)
        return jnp.swapaxes(c, 2, 3)

    caches = (pages_t(cache_fox_k), pages_t(cache_fox_v), jnp.swapaxes(cache_fox_logf, 2, 3),
              pages_t(cache_diff_k), pages_t(cache_diff_v))
    s_delta = state_delta.reshape(depth, n_smp, HEAD_W, DELTA_V_DIM)

    xp = x_prompt.reshape(n_seq * seq_len, d)
    xs = jnp.pad(x_sample, ((0, 0), (0, SROWS - dec_seq), (0, 0))).reshape(n_smp * SROWS, d)
    cos_p, sin_p = _rope_tables(seq_len, 0, seq_len)
    cos_s, sin_s = _rope_tables(n_smp * SROWS, past_len, SROWS)

    cache_bufs = {name: jnp.zeros((depth, n_seq, HEAD_W, seq_len), F32) for name in _CACHE_OUTS}

    new_p, new_s = [], []
    for l in range(depth):
        lam_init = 0.8 - 0.6 * math.exp(-0.3 * l)
        xp = _ffn(xp, norm_ffn1[l], wi1, wo1, l)
        xs = _ffn(xs, norm_ffn1[l], wi1, wo1, l)
        pp = _inproj(xp, norm_mix[l], w_p, l, cos_p, sin_p, fox_f_bias[l], delta_A_log[l], delta_dt_bias[l],
                     prompt=True, seq_len=seq_len, w_vt=w_vt, cache_bufs=cache_bufs)
        cache_bufs = {name: pp[name] for name in _CACHE_OUTS}
        ps = _inproj(xs, norm_mix[l], w_s, l, cos_s, sin_s, fox_f_bias[l], delta_A_log[l], delta_dt_bias[l],
                     prompt=False, seq_len=SROWS)

        of_p, od_p = _attn_prompt(pp["qcat"], pp["kcat"], pp["vcat"],
                                  diff_lambda[l], diff_norm[l], lam_init, n_seq, seq_len)
        ol_p, sbd = _delta_prompt(pp["cqkv"], pp["beta"], pp["g"], pp["cz"], delta_conv_w[l], delta_norm[l],
                                  n_seq, seq_len)
        of_s, od_s = _sample_attn(page_table, caches, l, ps, diff_lambda[l], diff_norm[l], lam_init)
        ol_s, s_new = _delta_sample(ps, state_conv, s_delta, l, delta_conv_w[l], delta_norm[l], dec_seq)

        final = norm_final if l == depth - 1 else None
        xp = _merge_ffn(xp, norm_mix[l], w_gates, wb, wo, l, (of_p, od_p, ol_p), norm_ffn2[l], wi2, wo2, final)
        xs = _merge_ffn(xs, norm_mix[l], w_gates, wb, wo, l, (of_s, od_s, ol_s), norm_ffn2[l], wi2, wo2, final)

        def p3(a):
            return a.reshape(n_seq, seq_len, a.shape[1])

        def s3(a):
            return a.reshape(n_smp, SROWS, a.shape[1])[:, :dec_seq]

        new_p.append(dict(
            fox_logf=p3(pp["small"])[:, :, :FOX_HEADS],
            delta=jnp.stack([sbd[:, h * 64:(h + 1) * 64, h * 64:(h + 1) * 64] for h in range(DELTA_HEADS)], axis=1),
            conv=p3(pp["cqkv"])[:, seq_len - (CONV_WIDTH - 1):]))
        new_s.append(dict(
            fox_k=s3(ps["fk"]).reshape(n_smp, dec_seq, FOX_HEADS, FOX_DIM),
            fox_v=s3(ps["fv"]).reshape(n_smp, dec_seq, FOX_HEADS, FOX_DIM),
            fox_logf=s3(ps["small"])[:, :, :FOX_HEADS],
            diff_k=s3(ps["dk"]).reshape(n_smp, dec_seq, DIFF_HEADS, 2, DIFF_QK_DIM),
            diff_v=s3(ps["dv"]).reshape(n_smp, dec_seq, DIFF_HEADS, DIFF_V_DIM),
            delta=s_new.reshape(n_smp, DELTA_HEADS, DELTA_K_DIM, DELTA_V_DIM),
            conv=s3(ps["cqkv"])[:, dec_seq - (CONV_WIDTH - 1):]))

    def stk(lst, name):
        return jnp.stack([dd[name] for dd in lst])

    def token_major(buf, *feat):
        nf = len(feat)
        b = buf.reshape((depth, n_seq) + feat + (seq_len,))
        return jnp.transpose(b, (0, 1, 2 + nf) + tuple(range(2, 2 + nf)))

    y_prompt = xp.reshape(n_seq, seq_len, d)
    y_sample = xs.reshape(n_smp, SROWS, d)[:, :dec_seq]
    return (y_prompt, y_sample,
            token_major(cache_bufs["fkt"], FOX_HEADS, FOX_DIM), stk(new_s, "fox_k"),
            token_major(cache_bufs["fvt32"], FOX_HEADS, FOX_DIM), stk(new_s, "fox_v"),
            stk(new_p, "fox_logf"), stk(new_s, "fox_logf"),
            token_major(cache_bufs["dkt"], DIFF_HEADS, 2, DIFF_QK_DIM), stk(new_s, "diff_k"),
            token_major(cache_bufs["dvt32"], DIFF_HEADS, DIFF_V_DIM), stk(new_s, "diff_v"),
            stk(new_p, "delta"), stk(new_s, "delta"),
            stk(new_p, "conv"), stk(new_s, "conv"))
```

```python
import functools
import math

import numpy as np
import jax
import jax.numpy as jnp
from jax import lax
from jax.experimental import pallas as pl
from jax.experimental.pallas import tpu as pltpu

F32 = jnp.float32
BF16 = jnp.bfloat16

FOX_HEADS = 4
FOX_DIM = 64
DIFF_HEADS = 4
DIFF_QK_DIM = 32
DIFF_V_DIM = 64
DELTA_HEADS = 4
DELTA_K_DIM = 64
DELTA_V_DIM = 64
CONV_WIDTH = 4
DELTA_CHUNK = 64
N_BRANCH = 3
HEAD_W = 256
PAGE_SIZE = 128
ROPE_THETA = 10000.0
NORM_EPS = 1e-6
LOG2E = 1.4426950408889634
FOX_QSCALE = FOX_DIM ** -0.5 * LOG2E
DIFF_QSCALE = DIFF_QK_DIM ** -0.5 * LOG2E

V7X_VMEM_LIMIT_BYTES = 56 * 1024 * 1024
AUG_C = 64
AUG_K = 67
ONES_LANE = 64


def _cparams(*sem):
    return pltpu.CompilerParams(dimension_semantics=sem, vmem_limit_bytes=V7X_VMEM_LIMIT_BYTES)


def _dot(a, b):
    return jnp.dot(a, b, preferred_element_type=F32)


def _dot_nt(a, b):
    return lax.dot_general(a, b, (((1,), (1,)), ((), ())), preferred_element_type=F32)


def _dot_tn(a, b):
    return lax.dot_general(a, b, (((0,), (0,)), ((), ())), preferred_element_type=F32)


def _split3(x):
    hi = x.astype(BF16)
    r = x - hi.astype(F32)
    mid = r.astype(BF16)
    lo = (r - mid.astype(F32)).astype(BF16)
    return hi, mid, lo


def _dot3_r(x, m):
    hi, mid, lo = _split3(x)
    return _dot(hi, m) + _dot(mid, m) + _dot(lo, m)


def _dot3_l(m, x):
    hi, mid, lo = _split3(x)
    return _dot(m, hi) + _dot(m, mid) + _dot(m, lo)


def _rms(x, gain):
    ms = jnp.mean(x * x, axis=-1, keepdims=True)
    return x * lax.rsqrt(ms + NORM_EPS) * gain


def _const_spec(shape):
    nd = len(shape)
    return pl.BlockSpec(shape, lambda *_: (0,) * nd, pipeline_mode=pl.Buffered(1))


V7X_MXU_DIM = 256


def _swiglu_residual(x, gain, wi_ref, wo_ref, dff, tf):
    h = _rms(x, gain).astype(BF16)
    acc = None
    for c0 in range(0, dff, tf):
        gate = _dot(h, wi_ref[:, c0:c0 + tf])
        up = _dot(h, wi_ref[:, dff + c0:dff + c0 + tf])
        act = (gate * jax.nn.sigmoid(gate) * up).astype(BF16)
        part = _dot(act, wo_ref[c0:c0 + tf, :])
        acc = part if acc is None else acc + part
    return x + 0.5 * acc


def _ffn_body(x_ref, g_ref, wi_ref, wo_ref, gf_ref, o_ref, *, final_norm, dff, tf):
    y = _swiglu_residual(x_ref[...], g_ref[...], wi_ref, wo_ref, dff, tf)
    if final_norm:
        y = _rms(y, gf_ref[...])
    o_ref[...] = y


def _ffn(x, gain, wi, wo, layer, final_gain=None):
    r, d = x.shape
    dff = wo.shape[1]
    tm = min(r, 512)
    tf = V7X_MXU_DIM if dff % V7X_MXU_DIM == 0 else dff
    final_norm = final_gain is not None
    gf = final_gain if final_norm else gain
    return pl.pallas_call(
        functools.partial(_ffn_body, final_norm=final_norm, dff=dff, tf=tf),
        out_shape=jax.ShapeDtypeStruct((r, d), F32),
        grid=(r // tm,),
        in_specs=[
            pl.BlockSpec((tm, d), lambda i: (i, 0)),
            _const_spec((1, d)),
            pl.BlockSpec((None, d, 2 * dff), lambda i: (layer, 0, 0), pipeline_mode=pl.Buffered(1)),
            pl.BlockSpec((None, dff, d), lambda i: (layer, 0, 0), pipeline_mode=pl.Buffered(1)),
            _const_spec((1, d)),
        ],
        out_specs=pl.BlockSpec((tm, d), lambda i: (i, 0)),
        compiler_params=_cparams("arbitrary"),
        name="ffn",
    )(x, gain.reshape(1, d), wi, wo, gf.reshape(1, d))


def _rope_body(inv_ref, cos_ref, sin_ref, *, tm, base, period):
    i = pl.program_id(0)
    row = lax.broadcasted_iota(jnp.int32, (tm, HEAD_W), 0) + i * tm
    lane = lax.broadcasted_iota(jnp.int32, (tm, HEAD_W), 1)
    pos = base + (row & (period - 1))
    ang = pos.astype(F32) * inv_ref[...]
    cos_ref[...] = jnp.cos(ang)
    s = jnp.sin(ang)
    first_half = (lane & (DIFF_QK_DIM - 1)) < DIFF_QK_DIM // 2
    sin_ref[...] = jnp.where(first_half, -s, s)


def _rope_tables(rows, base, period):
    assert period & (period - 1) == 0
    half = DIFF_QK_DIM // 2
    inv = ROPE_THETA ** (-jnp.arange(0, DIFF_QK_DIM, 2, dtype=F32) / DIFF_QK_DIM)
    inv_lanes = jnp.tile(inv, HEAD_W // half).reshape(1, HEAD_W)
    tm = min(rows, 512)
    return pl.pallas_call(
        functools.partial(_rope_body, tm=tm, base=base, period=period),
        out_shape=(jax.ShapeDtypeStruct((rows, HEAD_W), F32),) * 2,
        grid=(rows // tm,),
        in_specs=[pl.BlockSpec((1, HEAD_W), lambda i: (0, 0))],
        out_specs=(pl.BlockSpec((tm, HEAD_W), lambda i: (i, 0)),) * 2,
        compiler_params=_cparams("arbitrary"),
        name="rope_tables",
    )(inv_lanes)


_PROMPT_COLS = (("fq_pad", 512), ("fk_pad", 512), ("dq", 256), ("dk", 256),
                ("cqkv", 768), ("cz", 256), ("cb_rep", 256), ("ca_rep", 256), ("small", 128))
_SAMPLE_COLS = (("fq", 256), ("fk", 256), ("fv", 256), ("dq", 256), ("dk", 256), ("dv", 256),
                ("cqkv", 768), ("cz", 256), ("cb_rep", 256), ("ca_rep", 256), ("small", 128))


def _layout(cols):
    lay, off = {}, 0
    for name, width in cols:
        lay[name] = (off, width)
        off += width
    return lay, off


_W_IN_SIZES = (256, 256, 256, 4, 256, 256, 256, 768, 4, 4, 256)


def _split_w_in_t(w_in_t):
    offs = np.concatenate([[0], np.cumsum(_W_IN_SIZES)])
    return tuple(w_in_t[:, int(offs[i]):int(offs[i + 1]), :] for i in range(len(_W_IN_SIZES)))


def _pad_head_rows(w):
    depth, _, d = w.shape
    w = w.reshape(depth, 4, 64, d)
    return jnp.pad(w, ((0, 0), (0, 0), (0, 64), (0, 0))).reshape(depth, 512, d)


def _relayout_w_in(w_in_t, cols):
    fq, fk, fv, ff, dq, dk, dv, cqkv, cb, ca, cz = _split_w_in_t(w_in_t)

    def rep(w):
        return jnp.repeat(w, 64, axis=1)

    table = {
        "fq": fq, "fk": fk, "fv": fv, "fq_pad": _pad_head_rows(fq), "fk_pad": _pad_head_rows(fk),
        "dq": dq, "dk": dk, "dv": dv,
        "cqkv": cqkv, "cz": cz, "cb_rep": rep(cb), "ca_rep": rep(ca),
        "small": jnp.pad(ff, ((0, 0), (0, 124), (0, 0))),
    }
    return jnp.concatenate([table[name] for name, _ in cols], axis=1).astype(BF16)


def _relayout_w_values_t(w_in_t):
    _, _, fv, _, _, _, dv, _, _, _, _ = _split_w_in_t(w_in_t)
    return jnp.concatenate([_pad_head_rows(fv), _pad_head_rows(dv)], axis=1).astype(BF16)


def _aug_place_mats():
    pq = np.zeros((384, 512), np.float32)
    pk = np.zeros((384, 512), np.float32)
    for piece in range(3):
        for h in range(FOX_HEADS):
            pq[piece * 128 + h, h * 128 + AUG_C + piece] = 1.0
            pk[piece * 128 + h, h * 128 + AUG_K + piece] = -1.0
    return jnp.asarray(pq, BF16), jnp.asarray(pk, BF16)


def _lane_ones(shape, lo, hi):
    lane = lax.broadcasted_iota(jnp.int32, shape, 1) & 127
    mid2 = lo + hi
    return jnp.where(jnp.abs(2 * lane - mid2) <= hi - lo, 1.0, 0.0).astype(F32)


def _swap_rope_halves(x):
    half = DIFF_QK_DIM // 2
    lane = lax.broadcasted_iota(jnp.int32, (x.shape[0], 128), 1)
    first = (lane & (DIFF_QK_DIM - 1)) < half
    parts = []
    for c in range(0, x.shape[1], 128):
        xc = x[:, c:c + 128]
        parts.append(jnp.where(first, pltpu.roll(xc, 128 - half, axis=1), pltpu.roll(xc, half, axis=1)))
    return jnp.concatenate(parts, axis=1)


def _inproj_body(*refs, lay, prompt, tiles_per_seq, tm, n_alias=0):
    if prompt:
        (x_ref, g_ref, w_ref, cos_ref, sin_ref, fb_ref, alog_ref, dtb_ref, tri_ref, pq_ref, pk_ref, wvt_ref) = refs[:12]
        (qcat_o, kcat_o, vcat_o, small_o, cqkv_o, cz_o, beta_o, g_o,
         fkt_o, fvt32_o, dkt_o, dvt32_o, carry_scr) = refs[12 + n_alias:]
        fq_o, dq_o = qcat_o.at[:, 0:512], qcat_o.at[:, 512:768]
        fk_pad_o, dkb_o = kcat_o.at[:, 0:512], kcat_o.at[:, 512:768]
        fvt_o, dvt_o = vcat_o.at[0:512, :], vcat_o.at[512:1024, :]
    else:
        (x_ref, g_ref, w_ref, cos_ref, sin_ref, fb_ref, alog_ref, dtb_ref,
         fq_o, fk_o, fv_o, small_o, dq_o, dk_o, dv_o, cqkv_o, cz_o, beta_o, g_o) = refs

    h = _rms(x_ref[...], g_ref[...]).astype(BF16)

    def proj(name):
        off, width = lay[name]
        return _dot_nt(h, w_ref[off:off + width, :])

    logf = jax.nn.log_sigmoid(proj("small") + fb_ref[...])
    small_o[...] = logf

    if prompt:
        i = pl.program_id(0)

        @pl.when(i % tiles_per_seq == 0)
        def _():
            carry_scr[...] = jnp.zeros_like(carry_scr)

        c = _dot3_l(tri_ref[...], logf) + carry_scr[...]
        carry_scr[...] = c[tm - 1:tm, :]
        c1, c2, c3 = _split3(c * LOG2E)
        c123 = jnp.concatenate([c1, c2, c3], axis=1)
        shape = (tm, 512)
        q = proj("fq_pad") * FOX_QSCALE + _dot(c123, pq_ref[...]) + _lane_ones(shape, AUG_K, AUG_K + 2)
        fq_o[...] = q.astype(BF16)
        k32 = proj("fk_pad")
        fk_pad_o[...] = (k32 + _dot(c123, pk_ref[...]) + _lane_ones(shape, AUG_C, AUG_C + 2)).astype(BF16)
        k32_t = k32.T
        for hd in range(FOX_HEADS):
            fkt_o[hd * 64:(hd + 1) * 64, :] = k32_t[hd * 128:hd * 128 + 64]
        vt = _dot_nt(wvt_ref[...], h)
        for hd in range(4):
            fvt32_o[hd * 64:(hd + 1) * 64, :] = vt[hd * 128:hd * 128 + 64]
            dvt32_o[hd * 64:(hd + 1) * 64, :] = vt[512 + hd * 128:512 + hd * 128 + 64]
        row = lax.broadcasted_iota(jnp.int32, vt.shape, 0) & 127
        vt = (vt + jnp.where(row == ONES_LANE, 1.0, 0.0)).astype(BF16)
        fvt_o[...] = vt[0:512]
        dvt_o[...] = vt[512:1024]
    else:
        fq_o[...] = proj("fq") * FOX_QSCALE
        fk_o[...] = proj("fk")
        fv_o[...] = proj("fv")
        dv_o[...] = proj("dv")

    cos = cos_ref[...]
    sin = sin_ref[...]
    dq = proj("dq")
    dq_o[...] = ((dq * cos + _swap_rope_halves(dq) * sin) * DIFF_QSCALE).astype(BF16 if prompt else F32)
    dk = proj("dk")
    dk = dk * cos + _swap_rope_halves(dk) * sin
    if prompt:
        dkb_o[...] = dk.astype(BF16)
        dkt_o[...] = dk.T
    else:
        dk_o[...] = dk

    cqkv_o[...] = proj("cqkv")
    cz = proj("cz")
    cz_o[...] = cz * jax.nn.sigmoid(cz)
    beta_o[...] = jax.nn.sigmoid(proj("cb_rep"))
    g_o[...] = -jnp.exp(alog_ref[...]) * jax.nn.softplus(proj("ca_rep") + dtb_ref[...])


_CACHE_OUTS = ("fkt", "fvt32", "dkt", "dvt32")


def _inproj(x, gain, w, layer, cos, sin, fbias, alog, dtb, *, prompt, seq_len, w_vt=None, cache_bufs=None):
    r, d = x.shape
    lay, nc = _layout(_PROMPT_COLS if prompt else _SAMPLE_COLS)
    assert w.shape[1] == nc
    tm = min(r, 512 if prompt else 256)
    tm = min(tm, seq_len) if prompt else tm
    fb = jnp.pad(fbias.astype(F32), (0, 124)).reshape(1, 128)
    alog_l = jnp.repeat(alog.astype(F32), 64).reshape(1, 256)
    dtb_l = jnp.repeat(dtb.astype(F32), 64).reshape(1, 256)

    def rows(width):
        return pl.BlockSpec((tm, width), lambda i: (i, 0))

    rope_tiles = cos.shape[0] // tm
    rope = pl.BlockSpec((tm, 256), lambda i: (i % rope_tiles, 0))
    in_specs = [rows(d), _const_spec((1, d)),
                pl.BlockSpec((None, nc, d), lambda i: (layer, 0, 0), pipeline_mode=pl.Buffered(1)),
                rope, rope, _const_spec((1, 128)), _const_spec((1, 256)), _const_spec((1, 256))]
    args = [x, gain.reshape(1, d), w, cos, sin, fb, alog_l, dtb_l]
    if prompt:
        tri = jnp.asarray(np.tril(np.ones((tm, tm), np.float32)), BF16)
        pq, pk = _aug_place_mats()
        in_specs += [_const_spec((tm, tm)), _const_spec((384, 512)), _const_spec((384, 512)),
                     pl.BlockSpec((None, 1024, d), lambda i: (layer, 0, 0), pipeline_mode=pl.Buffered(1))]
        args += [tri, pq, pk, w_vt]
        outs = (("qcat", 768, BF16), ("kcat", 768, BF16), ("vcat", -1024, BF16), ("small", 128, F32),
                ("cqkv", 768, F32), ("cz", 256, F32), ("beta", 256, F32), ("g", 256, F32))
        scratch = [pltpu.VMEM((1, 128), F32)]
    else:
        outs = (("fq", 256, F32), ("fk", 256, F32), ("fv", 256, F32), ("small", 128, F32), ("dq", 256, F32),
                ("dk", 256, F32), ("dv", 256, F32), ("cqkv", 768, F32), ("cz", 256, F32),
                ("beta", 256, F32), ("g", 256, F32))
        scratch = []
    tps = max(seq_len // tm, 1)
    out_shape = [jax.ShapeDtypeStruct((r, wd) if wd > 0 else (-wd, r), dt) for _, wd, dt in outs]
    out_specs = [rows(wd) if wd > 0 else pl.BlockSpec((-wd, tm), lambda i: (0, i)) for _, wd, _ in outs]
    names = [name for name, _, _ in outs]
    aliases = {}
    if prompt:
        for j, name in enumerate(_CACHE_OUTS):
            buf = cache_bufs[name]
            out_shape.append(jax.ShapeDtypeStruct(buf.shape, buf.dtype))
            out_specs.append(pl.BlockSpec((None, None, HEAD_W, tm), lambda i: (layer, i // tps, 0, i % tps)))
            names.append(name)
            aliases[len(args)] = len(outs) + j
            in_specs.append(pl.BlockSpec(memory_space=pl.ANY))
            args.append(buf)
    res = pl.pallas_call(
        functools.partial(_inproj_body, lay=lay, prompt=prompt, tiles_per_seq=tps, tm=tm, n_alias=len(aliases)),
        out_shape=tuple(out_shape),
        grid=(r // tm,),
        in_specs=in_specs,
        out_specs=tuple(out_specs),
        scratch_shapes=scratch,
        input_output_aliases=aliases,
        compiler_params=_cparams("arbitrary"),
        name="inproj_prompt" if prompt else "inproj_sample",
    )(*args)
    return dict(zip(names, res))


def _online_softmax_step(st, vt, m_ref, acc_ref):
    m_old = m_ref[...]
    m_new = jnp.maximum(m_old, jnp.max(st, axis=0, keepdims=True))
    alpha = jnp.exp2(m_old - m_new)
    p = jnp.exp2(st - m_new)
    acc_ref[...] = alpha * acc_ref[...] + _dot(vt, p.astype(BF16))
    m_ref[...] = m_new


def _causal_mask(st, query0):
    key = lax.broadcasted_iota(jnp.int32, st.shape, 0)
    qry = lax.broadcasted_iota(jnp.int32, st.shape, 1) + query0
    return jnp.where(key <= qry, st, -jnp.inf)


def _diff_lambda(lam_ref, lam_init):
    lp = lam_ref[...]
    s1 = jnp.sum(lp[0:1] * lp[1:2], axis=1, keepdims=True)
    s2 = jnp.sum(lp[2:3] * lp[3:4], axis=1, keepdims=True)
    return jnp.exp(s1) - jnp.exp(s2) + lam_init


N_ATTN_MAPS = FOX_HEADS + 2 * DIFF_HEADS
ATTN_LOOKAHEAD = 3
ATTN_QUERY_SPLIT = 2


def _attn_body(q_ref, k_hbm, vt_hbm, lam_ref, gain_ref, of_ref, od_ref,
               k_buf, vt_buf, sems, qm_scr, m_scr, acc_scr, *, lam_init, nq, tq):
    n = pl.program_id(0)
    qi = pl.program_id(1)
    n_diff = 2 * DIFF_HEADS
    fq_ref, dq_ref = q_ref.at[:, 0:512], q_ref.at[:, 512:768]

    def tile_copies(kj, slot):
        off = pl.multiple_of((n * nq + kj) * tq, tq)
        return (pltpu.make_async_copy(k_hbm.at[pl.ds(off, tq), :], k_buf.at[slot], sems.at[0, slot]),
                pltpu.make_async_copy(vt_hbm.at[:, pl.ds(off, tq)], vt_buf.at[slot], sems.at[1, slot]))

    for cp in tile_copies(0, 0):
        cp.start()

    m_scr[...] = jnp.full_like(m_scr, -jnp.inf)
    acc_scr[...] = jnp.zeros_like(acc_scr)
    q = dq_ref[...]
    lane_map = lax.broadcasted_iota(jnp.int32, q.shape, 1) // DIFF_QK_DIM
    for m in range(n_diff):
        qm_scr[m] = jnp.where(lane_map == m, q, jnp.zeros_like(q))

    def step(slot, masked):
        k_ref, vt_ref = k_buf.at[slot], vt_buf.at[slot]

        qw = tq // ATTN_QUERY_SPLIT
        units = [(j, c) for j in range(N_ATTN_MAPS) for c in range(ATTN_QUERY_SPLIT)]

        def scores(j, c):
            rows = slice(c * qw, (c + 1) * qw)
            if j < FOX_HEADS:
                sl = slice(j * 128, (j + 1) * 128)
                return _dot_nt(k_ref[:, sl], fq_ref[rows, sl])
            return _dot_nt(k_ref[:, 512:768], qm_scr[j - FOX_HEADS, rows, :])

        def values_t(j):
            if j < FOX_HEADS:
                return vt_ref[j * 128:(j + 1) * 128, :]
            h = (j - FOX_HEADS) // 2
            return vt_ref[512 + h * 128:512 + (h + 1) * 128, :]

        pending = [scores(*u) for u in units[:ATTN_LOOKAHEAD]]
        for i, (j, c) in enumerate(units):
            st = pending.pop(0)
            if i + ATTN_LOOKAHEAD < len(units):
                pending.append(scores(*units[i + ATTN_LOOKAHEAD]))
            if masked:
                st = _causal_mask(st, c * qw)
            lanes = slice(c * qw, (c + 1) * qw)
            _online_softmax_step(st, values_t(j), m_scr.at[j, :, lanes], acc_scr.at[j, :, lanes])

    def below_diagonal(kj, carry):
        slot = kj % 2
        for cp in tile_copies(kj, slot):
            cp.wait()
        for cp in tile_copies(kj + 1, 1 - slot):
            cp.start()
        step(slot, False)
        return carry

    lax.fori_loop(0, qi, below_diagonal, 0)
    slot = qi % 2
    for cp in tile_copies(qi, slot):
        cp.wait()
    step(slot, True)

    for h in range(FOX_HEADS):
        a = acc_scr[h]
        o = a[:FOX_DIM] / a[ONES_LANE:ONES_LANE + 1]
        of_ref[:, h * FOX_DIM:(h + 1) * FOX_DIM] = o.T.astype(of_ref.dtype)
    lam = _diff_lambda(lam_ref, lam_init)
    gain = jnp.concatenate([gain_ref[...]] * (tq // 128), axis=1)
    for h in range(DIFF_HEADS):
        a0 = acc_scr[FOX_HEADS + 2 * h]
        a1 = acc_scr[FOX_HEADS + 2 * h + 1]
        o = (a0[:DIFF_V_DIM] / a0[ONES_LANE:ONES_LANE + 1]
             - lam * (a1[:DIFF_V_DIM] / a1[ONES_LANE:ONES_LANE + 1]))
        ms = jnp.mean(o * o, axis=0, keepdims=True)
        o = o * lax.rsqrt(ms + NORM_EPS) * gain * (1.0 - lam_init)
        od_ref[:, h * DIFF_V_DIM:(h + 1) * DIFF_V_DIM] = o.T.astype(od_ref.dtype)


def _attn_prompt(qcat, kcat, vcat, lam_p, gain, lam_init, n_seq, seq_len):
    tq = min(seq_len, 512)
    nq = seq_len // tq

    def q_rows(width):
        return pl.BlockSpec((tq, width), lambda n, qi: (n * nq + qi, 0))

    gain_rows = jnp.broadcast_to(gain.astype(F32).reshape(DIFF_V_DIM, 1), (DIFF_V_DIM, 128))
    return pl.pallas_call(
        functools.partial(_attn_body, lam_init=lam_init, nq=nq, tq=tq),
        out_shape=(jax.ShapeDtypeStruct((n_seq * seq_len, HEAD_W), BF16),) * 2,
        grid=(n_seq, nq),
        in_specs=[q_rows(768), pl.BlockSpec(memory_space=pl.ANY), pl.BlockSpec(memory_space=pl.ANY),
                  _const_spec((4, DIFF_QK_DIM)), _const_spec((DIFF_V_DIM, 128))],
        out_specs=(q_rows(HEAD_W), q_rows(HEAD_W)),
        scratch_shapes=[pltpu.VMEM((2, tq, 768), BF16), pltpu.VMEM((2, 1024, tq), BF16),
                        pltpu.SemaphoreType.DMA((2, 2)),
                        pltpu.VMEM((2 * DIFF_HEADS, tq, HEAD_W), BF16), pltpu.VMEM((N_ATTN_MAPS, 1, tq), F32),
                        pltpu.VMEM((N_ATTN_MAPS, 128, tq), F32)],
        compiler_params=_cparams("arbitrary", "arbitrary"),
        name="attn_prompt",
    )(qcat, kcat, vcat, lam_p.astype(F32), gain_rows)


def _merge_ffn_body(x_ref, g_ref, wg_ref, wb_ref, wo_ref, b0_ref, b1_ref, b2_ref,
                    g2_ref, wi_ref, wo2_ref, gf_ref, o_ref, *, final_norm, dff, tf):
    x = x_ref[...]
    d = x.shape[1]
    h = _rms(x, g_ref[...]).astype(BF16)
    mix = None
    for b, b_ref in enumerate((b0_ref, b1_ref, b2_ref)):
        gate = jax.nn.sigmoid(_dot_nt(h, wg_ref[b * d:(b + 1) * d, :]))
        term = gate * _dot(b_ref[...].astype(BF16), wb_ref[b])
        mix = term if mix is None else mix + term
    x = x + _dot(mix.astype(BF16), wo_ref[...])
    y = _swiglu_residual(x, g2_ref[...], wi_ref, wo2_ref, dff, tf)
    if final_norm:
        y = _rms(y, gf_ref[...])
    o_ref[...] = y


def _merge_ffn(x, gain, w_gates, w_branch, w_out, layer, branches, gain2, wi, wo2, final_gain=None):
    r, d = x.shape
    dff = wo2.shape[1]
    tm = min(r, 512)
    tf = V7X_MXU_DIM if dff % V7X_MXU_DIM == 0 else dff
    final_norm = final_gain is not None
    gf = final_gain if final_norm else gain2

    def rows(width):
        return pl.BlockSpec((tm, width), lambda i: (i, 0))

    def resident(*shape):
        nd = len(shape)
        return pl.BlockSpec((None,) + shape, lambda i: (layer,) + (0,) * nd, pipeline_mode=pl.Buffered(1))

    return pl.pallas_call(
        functools.partial(_merge_ffn_body, final_norm=final_norm, dff=dff, tf=tf),
        out_shape=jax.ShapeDtypeStruct((r, d), F32),
        grid=(r // tm,),
        in_specs=[rows(d), _const_spec((1, d)),
                  resident(N_BRANCH * d, d), resident(N_BRANCH, HEAD_W, d), resident(d, d),
                  rows(HEAD_W), rows(HEAD_W), rows(HEAD_W),
                  _const_spec((1, d)), resident(d, 2 * dff), resident(dff, d), _const_spec((1, d))],
        out_specs=rows(d),
        compiler_params=_cparams("arbitrary"),
        name="merge_ffn",
    )(x, gain.reshape(1, d), w_gates, w_branch, w_out, *branches,
      gain2.reshape(1, d), wi, wo2, gf.reshape(1, d))


def _delta_consts():
    c = DELTA_CHUNK
    i = np.arange(c)
    lane = np.arange(HEAD_W)
    lh, lj = lane // c, lane % c
    tril = i[:, None] >= i[None, :]
    suffix = i[None, :] > i[:, None]
    lvl = []
    for k in range(1, 7):
        b = 1 << k
        lvl.append((i[:, None] // b == lj[None, :] // b) & (i[:, None] // (b // 2) != lj[None, :] // (b // 2)))
    f = lambda a, dt: jnp.asarray(np.asarray(a, np.float32), dt)
    return dict(
        lhsg=f(np.concatenate([tril, suffix], axis=0), BF16),
        ones=f(np.ones((c, c)), BF16),
        triu4=f(i[:, None] <= lj[None, :], F32),
        tril4=f(i[:, None] >= lj[None, :], F32),
        stril4=f(i[:, None] > lj[None, :], F32),
        eye4=f(i[:, None] == lj[None, :], F32),
        lvl=f(np.stack(lvl), F32),
        maskbd=f(lh[:, None] == lh[None, :], BF16),
    )


def _bd(x4, maskbd):
    return jnp.concatenate([x4, x4, x4, x4], axis=0) * maskbd


def _mm_heads(xs, ys, maskbd):
    return [_dot(x.astype(BF16), _bd(y.astype(BF16), maskbd)) for x, y in zip(xs, ys)]


def _unit_lower_inverse(a4s, eye4, lvl_ref, maskbd):
    ts = [eye4 - a4 * lvl_ref[0] for a4 in a4s]
    for k in range(1, 6):
        ms = [a4 * lvl_ref[k] for a4 in a4s]
        xs = _mm_heads(ms, ts, maskbd)
        ts = [t - d for t, d in zip(ts, _mm_heads(ts, xs, maskbd))]
    return ts


def _head_sums(x, maskbd):
    hi = x.astype(BF16)
    lo = (x - hi.astype(F32)).astype(BF16)
    return _dot(hi, maskbd) + _dot(lo, maskbd)


def _delta_body(cqkv_ref, beta_ref, g_ref, cz_ref, cw_ref, gain_ref,
                lhsg_ref, ones_ref, triu4_ref, tril4_ref, stril4_ref, eye4_ref, lvl_ref, maskbd_ref,
                o_ref, s_ref, xbuf, ub_scr, w_scr, qk_scr, qe_scr, kd_scr, eg_scr, o_scr, sbd_scr, *, tc):
    j = pl.program_id(1)
    c = DELTA_CHUNK
    maskbd = maskbd_ref[...]
    maskf = maskbd.astype(F32)

    @pl.when(j == 0)
    def _():
        xbuf[0:8, :] = jnp.zeros((8, xbuf.shape[1]), F32)
        sbd_scr[...] = jnp.zeros_like(sbd_scr)

    xbuf[8:8 + tc, :] = cqkv_ref[...]
    cw = cw_ref[...]
    y = cw[3:4] * xbuf[8:8 + tc, :]
    for i in range(1, CONV_WIDTH):
        y = y + cw[3 - i:4 - i] * xbuf[8 - i:8 - i + tc, :]
    xbuf[0:8, :] = xbuf[tc:tc + 8, :]
    y = y * jax.nn.sigmoid(y)
    q = y[:, 0:HEAD_W]
    k = y[:, HEAD_W:2 * HEAD_W]
    qn = q * lax.rsqrt(_head_sums(q * q, maskbd) + 1e-6) * (DELTA_K_DIM ** -0.5)
    kn = k * lax.rsqrt(_head_sums(k * k, maskbd) + 1e-6)
    v = y[:, 2 * HEAD_W:3 * HEAD_W]
    beta = beta_ref[...]
    g = g_ref[...]

    cs = [slice(ci * c, (ci + 1) * c) for ci in range(tc // c)]
    qcs, kcs, vcs = [qn[s] for s in cs], [kn[s] for s in cs], [v[s] for s in cs]
    bcs, gcs = [beta[s] for s in cs], [g[s] for s in cs]
    gsums = [_dot3_l(lhsg_ref[...], gc) for gc in gcs]
    grows = [_dot3_l(ones_ref[...], gc * triu4_ref[...]) for gc in gcs]
    gcols, gsufs = [gs[0:c] for gs in gsums], [gs[c:2 * c] for gs in gsums]
    decays = [jnp.where(tril4_ref[...] > 0, jnp.exp(gcol - grow), 0.0) for gcol, grow in zip(gcols, grows)]
    exp_gs = [jnp.exp(gcol) for gcol in gcols]
    kbs = [kc * bc for kc, bc in zip(kcs, bcs)]
    kbds = [_bd(kc.astype(BF16), maskbd) for kc in kcs]
    kks = [_dot_nt(jnp.concatenate([kb, qc], axis=0).astype(BF16), kbd) for kb, qc, kbd in zip(kbs, qcs, kbds)]
    a4s = [kk[0:c] * decay * stril4_ref[...] for kk, decay in zip(kks, decays)]
    qk4s = [kk[c:2 * c] * decay for kk, decay in zip(kks, decays)]
    t16s = [t.astype(BF16) for t in _unit_lower_inverse(a4s, eye4_ref[...], lvl_ref, maskbd)]
    u_bases = [_dot(t16, _bd((vc * bc).astype(BF16), maskbd)) for t16, vc, bc in zip(t16s, vcs, bcs)]
    ws = [_dot(t16, _bd((kb * eg).astype(BF16), maskbd)) for t16, kb, eg in zip(t16s, kbs, exp_gs)]
    ub_scr[...] = jnp.concatenate(u_bases, axis=0)
    w_scr[...] = jnp.concatenate([w.astype(BF16) for w in ws], axis=0)
    qk_scr[...] = jnp.concatenate([x.astype(BF16) for x in qk4s], axis=0)
    qe_scr[...] = jnp.concatenate([(qc * eg).astype(BF16) for qc, eg in zip(qcs, exp_gs)], axis=0)
    kd_scr[...] = jnp.concatenate([(kc * jnp.exp(gs)).astype(BF16) for kc, gs in zip(kcs, gsufs)], axis=0)
    eg_scr[...] = jnp.concatenate([jnp.broadcast_to(eg[c - 1:c, :], (8, HEAD_W)) for eg in exp_gs], axis=0)

    def scan(ci, carry):
        sl = pl.ds(pl.multiple_of(ci * c, c), c)
        s_old = sbd_scr[...]
        s16 = s_old.astype(BF16)
        u16 = (ub_scr[sl, :] - _dot(w_scr[sl, :], s16)).astype(BF16)
        o_scr[sl, :] = _dot(qe_scr[sl, :], s16) + _dot(qk_scr[sl, :], _bd(u16, maskbd))
        exp_g_last = eg_scr[pl.ds(pl.multiple_of(ci * 8, 8), 8), :][0:1]
        sbd_scr[...] = s_old * exp_g_last + maskf * _dot_tn(kd_scr[sl, :], u16)
        return carry

    lax.fori_loop(0, tc // c, scan, 0)

    o = o_scr[...]
    ms = _head_sums(o * o, maskbd) * (1.0 / DELTA_V_DIM)
    o_ref[...] = (o * lax.rsqrt(ms + NORM_EPS) * gain_ref[...] * cz_ref[...]).astype(o_ref.dtype)

    @pl.when(j == pl.num_programs(1) - 1)
    def _():
        s_ref[...] = sbd_scr[...]


def _delta_prompt(cqkv, beta, g, cz, conv_w, gain, n_seq, seq_len):
    tc = min(seq_len, 512)
    nt = seq_len // tc
    cch = cqkv.shape[1]
    consts = _delta_consts()
    names = ("lhsg", "ones", "triu4", "tril4", "stril4", "eye4", "lvl", "maskbd")

    def rows(width):
        return pl.BlockSpec((tc, width), lambda n, j: (n * nt + j, 0))

    return pl.pallas_call(
        functools.partial(_delta_body, tc=tc),
        out_shape=(jax.ShapeDtypeStruct((n_seq * seq_len, HEAD_W), BF16),
                   jax.ShapeDtypeStruct((n_seq, HEAD_W, HEAD_W), F32)),
        grid=(n_seq, nt),
        in_specs=[rows(cch), rows(HEAD_W), rows(HEAD_W), rows(HEAD_W),
                  _const_spec((CONV_WIDTH, cch)), _const_spec((1, HEAD_W))]
                 + [_const_spec(consts[nm].shape) for nm in names],
        out_specs=(rows(HEAD_W), pl.BlockSpec((None, HEAD_W, HEAD_W), lambda n, j: (n, 0, 0))),
        scratch_shapes=[pltpu.VMEM((tc + 8, cch), F32), pltpu.VMEM((tc, HEAD_W), F32),
                        pltpu.VMEM((tc, HEAD_W), BF16), pltpu.VMEM((tc, HEAD_W), BF16),
                        pltpu.VMEM((tc, HEAD_W), BF16), pltpu.VMEM((tc, HEAD_W), BF16),
                        pltpu.VMEM((8 * (tc // DELTA_CHUNK), HEAD_W), F32),
                        pltpu.VMEM((tc, HEAD_W), F32), pltpu.VMEM((HEAD_W, HEAD_W), F32)],
        compiler_params=_cparams("arbitrary", "arbitrary"),
        name="delta_prompt",
    )(cqkv, beta, g, cz, conv_w.astype(F32), jnp.tile(gain.astype(F32), DELTA_HEADS).reshape(1, HEAD_W),
      *[consts[nm] for nm in names])


SROWS = 8


def _sample_attn_consts(n_pages):
    r = np.arange(PAGE_SIZE)
    m2 = r[:, None] > r[None, :]
    m2ones = np.concatenate([m2, np.ones((PAGE_SIZE, PAGE_SIZE), bool)], axis=1)
    idx = np.arange(4 * n_pages)
    u2 = (idx[None, :] // 4 > idx[:, None] // 4) & (idx[None, :] % 4 == idx[:, None] % 4)
    t = np.arange(SROWS)
    tri8 = t[None, :] <= t[:, None]
    sup = t[:, None] > r[None, :]
    lh = np.arange(HEAD_W) // 64
    f = lambda a, dt: jnp.asarray(np.asarray(a, np.float32), dt)
    return dict(m2ones=f(m2ones, BF16), u2=f(u2, BF16), tri8=f(tri8, BF16), sup=f(sup, F32),
                maskbd=f(lh[:, None] == lh[None, :], BF16))


def _pad_rows(x, rows):
    return jnp.concatenate([x, jnp.zeros((rows - x.shape[0], x.shape[1]), x.dtype)], axis=0)


def _block_rows(x, groups, lanes_per_group):
    xr = jnp.concatenate([x] * groups, axis=0)
    rg = lax.broadcasted_iota(jnp.int32, xr.shape, 0) // SROWS
    lg = lax.broadcasted_iota(jnp.int32, xr.shape, 1) // lanes_per_group
    return jnp.where(rg == lg, xr, 0.0).astype(BF16)


def _paged_softmax_pv(qbd, kt_pages, vt_pages, k_new, v_new, bias_pages, bias_new):
    n_pages = len(kt_pages)
    assert n_pages % 2 == 0
    tiles = []
    for p in range(0, n_pages, 2):
        kt2 = jnp.concatenate([kt_pages[p][...], kt_pages[p + 1][...]], axis=1).astype(BF16)
        s = _dot(qbd, kt2)
        if bias_pages is not None:
            s = s + jnp.concatenate([bias_pages[p], bias_pages[p + 1]], axis=1)
        tiles.append(s)
    s = _dot_nt(qbd, _pad_rows(k_new, PAGE_SIZE).astype(BF16))
    if bias_new is not None:
        s = s + bias_new
    row = lax.broadcasted_iota(jnp.int32, s.shape, 0) & (SROWS - 1)
    col = lax.broadcasted_iota(jnp.int32, s.shape, 1)
    tiles.append(jnp.where(col <= row, s, -jnp.inf))
    m = tiles[0].max(axis=1, keepdims=True)
    for s in tiles[1:]:
        m = jnp.maximum(m, s.max(axis=1, keepdims=True))
    acc = None
    den = None
    for i, s in enumerate(tiles):
        p = jnp.exp2(s - m)
        ps = p.sum(axis=1, keepdims=True)
        if 2 * i < n_pages:
            vt2 = jnp.concatenate([vt_pages[2 * i][...], vt_pages[2 * i + 1][...]], axis=1).astype(BF16)
            pv = _dot_nt(p.astype(BF16), vt2)
        else:
            pv = _dot(p.astype(BF16), _pad_rows(v_new, PAGE_SIZE).astype(BF16))
        acc = pv if acc is None else acc + pv
        den = ps if den is None else den + ps
    return acc / den


def _own_head_rows(x, stride):
    lh = lax.broadcasted_iota(jnp.int32, (SROWS, HEAD_W), 1) // 64
    out = None
    for h in range(4):
        part = jnp.where(lh == h, x[h * stride:h * stride + SROWS], 0.0)
        out = part if out is None else out + part
    return out


def _sample_attn_body(pt_ref, fk_hbm, fv_hbm, lf_hbm, dk_hbm, dv_hbm,
                      fq_ref, fkn_ref, fvn_ref, ln_ref, dq_ref, dkn_ref, dvn_ref,
                      m2_ref, u2_ref, tri8_ref, sup_ref, lam_ref, gain_ref, maskbd_ref,
                      of_ref, od_ref, fk_buf, fv_buf, lf_buf, dk_buf, dv_buf, sems, xm_scr, *, n_pages, layer, lam_init):
    b = pl.program_id(0)
    slot = b % 2
    streams = ((fk_hbm, fk_buf), (fv_hbm, fv_buf), (lf_hbm, lf_buf), (dk_hbm, dk_buf), (dv_hbm, dv_buf))

    def page_copies(sample, sl):
        cps = []
        for p in range(n_pages):
            page = pt_ref[sample, p]
            for hbm, buf in streams:
                cps.append(pltpu.make_async_copy(hbm.at[layer, page], buf.at[sl, p], sems.at[sl]))
        return cps

    @pl.when(b == 0)
    def _():
        for cp in page_copies(0, 0):
            cp.start()

    @pl.when(b + 1 < pl.num_programs(0))
    def _():
        for cp in page_copies(b + 1, 1 - slot):
            cp.start()

    for cp in page_copies(b, slot):
        cp.wait()

    fk_pages = [fk_buf.at[slot, p] for p in range(n_pages)]
    fv_pages = [fv_buf.at[slot, p] for p in range(n_pages)]
    dk_pages = [dk_buf.at[slot, p] for p in range(n_pages)]
    dv_pages = [dv_buf.at[slot, p] for p in range(n_pages)]

    for p in range(n_pages):
        xm_scr[4 * p:4 * p + 4, :] = lf_buf[slot, p]
    wt = _dot3_r(xm_scr[...], m2_ref[...])
    within, totals = wt[:, :PAGE_SIZE], wt[:, PAGE_SIZE:]
    d_past = (within + _dot3_l(u2_ref[...], totals)) * LOG2E
    lnew = ln_ref[...]
    tri8 = tri8_ref[...]
    cn = _dot3_l(tri8, lnew) * LOG2E
    cn_h = [jnp.broadcast_to(cn[:, h:h + 1], (SROWS, PAGE_SIZE)) for h in range(FOX_HEADS)]
    bias_pages = [
        jnp.concatenate([d_past[4 * p + h:4 * p + h + 1, :] + cn_h[h] for h in range(FOX_HEADS)], axis=0)
        for p in range(n_pages)]
    sup = sup_ref[...]
    bias_new = jnp.concatenate(
        [_dot3_l(tri8, jnp.broadcast_to(lnew[:, h:h + 1], (SROWS, PAGE_SIZE)) * sup) * LOG2E
         for h in range(FOX_HEADS)], axis=0)
    qbd = _block_rows(fq_ref[...], FOX_HEADS, FOX_DIM)
    o = _paged_softmax_pv(qbd, fk_pages, fv_pages, fkn_ref[...], fvn_ref[...], bias_pages, bias_new)
    of_ref[...] = _own_head_rows(o, SROWS)

    qbd = _block_rows(dq_ref[...], 2 * DIFF_HEADS, DIFF_QK_DIM)
    o = _paged_softmax_pv(qbd, dk_pages, dv_pages, dkn_ref[...], dvn_ref[...], None, None)
    lam = _diff_lambda(lam_ref, lam_init)
    o = _own_head_rows(o, 2 * SROWS) - lam * _own_head_rows(o[SROWS:], 2 * SROWS)
    ms = _head_sums(o * o, maskbd_ref[...]) * (1.0 / DIFF_V_DIM)
    od_ref[...] = o * lax.rsqrt(ms + NORM_EPS) * gain_ref[...] * (1.0 - lam_init)


def _sample_attn(page_table, caches, layer, proj, lam_p, gain, lam_init):
    n_samples, n_pages = page_table.shape
    consts = _sample_attn_consts(n_pages)

    def tile(width):
        return pl.BlockSpec((SROWS, width), lambda b, pt: (b, 0))

    def const(shape):
        nd = len(shape)
        return pl.BlockSpec(shape, lambda b, pt: (0,) * nd, pipeline_mode=pl.Buffered(1))

    in_specs = [pl.BlockSpec(memory_space=pl.ANY) for _ in caches]
    args = list(caches)
    new = (proj["fq"], proj["fk"], proj["fv"], proj["small"], proj["dq"], proj["dk"], proj["dv"])
    in_specs += [tile(a.shape[1]) for a in new]
    args += list(new)
    cvals = (consts["m2ones"], consts["u2"], consts["tri8"], consts["sup"], lam_p.astype(F32),
             jnp.tile(gain.astype(F32), DIFF_HEADS).reshape(1, HEAD_W), consts["maskbd"])
    in_specs += [const(c.shape) for c in cvals]
    args += list(cvals)
    gs = pltpu.PrefetchScalarGridSpec(
        num_scalar_prefetch=1, grid=(n_samples,), in_specs=in_specs,
        out_specs=(tile(HEAD_W), tile(HEAD_W)),
        scratch_shapes=[pltpu.VMEM((2, n_pages) + c.shape[2:], c.dtype) for c in caches]
                       + [pltpu.SemaphoreType.DMA((2,)), pltpu.VMEM((4 * n_pages, PAGE_SIZE), F32)])
    return pl.pallas_call(
        functools.partial(_sample_attn_body, n_pages=n_pages, layer=layer, lam_init=lam_init),
        out_shape=(jax.ShapeDtypeStruct((n_samples * SROWS, HEAD_W), F32),) * 2,
        grid_spec=gs,
        compiler_params=_cparams("arbitrary"),
        name="sample_attn",
    )(page_table, *args)


def _delta_sample_body(cqkv_ref, buf_ref, beta_ref, g_ref, cz_ref, cw_ref, gain_ref, s0_ref,
                       tile_ref, tile_t_ref, maskbd_ref, o_ref, s_ref, xp_scr, pr_scr, u_scr, kd_scr, o_scr,
                       *, n_tok, group):
    maskbd = maskbd_ref[...]
    maskf = maskbd.astype(F32)
    cch = xp_scr.shape[1]
    cw = cw_ref[...]
    ys = []
    for s in range(group):
        base = 16 * s
        xp_scr[base:base + 8, :] = jnp.zeros((8, cch), F32)
        xp_scr[base + 8 - (CONV_WIDTH - 1):base + 8, :] = buf_ref[s]
        xp_scr[base + 8:base + 16, :] = cqkv_ref[SROWS * s:SROWS * (s + 1), :]
        y = cw[3:4] * xp_scr[base + 8:base + 16, :]
        for i in range(1, CONV_WIDTH):
            y = y + cw[3 - i:4 - i] * xp_scr[base + 8 - i:base + 16 - i, :]
        ys.append(y)
    y = jnp.concatenate(ys, axis=0)
    y = y * jax.nn.sigmoid(y)
    q = y[:, 0:HEAD_W]
    k = y[:, HEAD_W:2 * HEAD_W]
    v = y[:, 2 * HEAD_W:3 * HEAD_W]
    q = q * lax.rsqrt(_head_sums(q * q, maskbd) + 1e-6) * (DELTA_K_DIM ** -0.5)
    k = k * lax.rsqrt(_head_sums(k * k, maskbd) + 1e-6)
    beta = beta_ref[...]
    a = jnp.exp(g_ref[...])

    pairs_k = [(t, j) for t in range(n_tok) for j in range(t)]
    pairs_q = [(t, j) for t in range(n_tok) for j in range(t + 1)]
    assert len(pairs_k) + len(pairs_q) <= 16
    pr_scr[...] = jnp.zeros_like(pr_scr)
    for s in range(group):
        r0, p0 = SROWS * s, 16 * s
        for r, (t, j) in enumerate(pairs_k):
            pr_scr[p0 + r:p0 + r + 1, :] = k[r0 + t:r0 + t + 1] * k[r0 + j:r0 + j + 1]
        for r, (t, j) in enumerate(pairs_q):
            rr = p0 + len(pairs_k) + r
            pr_scr[rr:rr + 1, :] = q[r0 + t:r0 + t + 1] * k[r0 + j:r0 + j + 1]
    dots = _head_sums(pr_scr[...], maskbd)

    rows = [slice(SROWS * s, SROWS * (s + 1)) for s in range(group)]
    sbd0s = [maskf * _dot3_r(s0_ref[s], tile_ref[...]) for s in range(group)]
    s16s = [sb.astype(BF16) for sb in sbd0s]
    rks = [_dot(k[r].astype(BF16), s16) for r, s16 in zip(rows, s16s)]
    rqs = [_dot(q[r].astype(BF16), s16) for r, s16 in zip(rows, s16s)]

    u_scr[...] = jnp.zeros_like(u_scr)
    kd_scr[...] = jnp.zeros_like(kd_scr)
    o_scr[...] = jnp.zeros_like(o_scr)
    a_alls = []
    for s in range(group):
        r0, p0 = SROWS * s, 16 * s
        kk = {tj: dots[p0 + r:p0 + r + 1] for r, tj in enumerate(pairs_k)}
        qk = {tj: dots[p0 + len(pairs_k) + r:p0 + len(pairs_k) + r + 1] for r, tj in enumerate(pairs_q)}
        rk_rows = [rks[s][t:t + 1] for t in range(n_tok)]
        rq_rows = [rqs[s][t:t + 1] for t in range(n_tok)]
        for t in range(n_tok):
            a_t = a[r0 + t:r0 + t + 1]
            u_t = beta[r0 + t:r0 + t + 1] * (v[r0 + t:r0 + t + 1] - a_t * rk_rows[t])
            for t2 in range(t + 1, n_tok):
                rk_rows[t2] = a_t * rk_rows[t2] + kk[(t2, t)] * u_t
            for t2 in range(t, n_tok):
                rq_rows[t2] = a_t * rq_rows[t2] + qk[(t2, t)] * u_t
            o_scr[r0 + t:r0 + t + 1, :] = rq_rows[t]
            u_scr[r0 + t:r0 + t + 1, :] = u_t
            tail = None
            for t2 in range(t + 1, n_tok):
                tail = a[r0 + t2:r0 + t2 + 1] if tail is None else tail * a[r0 + t2:r0 + t2 + 1]
            kd_scr[r0 + t:r0 + t + 1, :] = k[r0 + t:r0 + t + 1] if tail is None else k[r0 + t:r0 + t + 1] * tail
        a_all = a[r0:r0 + 1]
        for t in range(1, n_tok):
            a_all = a_all * a[r0 + t:r0 + t + 1]
        a_alls.append(a_all)
    upd = [_dot_tn(kd_scr[r, :].astype(BF16), u_scr[r, :].astype(BF16)) for r in rows]
    s_news = [sb * a_all + maskf * u for sb, a_all, u in zip(sbd0s, a_alls, upd)]
    for s in range(group):
        s_ref[s] = _dot3_r(s_news[s], tile_t_ref[...])

    o = o_scr[...]
    ms = _head_sums(o * o, maskbd) * (1.0 / DELTA_V_DIM)
    o_ref[...] = o * lax.rsqrt(ms + NORM_EPS) * gain_ref[...] * cz_ref[...]


def _delta_sample(proj, conv_state, state, layer, conv_w, gain, n_tok):
    n_samples = state.shape[1]
    cch = conv_state.shape[3]
    v_ = np.arange(DELTA_V_DIM)
    lane = np.arange(HEAD_W)
    tile_m = jnp.asarray((v_[:, None] == lane[None, :] % DELTA_V_DIM).astype(np.float32), BF16)
    tile_t = jnp.asarray((lane[:, None] % DELTA_V_DIM == v_[None, :]).astype(np.float32), BF16)
    lh = lane // 64
    maskbd = jnp.asarray((lh[:, None] == lh[None, :]).astype(np.float32), BF16)

    group = math.gcd(n_samples, 8)
    rows = group * SROWS

    def tile(width):
        return pl.BlockSpec((rows, width), lambda b: (b, 0))

    return pl.pallas_call(
        functools.partial(_delta_sample_body, n_tok=n_tok, group=group),
        out_shape=(jax.ShapeDtypeStruct((n_samples * SROWS, HEAD_W), F32),
                   jax.ShapeDtypeStruct((n_samples, HEAD_W, DELTA_V_DIM), F32)),
        grid=(n_samples // group,),
        in_specs=[tile(cch),
                  pl.BlockSpec((None, group, CONV_WIDTH - 1, cch), lambda b: (layer, b, 0, 0)),
                  tile(HEAD_W), tile(HEAD_W), tile(HEAD_W),
                  _const_spec((CONV_WIDTH, cch)), _const_spec((1, HEAD_W)),
                  pl.BlockSpec((None, group, HEAD_W, DELTA_V_DIM), lambda b: (layer, b, 0, 0)),
                  _const_spec(tile_m.shape), _const_spec(tile_t.shape), _const_spec(maskbd.shape)],
        out_specs=(tile(HEAD_W), pl.BlockSpec((group, HEAD_W, DELTA_V_DIM), lambda b: (b, 0, 0))),
        scratch_shapes=[pltpu.VMEM((2 * rows, cch), F32), pltpu.VMEM((2 * rows, HEAD_W), F32),
                        pltpu.VMEM((rows, HEAD_W), F32), pltpu.VMEM((rows, HEAD_W), F32),
                        pltpu.VMEM((rows, HEAD_W), F32)],
        compiler_params=_cparams("arbitrary"),
        name="delta_sample",
    )(proj["cqkv"], conv_state, proj["beta"], proj["g"], proj["cz"], conv_w.astype(F32),
      jnp.tile(gain.astype(F32), DELTA_HEADS).reshape(1, HEAD_W), state, tile_m, tile_t, maskbd)


def kernel(x_prompt, x_sample, cache_fox_k, cache_fox_v, cache_fox_logf, cache_diff_k, cache_diff_v, state_delta, state_conv, page_table, norm_ffn1, ffn1_wi, ffn1_wo, norm_mix, w_in, fox_f_bias, diff_lambda, diff_norm, delta_conv_w, delta_A_log, delta_dt_bias, delta_norm, w_branch, w_out, norm_ffn2, ffn2_wi, ffn2_wo, norm_final):
    n_seq, seq_len, d = x_prompt.shape
    n_smp, dec_seq, _ = x_sample.shape
    depth = w_in.shape[0]
    n_pool = cache_fox_k.shape[1]
    n_pages = page_table.shape[1]
    past_len = n_pages * PAGE_SIZE
    assert CONV_WIDTH - 1 <= dec_seq <= SROWS and seq_len % DELTA_CHUNK == 0

    wi1, wo1 = ffn1_wi.astype(BF16), ffn1_wo.astype(BF16)
    wi2, wo2 = ffn2_wi.astype(BF16), ffn2_wo.astype(BF16)
    w_in_t = jnp.swapaxes(w_in, 1, 2)
    w_p = _relayout_w_in(w_in_t, _PROMPT_COLS)
    w_s = _relayout_w_in(w_in_t, _SAMPLE_COLS)
    w_vt = _relayout_w_values_t(w_in_t)
    w_gates = w_in_t[:, w_in_t.shape[1] - N_BRANCH * d:, :].astype(BF16)
    wb, wo = w_branch.astype(BF16), w_out.astype(BF16)

    def pages_t(cache):
        c = cache.reshape(depth, n_pool, PAGE_SIZE, HEAD_W)
        return jnp.swapaxes(c, 2, 3)

    caches = (pages_t(cache_fox_k), pages_t(cache_fox_v), jnp.swapaxes(cache_fox_logf, 2, 3),
              pages_t(cache_diff_k), pages_t(cache_diff_v))
    s_delta = state_delta.reshape(depth, n_smp, HEAD_W, DELTA_V_DIM)

    xp = x_prompt.reshape(n_seq * seq_len, d)
    xs = jnp.pad(x_sample, ((0, 0), (0, SROWS - dec_seq), (0, 0))).reshape(n_smp * SROWS, d)
    cos_p, sin_p = _rope_tables(seq_len, 0, seq_len)
    cos_s, sin_s = _rope_tables(n_smp * SROWS, past_len, SROWS)

    cache_bufs = {name: jnp.zeros((depth, n_seq, HEAD_W, seq_len), F32) for name in _CACHE_OUTS}

    new_p, new_s = [], []
    for l in range(depth):
        lam_init = 0.8 - 0.6 * math.exp(-0.3 * l)
        xp = _ffn(xp, norm_ffn1[l], wi1, wo1, l)
        xs = _ffn(xs, norm_ffn1[l], wi1, wo1, l)
        pp = _inproj(xp, norm_mix[l], w_p, l, cos_p, sin_p, fox_f_bias[l], delta_A_log[l], delta_dt_bias[l],
                     prompt=True, seq_len=seq_len, w_vt=w_vt, cache_bufs=cache_bufs)
        cache_bufs = {name: pp[name] for name in _CACHE_OUTS}
        ps = _inproj(xs, norm_mix[l], w_s, l, cos_s, sin_s, fox_f_bias[l], delta_A_log[l], delta_dt_bias[l],
                     prompt=False, seq_len=SROWS)

        of_p, od_p = _attn_prompt(pp["qcat"], pp["kcat"], pp["vcat"],
                                  diff_lambda[l], diff_norm[l], lam_init, n_seq, seq_len)
        ol_p, sbd = _delta_prompt(pp["cqkv"], pp["beta"], pp["g"], pp["cz"], delta_conv_w[l], delta_norm[l],
                                  n_seq, seq_len)
        of_s, od_s = _sample_attn(page_table, caches, l, ps, diff_lambda[l], diff_norm[l], lam_init)
        ol_s, s_new = _delta_sample(ps, state_conv, s_delta, l, delta_conv_w[l], delta_norm[l], dec_seq)

        final = norm_final if l == depth - 1 else None
        xp = _merge_ffn(xp, norm_mix[l], w_gates, wb, wo, l, (of_p, od_p, ol_p), norm_ffn2[l], wi2, wo2, final)
        xs = _merge_ffn(xs, norm_mix[l], w_gates, wb, wo, l, (of_s, od_s, ol_s), norm_ffn2[l], wi2, wo2, final)

        def p3(a):
            return a.reshape(n_seq, seq_len, a.shape[1])

        def s3(a):
            return a.reshape(n_smp, SROWS, a.shape[1])[:, :dec_seq]

        new_p.append(dict(
            fox_logf=p3(pp["small"])[:, :, :FOX_HEADS],
            delta=jnp.stack([sbd[:, h * 64:(h + 1) * 64, h * 64:(h + 1) * 64] for h in range(DELTA_HEADS)], axis=1),
            conv=p3(pp["cqkv"])[:, seq_len - (CONV_WIDTH - 1):]))
        new_s.append(dict(
            fox_k=s3(ps["fk"]).reshape(n_smp, dec_seq, FOX_HEADS, FOX_DIM),
            fox_v=s3(ps["fv"]).reshape(n_smp, dec_seq, FOX_HEADS, FOX_DIM),
            fox_logf=s3(ps["small"])[:, :, :FOX_HEADS],
            diff_k=s3(ps["dk"]).reshape(n_smp, dec_seq, DIFF_HEADS, 2, DIFF_QK_DIM),
            diff_v=s3(ps["dv"]).reshape(n_smp, dec_seq, DIFF_HEADS, DIFF_V_DIM),
            delta=s_new.reshape(n_smp, DELTA_HEADS, DELTA_K_DIM, DELTA_V_DIM),
            conv=s3(ps["cqkv"])[:, dec_seq - (CONV_WIDTH - 1):]))

    def stk(lst, name):
        return jnp.stack([dd[name] for dd in lst])

    def token_major(buf, *feat):
        nf = len(feat)
        b = buf.reshape((depth, n_seq) + feat + (seq_len,))
        return jnp.transpose(b, (0, 1, 2 + nf) + tuple(range(2, 2 + nf)))

    y_prompt = xp.reshape(n_seq, seq_len, d)
    y_sample = xs.reshape(n_smp, SROWS, d)[:, :dec_seq]
    return (y_prompt, y_sample,
            token_major(cache_bufs["fkt"], FOX_HEADS, FOX_DIM), stk(new_s, "fox_k"),
            token_major(cache_bufs["fvt32"], FOX_HEADS, FOX_DIM), stk(new_s, "fox_v"),
            stk(new_p, "fox_logf"), stk(new_s, "fox_logf"),
            token_major(cache_bufs["dkt"], DIFF_HEADS, 2, DIFF_QK_DIM), stk(new_s, "diff_k"),
            token_major(cache_bufs["dvt32"], DIFF_HEADS, DIFF_V_DIM), stk(new_s, "diff_v"),
            stk(new_p, "delta"), stk(new_s, "delta"),
            stk(new_p, "conv"), stk(new_s, "conv"))
```

```python
import functools
import math

import numpy as np
import jax
import jax.numpy as jnp
from jax import lax
from jax.experimental import pallas as pl
from jax.experimental.pallas import tpu as pltpu

F32 = jnp.float32
BF16 = jnp.bfloat16

FOX_HEADS = 4
FOX_DIM = 64
DIFF_HEADS = 4
DIFF_QK_DIM = 32
DIFF_V_DIM = 64
DELTA_HEADS = 4
DELTA_K_DIM = 64
DELTA_V_DIM = 64
CONV_WIDTH = 4
DELTA_CHUNK = 64
N_BRANCH = 3
HEAD_W = 256
PAGE_SIZE = 128
ROPE_THETA = 10000.0
NORM_EPS = 1e-6
LOG2E = 1.4426950408889634
FOX_QSCALE = FOX_DIM ** -0.5 * LOG2E
DIFF_QSCALE = DIFF_QK_DIM ** -0.5 * LOG2E

V7X_VMEM_LIMIT_BYTES = 56 * 1024 * 1024
AUG_C = 64
AUG_K = 67
ONES_LANE = 64


def _cparams(*sem):
    return pltpu.CompilerParams(dimension_semantics=sem, vmem_limit_bytes=V7X_VMEM_LIMIT_BYTES)


def _dot(a, b):
    return jnp.dot(a, b, preferred_element_type=F32)


def _dot_nt(a, b):
    return lax.dot_general(a, b, (((1,), (1,)), ((), ())), preferred_element_type=F32)


def _dot_tn(a, b):
    return lax.dot_general(a, b, (((0,), (0,)), ((), ())), preferred_element_type=F32)


def _split3(x):
    hi = x.astype(BF16)
    r = x - hi.astype(F32)
    mid = r.astype(BF16)
    lo = (r - mid.astype(F32)).astype(BF16)
    return hi, mid, lo


def _dot3_r(x, m):
    hi, mid, lo = _split3(x)
    return _dot(hi, m) + _dot(mid, m) + _dot(lo, m)


def _dot3_l(m, x):
    hi, mid, lo = _split3(x)
    return _dot(m, hi) + _dot(m, mid) + _dot(m, lo)


def _rms(x, gain):
    ms = jnp.mean(x * x, axis=-1, keepdims=True)
    return x * lax.rsqrt(ms + NORM_EPS) * gain


def _const_spec(shape):
    nd = len(shape)
    return pl.BlockSpec(shape, lambda *_: (0,) * nd, pipeline_mode=pl.Buffered(1))


V7X_MXU_DIM = 256


def _swiglu_residual(x, gain, wi_ref, wo_ref, dff, tf):
    h = _rms(x, gain).astype(BF16)
    acc = None
    for c0 in range(0, dff, tf):
        gate = _dot(h, wi_ref[:, c0:c0 + tf])
        up = _dot(h, wi_ref[:, dff + c0:dff + c0 + tf])
        act = (gate * jax.nn.sigmoid(gate) * up).astype(BF16)
        part = _dot(act, wo_ref[c0:c0 + tf, :])
        acc = part if acc is None else acc + part
    return x + 0.5 * acc


def _ffn_body(x_ref, g_ref, wi_ref, wo_ref, gf_ref, o_ref, *, final_norm, dff, tf):
    y = _swiglu_residual(x_ref[...], g_ref[...], wi_ref, wo_ref, dff, tf)
    if final_norm:
        y = _rms(y, gf_ref[...])
    o_ref[...] = y


def _ffn(x, gain, wi, wo, layer, final_gain=None):
    r, d = x.shape
    dff = wo.shape[1]
    tm = min(r, 512)
    tf = V7X_MXU_DIM if dff % V7X_MXU_DIM == 0 else dff
    final_norm = final_gain is not None
    gf = final_gain if final_norm else gain
    return pl.pallas_call(
        functools.partial(_ffn_body, final_norm=final_norm, dff=dff, tf=tf),
        out_shape=jax.ShapeDtypeStruct((r, d), F32),
        grid=(r // tm,),
        in_specs=[
            pl.BlockSpec((tm, d), lambda i: (i, 0)),
            _const_spec((1, d)),
            pl.BlockSpec((None, d, 2 * dff), lambda i: (layer, 0, 0), pipeline_mode=pl.Buffered(1)),
            pl.BlockSpec((None, dff, d), lambda i: (layer, 0, 0), pipeline_mode=pl.Buffered(1)),
            _const_spec((1, d)),
        ],
        out_specs=pl.BlockSpec((tm, d), lambda i: (i, 0)),
        compiler_params=_cparams("arbitrary"),
        name="ffn",
    )(x, gain.reshape(1, d), wi, wo, gf.reshape(1, d))


def _rope_body(inv_ref, cos_ref, sin_ref, *, tm, base, period):
    i = pl.program_id(0)
    row = lax.broadcasted_iota(jnp.int32, (tm, HEAD_W), 0) + i * tm
    lane = lax.broadcasted_iota(jnp.int32, (tm, HEAD_W), 1)
    pos = base + (row & (period - 1))
    ang = pos.astype(F32) * inv_ref[...]
    cos_ref[...] = jnp.cos(ang)
    s = jnp.sin(ang)
    first_half = (lane & (DIFF_QK_DIM - 1)) < DIFF_QK_DIM // 2
    sin_ref[...] = jnp.where(first_half, -s, s)


def _rope_tables(rows, base, period):
    assert period & (period - 1) == 0
    half = DIFF_QK_DIM // 2
    inv = ROPE_THETA ** (-jnp.arange(0, DIFF_QK_DIM, 2, dtype=F32) / DIFF_QK_DIM)
    inv_lanes = jnp.tile(inv, HEAD_W // half).reshape(1, HEAD_W)
    tm = min(rows, 512)
    return pl.pallas_call(
        functools.partial(_rope_body, tm=tm, base=base, period=period),
        out_shape=(jax.ShapeDtypeStruct((rows, HEAD_W), F32),) * 2,
        grid=(rows // tm,),
        in_specs=[pl.BlockSpec((1, HEAD_W), lambda i: (0, 0))],
        out_specs=(pl.BlockSpec((tm, HEAD_W), lambda i: (i, 0)),) * 2,
        compiler_params=_cparams("arbitrary"),
        name="rope_tables",
    )(inv_lanes)


_PROMPT_COLS = (("fq_pad", 512), ("fk_pad", 512), ("dq", 256), ("dk", 256),
                ("cqkv", 768), ("cz", 256), ("cb_rep", 256), ("ca_rep", 256), ("small", 128))
_SAMPLE_COLS = (("fq", 256), ("fk", 256), ("fv", 256), ("dq", 256), ("dk", 256), ("dv", 256),
                ("cqkv", 768), ("cz", 256), ("cb_rep", 256), ("ca_rep", 256), ("small", 128))


def _layout(cols):
    lay, off = {}, 0
    for name, width in cols:
        lay[name] = (off, width)
        off += width
    return lay, off


_W_IN_SIZES = (256, 256, 256, 4, 256, 256, 256, 768, 4, 4, 256)


def _split_w_in_t(w_in_t):
    offs = np.concatenate([[0], np.cumsum(_W_IN_SIZES)])
    return tuple(w_in_t[:, int(offs[i]):int(offs[i + 1]), :] for i in range(len(_W_IN_SIZES)))


def _pad_head_rows(w):
    depth, _, d = w.shape
    w = w.reshape(depth, 4, 64, d)
    return jnp.pad(w, ((0, 0), (0, 0), (0, 64), (0, 0))).reshape(depth, 512, d)


def _relayout_w_in(w_in_t, cols):
    fq, fk, fv, ff, dq, dk, dv, cqkv, cb, ca, cz = _split_w_in_t(w_in_t)

    def rep(w):
        return jnp.repeat(w, 64, axis=1)

    table = {
        "fq": fq, "fk": fk, "fv": fv, "fq_pad": _pad_head_rows(fq), "fk_pad": _pad_head_rows(fk),
        "dq": dq, "dk": dk, "dv": dv,
        "cqkv": cqkv, "cz": cz, "cb_rep": rep(cb), "ca_rep": rep(ca),
        "small": jnp.pad(ff, ((0, 0), (0, 124), (0, 0))),
    }
    return jnp.concatenate([table[name] for name, _ in cols], axis=1).astype(BF16)


def _relayout_w_values_t(w_in_t):
    _, _, fv, _, _, _, dv, _, _, _, _ = _split_w_in_t(w_in_t)
    return jnp.concatenate([_pad_head_rows(fv), _pad_head_rows(dv)], axis=1).astype(BF16)


def _aug_place_mats():
    pq = np.zeros((384, 512), np.float32)
    pk = np.zeros((384, 512), np.float32)
    for piece in range(3):
        for h in range(FOX_HEADS):
            pq[piece * 128 + h, h * 128 + AUG_C + piece] = 1.0
            pk[piece * 128 + h, h * 128 + AUG_K + piece] = -1.0
    return jnp.asarray(pq, BF16), jnp.asarray(pk, BF16)


def _lane_ones(shape, lo, hi):
    lane = lax.broadcasted_iota(jnp.int32, shape, 1) & 127
    mid2 = lo + hi
    return jnp.where(jnp.abs(2 * lane - mid2) <= hi - lo, 1.0, 0.0).astype(F32)


def _swap_rope_halves(x):
    half = DIFF_QK_DIM // 2
    lane = lax.broadcasted_iota(jnp.int32, (x.shape[0], 128), 1)
    first = (lane & (DIFF_QK_DIM - 1)) < half
    parts = []
    for c in range(0, x.shape[1], 128):
        xc = x[:, c:c + 128]
        parts.append(jnp.where(first, pltpu.roll(xc, 128 - half, axis=1), pltpu.roll(xc, half, axis=1)))
    return jnp.concatenate(parts, axis=1)


def _inproj_body(*refs, lay, prompt, tiles_per_seq, tm, n_alias=0):
    if prompt:
        (x_ref, g_ref, w_ref, cos_ref, sin_ref, fb_ref, alog_ref, dtb_ref, tri_ref, pq_ref, pk_ref, wvt_ref) = refs[:12]
        (qcat_o, kcat_o, vcat_o, small_o, cqkv_o, cz_o, beta_o, g_o,
         fkt_o, fvt32_o, dkt_o, dvt32_o, carry_scr) = refs[12 + n_alias:]
        fq_o, dq_o = qcat_o.at[:, 0:512], qcat_o.at[:, 512:768]
        fk_pad_o, dkb_o = kcat_o.at[:, 0:512], kcat_o.at[:, 512:768]
        fvt_o, dvt_o = vcat_o.at[0:512, :], vcat_o.at[512:1024, :]
    else:
        (x_ref, g_ref, w_ref, cos_ref, sin_ref, fb_ref, alog_ref, dtb_ref,
         fq_o, fk_o, fv_o, small_o, dq_o, dk_o, dv_o, cqkv_o, cz_o, beta_o, g_o) = refs

    h = _rms(x_ref[...], g_ref[...]).astype(BF16)

    def proj(name):
        off, width = lay[name]
        return _dot_nt(h, w_ref[off:off + width, :])

    logf = jax.nn.log_sigmoid(proj("small") + fb_ref[...])
    small_o[...] = logf

    if prompt:
        i = pl.program_id(0)

        @pl.when(i % tiles_per_seq == 0)
        def _():
            carry_scr[...] = jnp.zeros_like(carry_scr)

        c = _dot3_l(tri_ref[...], logf) + carry_scr[...]
        carry_scr[...] = c[tm - 1:tm, :]
        c1, c2, c3 = _split3(c * LOG2E)
        c123 = jnp.concatenate([c1, c2, c3], axis=1)
        shape = (tm, 512)
        q = proj("fq_pad") * FOX_QSCALE + _dot(c123, pq_ref[...]) + _lane_ones(shape, AUG_K, AUG_K + 2)
        fq_o[...] = q.astype(BF16)
        k32 = proj("fk_pad")
        fk_pad_o[...] = (k32 + _dot(c123, pk_ref[...]) + _lane_ones(shape, AUG_C, AUG_C + 2)).astype(BF16)
        k32_t = k32.T
        for hd in range(FOX_HEADS):
            fkt_o[hd * 64:(hd + 1) * 64, :] = k32_t[hd * 128:hd * 128 + 64]
        vt = _dot_nt(wvt_ref[...], h)
        for hd in range(4):
            fvt32_o[hd * 64:(hd + 1) * 64, :] = vt[hd * 128:hd * 128 + 64]
            dvt32_o[hd * 64:(hd + 1) * 64, :] = vt[512 + hd * 128:512 + hd * 128 + 64]
        row = lax.broadcasted_iota(jnp.int32, vt.shape, 0) & 127
        vt = (vt + jnp.where(row == ONES_LANE, 1.0, 0.0)).astype(BF16)
        fvt_o[...] = vt[0:512]
        dvt_o[...] = vt[512:1024]
    else:
        fq_o[...] = proj("fq") * FOX_QSCALE
        fk_o[...] = proj("fk")
        fv_o[...] = proj("fv")
        dv_o[...] = proj("dv")

    cos = cos_ref[...]
    sin = sin_ref[...]
    dq = proj("dq")
    dq_o[...] = ((dq * cos + _swap_rope_halves(dq) * sin) * DIFF_QSCALE).astype(BF16 if prompt else F32)
    dk = proj("dk")
    dk = dk * cos + _swap_rope_halves(dk) * sin
    if prompt:
        dkb_o[...] = dk.astype(BF16)
        dkt_o[...] = dk.T
    else:
        dk_o[...] = dk

    cqkv_o[...] = proj("cqkv")
    cz = proj("cz")
    cz_o[...] = cz * jax.nn.sigmoid(cz)
    beta_o[...] = jax.nn.sigmoid(proj("cb_rep"))
    g_o[...] = -jnp.exp(alog_ref[...]) * jax.nn.softplus(proj("ca_rep") + dtb_ref[...])


_CACHE_OUTS = ("fkt", "fvt32", "dkt", "dvt32")


def _inproj(x, gain, w, layer, cos, sin, fbias, alog, dtb, *, prompt, seq_len, w_vt=None, cache_bufs=None):
    r, d = x.shape
    lay, nc = _layout(_PROMPT_COLS if prompt else _SAMPLE_COLS)
    assert w.shape[1] == nc
    tm = min(r, 512 if prompt else 256)
    tm = min(tm, seq_len) if prompt else tm
    fb = jnp.pad(fbias.astype(F32), (0, 124)).reshape(1, 128)
    alog_l = jnp.repeat(alog.astype(F32), 64).reshape(1, 256)
    dtb_l = jnp.repeat(dtb.astype(F32), 64).reshape(1, 256)

    def rows(width):
        return pl.BlockSpec((tm, width), lambda i: (i, 0))

    rope_tiles = cos.shape[0] // tm
    rope = pl.BlockSpec((tm, 256), lambda i: (i % rope_tiles, 0))
    in_specs = [rows(d), _const_spec((1, d)),
                pl.BlockSpec((None, nc, d), lambda i: (layer, 0, 0), pipeline_mode=pl.Buffered(1)),
                rope, rope, _const_spec((1, 128)), _const_spec((1, 256)), _const_spec((1, 256))]
    args = [x, gain.reshape(1, d), w, cos, sin, fb, alog_l, dtb_l]
    if prompt:
        tri = jnp.asarray(np.tril(np.ones((tm, tm), np.float32)), BF16)
        pq, pk = _aug_place_mats()
        in_specs += [_const_spec((tm, tm)), _const_spec((384, 512)), _const_spec((384, 512)),
                     pl.BlockSpec((None, 1024, d), lambda i: (layer, 0, 0), pipeline_mode=pl.Buffered(1))]
        args += [tri, pq, pk, w_vt]
        outs = (("qcat", 768, BF16), ("kcat", 768, BF16), ("vcat", -1024, BF16), ("small", 128, F32),
                ("cqkv", 768, F32), ("cz", 256, F32), ("beta", 256, F32), ("g", 256, F32))
        scratch = [pltpu.VMEM((1, 128), F32)]
    else:
        outs = (("fq", 256, F32), ("fk", 256, F32), ("fv", 256, F32), ("small", 128, F32), ("dq", 256, F32),
                ("dk", 256, F32), ("dv", 256, F32), ("cqkv", 768, F32), ("cz", 256, F32),
                ("beta", 256, F32), ("g", 256, F32))
        scratch = []
    tps = max(seq_len // tm, 1)
    out_shape = [jax.ShapeDtypeStruct((r, wd) if wd > 0 else (-wd, r), dt) for _, wd, dt in outs]
    out_specs = [rows(wd) if wd > 0 else pl.BlockSpec((-wd, tm), lambda i: (0, i)) for _, wd, _ in outs]
    names = [name for name, _, _ in outs]
    aliases = {}
    if prompt:
        for j, name in enumerate(_CACHE_OUTS):
            buf = cache_bufs[name]
            out_shape.append(jax.ShapeDtypeStruct(buf.shape, buf.dtype))
            out_specs.append(pl.BlockSpec((None, None, HEAD_W, tm), lambda i: (layer, i // tps, 0, i % tps)))
            names.append(name)
            aliases[len(args)] = len(outs) + j
            in_specs.append(pl.BlockSpec(memory_space=pl.ANY))
            args.append(buf)
    res = pl.pallas_call(
        functools.partial(_inproj_body, lay=lay, prompt=prompt, tiles_per_seq=tps, tm=tm, n_alias=len(aliases)),
        out_shape=tuple(out_shape),
        grid=(r // tm,),
        in_specs=in_specs,
        out_specs=tuple(out_specs),
        scratch_shapes=scratch,
        input_output_aliases=aliases,
        compiler_params=_cparams("arbitrary"),
        name="inproj_prompt" if prompt else "inproj_sample",
    )(*args)
    return dict(zip(names, res))


def _online_softmax_step(st, vt, m_ref, acc_ref):
    m_old = m_ref[...]
    m_new = jnp.maximum(m_old, jnp.max(st, axis=0, keepdims=True))
    alpha = jnp.exp2(m_old - m_new)
    p = jnp.exp2(st - m_new)
    acc_ref[...] = alpha * acc_ref[...] + _dot(vt, p.astype(BF16))
    m_ref[...] = m_new


def _causal_mask(st, query0):
    key = lax.broadcasted_iota(jnp.int32, st.shape, 0)
    qry = lax.broadcasted_iota(jnp.int32, st.shape, 1) + query0
    return jnp.where(key <= qry, st, -jnp.inf)


def _diff_lambda(lam_ref, lam_init):
    lp = lam_ref[...]
    s1 = jnp.sum(lp[0:1] * lp[1:2], axis=1, keepdims=True)
    s2 = jnp.sum(lp[2:3] * lp[3:4], axis=1, keepdims=True)
    return jnp.exp(s1) - jnp.exp(s2) + lam_init


N_ATTN_MAPS = FOX_HEADS + 2 * DIFF_HEADS
ATTN_LOOKAHEAD = 3
ATTN_QUERY_SPLIT = 2


def _attn_body(q_ref, k_hbm, vt_hbm, lam_ref, gain_ref, of_ref, od_ref,
               k_buf, vt_buf, sems, qm_scr, m_scr, acc_scr, *, lam_init, nq, tq):
    n = pl.program_id(0)
    qi = pl.program_id(1)
    n_diff = 2 * DIFF_HEADS
    fq_ref, dq_ref = q_ref.at[:, 0:512], q_ref.at[:, 512:768]

    def tile_copies(kj, slot):
        off = pl.multiple_of((n * nq + kj) * tq, tq)
        return (pltpu.make_async_copy(k_hbm.at[pl.ds(off, tq), :], k_buf.at[slot], sems.at[0, slot]),
                pltpu.make_async_copy(vt_hbm.at[:, pl.ds(off, tq)], vt_buf.at[slot], sems.at[1, slot]))

    for cp in tile_copies(0, 0):
        cp.start()

    m_scr[...] = jnp.full_like(m_scr, -jnp.inf)
    acc_scr[...] = jnp.zeros_like(acc_scr)
    q = dq_ref[...]
    lane_map = lax.broadcasted_iota(jnp.int32, q.shape, 1) // DIFF_QK_DIM
    for m in range(n_diff):
        qm_scr[m] = jnp.where(lane_map == m, q, jnp.zeros_like(q))

    def step(slot, masked):
        k_ref, vt_ref = k_buf.at[slot], vt_buf.at[slot]

        qw = tq // ATTN_QUERY_SPLIT
        units = [(j, c) for j in range(N_ATTN_MAPS) for c in range(ATTN_QUERY_SPLIT)]

        def n_keys(c):
            return (c + 1) * qw if masked else tq

        def scores(j, c):
            rows = slice(c * qw, (c + 1) * qw)
            keys = slice(0, n_keys(c))
            if j < FOX_HEADS:
                sl = slice(j * 128, (j + 1) * 128)
                return _dot_nt(k_ref[keys, sl], fq_ref[rows, sl])
            return _dot_nt(k_ref[keys, 512:768], qm_scr[j - FOX_HEADS, rows, :])

        def values_t(j, c):
            keys = slice(0, n_keys(c))
            if j < FOX_HEADS:
                return vt_ref[j * 128:(j + 1) * 128, keys]
            h = (j - FOX_HEADS) // 2
            return vt_ref[512 + h * 128:512 + (h + 1) * 128, keys]

        pending = [scores(*u) for u in units[:ATTN_LOOKAHEAD]]
        for i, (j, c) in enumerate(units):
            st = pending.pop(0)
            if i + ATTN_LOOKAHEAD < len(units):
                pending.append(scores(*units[i + ATTN_LOOKAHEAD]))
            if masked:
                st = _causal_mask(st, c * qw)
            lanes = slice(c * qw, (c + 1) * qw)
            _online_softmax_step(st, values_t(j, c), m_scr.at[j, :, lanes], acc_scr.at[j, :, lanes])

    def below_diagonal(kj, carry):
        slot = kj % 2
        for cp in tile_copies(kj, slot):
            cp.wait()
        for cp in tile_copies(kj + 1, 1 - slot):
            cp.start()
        step(slot, False)
        return carry

    lax.fori_loop(0, qi, below_diagonal, 0)
    slot = qi % 2
    for cp in tile_copies(qi, slot):
        cp.wait()
    step(slot, True)

    for h in range(FOX_HEADS):
        a = acc_scr[h]
        o = a[:FOX_DIM] / a[ONES_LANE:ONES_LANE + 1]
        of_ref[:, h * FOX_DIM:(h + 1) * FOX_DIM] = o.T.astype(of_ref.dtype)
    lam = _diff_lambda(lam_ref, lam_init)
    gain = jnp.concatenate([gain_ref[...]] * (tq // 128), axis=1)
    for h in range(DIFF_HEADS):
        a0 = acc_scr[FOX_HEADS + 2 * h]
        a1 = acc_scr[FOX_HEADS + 2 * h + 1]
        o = (a0[:DIFF_V_DIM] / a0[ONES_LANE:ONES_LANE + 1]
             - lam * (a1[:DIFF_V_DIM] / a1[ONES_LANE:ONES_LANE + 1]))
        ms = jnp.mean(o * o, axis=0, keepdims=True)
        o = o * lax.rsqrt(ms + NORM_EPS) * gain * (1.0 - lam_init)
        od_ref[:, h * DIFF_V_DIM:(h + 1) * DIFF_V_DIM] = o.T.astype(od_ref.dtype)


def _attn_prompt(qcat, kcat, vcat, lam_p, gain, lam_init, n_seq, seq_len):
    tq = min(seq_len, 512)
    nq = seq_len // tq

    def q_rows(width):
        return pl.BlockSpec((tq, width), lambda n, qi: (n * nq + qi, 0))

    gain_rows = jnp.broadcast_to(gain.astype(F32).reshape(DIFF_V_DIM, 1), (DIFF_V_DIM, 128))
    return pl.pallas_call(
        functools.partial(_attn_body, lam_init=lam_init, nq=nq, tq=tq),
        out_shape=(jax.ShapeDtypeStruct((n_seq * seq_len, HEAD_W), BF16),) * 2,
        grid=(n_seq, nq),
        in_specs=[q_rows(768), pl.BlockSpec(memory_space=pl.ANY), pl.BlockSpec(memory_space=pl.ANY),
                  _const_spec((4, DIFF_QK_DIM)), _const_spec((DIFF_V_DIM, 128))],
        out_specs=(q_rows(HEAD_W), q_rows(HEAD_W)),
        scratch_shapes=[pltpu.VMEM((2, tq, 768), BF16), pltpu.VMEM((2, 1024, tq), BF16),
                        pltpu.SemaphoreType.DMA((2, 2)),
                        pltpu.VMEM((2 * DIFF_HEADS, tq, HEAD_W), BF16), pltpu.VMEM((N_ATTN_MAPS, 1, tq), F32),
                        pltpu.VMEM((N_ATTN_MAPS, 128, tq), F32)],
        compiler_params=_cparams("arbitrary", "arbitrary"),
        name="attn_prompt",
    )(qcat, kcat, vcat, lam_p.astype(F32), gain_rows)


def _merge_ffn_body(x_ref, g_ref, wg_ref, wb_ref, wo_ref, b0_ref, b1_ref, b2_ref,
                    g2_ref, wi_ref, wo2_ref, gf_ref, o_ref, *, final_norm, dff, tf):
    x = x_ref[...]
    d = x.shape[1]
    h = _rms(x, g_ref[...]).astype(BF16)
    mix = None
    for b, b_ref in enumerate((b0_ref, b1_ref, b2_ref)):
        gate = jax.nn.sigmoid(_dot_nt(h, wg_ref[b * d:(b + 1) * d, :]))
        term = gate * _dot(b_ref[...].astype(BF16), wb_ref[b])
        mix = term if mix is None else mix + term
    x = x + _dot(mix.astype(BF16), wo_ref[...])
    y = _swiglu_residual(x, g2_ref[...], wi_ref, wo2_ref, dff, tf)
    if final_norm:
        y = _rms(y, gf_ref[...])
    o_ref[...] = y


def _merge_ffn(x, gain, w_gates, w_branch, w_out, layer, branches, gain2, wi, wo2, final_gain=None):
    r, d = x.shape
    dff = wo2.shape[1]
    tm = min(r, 512)
    tf = V7X_MXU_DIM if dff % V7X_MXU_DIM == 0 else dff
    final_norm = final_gain is not None
    gf = final_gain if final_norm else gain2

    def rows(width):
        return pl.BlockSpec((tm, width), lambda i: (i, 0))

    def resident(*shape):
        nd = len(shape)
        return pl.BlockSpec((None,) + shape, lambda i: (layer,) + (0,) * nd, pipeline_mode=pl.Buffered(1))

    return pl.pallas_call(
        functools.partial(_merge_ffn_body, final_norm=final_norm, dff=dff, tf=tf),
        out_shape=jax.ShapeDtypeStruct((r, d), F32),
        grid=(r // tm,),
        in_specs=[rows(d), _const_spec((1, d)),
                  resident(N_BRANCH * d, d), resident(N_BRANCH, HEAD_W, d), resident(d, d),
                  rows(HEAD_W), rows(HEAD_W), rows(HEAD_W),
                  _const_spec((1, d)), resident(d, 2 * dff), resident(dff, d), _const_spec((1, d))],
        out_specs=rows(d),
        compiler_params=_cparams("arbitrary"),
        name="merge_ffn",
    )(x, gain.reshape(1, d), w_gates, w_branch, w_out, *branches,
      gain2.reshape(1, d), wi, wo2, gf.reshape(1, d))


def _delta_consts():
    c = DELTA_CHUNK
    i = np.arange(c)
    lane = np.arange(HEAD_W)
    lh, lj = lane // c, lane % c
    tril = i[:, None] >= i[None, :]
    suffix = i[None, :] > i[:, None]
    lvl = []
    for k in range(1, 7):
        b = 1 << k
        lvl.append((i[:, None] // b == lj[None, :] // b) & (i[:, None] // (b // 2) != lj[None, :] // (b // 2)))
    f = lambda a, dt: jnp.asarray(np.asarray(a, np.float32), dt)
    return dict(
        lhsg=f(np.concatenate([tril, suffix], axis=0), BF16),
        ones=f(np.ones((c, c)), BF16),
        triu4=f(i[:, None] <= lj[None, :], F32),
        tril4=f(i[:, None] >= lj[None, :], F32),
        stril4=f(i[:, None] > lj[None, :], F32),
        eye4=f(i[:, None] == lj[None, :], F32),
        lvl=f(np.stack(lvl), F32),
        maskbd=f(lh[:, None] == lh[None, :], BF16),
    )


def _bd(x4, maskbd):
    return jnp.concatenate([x4, x4, x4, x4], axis=0) * maskbd


def _mm_heads(xs, ys, maskbd):
    return [_dot(x.astype(BF16), _bd(y.astype(BF16), maskbd)) for x, y in zip(xs, ys)]


def _unit_lower_inverse(a4s, eye4, lvl_ref, maskbd):
    ts = [eye4 - a4 * lvl_ref[0] for a4 in a4s]
    for k in range(1, 6):
        ms = [a4 * lvl_ref[k] for a4 in a4s]
        xs = _mm_heads(ms, ts, maskbd)
        ts = [t - d for t, d in zip(ts, _mm_heads(ts, xs, maskbd))]
    return ts


def _head_sums(x, maskbd):
    hi = x.astype(BF16)
    lo = (x - hi.astype(F32)).astype(BF16)
    return _dot(hi, maskbd) + _dot(lo, maskbd)


def _delta_body(cqkv_ref, beta_ref, g_ref, cz_ref, cw_ref, gain_ref,
                lhsg_ref, ones_ref, triu4_ref, tril4_ref, stril4_ref, eye4_ref, lvl_ref, maskbd_ref,
                o_ref, s_ref, xbuf, ub_scr, w_scr, qk_scr, qe_scr, kd_scr, eg_scr, o_scr, sbd_scr, *, tc):
    j = pl.program_id(1)
    c = DELTA_CHUNK
    maskbd = maskbd_ref[...]
    maskf = maskbd.astype(F32)

    @pl.when(j == 0)
    def _():
        xbuf[0:8, :] = jnp.zeros((8, xbuf.shape[1]), F32)
        sbd_scr[...] = jnp.zeros_like(sbd_scr)

    xbuf[8:8 + tc, :] = cqkv_ref[...]
    cw = cw_ref[...]
    y = cw[3:4] * xbuf[8:8 + tc, :]
    for i in range(1, CONV_WIDTH):
        y = y + cw[3 - i:4 - i] * xbuf[8 - i:8 - i + tc, :]
    xbuf[0:8, :] = xbuf[tc:tc + 8, :]
    y = y * jax.nn.sigmoid(y)
    q = y[:, 0:HEAD_W]
    k = y[:, HEAD_W:2 * HEAD_W]
    qn = q * lax.rsqrt(_head_sums(q * q, maskbd) + 1e-6) * (DELTA_K_DIM ** -0.5)
    kn = k * lax.rsqrt(_head_sums(k * k, maskbd) + 1e-6)
    v = y[:, 2 * HEAD_W:3 * HEAD_W]
    beta = beta_ref[...]
    g = g_ref[...]

    cs = [slice(ci * c, (ci + 1) * c) for ci in range(tc // c)]
    qcs, kcs, vcs = [qn[s] for s in cs], [kn[s] for s in cs], [v[s] for s in cs]
    bcs, gcs = [beta[s] for s in cs], [g[s] for s in cs]
    gsums = [_dot3_l(lhsg_ref[...], gc) for gc in gcs]
    grows = [_dot3_l(ones_ref[...], gc * triu4_ref[...]) for gc in gcs]
    gcols, gsufs = [gs[0:c] for gs in gsums], [gs[c:2 * c] for gs in gsums]
    decays = [jnp.where(tril4_ref[...] > 0, jnp.exp(gcol - grow), 0.0) for gcol, grow in zip(gcols, grows)]
    exp_gs = [jnp.exp(gcol) for gcol in gcols]
    kbs = [kc * bc for kc, bc in zip(kcs, bcs)]
    kbds = [_bd(kc.astype(BF16), maskbd) for kc in kcs]
    kks = [_dot_nt(jnp.concatenate([kb, qc], axis=0).astype(BF16), kbd) for kb, qc, kbd in zip(kbs, qcs, kbds)]
    a4s = [kk[0:c] * decay * stril4_ref[...] for kk, decay in zip(kks, decays)]
    qk4s = [kk[c:2 * c] * decay for kk, decay in zip(kks, decays)]
    t16s = [t.astype(BF16) for t in _unit_lower_inverse(a4s, eye4_ref[...], lvl_ref, maskbd)]
    u_bases = [_dot(t16, _bd((vc * bc).astype(BF16), maskbd)) for t16, vc, bc in zip(t16s, vcs, bcs)]
    ws = [_dot(t16, _bd((kb * eg).astype(BF16), maskbd)) for t16, kb, eg in zip(t16s, kbs, exp_gs)]
    ub_scr[...] = jnp.concatenate(u_bases, axis=0)
    w_scr[...] = jnp.concatenate([w.astype(BF16) for w in ws], axis=0)
    qk_scr[...] = jnp.concatenate([x.astype(BF16) for x in qk4s], axis=0)
    qe_scr[...] = jnp.concatenate([(qc * eg).astype(BF16) for qc, eg in zip(qcs, exp_gs)], axis=0)
    kd_scr[...] = jnp.concatenate([(kc * jnp.exp(gs)).astype(BF16) for kc, gs in zip(kcs, gsufs)], axis=0)
    eg_scr[...] = jnp.concatenate([jnp.broadcast_to(eg[c - 1:c, :], (8, HEAD_W)) for eg in exp_gs], axis=0)

    def scan(ci, carry):
        sl = pl.ds(pl.multiple_of(ci * c, c), c)
        s_old = sbd_scr[...]
        s16 = s_old.astype(BF16)
        u16 = (ub_scr[sl, :] - _dot(w_scr[sl, :], s16)).astype(BF16)
        o_scr[sl, :] = _dot(qe_scr[sl, :], s16) + _dot(qk_scr[sl, :], _bd(u16, maskbd))
        exp_g_last = eg_scr[pl.ds(pl.multiple_of(ci * 8, 8), 8), :][0:1]
        sbd_scr[...] = s_old * exp_g_last + maskf * _dot_tn(kd_scr[sl, :], u16)
        return carry

    lax.fori_loop(0, tc // c, scan, 0)

    o = o_scr[...]
    ms = _head_sums(o * o, maskbd) * (1.0 / DELTA_V_DIM)
    o_ref[...] = (o * lax.rsqrt(ms + NORM_EPS) * gain_ref[...] * cz_ref[...]).astype(o_ref.dtype)

    @pl.when(j == pl.num_programs(1) - 1)
    def _():
        s_ref[...] = sbd_scr[...]


def _delta_prompt(cqkv, beta, g, cz, conv_w, gain, n_seq, seq_len):
    tc = min(seq_len, 512)
    nt = seq_len // tc
    cch = cqkv.shape[1]
    consts = _delta_consts()
    names = ("lhsg", "ones", "triu4", "tril4", "stril4", "eye4", "lvl", "maskbd")

    def rows(width):
        return pl.BlockSpec((tc, width), lambda n, j: (n * nt + j, 0))

    return pl.pallas_call(
        functools.partial(_delta_body, tc=tc),
        out_shape=(jax.ShapeDtypeStruct((n_seq * seq_len, HEAD_W), BF16),
                   jax.ShapeDtypeStruct((n_seq, HEAD_W, HEAD_W), F32)),
        grid=(n_seq, nt),
        in_specs=[rows(cch), rows(HEAD_W), rows(HEAD_W), rows(HEAD_W),
                  _const_spec((CONV_WIDTH, cch)), _const_spec((1, HEAD_W))]
                 + [_const_spec(consts[nm].shape) for nm in names],
        out_specs=(rows(HEAD_W), pl.BlockSpec((None, HEAD_W, HEAD_W), lambda n, j: (n, 0, 0))),
        scratch_shapes=[pltpu.VMEM((tc + 8, cch), F32), pltpu.VMEM((tc, HEAD_W), F32),
                        pltpu.VMEM((tc, HEAD_W), BF16), pltpu.VMEM((tc, HEAD_W), BF16),
                        pltpu.VMEM((tc, HEAD_W), BF16), pltpu.VMEM((tc, HEAD_W), BF16),
                        pltpu.VMEM((8 * (tc // DELTA_CHUNK), HEAD_W), F32),
                        pltpu.VMEM((tc, HEAD_W), F32), pltpu.VMEM((HEAD_W, HEAD_W), F32)],
        compiler_params=_cparams("arbitrary", "arbitrary"),
        name="delta_prompt",
    )(cqkv, beta, g, cz, conv_w.astype(F32), jnp.tile(gain.astype(F32), DELTA_HEADS).reshape(1, HEAD_W),
      *[consts[nm] for nm in names])


SROWS = 8


def _sample_attn_consts(n_pages):
    r = np.arange(PAGE_SIZE)
    m2 = r[:, None] > r[None, :]
    m2ones = np.concatenate([m2, np.ones((PAGE_SIZE, PAGE_SIZE), bool)], axis=1)
    idx = np.arange(4 * n_pages)
    u2 = (idx[None, :] // 4 > idx[:, None] // 4) & (idx[None, :] % 4 == idx[:, None] % 4)
    t = np.arange(SROWS)
    tri8 = t[None, :] <= t[:, None]
    sup = t[:, None] > r[None, :]
    lh = np.arange(HEAD_W) // 64
    f = lambda a, dt: jnp.asarray(np.asarray(a, np.float32), dt)
    return dict(m2ones=f(m2ones, BF16), u2=f(u2, BF16), tri8=f(tri8, BF16), sup=f(sup, F32),
                maskbd=f(lh[:, None] == lh[None, :], BF16))


def _pad_rows(x, rows):
    return jnp.concatenate([x, jnp.zeros((rows - x.shape[0], x.shape[1]), x.dtype)], axis=0)


def _block_rows(x, groups, lanes_per_group):
    xr = jnp.concatenate([x] * groups, axis=0)
    rg = lax.broadcasted_iota(jnp.int32, xr.shape, 0) // SROWS
    lg = lax.broadcasted_iota(jnp.int32, xr.shape, 1) // lanes_per_group
    return jnp.where(rg == lg, xr, 0.0).astype(BF16)


def _paged_softmax_pv(qbd, kt_pages, vt_pages, k_new, v_new, bias_pages, bias_new):
    n_pages = len(kt_pages)
    assert n_pages % 2 == 0
    tiles = []
    for p in range(0, n_pages, 2):
        kt2 = jnp.concatenate([kt_pages[p][...], kt_pages[p + 1][...]], axis=1).astype(BF16)
        s = _dot(qbd, kt2)
        if bias_pages is not None:
            s = s + jnp.concatenate([bias_pages[p], bias_pages[p + 1]], axis=1)
        tiles.append(s)
    s = _dot_nt(qbd, _pad_rows(k_new, PAGE_SIZE).astype(BF16))
    if bias_new is not None:
        s = s + bias_new
    row = lax.broadcasted_iota(jnp.int32, s.shape, 0) & (SROWS - 1)
    col = lax.broadcasted_iota(jnp.int32, s.shape, 1)
    tiles.append(jnp.where(col <= row, s, -jnp.inf))
    m = tiles[0].max(axis=1, keepdims=True)
    for s in tiles[1:]:
        m = jnp.maximum(m, s.max(axis=1, keepdims=True))
    acc = None
    den = None
    for i, s in enumerate(tiles):
        p = jnp.exp2(s - m)
        ps = p.sum(axis=1, keepdims=True)
        if 2 * i < n_pages:
            vt2 = jnp.concatenate([vt_pages[2 * i][...], vt_pages[2 * i + 1][...]], axis=1).astype(BF16)
            pv = _dot_nt(p.astype(BF16), vt2)
        else:
            pv = _dot(p.astype(BF16), _pad_rows(v_new, PAGE_SIZE).astype(BF16))
        acc = pv if acc is None else acc + pv
        den = ps if den is None else den + ps
    return acc / den


def _own_head_rows(x, stride):
    lh = lax.broadcasted_iota(jnp.int32, (SROWS, HEAD_W), 1) // 64
    out = None
    for h in range(4):
        part = jnp.where(lh == h, x[h * stride:h * stride + SROWS], 0.0)
        out = part if out is None else out + part
    return out


def _sample_attn_body(pt_ref, fk_hbm, fv_hbm, lf_hbm, dk_hbm, dv_hbm,
                      fq_ref, fkn_ref, fvn_ref, ln_ref, dq_ref, dkn_ref, dvn_ref,
                      m2_ref, u2_ref, tri8_ref, sup_ref, lam_ref, gain_ref, maskbd_ref,
                      of_ref, od_ref, fk_buf, fv_buf, lf_buf, dk_buf, dv_buf, sems, xm_scr, *, n_pages, layer, lam_init):
    b = pl.program_id(0)
    slot = b % 2
    streams = ((fk_hbm, fk_buf), (fv_hbm, fv_buf), (lf_hbm, lf_buf), (dk_hbm, dk_buf), (dv_hbm, dv_buf))

    def page_copies(sample, sl):
        cps = []
        for p in range(n_pages):
            page = pt_ref[sample, p]
            for hbm, buf in streams:
                cps.append(pltpu.make_async_copy(hbm.at[layer, page], buf.at[sl, p], sems.at[sl]))
        return cps

    @pl.when(b == 0)
    def _():
        for cp in page_copies(0, 0):
            cp.start()

    @pl.when(b + 1 < pl.num_programs(0))
    def _():
        for cp in page_copies(b + 1, 1 - slot):
            cp.start()

    for cp in page_copies(b, slot):
        cp.wait()

    fk_pages = [fk_buf.at[slot, p] for p in range(n_pages)]
    fv_pages = [fv_buf.at[slot, p] for p in range(n_pages)]
    dk_pages = [dk_buf.at[slot, p] for p in range(n_pages)]
    dv_pages = [dv_buf.at[slot, p] for p in range(n_pages)]

    for p in range(n_pages):
        xm_scr[4 * p:4 * p + 4, :] = lf_buf[slot, p]
    wt = _dot3_r(xm_scr[...], m2_ref[...])
    within, totals = wt[:, :PAGE_SIZE], wt[:, PAGE_SIZE:]
    d_past = (within + _dot3_l(u2_ref[...], totals)) * LOG2E
    lnew = ln_ref[...]
    tri8 = tri8_ref[...]
    cn = _dot3_l(tri8, lnew) * LOG2E
    cn_h = [jnp.broadcast_to(cn[:, h:h + 1], (SROWS, PAGE_SIZE)) for h in range(FOX_HEADS)]
    bias_pages = [
        jnp.concatenate([d_past[4 * p + h:4 * p + h + 1, :] + cn_h[h] for h in range(FOX_HEADS)], axis=0)
        for p in range(n_pages)]
    sup = sup_ref[...]
    bias_new = jnp.concatenate(
        [_dot3_l(tri8, jnp.broadcast_to(lnew[:, h:h + 1], (SROWS, PAGE_SIZE)) * sup) * LOG2E
         for h in range(FOX_HEADS)], axis=0)
    qbd = _block_rows(fq_ref[...], FOX_HEADS, FOX_DIM)
    o = _paged_softmax_pv(qbd, fk_pages, fv_pages, fkn_ref[...], fvn_ref[...], bias_pages, bias_new)
    of_ref[...] = _own_head_rows(o, SROWS)

    qbd = _block_rows(dq_ref[...], 2 * DIFF_HEADS, DIFF_QK_DIM)
    o = _paged_softmax_pv(qbd, dk_pages, dv_pages, dkn_ref[...], dvn_ref[...], None, None)
    lam = _diff_lambda(lam_ref, lam_init)
    o = _own_head_rows(o, 2 * SROWS) - lam * _own_head_rows(o[SROWS:], 2 * SROWS)
    ms = _head_sums(o * o, maskbd_ref[...]) * (1.0 / DIFF_V_DIM)
    od_ref[...] = o * lax.rsqrt(ms + NORM_EPS) * gain_ref[...] * (1.0 - lam_init)


def _sample_attn(page_table, caches, layer, proj, lam_p, gain, lam_init):
    n_samples, n_pages = page_table.shape
    consts = _sample_attn_consts(n_pages)

    def tile(width):
        return pl.BlockSpec((SROWS, width), lambda b, pt: (b, 0))

    def const(shape):
        nd = len(shape)
        return pl.BlockSpec(shape, lambda b, pt: (0,) * nd, pipeline_mode=pl.Buffered(1))

    in_specs = [pl.BlockSpec(memory_space=pl.ANY) for _ in caches]
    args = list(caches)
    new = (proj["fq"], proj["fk"], proj["fv"], proj["small"], proj["dq"], proj["dk"], proj["dv"])
    in_specs += [tile(a.shape[1]) for a in new]
    args += list(new)
    cvals = (consts["m2ones"], consts["u2"], consts["tri8"], consts["sup"], lam_p.astype(F32),
             jnp.tile(gain.astype(F32), DIFF_HEADS).reshape(1, HEAD_W), consts["maskbd"])
    in_specs += [const(c.shape) for c in cvals]
    args += list(cvals)
    gs = pltpu.PrefetchScalarGridSpec(
        num_scalar_prefetch=1, grid=(n_samples,), in_specs=in_specs,
        out_specs=(tile(HEAD_W), tile(HEAD_W)),
        scratch_shapes=[pltpu.VMEM((2, n_pages) + c.shape[2:], c.dtype) for c in caches]
                       + [pltpu.SemaphoreType.DMA((2,)), pltpu.VMEM((4 * n_pages, PAGE_SIZE), F32)])
    return pl.pallas_call(
        functools.partial(_sample_attn_body, n_pages=n_pages, layer=layer, lam_init=lam_init),
        out_shape=(jax.ShapeDtypeStruct((n_samples * SROWS, HEAD_W), F32),) * 2,
        grid_spec=gs,
        compiler_params=_cparams("arbitrary"),
        name="sample_attn",
    )(page_table, *args)


def _delta_sample_body(cqkv_ref, buf_ref, beta_ref, g_ref, cz_ref, cw_ref, gain_ref, s0_ref,
                       tile_ref, tile_t_ref, maskbd_ref, o_ref, s_ref, xp_scr, pr_scr, u_scr, kd_scr, o_scr,
                       *, n_tok, group):
    maskbd = maskbd_ref[...]
    maskf = maskbd.astype(F32)
    cch = xp_scr.shape[1]
    cw = cw_ref[...]
    ys = []
    for s in range(group):
        base = 16 * s
        xp_scr[base:base + 8, :] = jnp.zeros((8, cch), F32)
        xp_scr[base + 8 - (CONV_WIDTH - 1):base + 8, :] = buf_ref[s]
        xp_scr[base + 8:base + 16, :] = cqkv_ref[SROWS * s:SROWS * (s + 1), :]
        y = cw[3:4] * xp_scr[base + 8:base + 16, :]
        for i in range(1, CONV_WIDTH):
            y = y + cw[3 - i:4 - i] * xp_scr[base + 8 - i:base + 16 - i, :]
        ys.append(y)
    y = jnp.concatenate(ys, axis=0)
    y = y * jax.nn.sigmoid(y)
    q = y[:, 0:HEAD_W]
    k = y[:, HEAD_W:2 * HEAD_W]
    v = y[:, 2 * HEAD_W:3 * HEAD_W]
    q = q * lax.rsqrt(_head_sums(q * q, maskbd) + 1e-6) * (DELTA_K_DIM ** -0.5)
    k = k * lax.rsqrt(_head_sums(k * k, maskbd) + 1e-6)
    beta = beta_ref[...]
    a = jnp.exp(g_ref[...])

    pairs_k = [(t, j) for t in range(n_tok) for j in range(t)]
    pairs_q = [(t, j) for t in range(n_tok) for j in range(t + 1)]
    assert len(pairs_k) + len(pairs_q) <= 16
    pr_scr[...] = jnp.zeros_like(pr_scr)
    for s in range(group):
        r0, p0 = SROWS * s, 16 * s
        for r, (t, j) in enumerate(pairs_k):
            pr_scr[p0 + r:p0 + r + 1, :] = k[r0 + t:r0 + t + 1] * k[r0 + j:r0 + j + 1]
        for r, (t, j) in enumerate(pairs_q):
            rr = p0 + len(pairs_k) + r
            pr_scr[rr:rr + 1, :] = q[r0 + t:r0 + t + 1] * k[r0 + j:r0 + j + 1]
    dots = _head_sums(pr_scr[...], maskbd)

    rows = [slice(SROWS * s, SROWS * (s + 1)) for s in range(group)]
    sbd0s = [maskf * _dot3_r(s0_ref[s], tile_ref[...]) for s in range(group)]
    s16s = [sb.astype(BF16) for sb in sbd0s]
    rks = [_dot(k[r].astype(BF16), s16) for r, s16 in zip(rows, s16s)]
    rqs = [_dot(q[r].astype(BF16), s16) for r, s16 in zip(rows, s16s)]

    u_scr[...] = jnp.zeros_like(u_scr)
    kd_scr[...] = jnp.zeros_like(kd_scr)
    o_scr[...] = jnp.zeros_like(o_scr)
    a_alls = []
    for s in range(group):
        r0, p0 = SROWS * s, 16 * s
        kk = {tj: dots[p0 + r:p0 + r + 1] for r, tj in enumerate(pairs_k)}
        qk = {tj: dots[p0 + len(pairs_k) + r:p0 + len(pairs_k) + r + 1] for r, tj in enumerate(pairs_q)}
        rk_rows = [rks[s][t:t + 1] for t in range(n_tok)]
        rq_rows = [rqs[s][t:t + 1] for t in range(n_tok)]
        for t in range(n_tok):
            a_t = a[r0 + t:r0 + t + 1]
            u_t = beta[r0 + t:r0 + t + 1] * (v[r0 + t:r0 + t + 1] - a_t * rk_rows[t])
            for t2 in range(t + 1, n_tok):
                rk_rows[t2] = a_t * rk_rows[t2] + kk[(t2, t)] * u_t
            for t2 in range(t, n_tok):
                rq_rows[t2] = a_t * rq_rows[t2] + qk[(t2, t)] * u_t
            o_scr[r0 + t:r0 + t + 1, :] = rq_rows[t]
            u_scr[r0 + t:r0 + t + 1, :] = u_t
            tail = None
            for t2 in range(t + 1, n_tok):
                tail = a[r0 + t2:r0 + t2 + 1] if tail is None else tail * a[r0 + t2:r0 + t2 + 1]
            kd_scr[r0 + t:r0 + t + 1, :] = k[r0 + t:r0 + t + 1] if tail is None else k[r0 + t:r0 + t + 1] * tail
        a_all = a[r0:r0 + 1]
        for t in range(1, n_tok):
            a_all = a_all * a[r0 + t:r0 + t + 1]
        a_alls.append(a_all)
    upd = [_dot_tn(kd_scr[r, :].astype(BF16), u_scr[r, :].astype(BF16)) for r in rows]
    s_news = [sb * a_all + maskf * u for sb, a_all, u in zip(sbd0s, a_alls, upd)]
    for s in range(group):
        s_ref[s] = _dot3_r(s_news[s], tile_t_ref[...])

    o = o_scr[...]
    ms = _head_sums(o * o, maskbd) * (1.0 / DELTA_V_DIM)
    o_ref[...] = o * lax.rsqrt(ms + NORM_EPS) * gain_ref[...] * cz_ref[...]


def _delta_sample(proj, conv_state, state, layer, conv_w, gain, n_tok):
    n_samples = state.shape[1]
    cch = conv_state.shape[3]
    v_ = np.arange(DELTA_V_DIM)
    lane = np.arange(HEAD_W)
    tile_m = jnp.asarray((v_[:, None] == lane[None, :] % DELTA_V_DIM).astype(np.float32), BF16)
    tile_t = jnp.asarray((lane[:, None] % DELTA_V_DIM == v_[None, :]).astype(np.float32), BF16)
    lh = lane // 64
    maskbd = jnp.asarray((lh[:, None] == lh[None, :]).astype(np.float32), BF16)

    group = math.gcd(n_samples, 8)
    rows = group * SROWS

    def tile(width):
        return pl.BlockSpec((rows, width), lambda b: (b, 0))

    return pl.pallas_call(
        functools.partial(_delta_sample_body, n_tok=n_tok, group=group),
        out_shape=(jax.ShapeDtypeStruct((n_samples * SROWS, HEAD_W), F32),
                   jax.ShapeDtypeStruct((n_samples, HEAD_W, DELTA_V_DIM), F32)),
        grid=(n_samples // group,),
        in_specs=[tile(cch),
                  pl.BlockSpec((None, group, CONV_WIDTH - 1, cch), lambda b: (layer, b, 0, 0)),
                  tile(HEAD_W), tile(HEAD_W), tile(HEAD_W),
                  _const_spec((CONV_WIDTH, cch)), _const_spec((1, HEAD_W)),
                  pl.BlockSpec((None, group, HEAD_W, DELTA_V_DIM), lambda b: (layer, b, 0, 0)),
                  _const_spec(tile_m.shape), _const_spec(tile_t.shape), _const_spec(maskbd.shape)],
        out_specs=(tile(HEAD_W), pl.BlockSpec((group, HEAD_W, DELTA_V_DIM), lambda b: (b, 0, 0))),
        scratch_shapes=[pltpu.VMEM((2 * rows, cch), F32), pltpu.VMEM((2 * rows, HEAD_W), F32),
                        pltpu.VMEM((rows, HEAD_W), F32), pltpu.VMEM((rows, HEAD_W), F32),
                        pltpu.VMEM((rows, HEAD_W), F32)],
        compiler_params=_cparams("arbitrary"),
        name="delta_sample",
    )(proj["cqkv"], conv_state, proj["beta"], proj["g"], proj["cz"], conv_w.astype(F32),
      jnp.tile(gain.astype(F32), DELTA_HEADS).reshape(1, HEAD_W), state, tile_m, tile_t, maskbd)


def kernel(x_prompt, x_sample, cache_fox_k, cache_fox_v, cache_fox_logf, cache_diff_k, cache_diff_v, state_delta, state_conv, page_table, norm_ffn1, ffn1_wi, ffn1_wo, norm_mix, w_in, fox_f_bias, diff_lambda, diff_norm, delta_conv_w, delta_A_log, delta_dt_bias, delta_norm, w_branch, w_out, norm_ffn2, ffn2_wi, ffn2_wo, norm_final):
    n_seq, seq_len, d = x_prompt.shape
    n_smp, dec_seq, _ = x_sample.shape
    depth = w_in.shape[0]
    n_pool = cache_fox_k.shape[1]
    n_pages = page_table.shape[1]
    past_len = n_pages * PAGE_SIZE
    assert CONV_WIDTH - 1 <= dec_seq <= SROWS and seq_len % DELTA_CHUNK == 0

    wi1, wo1 = ffn1_wi.astype(BF16), ffn1_wo.astype(BF16)
    wi2, wo2 = ffn2_wi.astype(BF16), ffn2_wo.astype(BF16)
    w_in_t = jnp.swapaxes(w_in, 1, 2)
    w_p = _relayout_w_in(w_in_t, _PROMPT_COLS)
    w_s = _relayout_w_in(w_in_t, _SAMPLE_COLS)
    w_vt = _relayout_w_values_t(w_in_t)
    w_gates = w_in_t[:, w_in_t.shape[1] - N_BRANCH * d:, :].astype(BF16)
    wb, wo = w_branch.astype(BF16), w_out.astype(BF16)

    def pages_t(cache):
        c = cache.reshape(depth, n_pool, PAGE_SIZE, HEAD_W)
        return jnp.swapaxes(c, 2, 3)

    caches = (pages_t(cache_fox_k), pages_t(cache_fox_v), jnp.swapaxes(cache_fox_logf, 2, 3),
              pages_t(cache_diff_k), pages_t(cache_diff_v))
    s_delta = state_delta.reshape(depth, n_smp, HEAD_W, DELTA_V_DIM)

    xp = x_prompt.reshape(n_seq * seq_len, d)
    xs = jnp.pad(x_sample, ((0, 0), (0, SROWS - dec_seq), (0, 0))).reshape(n_smp * SROWS, d)
    cos_p, sin_p = _rope_tables(seq_len, 0, seq_len)
    cos_s, sin_s = _rope_tables(n_smp * SROWS, past_len, SROWS)

    cache_bufs = {name: jnp.zeros((depth, n_seq, HEAD_W, seq_len), F32) for name in _CACHE_OUTS}

    new_p, new_s = [], []
    for l in range(depth):
        lam_init = 0.8 - 0.6 * math.exp(-0.3 * l)
        xp = _ffn(xp, norm_ffn1[l], wi1, wo1, l)
        xs = _ffn(xs, norm_ffn1[l], wi1, wo1, l)
        pp = _inproj(xp, norm_mix[l], w_p, l, cos_p, sin_p, fox_f_bias[l], delta_A_log[l], delta_dt_bias[l],
                     prompt=True, seq_len=seq_len, w_vt=w_vt, cache_bufs=cache_bufs)
        cache_bufs = {name: pp[name] for name in _CACHE_OUTS}
        ps = _inproj(xs, norm_mix[l], w_s, l, cos_s, sin_s, fox_f_bias[l], delta_A_log[l], delta_dt_bias[l],
                     prompt=False, seq_len=SROWS)

        of_p, od_p = _attn_prompt(pp["qcat"], pp["kcat"], pp["vcat"],
                                  diff_lambda[l], diff_norm[l], lam_init, n_seq, seq_len)
        ol_p, sbd = _delta_prompt(pp["cqkv"], pp["beta"], pp["g"], pp["cz"], delta_conv_w[l], delta_norm[l],
                                  n_seq, seq_len)
        of_s, od_s = _sample_attn(page_table, caches, l, ps, diff_lambda[l], diff_norm[l], lam_init)
        ol_s, s_new = _delta_sample(ps, state_conv, s_delta, l, delta_conv_w[l], delta_norm[l], dec_seq)

        final = norm_final if l == depth - 1 else None
        xp = _merge_ffn(xp, norm_mix[l], w_gates, wb, wo, l, (of_p, od_p, ol_p), norm_ffn2[l], wi2, wo2, final)
        xs = _merge_ffn(xs, norm_mix[l], w_gates, wb, wo, l, (of_s, od_s, ol_s), norm_ffn2[l], wi2, wo2, final)

        def p3(a):
            return a.reshape(n_seq, seq_len, a.shape[1])

        def s3(a):
            return a.reshape(n_smp, SROWS, a.shape[1])[:, :dec_seq]

        new_p.append(dict(
            fox_logf=p3(pp["small"])[:, :, :FOX_HEADS],
            delta=jnp.stack([sbd[:, h * 64:(h + 1) * 64, h * 64:(h + 1) * 64] for h in range(DELTA_HEADS)], axis=1),
            conv=p3(pp["cqkv"])[:, seq_len - (CONV_WIDTH - 1):]))
        new_s.append(dict(
            fox_k=s3(ps["fk"]).reshape(n_smp, dec_seq, FOX_HEADS, FOX_DIM),
            fox_v=s3(ps["fv"]).reshape(n_smp, dec_seq, FOX_HEADS, FOX_DIM),
            fox_logf=s3(ps["small"])[:, :, :FOX_HEADS],
            diff_k=s3(ps["dk"]).reshape(n_smp, dec_seq, DIFF_HEADS, 2, DIFF_QK_DIM),
            diff_v=s3(ps["dv"]).reshape(n_smp, dec_seq, DIFF_HEADS, DIFF_V_DIM),
            delta=s_new.reshape(n_smp, DELTA_HEADS, DELTA_K_DIM, DELTA_V_DIM),
            conv=s3(ps["cqkv"])[:, dec_seq - (CONV_WIDTH - 1):]))

    def stk(lst, name):
        return jnp.stack([dd[name] for dd in lst])

    def token_major(buf, *feat):
        nf = len(feat)
        b = buf.reshape((depth, n_seq) + feat + (seq_len,))
        return jnp.transpose(b, (0, 1, 2 + nf) + tuple(range(2, 2 + nf)))

    y_prompt = xp.reshape(n_seq, seq_len, d)
    y_sample = xs.reshape(n_smp, SROWS, d)[:, :dec_seq]
    return (y_prompt, y_sample,
            token_major(cache_bufs["fkt"], FOX_HEADS, FOX_DIM), stk(new_s, "fox_k"),
            token_major(cache_bufs["fvt32"], FOX_HEADS, FOX_DIM), stk(new_s, "fox_v"),
            stk(new_p, "fox_logf"), stk(new_s, "fox_logf"),
            token_major(cache_bufs["dkt"], DIFF_HEADS, 2, DIFF_QK_DIM), stk(new_s, "diff_k"),
            token_major(cache_bufs["dvt32"], DIFF_HEADS, DIFF_V_DIM), stk(new_s, "diff_v"),
            stk(new_p, "delta"), stk(new_s, "delta"),
            stk(new_p, "conv"), stk(new_s, "conv"))
```

```python
import functools
import math

import numpy as np
import jax
import jax.numpy as jnp
from jax import lax
from jax.experimental import pallas as pl
from jax.experimental.pallas import tpu as pltpu

F32 = jnp.float32
BF16 = jnp.bfloat16

FOX_HEADS = 4
FOX_DIM = 64
DIFF_HEADS = 4
DIFF_QK_DIM = 32
DIFF_V_DIM = 64
DELTA_HEADS = 4
DELTA_K_DIM = 64
DELTA_V_DIM = 64
CONV_WIDTH = 4
DELTA_CHUNK = 64
N_BRANCH = 3
HEAD_W = 256
PAGE_SIZE = 128
ROPE_THETA = 10000.0
NORM_EPS = 1e-6
LOG2E = 1.4426950408889634
FOX_QSCALE = FOX_DIM ** -0.5 * LOG2E
DIFF_QSCALE = DIFF_QK_DIM ** -0.5 * LOG2E

V7X_VMEM_LIMIT_BYTES = 56 * 1024 * 1024
AUG_C = 64
AUG_K = 67
ONES_LANE = 64


def _cparams(*sem):
    return pltpu.CompilerParams(dimension_semantics=sem, vmem_limit_bytes=V7X_VMEM_LIMIT_BYTES)


def _dot(a, b):
    return jnp.dot(a, b, preferred_element_type=F32)


def _dot_nt(a, b):
    return lax.dot_general(a, b, (((1,), (1,)), ((), ())), preferred_element_type=F32)


def _dot_tn(a, b):
    return lax.dot_general(a, b, (((0,), (0,)), ((), ())), preferred_element_type=F32)


def _split3(x):
    hi = x.astype(BF16)
    r = x - hi.astype(F32)
    mid = r.astype(BF16)
    lo = (r - mid.astype(F32)).astype(BF16)
    return hi, mid, lo


def _dot3_r(x, m):
    hi, mid, lo = _split3(x)
    return _dot(hi, m) + _dot(mid, m) + _dot(lo, m)


def _dot3_l(m, x):
    hi, mid, lo = _split3(x)
    return _dot(m, hi) + _dot(m, mid) + _dot(m, lo)


def _rms(x, gain):
    ms = jnp.mean(x * x, axis=-1, keepdims=True)
    return x * lax.rsqrt(ms + NORM_EPS) * gain


def _const_spec(shape):
    nd = len(shape)
    return pl.BlockSpec(shape, lambda *_: (0,) * nd, pipeline_mode=pl.Buffered(1))


V7X_MXU_DIM = 256


def _swiglu_residual(x, gain, wi_ref, wo_ref, dff, tf):
    h = _rms(x, gain).astype(BF16)
    acc = None
    for c0 in range(0, dff, tf):
        gate = _dot(h, wi_ref[:, c0:c0 + tf])
        up = _dot(h, wi_ref[:, dff + c0:dff + c0 + tf])
        act = (gate * jax.nn.sigmoid(gate) * up).astype(BF16)
        part = _dot(act, wo_ref[c0:c0 + tf, :])
        acc = part if acc is None else acc + part
    return x + 0.5 * acc


def _ffn_body(x_ref, g_ref, wi_ref, wo_ref, gf_ref, o_ref, *, final_norm, dff, tf):
    y = _swiglu_residual(x_ref[...], g_ref[...], wi_ref, wo_ref, dff, tf)
    if final_norm:
        y = _rms(y, gf_ref[...])
    o_ref[...] = y


def _ffn(x, gain, wi, wo, layer, final_gain=None):
    r, d = x.shape
    dff = wo.shape[1]
    tm = min(r, 512)
    tf = V7X_MXU_DIM if dff % V7X_MXU_DIM == 0 else dff
    final_norm = final_gain is not None
    gf = final_gain if final_norm else gain
    return pl.pallas_call(
        functools.partial(_ffn_body, final_norm=final_norm, dff=dff, tf=tf),
        out_shape=jax.ShapeDtypeStruct((r, d), F32),
        grid=(r // tm,),
        in_specs=[
            pl.BlockSpec((tm, d), lambda i: (i, 0)),
            _const_spec((1, d)),
            pl.BlockSpec((None, d, 2 * dff), lambda i: (layer, 0, 0), pipeline_mode=pl.Buffered(1)),
            pl.BlockSpec((None, dff, d), lambda i: (layer, 0, 0), pipeline_mode=pl.Buffered(1)),
            _const_spec((1, d)),
        ],
        out_specs=pl.BlockSpec((tm, d), lambda i: (i, 0)),
        compiler_params=_cparams("arbitrary"),
        name="ffn",
    )(x, gain.reshape(1, d), wi, wo, gf.reshape(1, d))


def _rope_body(inv_ref, cos_ref, sin_ref, *, tm, base, period):
    i = pl.program_id(0)
    row = lax.broadcasted_iota(jnp.int32, (tm, HEAD_W), 0) + i * tm
    lane = lax.broadcasted_iota(jnp.int32, (tm, HEAD_W), 1)
    pos = base + (row & (period - 1))
    ang = pos.astype(F32) * inv_ref[...]
    cos_ref[...] = jnp.cos(ang)
    s = jnp.sin(ang)
    first_half = (lane & (DIFF_QK_DIM - 1)) < DIFF_QK_DIM // 2
    sin_ref[...] = jnp.where(first_half, -s, s)


def _rope_tables(rows, base, period):
    assert period & (period - 1) == 0
    half = DIFF_QK_DIM // 2
    inv = ROPE_THETA ** (-jnp.arange(0, DIFF_QK_DIM, 2, dtype=F32) / DIFF_QK_DIM)
    inv_lanes = jnp.tile(inv, HEAD_W // half).reshape(1, HEAD_W)
    tm = min(rows, 512)
    return pl.pallas_call(
        functools.partial(_rope_body, tm=tm, base=base, period=period),
        out_shape=(jax.ShapeDtypeStruct((rows, HEAD_W), F32),) * 2,
        grid=(rows // tm,),
        in_specs=[pl.BlockSpec((1, HEAD_W), lambda i: (0, 0))],
        out_specs=(pl.BlockSpec((tm, HEAD_W), lambda i: (i, 0)),) * 2,
        compiler_params=_cparams("arbitrary"),
        name="rope_tables",
    )(inv_lanes)


_PROMPT_COLS = (("fq_pad", 512), ("fk_pad", 512), ("dq", 256), ("dk", 256),
                ("cqkv", 768), ("cz", 256), ("cb_rep", 256), ("ca_rep", 256), ("small", 128))
_SAMPLE_COLS = (("fq", 256), ("fk", 256), ("fv", 256), ("dq", 256), ("dk", 256), ("dv", 256),
                ("cqkv", 768), ("cz", 256), ("cb_rep", 256), ("ca_rep", 256), ("small", 128))


def _layout(cols):
    lay, off = {}, 0
    for name, width in cols:
        lay[name] = (off, width)
        off += width
    return lay, off


_W_IN_SIZES = (256, 256, 256, 4, 256, 256, 256, 768, 4, 4, 256)


def _split_w_in_t(w_in_t):
    offs = np.concatenate([[0], np.cumsum(_W_IN_SIZES)])
    return tuple(w_in_t[:, int(offs[i]):int(offs[i + 1]), :] for i in range(len(_W_IN_SIZES)))


def _pad_head_rows(w):
    depth, _, d = w.shape
    w = w.reshape(depth, 4, 64, d)
    return jnp.pad(w, ((0, 0), (0, 0), (0, 64), (0, 0))).reshape(depth, 512, d)


def _relayout_w_in(w_in_t, cols):
    fq, fk, fv, ff, dq, dk, dv, cqkv, cb, ca, cz = _split_w_in_t(w_in_t)

    def rep(w):
        return jnp.repeat(w, 64, axis=1)

    table = {
        "fq": fq, "fk": fk, "fv": fv, "fq_pad": _pad_head_rows(fq), "fk_pad": _pad_head_rows(fk),
        "dq": dq, "dk": dk, "dv": dv,
        "cqkv": cqkv, "cz": cz, "cb_rep": rep(cb), "ca_rep": rep(ca),
        "small": jnp.pad(ff, ((0, 0), (0, 124), (0, 0))),
    }
    return jnp.concatenate([table[name] for name, _ in cols], axis=1).astype(BF16)


def _relayout_w_values_t(w_in_t):
    _, _, fv, _, _, _, dv, _, _, _, _ = _split_w_in_t(w_in_t)
    return jnp.concatenate([_pad_head_rows(fv), _pad_head_rows(dv)], axis=1).astype(BF16)


def _aug_place_mats():
    pq = np.zeros((384, 512), np.float32)
    pk = np.zeros((384, 512), np.float32)
    for piece in range(3):
        for h in range(FOX_HEADS):
            pq[piece * 128 + h, h * 128 + AUG_C + piece] = 1.0
            pk[piece * 128 + h, h * 128 + AUG_K + piece] = -1.0
    return jnp.asarray(pq, BF16), jnp.asarray(pk, BF16)


def _lane_ones(shape, lo, hi):
    lane = lax.broadcasted_iota(jnp.int32, shape, 1) & 127
    mid2 = lo + hi
    return jnp.where(jnp.abs(2 * lane - mid2) <= hi - lo, 1.0, 0.0).astype(F32)


def _swap_rope_halves(x):
    half = DIFF_QK_DIM // 2
    lane = lax.broadcasted_iota(jnp.int32, (x.shape[0], 128), 1)
    first = (lane & (DIFF_QK_DIM - 1)) < half
    parts = []
    for c in range(0, x.shape[1], 128):
        xc = x[:, c:c + 128]
        parts.append(jnp.where(first, pltpu.roll(xc, 128 - half, axis=1), pltpu.roll(xc, half, axis=1)))
    return jnp.concatenate(parts, axis=1)


def _inproj_body(*refs, lay, prompt, tiles_per_seq, tm, n_alias=0):
    if prompt:
        (x_ref, g_ref, w_ref, cos_ref, sin_ref, fb_ref, alog_ref, dtb_ref, tri_ref, pq_ref, pk_ref, wvt_ref) = refs[:12]
        (qcat_o, kcat_o, vcat_o, small_o, cqkv_o, cz_o, beta_o, g_o,
         fkt_o, fvt32_o, dkt_o, dvt32_o, carry_scr) = refs[12 + n_alias:]
        fq_o, dq_o = qcat_o.at[:, 0:512], qcat_o.at[:, 512:768]
        fk_pad_o, dkb_o = kcat_o.at[:, 0:512], kcat_o.at[:, 512:768]
        fvt_o, dvt_o = vcat_o.at[0:512, :], vcat_o.at[512:1024, :]
    else:
        (x_ref, g_ref, w_ref, cos_ref, sin_ref, fb_ref, alog_ref, dtb_ref,
         fq_o, fk_o, fv_o, small_o, dq_o, dk_o, dv_o, cqkv_o, cz_o, beta_o, g_o) = refs

    h = _rms(x_ref[...], g_ref[...]).astype(BF16)

    def proj(name):
        off, width = lay[name]
        return _dot_nt(h, w_ref[off:off + width, :])

    logf = jax.nn.log_sigmoid(proj("small") + fb_ref[...])
    small_o[...] = logf

    if prompt:
        i = pl.program_id(0)

        @pl.when(i % tiles_per_seq == 0)
        def _():
            carry_scr[...] = jnp.zeros_like(carry_scr)

        c = _dot3_l(tri_ref[...], logf) + carry_scr[...]
        carry_scr[...] = c[tm - 1:tm, :]
        c1, c2, c3 = _split3(c * LOG2E)
        c123 = jnp.concatenate([c1, c2, c3], axis=1)
        shape = (tm, 512)
        q = proj("fq_pad") * FOX_QSCALE + _dot(c123, pq_ref[...]) + _lane_ones(shape, AUG_K, AUG_K + 2)
        fq_o[...] = q.astype(BF16)
        k32 = proj("fk_pad")
        fk_pad_o[...] = (k32 + _dot(c123, pk_ref[...]) + _lane_ones(shape, AUG_C, AUG_C + 2)).astype(BF16)
        k32_t = k32.T
        for hd in range(FOX_HEADS):
            fkt_o[hd * 64:(hd + 1) * 64, :] = k32_t[hd * 128:hd * 128 + 64]
        vt = _dot_nt(wvt_ref[...], h)
        for hd in range(4):
            fvt32_o[hd * 64:(hd + 1) * 64, :] = vt[hd * 128:hd * 128 + 64]
            dvt32_o[hd * 64:(hd + 1) * 64, :] = vt[512 + hd * 128:512 + hd * 128 + 64]
        row = lax.broadcasted_iota(jnp.int32, vt.shape, 0) & 127
        vt = (vt + jnp.where(row == ONES_LANE, 1.0, 0.0)).astype(BF16)
        fvt_o[...] = vt[0:512]
        dvt_o[...] = vt[512:1024]
    else:
        fq_o[...] = proj("fq") * FOX_QSCALE
        fk_o[...] = proj("fk")
        fv_o[...] = proj("fv")
        dv_o[...] = proj("dv")

    cos = cos_ref[...]
    sin = sin_ref[...]
    dq = proj("dq")
    dq_o[...] = ((dq * cos + _swap_rope_halves(dq) * sin) * DIFF_QSCALE).astype(BF16 if prompt else F32)
    dk = proj("dk")
    dk = dk * cos + _swap_rope_halves(dk) * sin
    if prompt:
        dkb_o[...] = dk.astype(BF16)
        dkt_o[...] = dk.T
    else:
        dk_o[...] = dk

    cqkv_o[...] = proj("cqkv")
    cz = proj("cz")
    cz_o[...] = cz * jax.nn.sigmoid(cz)
    beta_o[...] = jax.nn.sigmoid(proj("cb_rep"))
    g_o[...] = -jnp.exp(alog_ref[...]) * jax.nn.softplus(proj("ca_rep") + dtb_ref[...])


_CACHE_OUTS = ("fkt", "fvt32", "dkt", "dvt32")


def _inproj(x, gain, w, layer, cos, sin, fbias, alog, dtb, *, prompt, seq_len, w_vt=None, cache_bufs=None):
    r, d = x.shape
    lay, nc = _layout(_PROMPT_COLS if prompt else _SAMPLE_COLS)
    assert w.shape[1] == nc
    tm = min(r, 512 if prompt else 256)
    tm = min(tm, seq_len) if prompt else tm
    fb = jnp.pad(fbias.astype(F32), (0, 124)).reshape(1, 128)
    alog_l = jnp.repeat(alog.astype(F32), 64).reshape(1, 256)
    dtb_l = jnp.repeat(dtb.astype(F32), 64).reshape(1, 256)

    def rows(width):
        return pl.BlockSpec((tm, width), lambda i: (i, 0))

    rope_tiles = cos.shape[0] // tm
    rope = pl.BlockSpec((tm, 256), lambda i: (i % rope_tiles, 0))
    in_specs = [rows(d), _const_spec((1, d)),
                pl.BlockSpec((None, nc, d), lambda i: (layer, 0, 0), pipeline_mode=pl.Buffered(1)),
                rope, rope, _const_spec((1, 128)), _const_spec((1, 256)), _const_spec((1, 256))]
    args = [x, gain.reshape(1, d), w, cos, sin, fb, alog_l, dtb_l]
    if prompt:
        tri = jnp.asarray(np.tril(np.ones((tm, tm), np.float32)), BF16)
        pq, pk = _aug_place_mats()
        in_specs += [_const_spec((tm, tm)), _const_spec((384, 512)), _const_spec((384, 512)),
                     pl.BlockSpec((None, 1024, d), lambda i: (layer, 0, 0), pipeline_mode=pl.Buffered(1))]
        args += [tri, pq, pk, w_vt]
        outs = (("qcat", 768, BF16), ("kcat", 768, BF16), ("vcat", -1024, BF16), ("small", 128, F32),
                ("cqkv", 768, F32), ("cz", 256, F32), ("beta", 256, F32), ("g", 256, F32))
        scratch = [pltpu.VMEM((1, 128), F32)]
    else:
        outs = (("fq", 256, F32), ("fk", 256, F32), ("fv", 256, F32), ("small", 128, F32), ("dq", 256, F32),
                ("dk", 256, F32), ("dv", 256, F32), ("cqkv", 768, F32), ("cz", 256, F32),
                ("beta", 256, F32), ("g", 256, F32))
        scratch = []
    tps = max(seq_len // tm, 1)
    out_shape = [jax.ShapeDtypeStruct((r, wd) if wd > 0 else (-wd, r), dt) for _, wd, dt in outs]
    out_specs = [rows(wd) if wd > 0 else pl.BlockSpec((-wd, tm), lambda i: (0, i)) for _, wd, _ in outs]
    names = [name for name, _, _ in outs]
    aliases = {}
    if prompt:
        for j, name in enumerate(_CACHE_OUTS):
            buf = cache_bufs[name]
            out_shape.append(jax.ShapeDtypeStruct(buf.shape, buf.dtype))
            out_specs.append(pl.BlockSpec((None, None, HEAD_W, tm), lambda i: (layer, i // tps, 0, i % tps)))
            names.append(name)
            aliases[len(args)] = len(outs) + j
            in_specs.append(pl.BlockSpec(memory_space=pl.ANY))
            args.append(buf)
    res = pl.pallas_call(
        functools.partial(_inproj_body, lay=lay, prompt=prompt, tiles_per_seq=tps, tm=tm, n_alias=len(aliases)),
        out_shape=tuple(out_shape),
        grid=(r // tm,),
        in_specs=in_specs,
        out_specs=tuple(out_specs),
        scratch_shapes=scratch,
        input_output_aliases=aliases,
        compiler_params=_cparams("arbitrary"),
        name="inproj_prompt" if prompt else "inproj_sample",
    )(*args)
    return dict(zip(names, res))


def _online_softmax_step(st, vt, m_ref, acc_ref):
    m_old = m_ref[...]
    m_new = jnp.maximum(m_old, jnp.max(st, axis=0, keepdims=True))
    alpha = jnp.exp2(m_old - m_new)
    p = jnp.exp2(st - m_new)
    acc_ref[...] = alpha * acc_ref[...] + _dot(vt, p.astype(BF16))
    m_ref[...] = m_new


def _causal_mask(st, query0):
    key = lax.broadcasted_iota(jnp.int32, st.shape, 0)
    qry = lax.broadcasted_iota(jnp.int32, st.shape, 1) + query0
    return jnp.where(key <= qry, st, -jnp.inf)


def _diff_lambda(lam_ref, lam_init):
    lp = lam_ref[...]
    s1 = jnp.sum(lp[0:1] * lp[1:2], axis=1, keepdims=True)
    s2 = jnp.sum(lp[2:3] * lp[3:4], axis=1, keepdims=True)
    return jnp.exp(s1) - jnp.exp(s2) + lam_init


N_ATTN_MAPS = FOX_HEADS + 2 * DIFF_HEADS
ATTN_LOOKAHEAD = 4
ATTN_QUERY_SPLIT = 2


def _attn_body(q_ref, k_hbm, vt_hbm, lam_ref, gain_ref, of_ref, od_ref,
               k_buf, vt_buf, sems, qm_scr, m_scr, acc_scr, *, lam_init, nq, tq):
    n = pl.program_id(0)
    qi = pl.program_id(1)
    n_diff = 2 * DIFF_HEADS
    fq_ref, dq_ref = q_ref.at[:, 0:512], q_ref.at[:, 512:768]

    def tile_copies(kj, slot):
        off = pl.multiple_of((n * nq + kj) * tq, tq)
        return (pltpu.make_async_copy(k_hbm.at[pl.ds(off, tq), :], k_buf.at[slot], sems.at[0, slot]),
                pltpu.make_async_copy(vt_hbm.at[:, pl.ds(off, tq)], vt_buf.at[slot], sems.at[1, slot]))

    for cp in tile_copies(0, 0):
        cp.start()

    m_scr[...] = jnp.full_like(m_scr, -jnp.inf)
    acc_scr[...] = jnp.zeros_like(acc_scr)
    q = dq_ref[...]
    lane_map = lax.broadcasted_iota(jnp.int32, q.shape, 1) // DIFF_QK_DIM
    for m in range(n_diff):
        qm_scr[m] = jnp.where(lane_map == m, q, jnp.zeros_like(q))

    def step(slot, masked):
        k_ref, vt_ref = k_buf.at[slot], vt_buf.at[slot]

        qw = tq // ATTN_QUERY_SPLIT
        units = [(j, c) for j in range(N_ATTN_MAPS) for c in range(ATTN_QUERY_SPLIT)]

        def n_keys(c):
            return (c + 1) * qw if masked else tq

        def scores(j, c):
            rows = slice(c * qw, (c + 1) * qw)
            keys = slice(0, n_keys(c))
            if j < FOX_HEADS:
                sl = slice(j * 128, (j + 1) * 128)
                return _dot_nt(k_ref[keys, sl], fq_ref[rows, sl])
            return _dot_nt(k_ref[keys, 512:768], qm_scr[j - FOX_HEADS, rows, :])

        def values_t(j, c):
            keys = slice(0, n_keys(c))
            if j < FOX_HEADS:
                return vt_ref[j * 128:(j + 1) * 128, keys]
            h = (j - FOX_HEADS) // 2
            return vt_ref[512 + h * 128:512 + (h + 1) * 128, keys]

        pending = [scores(*u) for u in units[:ATTN_LOOKAHEAD]]
        for i, (j, c) in enumerate(units):
            st = pending.pop(0)
            if i + ATTN_LOOKAHEAD < len(units):
                pending.append(scores(*units[i + ATTN_LOOKAHEAD]))
            if masked:
                st = _causal_mask(st, c * qw)
            lanes = slice(c * qw, (c + 1) * qw)
            _online_softmax_step(st, values_t(j, c), m_scr.at[j, :, lanes], acc_scr.at[j, :, lanes])

    def below_diagonal(kj, carry):
        slot = kj % 2
        for cp in tile_copies(kj, slot):
            cp.wait()
        for cp in tile_copies(kj + 1, 1 - slot):
            cp.start()
        step(slot, False)
        return carry

    lax.fori_loop(0, qi, below_diagonal, 0)
    slot = qi % 2
    for cp in tile_copies(qi, slot):
        cp.wait()
    step(slot, True)

    for h in range(FOX_HEADS):
        a = acc_scr[h]
        o = a[:FOX_DIM] / a[ONES_LANE:ONES_LANE + 1]
        of_ref[:, h * FOX_DIM:(h + 1) * FOX_DIM] = o.T.astype(of_ref.dtype)
    lam = _diff_lambda(lam_ref, lam_init)
    gain = jnp.concatenate([gain_ref[...]] * (tq // 128), axis=1)
    for h in range(DIFF_HEADS):
        a0 = acc_scr[FOX_HEADS + 2 * h]
        a1 = acc_scr[FOX_HEADS + 2 * h + 1]
        o = (a0[:DIFF_V_DIM] / a0[ONES_LANE:ONES_LANE + 1]
             - lam * (a1[:DIFF_V_DIM] / a1[ONES_LANE:ONES_LANE + 1]))
        ms = jnp.mean(o * o, axis=0, keepdims=True)
        o = o * lax.rsqrt(ms + NORM_EPS) * gain * (1.0 - lam_init)
        od_ref[:, h * DIFF_V_DIM:(h + 1) * DIFF_V_DIM] = o.T.astype(od_ref.dtype)


def _attn_prompt(qcat, kcat, vcat, lam_p, gain, lam_init, n_seq, seq_len):
    tq = min(seq_len, 512)
    nq = seq_len // tq

    def q_rows(width):
        return pl.BlockSpec((tq, width), lambda n, qi: (n * nq + qi, 0))

    gain_rows = jnp.broadcast_to(gain.astype(F32).reshape(DIFF_V_DIM, 1), (DIFF_V_DIM, 128))
    return pl.pallas_call(
        functools.partial(_attn_body, lam_init=lam_init, nq=nq, tq=tq),
        out_shape=(jax.ShapeDtypeStruct((n_seq * seq_len, HEAD_W), BF16),) * 2,
        grid=(n_seq, nq),
        in_specs=[q_rows(768), pl.BlockSpec(memory_space=pl.ANY), pl.BlockSpec(memory_space=pl.ANY),
                  _const_spec((4, DIFF_QK_DIM)), _const_spec((DIFF_V_DIM, 128))],
        out_specs=(q_rows(HEAD_W), q_rows(HEAD_W)),
        scratch_shapes=[pltpu.VMEM((2, tq, 768), BF16), pltpu.VMEM((2, 1024, tq), BF16),
                        pltpu.SemaphoreType.DMA((2, 2)),
                        pltpu.VMEM((2 * DIFF_HEADS, tq, HEAD_W), BF16), pltpu.VMEM((N_ATTN_MAPS, 1, tq), F32),
                        pltpu.VMEM((N_ATTN_MAPS, 128, tq), F32)],
        compiler_params=_cparams("arbitrary", "arbitrary"),
        name="attn_prompt",
    )(qcat, kcat, vcat, lam_p.astype(F32), gain_rows)


def _merge_ffn_body(x_ref, g_ref, wg_ref, wb_ref, wo_ref, b0_ref, b1_ref, b2_ref,
                    g2_ref, wi_ref, wo2_ref, gf_ref, o_ref, *, final_norm, dff, tf):
    x = x_ref[...]
    d = x.shape[1]
    h = _rms(x, g_ref[...]).astype(BF16)
    mix = None
    for b, b_ref in enumerate((b0_ref, b1_ref, b2_ref)):
        gate = jax.nn.sigmoid(_dot_nt(h, wg_ref[b * d:(b + 1) * d, :]))
        term = gate * _dot(b_ref[...].astype(BF16), wb_ref[b])
        mix = term if mix is None else mix + term
    x = x + _dot(mix.astype(BF16), wo_ref[...])
    y = _swiglu_residual(x, g2_ref[...], wi_ref, wo2_ref, dff, tf)
    if final_norm:
        y = _rms(y, gf_ref[...])
    o_ref[...] = y


def _merge_ffn(x, gain, w_gates, w_branch, w_out, layer, branches, gain2, wi, wo2, final_gain=None):
    r, d = x.shape
    dff = wo2.shape[1]
    tm = min(r, 512)
    tf = V7X_MXU_DIM if dff % V7X_MXU_DIM == 0 else dff
    final_norm = final_gain is not None
    gf = final_gain if final_norm else gain2

    def rows(width):
        return pl.BlockSpec((tm, width), lambda i: (i, 0))

    def resident(*shape):
        nd = len(shape)
        return pl.BlockSpec((None,) + shape, lambda i: (layer,) + (0,) * nd, pipeline_mode=pl.Buffered(1))

    return pl.pallas_call(
        functools.partial(_merge_ffn_body, final_norm=final_norm, dff=dff, tf=tf),
        out_shape=jax.ShapeDtypeStruct((r, d), F32),
        grid=(r // tm,),
        in_specs=[rows(d), _const_spec((1, d)),
                  resident(N_BRANCH * d, d), resident(N_BRANCH, HEAD_W, d), resident(d, d),
                  rows(HEAD_W), rows(HEAD_W), rows(HEAD_W),
                  _const_spec((1, d)), resident(d, 2 * dff), resident(dff, d), _const_spec((1, d))],
        out_specs=rows(d),
        compiler_params=_cparams("arbitrary"),
        name="merge_ffn",
    )(x, gain.reshape(1, d), w_gates, w_branch, w_out, *branches,
      gain2.reshape(1, d), wi, wo2, gf.reshape(1, d))


def _delta_consts():
    c = DELTA_CHUNK
    i = np.arange(c)
    lane = np.arange(HEAD_W)
    lh, lj = lane // c, lane % c
    tril = i[:, None] >= i[None, :]
    suffix = i[None, :] > i[:, None]
    lvl = []
    for k in range(1, 7):
        b = 1 << k
        lvl.append((i[:, None] // b == lj[None, :] // b) & (i[:, None] // (b // 2) != lj[None, :] // (b // 2)))
    f = lambda a, dt: jnp.asarray(np.asarray(a, np.float32), dt)
    return dict(
        lhsg=f(np.concatenate([tril, suffix], axis=0), BF16),
        ones=f(np.ones((c, c)), BF16),
        triu4=f(i[:, None] <= lj[None, :], F32),
        tril4=f(i[:, None] >= lj[None, :], F32),
        stril4=f(i[:, None] > lj[None, :], F32),
        eye4=f(i[:, None] == lj[None, :], F32),
        lvl=f(np.stack(lvl), F32),
        maskbd=f(lh[:, None] == lh[None, :], BF16),
    )


def _bd(x4, maskbd):
    return jnp.concatenate([x4, x4, x4, x4], axis=0) * maskbd


def _mm_heads(xs, ys, maskbd):
    return [_dot(x.astype(BF16), _bd(y.astype(BF16), maskbd)) for x, y in zip(xs, ys)]


def _unit_lower_inverse(a4s, eye4, lvl_ref, maskbd):
    ts = [eye4 - a4 * lvl_ref[0] for a4 in a4s]
    for k in range(1, 6):
        ms = [a4 * lvl_ref[k] for a4 in a4s]
        xs = _mm_heads(ms, ts, maskbd)
        ts = [t - d for t, d in zip(ts, _mm_heads(ts, xs, maskbd))]
    return ts


def _head_sums(x, maskbd):
    hi = x.astype(BF16)
    lo = (x - hi.astype(F32)).astype(BF16)
    return _dot(hi, maskbd) + _dot(lo, maskbd)


def _delta_body(cqkv_ref, beta_ref, g_ref, cz_ref, cw_ref, gain_ref,
                lhsg_ref, ones_ref, triu4_ref, tril4_ref, stril4_ref, eye4_ref, lvl_ref, maskbd_ref,
                o_ref, s_ref, xbuf, ub_scr, w_scr, qk_scr, qe_scr, kd_scr, eg_scr, o_scr, sbd_scr, *, tc):
    j = pl.program_id(1)
    c = DELTA_CHUNK
    maskbd = maskbd_ref[...]
    maskf = maskbd.astype(F32)

    @pl.when(j == 0)
    def _():
        xbuf[0:8, :] = jnp.zeros((8, xbuf.shape[1]), F32)
        sbd_scr[...] = jnp.zeros_like(sbd_scr)

    xbuf[8:8 + tc, :] = cqkv_ref[...]
    cw = cw_ref[...]
    y = cw[3:4] * xbuf[8:8 + tc, :]
    for i in range(1, CONV_WIDTH):
        y = y + cw[3 - i:4 - i] * xbuf[8 - i:8 - i + tc, :]
    xbuf[0:8, :] = xbuf[tc:tc + 8, :]
    y = y * jax.nn.sigmoid(y)
    q = y[:, 0:HEAD_W]
    k = y[:, HEAD_W:2 * HEAD_W]
    qn = q * lax.rsqrt(_head_sums(q * q, maskbd) + 1e-6) * (DELTA_K_DIM ** -0.5)
    kn = k * lax.rsqrt(_head_sums(k * k, maskbd) + 1e-6)
    v = y[:, 2 * HEAD_W:3 * HEAD_W]
    beta = beta_ref[...]
    g = g_ref[...]

    cs = [slice(ci * c, (ci + 1) * c) for ci in range(tc // c)]
    qcs, kcs, vcs = [qn[s] for s in cs], [kn[s] for s in cs], [v[s] for s in cs]
    bcs, gcs = [beta[s] for s in cs], [g[s] for s in cs]
    gsums = [_dot3_l(lhsg_ref[...], gc) for gc in gcs]
    grows = [_dot3_l(ones_ref[...], gc * triu4_ref[...]) for gc in gcs]
    gcols, gsufs = [gs[0:c] for gs in gsums], [gs[c:2 * c] for gs in gsums]
    decays = [jnp.where(tril4_ref[...] > 0, jnp.exp(gcol - grow), 0.0) for gcol, grow in zip(gcols, grows)]
    exp_gs = [jnp.exp(gcol) for gcol in gcols]
    kbs = [kc * bc for kc, bc in zip(kcs, bcs)]
    kbds = [_bd(kc.astype(BF16), maskbd) for kc in kcs]
    kks = [_dot_nt(jnp.concatenate([kb, qc], axis=0).astype(BF16), kbd) for kb, qc, kbd in zip(kbs, qcs, kbds)]
    a4s = [kk[0:c] * decay * stril4_ref[...] for kk, decay in zip(kks, decays)]
    qk4s = [kk[c:2 * c] * decay for kk, decay in zip(kks, decays)]
    t16s = [t.astype(BF16) for t in _unit_lower_inverse(a4s, eye4_ref[...], lvl_ref, maskbd)]
    u_bases = [_dot(t16, _bd((vc * bc).astype(BF16), maskbd)) for t16, vc, bc in zip(t16s, vcs, bcs)]
    ws = [_dot(t16, _bd((kb * eg).astype(BF16), maskbd)) for t16, kb, eg in zip(t16s, kbs, exp_gs)]
    ub_scr[...] = jnp.concatenate(u_bases, axis=0)
    w_scr[...] = jnp.concatenate([w.astype(BF16) for w in ws], axis=0)
    qk_scr[...] = jnp.concatenate([x.astype(BF16) for x in qk4s], axis=0)
    qe_scr[...] = jnp.concatenate([(qc * eg).astype(BF16) for qc, eg in zip(qcs, exp_gs)], axis=0)
    kd_scr[...] = jnp.concatenate([(kc * jnp.exp(gs)).astype(BF16) for kc, gs in zip(kcs, gsufs)], axis=0)
    eg_scr[...] = jnp.concatenate([jnp.broadcast_to(eg[c - 1:c, :], (8, HEAD_W)) for eg in exp_gs], axis=0)

    def scan(ci, carry):
        sl = pl.ds(pl.multiple_of(ci * c, c), c)
        s_old = sbd_scr[...]
        s16 = s_old.astype(BF16)
        u16 = (ub_scr[sl, :] - _dot(w_scr[sl, :], s16)).astype(BF16)
        o_scr[sl, :] = _dot(qe_scr[sl, :], s16) + _dot(qk_scr[sl, :], _bd(u16, maskbd))
        exp_g_last = eg_scr[pl.ds(pl.multiple_of(ci * 8, 8), 8), :][0:1]
        sbd_scr[...] = s_old * exp_g_last + maskf * _dot_tn(kd_scr[sl, :], u16)
        return carry

    lax.fori_loop(0, tc // c, scan, 0)

    o = o_scr[...]
    ms = _head_sums(o * o, maskbd) * (1.0 / DELTA_V_DIM)
    o_ref[...] = (o * lax.rsqrt(ms + NORM_EPS) * gain_ref[...] * cz_ref[...]).astype(o_ref.dtype)

    @pl.when(j == pl.num_programs(1) - 1)
    def _():
        s_ref[...] = sbd_scr[...]


def _delta_prompt(cqkv, beta, g, cz, conv_w, gain, n_seq, seq_len):
    tc = min(seq_len, 512)
    nt = seq_len // tc
    cch = cqkv.shape[1]
    consts = _delta_consts()
    names = ("lhsg", "ones", "triu4", "tril4", "stril4", "eye4", "lvl", "maskbd")

    def rows(width):
        return pl.BlockSpec((tc, width), lambda n, j: (n * nt + j, 0))

    return pl.pallas_call(
        functools.partial(_delta_body, tc=tc),
        out_shape=(jax.ShapeDtypeStruct((n_seq * seq_len, HEAD_W), BF16),
                   jax.ShapeDtypeStruct((n_seq, HEAD_W, HEAD_W), F32)),
        grid=(n_seq, nt),
        in_specs=[rows(cch), rows(HEAD_W), rows(HEAD_W), rows(HEAD_W),
                  _const_spec((CONV_WIDTH, cch)), _const_spec((1, HEAD_W))]
                 + [_const_spec(consts[nm].shape) for nm in names],
        out_specs=(rows(HEAD_W), pl.BlockSpec((None, HEAD_W, HEAD_W), lambda n, j: (n, 0, 0))),
        scratch_shapes=[pltpu.VMEM((tc + 8, cch), F32), pltpu.VMEM((tc, HEAD_W), F32),
                        pltpu.VMEM((tc, HEAD_W), BF16), pltpu.VMEM((tc, HEAD_W), BF16),
                        pltpu.VMEM((tc, HEAD_W), BF16), pltpu.VMEM((tc, HEAD_W), BF16),
                        pltpu.VMEM((8 * (tc // DELTA_CHUNK), HEAD_W), F32),
                        pltpu.VMEM((tc, HEAD_W), F32), pltpu.VMEM((HEAD_W, HEAD_W), F32)],
        compiler_params=_cparams("arbitrary", "arbitrary"),
        name="delta_prompt",
    )(cqkv, beta, g, cz, conv_w.astype(F32), jnp.tile(gain.astype(F32), DELTA_HEADS).reshape(1, HEAD_W),
      *[consts[nm] for nm in names])


SROWS = 8


def _sample_attn_consts(n_pages):
    r = np.arange(PAGE_SIZE)
    m2 = r[:, None] > r[None, :]
    m2ones = np.concatenate([m2, np.ones((PAGE_SIZE, PAGE_SIZE), bool)], axis=1)
    idx = np.arange(4 * n_pages)
    u2 = (idx[None, :] // 4 > idx[:, None] // 4) & (idx[None, :] % 4 == idx[:, None] % 4)
    t = np.arange(SROWS)
    tri8 = t[None, :] <= t[:, None]
    sup = t[:, None] > r[None, :]
    lh = np.arange(HEAD_W) // 64
    f = lambda a, dt: jnp.asarray(np.asarray(a, np.float32), dt)
    return dict(m2ones=f(m2ones, BF16), u2=f(u2, BF16), tri8=f(tri8, BF16), sup=f(sup, F32),
                maskbd=f(lh[:, None] == lh[None, :], BF16))


def _pad_rows(x, rows):
    return jnp.concatenate([x, jnp.zeros((rows - x.shape[0], x.shape[1]), x.dtype)], axis=0)


def _block_rows(x, groups, lanes_per_group):
    xr = jnp.concatenate([x] * groups, axis=0)
    rg = lax.broadcasted_iota(jnp.int32, xr.shape, 0) // SROWS
    lg = lax.broadcasted_iota(jnp.int32, xr.shape, 1) // lanes_per_group
    return jnp.where(rg == lg, xr, 0.0).astype(BF16)


def _paged_softmax_pv(qbd, kt_pages, vt_pages, k_new, v_new, bias_pages, bias_new):
    n_pages = len(kt_pages)
    assert n_pages % 2 == 0
    tiles = []
    for p in range(0, n_pages, 2):
        kt2 = jnp.concatenate([kt_pages[p][...], kt_pages[p + 1][...]], axis=1).astype(BF16)
        s = _dot(qbd, kt2)
        if bias_pages is not None:
            s = s + jnp.concatenate([bias_pages[p], bias_pages[p + 1]], axis=1)
        tiles.append(s)
    s = _dot_nt(qbd, _pad_rows(k_new, PAGE_SIZE).astype(BF16))
    if bias_new is not None:
        s = s + bias_new
    row = lax.broadcasted_iota(jnp.int32, s.shape, 0) & (SROWS - 1)
    col = lax.broadcasted_iota(jnp.int32, s.shape, 1)
    tiles.append(jnp.where(col <= row, s, -jnp.inf))
    m = tiles[0].max(axis=1, keepdims=True)
    for s in tiles[1:]:
        m = jnp.maximum(m, s.max(axis=1, keepdims=True))
    acc = None
    den = None
    for i, s in enumerate(tiles):
        p = jnp.exp2(s - m)
        ps = p.sum(axis=1, keepdims=True)
        if 2 * i < n_pages:
            vt2 = jnp.concatenate([vt_pages[2 * i][...], vt_pages[2 * i + 1][...]], axis=1).astype(BF16)
            pv = _dot_nt(p.astype(BF16), vt2)
        else:
            pv = _dot(p.astype(BF16), _pad_rows(v_new, PAGE_SIZE).astype(BF16))
        acc = pv if acc is None else acc + pv
        den = ps if den is None else den + ps
    return acc / den


def _own_head_rows(x, stride):
    lh = lax.broadcasted_iota(jnp.int32, (SROWS, HEAD_W), 1) // 64
    out = None
    for h in range(4):
        part = jnp.where(lh == h, x[h * stride:h * stride + SROWS], 0.0)
        out = part if out is None else out + part
    return out


def _sample_attn_body(pt_ref, fk_hbm, fv_hbm, lf_hbm, dk_hbm, dv_hbm,
                      fq_ref, fkn_ref, fvn_ref, ln_ref, dq_ref, dkn_ref, dvn_ref,
                      m2_ref, u2_ref, tri8_ref, sup_ref, lam_ref, gain_ref, maskbd_ref,
                      of_ref, od_ref, fk_buf, fv_buf, lf_buf, dk_buf, dv_buf, sems, xm_scr, *, n_pages, layer, lam_init):
    b = pl.program_id(0)
    slot = b % 2
    streams = ((fk_hbm, fk_buf), (fv_hbm, fv_buf), (lf_hbm, lf_buf), (dk_hbm, dk_buf), (dv_hbm, dv_buf))

    def page_copies(sample, sl):
        cps = []
        for p in range(n_pages):
            page = pt_ref[sample, p]
            for hbm, buf in streams:
                cps.append(pltpu.make_async_copy(hbm.at[layer, page], buf.at[sl, p], sems.at[sl]))
        return cps

    @pl.when(b == 0)
    def _():
        for cp in page_copies(0, 0):
            cp.start()

    @pl.when(b + 1 < pl.num_programs(0))
    def _():
        for cp in page_copies(b + 1, 1 - slot):
            cp.start()

    for cp in page_copies(b, slot):
        cp.wait()

    fk_pages = [fk_buf.at[slot, p] for p in range(n_pages)]
    fv_pages = [fv_buf.at[slot, p] for p in range(n_pages)]
    dk_pages = [dk_buf.at[slot, p] for p in range(n_pages)]
    dv_pages = [dv_buf.at[slot, p] for p in range(n_pages)]

    for p in range(n_pages):
        xm_scr[4 * p:4 * p + 4, :] = lf_buf[slot, p]
    wt = _dot3_r(xm_scr[...], m2_ref[...])
    within, totals = wt[:, :PAGE_SIZE], wt[:, PAGE_SIZE:]
    d_past = (within + _dot3_l(u2_ref[...], totals)) * LOG2E
    lnew = ln_ref[...]
    tri8 = tri8_ref[...]
    cn = _dot3_l(tri8, lnew) * LOG2E
    cn_h = [jnp.broadcast_to(cn[:, h:h + 1], (SROWS, PAGE_SIZE)) for h in range(FOX_HEADS)]
    bias_pages = [
        jnp.concatenate([d_past[4 * p + h:4 * p + h + 1, :] + cn_h[h] for h in range(FOX_HEADS)], axis=0)
        for p in range(n_pages)]
    sup = sup_ref[...]
    bias_new = jnp.concatenate(
        [_dot3_l(tri8, jnp.broadcast_to(lnew[:, h:h + 1], (SROWS, PAGE_SIZE)) * sup) * LOG2E
         for h in range(FOX_HEADS)], axis=0)
    qbd = _block_rows(fq_ref[...], FOX_HEADS, FOX_DIM)
    o = _paged_softmax_pv(qbd, fk_pages, fv_pages, fkn_ref[...], fvn_ref[...], bias_pages, bias_new)
    of_ref[...] = _own_head_rows(o, SROWS)

    qbd = _block_rows(dq_ref[...], 2 * DIFF_HEADS, DIFF_QK_DIM)
    o = _paged_softmax_pv(qbd, dk_pages, dv_pages, dkn_ref[...], dvn_ref[...], None, None)
    lam = _diff_lambda(lam_ref, lam_init)
    o = _own_head_rows(o, 2 * SROWS) - lam * _own_head_rows(o[SROWS:], 2 * SROWS)
    ms = _head_sums(o * o, maskbd_ref[...]) * (1.0 / DIFF_V_DIM)
    od_ref[...] = o * lax.rsqrt(ms + NORM_EPS) * gain_ref[...] * (1.0 - lam_init)


def _sample_attn(page_table, caches, layer, proj, lam_p, gain, lam_init):
    n_samples, n_pages = page_table.shape
    consts = _sample_attn_consts(n_pages)

    def tile(width):
        return pl.BlockSpec((SROWS, width), lambda b, pt: (b, 0))

    def const(shape):
        nd = len(shape)
        return pl.BlockSpec(shape, lambda b, pt: (0,) * nd, pipeline_mode=pl.Buffered(1))

    in_specs = [pl.BlockSpec(memory_space=pl.ANY) for _ in caches]
    args = list(caches)
    new = (proj["fq"], proj["fk"], proj["fv"], proj["small"], proj["dq"], proj["dk"], proj["dv"])
    in_specs += [tile(a.shape[1]) for a in new]
    args += list(new)
    cvals = (consts["m2ones"], consts["u2"], consts["tri8"], consts["sup"], lam_p.astype(F32),
             jnp.tile(gain.astype(F32), DIFF_HEADS).reshape(1, HEAD_W), consts["maskbd"])
    in_specs += [const(c.shape) for c in cvals]
    args += list(cvals)
    gs = pltpu.PrefetchScalarGridSpec(
        num_scalar_prefetch=1, grid=(n_samples,), in_specs=in_specs,
        out_specs=(tile(HEAD_W), tile(HEAD_W)),
        scratch_shapes=[pltpu.VMEM((2, n_pages) + c.shape[2:], c.dtype) for c in caches]
                       + [pltpu.SemaphoreType.DMA((2,)), pltpu.VMEM((4 * n_pages, PAGE_SIZE), F32)])
    return pl.pallas_call(
        functools.partial(_sample_attn_body, n_pages=n_pages, layer=layer, lam_init=lam_init),
        out_shape=(jax.ShapeDtypeStruct((n_samples * SROWS, HEAD_W), F32),) * 2,
        grid_spec=gs,
        compiler_params=_cparams("arbitrary"),
        name="sample_attn",
    )(page_table, *args)


def _delta_sample_body(cqkv_ref, buf_ref, beta_ref, g_ref, cz_ref, cw_ref, gain_ref, s0_ref,
                       tile_ref, tile_t_ref, maskbd_ref, o_ref, s_ref, xp_scr, pr_scr, u_scr, kd_scr, o_scr,
                       *, n_tok, group):
    maskbd = maskbd_ref[...]
    maskf = maskbd.astype(F32)
    cch = xp_scr.shape[1]
    cw = cw_ref[...]
    ys = []
    for s in range(group):
        base = 16 * s
        xp_scr[base:base + 8, :] = jnp.zeros((8, cch), F32)
        xp_scr[base + 8 - (CONV_WIDTH - 1):base + 8, :] = buf_ref[s]
        xp_scr[base + 8:base + 16, :] = cqkv_ref[SROWS * s:SROWS * (s + 1), :]
        y = cw[3:4] * xp_scr[base + 8:base + 16, :]
        for i in range(1, CONV_WIDTH):
            y = y + cw[3 - i:4 - i] * xp_scr[base + 8 - i:base + 16 - i, :]
        ys.append(y)
    y = jnp.concatenate(ys, axis=0)
    y = y * jax.nn.sigmoid(y)
    q = y[:, 0:HEAD_W]
    k = y[:, HEAD_W:2 * HEAD_W]
    v = y[:, 2 * HEAD_W:3 * HEAD_W]
    q = q * lax.rsqrt(_head_sums(q * q, maskbd) + 1e-6) * (DELTA_K_DIM ** -0.5)
    k = k * lax.rsqrt(_head_sums(k * k, maskbd) + 1e-6)
    beta = beta_ref[...]
    a = jnp.exp(g_ref[...])

    pairs_k = [(t, j) for t in range(n_tok) for j in range(t)]
    pairs_q = [(t, j) for t in range(n_tok) for j in range(t + 1)]
    assert len(pairs_k) + len(pairs_q) <= 16
    pr_scr[...] = jnp.zeros_like(pr_scr)
    for s in range(group):
        r0, p0 = SROWS * s, 16 * s
        for r, (t, j) in enumerate(pairs_k):
            pr_scr[p0 + r:p0 + r + 1, :] = k[r0 + t:r0 + t + 1] * k[r0 + j:r0 + j + 1]
        for r, (t, j) in enumerate(pairs_q):
            rr = p0 + len(pairs_k) + r
            pr_scr[rr:rr + 1, :] = q[r0 + t:r0 + t + 1] * k[r0 + j:r0 + j + 1]
    dots = _head_sums(pr_scr[...], maskbd)

    rows = [slice(SROWS * s, SROWS * (s + 1)) for s in range(group)]
    sbd0s = [maskf * _dot3_r(s0_ref[s], tile_ref[...]) for s in range(group)]
    s16s = [sb.astype(BF16) for sb in sbd0s]
    rks = [_dot(k[r].astype(BF16), s16) for r, s16 in zip(rows, s16s)]
    rqs = [_dot(q[r].astype(BF16), s16) for r, s16 in zip(rows, s16s)]

    u_scr[...] = jnp.zeros_like(u_scr)
    kd_scr[...] = jnp.zeros_like(kd_scr)
    o_scr[...] = jnp.zeros_like(o_scr)
    a_alls = []
    for s in range(group):
        r0, p0 = SROWS * s, 16 * s
        kk = {tj: dots[p0 + r:p0 + r + 1] for r, tj in enumerate(pairs_k)}
        qk = {tj: dots[p0 + len(pairs_k) + r:p0 + len(pairs_k) + r + 1] for r, tj in enumerate(pairs_q)}
        rk_rows = [rks[s][t:t + 1] for t in range(n_tok)]
        rq_rows = [rqs[s][t:t + 1] for t in range(n_tok)]
        for t in range(n_tok):
            a_t = a[r0 + t:r0 + t + 1]
            u_t = beta[r0 + t:r0 + t + 1] * (v[r0 + t:r0 + t + 1] - a_t * rk_rows[t])
            for t2 in range(t + 1, n_tok):
                rk_rows[t2] = a_t * rk_rows[t2] + kk[(t2, t)] * u_t
            for t2 in range(t, n_tok):
                rq_rows[t2] = a_t * rq_rows[t2] + qk[(t2, t)] * u_t
            o_scr[r0 + t:r0 + t + 1, :] = rq_rows[t]
            u_scr[r0 + t:r0 + t + 1, :] = u_t
            tail = None
            for t2 in range(t + 1, n_tok):
                tail = a[r0 + t2:r0 + t2 + 1] if tail is None else tail * a[r0 + t2:r0 + t2 + 1]
            kd_scr[r0 + t:r0 + t + 1, :] = k[r0 + t:r0 + t + 1] if tail is None else k[r0 + t:r0 + t + 1] * tail
        a_all = a[r0:r0 + 1]
        for t in range(1, n_tok):
            a_all = a_all * a[r0 + t:r0 + t + 1]
        a_alls.append(a_all)
    upd = [_dot_tn(kd_scr[r, :].astype(BF16), u_scr[r, :].astype(BF16)) for r in rows]
    s_news = [sb * a_all + maskf * u for sb, a_all, u in zip(sbd0s, a_alls, upd)]
    for s in range(group):
        s_ref[s] = _dot3_r(s_news[s], tile_t_ref[...])

    o = o_scr[...]
    ms = _head_sums(o * o, maskbd) * (1.0 / DELTA_V_DIM)
    o_ref[...] = o * lax.rsqrt(ms + NORM_EPS) * gain_ref[...] * cz_ref[...]


def _delta_sample(proj, conv_state, state, layer, conv_w, gain, n_tok):
    n_samples = state.shape[1]
    cch = conv_state.shape[3]
    v_ = np.arange(DELTA_V_DIM)
    lane = np.arange(HEAD_W)
    tile_m = jnp.asarray((v_[:, None] == lane[None, :] % DELTA_V_DIM).astype(np.float32), BF16)
    tile_t = jnp.asarray((lane[:, None] % DELTA_V_DIM == v_[None, :]).astype(np.float32), BF16)
    lh = lane // 64
    maskbd = jnp.asarray((lh[:, None] == lh[None, :]).astype(np.float32), BF16)

    group = math.gcd(n_samples, 8)
    rows = group * SROWS

    def tile(width):
        return pl.BlockSpec((rows, width), lambda b: (b, 0))

    return pl.pallas_call(
        functools.partial(_delta_sample_body, n_tok=n_tok, group=group),
        out_shape=(jax.ShapeDtypeStruct((n_samples * SROWS, HEAD_W), F32),
                   jax.ShapeDtypeStruct((n_samples, HEAD_W, DELTA_V_DIM), F32)),
        grid=(n_samples // group,),
        in_specs=[tile(cch),
                  pl.BlockSpec((None, group, CONV_WIDTH - 1, cch), lambda b: (layer, b, 0, 0)),
                  tile(HEAD_W), tile(HEAD_W), tile(HEAD_W),
                  _const_spec((CONV_WIDTH, cch)), _const_spec((1, HEAD_W)),
                  pl.BlockSpec((None, group, HEAD_W, DELTA_V_DIM), lambda b: (layer, b, 0, 0)),
                  _const_spec(tile_m.shape), _const_spec(tile_t.shape), _const_spec(maskbd.shape)],
        out_specs=(tile(HEAD_W), pl.BlockSpec((group, HEAD_W, DELTA_V_DIM), lambda b: (b, 0, 0))),
        scratch_shapes=[pltpu.VMEM((2 * rows, cch), F32), pltpu.VMEM((2 * rows, HEAD_W), F32),
                        pltpu.VMEM((rows, HEAD_W), F32), pltpu.VMEM((rows, HEAD_W), F32),
                        pltpu.VMEM((rows, HEAD_W), F32)],
        compiler_params=_cparams("arbitrary"),
        name="delta_sample",
    )(proj["cqkv"], conv_state, proj["beta"], proj["g"], proj["cz"], conv_w.astype(F32),
      jnp.tile(gain.astype(F32), DELTA_HEADS).reshape(1, HEAD_W), state, tile_m, tile_t, maskbd)


def kernel(x_prompt, x_sample, cache_fox_k, cache_fox_v, cache_fox_logf, cache_diff_k, cache_diff_v, state_delta, state_conv, page_table, norm_ffn1, ffn1_wi, ffn1_wo, norm_mix, w_in, fox_f_bias, diff_lambda, diff_norm, delta_conv_w, delta_A_log, delta_dt_bias, delta_norm, w_branch, w_out, norm_ffn2, ffn2_wi, ffn2_wo, norm_final):
    n_seq, seq_len, d = x_prompt.shape
    n_smp, dec_seq, _ = x_sample.shape
    depth = w_in.shape[0]
    n_pool = cache_fox_k.shape[1]
    n_pages = page_table.shape[1]
    past_len = n_pages * PAGE_SIZE
    assert CONV_WIDTH - 1 <= dec_seq <= SROWS and seq_len % DELTA_CHUNK == 0

    wi1, wo1 = ffn1_wi.astype(BF16), ffn1_wo.astype(BF16)
    wi2, wo2 = ffn2_wi.astype(BF16), ffn2_wo.astype(BF16)
    w_in_t = jnp.swapaxes(w_in, 1, 2)
    w_p = _relayout_w_in(w_in_t, _PROMPT_COLS)
    w_s = _relayout_w_in(w_in_t, _SAMPLE_COLS)
    w_vt = _relayout_w_values_t(w_in_t)
    w_gates = w_in_t[:, w_in_t.shape[1] - N_BRANCH * d:, :].astype(BF16)
    wb, wo = w_branch.astype(BF16), w_out.astype(BF16)

    def pages_t(cache):
        c = cache.reshape(depth, n_pool, PAGE_SIZE, HEAD_W)
        return jnp.swapaxes(c, 2, 3)

    caches = (pages_t(cache_fox_k), pages_t(cache_fox_v), jnp.swapaxes(cache_fox_logf, 2, 3),
              pages_t(cache_diff_k), pages_t(cache_diff_v))
    s_delta = state_delta.reshape(depth, n_smp, HEAD_W, DELTA_V_DIM)

    xp = x_prompt.reshape(n_seq * seq_len, d)
    xs = jnp.pad(x_sample, ((0, 0), (0, SROWS - dec_seq), (0, 0))).reshape(n_smp * SROWS, d)
    cos_p, sin_p = _rope_tables(seq_len, 0, seq_len)
    cos_s, sin_s = _rope_tables(n_smp * SROWS, past_len, SROWS)

    cache_bufs = {name: jnp.zeros((depth, n_seq, HEAD_W, seq_len), F32) for name in _CACHE_OUTS}

    new_p, new_s = [], []
    for l in range(depth):
        lam_init = 0.8 - 0.6 * math.exp(-0.3 * l)
        xp = _ffn(xp, norm_ffn1[l], wi1, wo1, l)
        xs = _ffn(xs, norm_ffn1[l], wi1, wo1, l)
        pp = _inproj(xp, norm_mix[l], w_p, l, cos_p, sin_p, fox_f_bias[l], delta_A_log[l], delta_dt_bias[l],
                     prompt=True, seq_len=seq_len, w_vt=w_vt, cache_bufs=cache_bufs)
        cache_bufs = {name: pp[name] for name in _CACHE_OUTS}
        ps = _inproj(xs, norm_mix[l], w_s, l, cos_s, sin_s, fox_f_bias[l], delta_A_log[l], delta_dt_bias[l],
                     prompt=False, seq_len=SROWS)

        of_p, od_p = _attn_prompt(pp["qcat"], pp["kcat"], pp["vcat"],
                                  diff_lambda[l], diff_norm[l], lam_init, n_seq, seq_len)
        ol_p, sbd = _delta_prompt(pp["cqkv"], pp["beta"], pp["g"], pp["cz"], delta_conv_w[l], delta_norm[l],
                                  n_seq, seq_len)
        of_s, od_s = _sample_attn(page_table, caches, l, ps, diff_lambda[l], diff_norm[l], lam_init)
        ol_s, s_new = _delta_sample(ps, state_conv, s_delta, l, delta_conv_w[l], delta_norm[l], dec_seq)

        final = norm_final if l == depth - 1 else None
        xp = _merge_ffn(xp, norm_mix[l], w_gates, wb, wo, l, (of_p, od_p, ol_p), norm_ffn2[l], wi2, wo2, final)
        xs = _merge_ffn(xs, norm_mix[l], w_gates, wb, wo, l, (of_s, od_s, ol_s), norm_ffn2[l], wi2, wo2, final)

        def p3(a):
            return a.reshape(n_seq, seq_len, a.shape[1])

        def s3(a):
            return a.reshape(n_smp, SROWS, a.shape[1])[:, :dec_seq]

        new_p.append(dict(
            fox_logf=p3(pp["small"])[:, :, :FOX_HEADS],
            delta=jnp.stack([sbd[:, h * 64:(h + 1) * 64, h * 64:(h + 1) * 64] for h in range(DELTA_HEADS)], axis=1),
            conv=p3(pp["cqkv"])[:, seq_len - (CONV_WIDTH - 1):]))
        new_s.append(dict(
            fox_k=s3(ps["fk"]).reshape(n_smp, dec_seq, FOX_HEADS, FOX_DIM),
            fox_v=s3(ps["fv"]).reshape(n_smp, dec_seq, FOX_HEADS, FOX_DIM),
            fox_logf=s3(ps["small"])[:, :, :FOX_HEADS],
            diff_k=s3(ps["dk"]).reshape(n_smp, dec_seq, DIFF_HEADS, 2, DIFF_QK_DIM),
            diff_v=s3(ps["dv"]).reshape(n_smp, dec_seq, DIFF_HEADS, DIFF_V_DIM),
            delta=s_new.reshape(n_smp, DELTA_HEADS, DELTA_K_DIM, DELTA_V_DIM),
            conv=s3(ps["cqkv"])[:, dec_seq - (CONV_WIDTH - 1):]))

    def stk(lst, name):
        return jnp.stack([dd[name] for dd in lst])

    def token_major(buf, *feat):
        nf = len(feat)
        b = buf.reshape((depth, n_seq) + feat + (seq_len,))
        return jnp.transpose(b, (0, 1, 2 + nf) + tuple(range(2, 2 + nf)))

    y_prompt = xp.reshape(n_seq, seq_len, d)
    y_sample = xs.reshape(n_smp, SROWS, d)[:, :dec_seq]
    return (y_prompt, y_sample,
            token_major(cache_bufs["fkt"], FOX_HEADS, FOX_DIM), stk(new_s, "fox_k"),
            token_major(cache_bufs["fvt32"], FOX_HEADS, FOX_DIM), stk(new_s, "fox_v"),
            stk(new_p, "fox_logf"), stk(new_s, "fox_logf"),
            token_major(cache_bufs["dkt"], DIFF_HEADS, 2, DIFF_QK_DIM), stk(new_s, "diff_k"),
            token_major(cache_bufs["dvt32"], DIFF_HEADS, DIFF_V_DIM), stk(new_s, "diff_v"),
            stk(new_p, "delta"), stk(new_s, "delta"),
            stk(new_p, "conv"), stk(new_s, "conv"))
```
